```python
import jax, jax.numpy as jnp
from jax import lax
import numpy as np

D_MODEL = 2048
BATCH = 2
SEQ = 4096
DEPTH = 1

GRID_W = 64
CTX_LEN = 256
M_HEADS = 4
M_V = D_MODEL // 2
M_DV = M_V // M_HEADS
M_QK = M_V // 2
M_DQK = M_QK // M_HEADS
M_CHUNK = 64
R_W = D_MODEL // 2
R_N = 64
R_HEADS = R_W // R_N
R_LORA_W = 64
R_LORA_A = 64
R_LORA_G = 160
D_MIX = M_V + R_W
SPLIT_SIZES = (M_QK, M_QK, M_V, M_V, 4 * M_HEADS, R_W, R_W, R_W, 2 * R_LORA_W, 2 * R_LORA_A, R_LORA_G)
N_IN = 2 * M_QK + 2 * M_V + 4 * M_HEADS + 3 * R_W + 2 * R_LORA_W + 2 * R_LORA_A + R_LORA_G
D_FF = ((8 * D_MODEL // 3 + 255) // 256) * 256
NORM_EPS = 1e-6
GN_EPS = 64e-5

kernel_name = "hybrid_mlstm_rwkv7_prefix_dit_block"


def rms_norm(x, g):
    x32 = x.astype(jnp.float32)
    y = x32 * lax.rsqrt(jnp.mean(x32 * x32, axis=-1, keepdims=True) + NORM_EPS)
    return (y * g.astype(jnp.float32)).astype(x.dtype)


def adaln(cvec, w, b):
    mod = (jax.nn.silu(cvec) @ w + b).reshape(-1, 1, 6 * D_MODEL)
    return jnp.split(mod, 6, axis=-1)


def modulate(h, shift, scale):
    return h * (1 + scale) + shift


def conv1d_centred(x, w, b):
    xp = jnp.pad(x, ((0, 0), (1, 1), (0, 0)))
    return xp[:, :-2] * w[0] + xp[:, 1:-1] * w[1] + xp[:, 2:] * w[2] + b


def mlstm_chunkwise(q, k, v, i_pre, logf, state):
    B, H, L, _ = q.shape
    nc = L // M_CHUNK
    to_chunks = lambda t: jnp.moveaxis(t.reshape((B, H, nc, M_CHUNK) + t.shape[3:]), 2, 0)
    tri = jnp.tril(jnp.ones((M_CHUNK, M_CHUNK), dtype=bool))

    def body(carry, xs):
        C, n, m = carry
        qc, kc, vc, ic, fc = xs
        bcum = jnp.cumsum(fc, axis=-1)
        dmat = bcum[..., :, None] - bcum[..., None, :] + ic[..., None, :]
        dmat = jnp.where(tri, dmat, -jnp.inf)
        m_inter = bcum + m[..., None]
        m_j = jnp.maximum(m_inter, jnp.max(dmat, axis=-1))
        s = jnp.einsum("bhjd,bhsd->bhjs", qc, kc) * jnp.exp(dmat - m_j[..., None])
        inter = jnp.exp(m_inter - m_j)
        num = jnp.einsum("bhjs,bhsv->bhjv", s, vc) + inter[..., None] * jnp.einsum("bhvd,bhjd->bhjv", C, qc)
        den = jnp.sum(s, axis=-1) + inter * jnp.einsum("bhd,bhjd->bhj", n, qc)
        h = num / jnp.maximum(jnp.abs(den), jnp.exp(-m_j))[..., None]
        b_last = bcum[..., -1]
        glog = b_last[..., None] - bcum + ic
        m_new = jnp.maximum(b_last + m, jnp.max(glog, axis=-1))
        wk = jnp.exp(glog - m_new[..., None])
        decay = jnp.exp(b_last + m - m_new)
        C_new = decay[..., None, None] * C + jnp.einsum("bhs,bhsv,bhsd->bhvd", wk, vc, kc)
        n_new = decay[..., None] * n + jnp.einsum("bhs,bhsd->bhd", wk, kc)
        return (C_new, n_new, m_new), h

    xs = (to_chunks(q), to_chunks(k), to_chunks(v), to_chunks(i_pre), to_chunks(logf))
    final, hs = lax.scan(body, state, xs)
    h = jnp.moveaxis(hs, 0, 2).reshape(B, H, L, v.shape[-1])
    return h, final


def rwkv7_scan(r, w, k, v, kk, a, s0, reverse):
    def step(s, xs):
        r_t, w_t, k_t, v_t, kk_t, a_t = xs
        s = (s * w_t[:, :, None, :]
             - jnp.einsum("bhvk,bhk->bhv", s, kk_t)[..., None] * (kk_t * a_t)[:, :, None, :]
             + v_t[..., :, None] * k_t[..., None, :])
        return s, jnp.einsum("bhvk,bhk->bhv", s, r_t)
    xs = tuple(jnp.moveaxis(t, 1, 0) for t in (r, w, k, v, kk, a))
    s_final, ys = lax.scan(step, s0, xs, reverse=reverse)
    return jnp.moveaxis(ys, 0, 1), s_final


def zero_state(batch):
    mz = (jnp.zeros((batch, M_HEADS, M_DV, M_DQK), jnp.float32),
          jnp.zeros((batch, M_HEADS, M_DQK), jnp.float32),
          jnp.zeros((batch, M_HEADS), jnp.float32))
    sz = jnp.zeros((batch, R_HEADS, R_N, R_N), jnp.float32)
    return (mz, mz, sz, sz)


def hybrid_mixers(h, p, state):
    B, L, _ = h.shape
    proj = (h @ p["w_in"]).astype(jnp.float32)
    (q, k, v_m, o_m, gates, r, k_r, v_r, lw, la, lg) = jnp.split(
        proj, np.cumsum(SPLIT_SIZES)[:-1].tolist(), axis=-1)

    qk = conv1d_centred(jnp.concatenate([q, k], axis=-1), p["m_conv_w"], p["m_conv_b"])
    q, k = jnp.split(qk, 2, axis=-1)
    heads = lambda t, n: t.reshape(B, L, -1, n).transpose(0, 2, 1, 3)
    q = heads(q, M_DQK) * (M_DQK ** -0.5)
    k = heads(k, M_DQK)
    vm = heads(v_m, M_DV)
    g4 = (gates.reshape(B, L, 4, M_HEADS) + p["m_gate_b"]).transpose(2, 0, 3, 1)
    i_f, i_b = g4[0], g4[1]
    lf_f, lf_b = jax.nn.log_sigmoid(g4[2]), jax.nn.log_sigmoid(g4[3])
    flip = lambda t: jnp.flip(t, axis=2)
    h_f, st_mf = mlstm_chunkwise(q, k, vm, i_f, lf_f, state[0])
    h_b, st_mb = mlstm_chunkwise(flip(q), flip(k), flip(vm), flip(i_b), flip(lf_b), state[1])
    hm = h_f + flip(h_b)
    hm = hm * lax.rsqrt(jnp.mean(hm * hm, axis=-1, keepdims=True) + NORM_EPS)
    hm = hm.transpose(0, 2, 1, 3).reshape(B, L, M_V) * p["m_norm_g"] * jax.nn.sigmoid(o_m)

    lw = lw.reshape(B, L, 2, R_LORA_W)
    la = la.reshape(B, L, 2, R_LORA_A)
    wlog = -jax.nn.softplus(-(p["r_w0"] + jnp.einsum("bldr,drc->bldc", jnp.tanh(lw), p["r_w_up"]))) - 0.5
    decay = jnp.exp(-jnp.exp(wlog))
    a = jax.nn.sigmoid(p["r_a0"] + jnp.einsum("bldr,drc->bldc", la, p["r_a_up"]))
    g = jax.nn.sigmoid(lg) @ p["r_g_up"]
    hd = lambda t: t.reshape(B, L, R_HEADS, R_N)
    kk = hd(k_r * p["r_k_k"])
    kk = kk * lax.rsqrt(jnp.sum(kk * kk, axis=-1, keepdims=True) + 1e-12)
    k_dir = k_r[:, :, None, :] * (1 + (a - 1) * p["r_k_a"])
    rh, vh = hd(r), hd(v_r)
    y_f, s_f = rwkv7_scan(rh, hd(decay[:, :, 0]), hd(k_dir[:, :, 0]), vh, kk, hd(a[:, :, 0]), state[2], False)
    y_b, s_b = rwkv7_scan(rh, hd(decay[:, :, 1]), hd(k_dir[:, :, 1]), vh, kk, hd(a[:, :, 1]), state[3], True)
    y = y_f + y_b
    mu = jnp.mean(y, axis=-1, keepdims=True)
    var = jnp.mean(jnp.square(y - mu), axis=-1, keepdims=True)
    y = ((y - mu) * lax.rsqrt(var + GN_EPS)).reshape(B, L, R_W) * p["r_gn_w"] + p["r_gn_b"]
    bonus = jnp.sum(rh * hd(k_dir[:, :, 0] + k_dir[:, :, 1]) * p["r_r_k"], axis=-1, keepdims=True) * vh
    y = (y + bonus.reshape(B, L, R_W)) * g

    out = jnp.concatenate([hm, y], axis=-1).astype(h.dtype)
    return out, (st_mf, st_mb, s_f, s_b)


def conv_ffn(h, rows, cols, p):
    B, L, _ = h.shape
    u = (h @ p["f_w_up"]).reshape(B, rows, cols, D_FF)
    gt = h @ p["f_w_gate"]
    kern = p["f_conv_w"][:, :, None, :].astype(u.dtype)
    u = lax.conv_general_dilated(u, kern, (1, 1), "SAME",
                                 dimension_numbers=("NHWC", "HWIO", "NHWC"),
                                 feature_group_count=D_FF) + p["f_conv_b"]
    u = u.reshape(B, L, D_FF)
    return (jax.nn.gelu(u, approximate=True) * gt) @ p["f_w_down"]


def setup_inputs(seed: int = 0) -> dict:
    key = jax.random.key(seed)
    ks = jax.random.split(key, 32)
    nrm = lambda k, shape, s: jax.random.normal(k, shape, jnp.float32) * s
    D = D_MODEL
    m_gate_b = jnp.concatenate([
        nrm(ks[9], (DEPTH, 2, M_HEADS), 0.1),
        3.0 + 3.0 * jax.random.uniform(ks[10], (DEPTH, 2, M_HEADS), jnp.float32)], axis=1)
    return {
        "x": nrm(ks[0], (BATCH, SEQ, D), 1.0),
        "c": nrm(ks[1], (BATCH, D), 1.0),
        "ctx": nrm(ks[2], (BATCH, CTX_LEN, D), 1.0),
        "c_ctx": nrm(ks[3], (D,), 1.0),
        "w_mod": nrm(ks[4], (DEPTH, D, 6 * D), 0.5 * D ** -0.5),
        "b_mod": nrm(ks[5], (DEPTH, 6 * D), 0.02),
        "g_norm1": 1.0 + nrm(ks[6], (DEPTH, D), 0.02),
        "g_norm2": 1.0 + nrm(ks[7], (DEPTH, D), 0.02),
        "w_in": nrm(ks[8], (DEPTH, D, N_IN), D ** -0.5),
        "m_conv_w": nrm(ks[11], (DEPTH, 3, 2 * M_QK), 3 ** -0.5),
        "m_conv_b": nrm(ks[12], (DEPTH, 2 * M_QK), 0.02),
        "m_gate_b": m_gate_b,
        "m_norm_g": 1.0 + nrm(ks[13], (DEPTH, M_V), 0.02),
        "r_w0": jax.random.uniform(ks[14], (DEPTH, 2, R_W), jnp.float32, -6.0, 0.0),
        "r_w_up": nrm(ks[15], (DEPTH, 2, R_LORA_W, R_W), 0.5 * R_LORA_W ** -0.5),
        "r_a0": nrm(ks[16], (DEPTH, 2, R_W), 0.1),
        "r_a_up": nrm(ks[17], (DEPTH, 2, R_LORA_A, R_W), 0.5 * R_LORA_A ** -0.5),
        "r_g_up": nrm(ks[18], (DEPTH, R_LORA_G, R_W), R_LORA_G ** -0.5),
        "r_k_k": 0.85 + nrm(ks[19], (DEPTH, R_W), 0.05),
        "r_k_a": 1.0 + nrm(ks[20], (DEPTH, R_W), 0.05),
        "r_r_k": nrm(ks[21], (DEPTH, R_HEADS, R_N), 0.1),
        "r_gn_w": 1.0 + nrm(ks[22], (DEPTH, R_W), 0.02),
        "r_gn_b": nrm(ks[23], (DEPTH, R_W), 0.02),
        "w_out": nrm(ks[24], (DEPTH, D_MIX, D), D_MIX ** -0.5),
        "f_w_up": nrm(ks[25], (DEPTH, D, D_FF), D ** -0.5),
        "f_w_gate": nrm(ks[26], (DEPTH, D, D_FF), D ** -0.5),
        "f_conv_w": nrm(ks[27], (DEPTH, 3, 3, D_FF), 1.0 / 3.0),
        "f_conv_b": nrm(ks[28], (DEPTH, D_FF), 0.02),
        "f_w_down": nrm(ks[29], (DEPTH, D_FF, D), D_FF ** -0.5),
        "g_final": 1.0 + nrm(ks[30], (D,), 0.02),
    }


def reference(x, c, ctx, c_ctx, w_mod, b_mod, g_norm1, g_norm2, w_in, m_conv_w, m_conv_b, m_gate_b,
              m_norm_g, r_w0, r_w_up, r_a0, r_a_up, r_g_up, r_k_k, r_k_a, r_r_k, r_gn_w, r_gn_b,
              w_out, f_w_up, f_w_gate, f_conv_w, f_conv_b, f_w_down, g_final):
    B, L, _ = x.shape
    rows = L // GRID_W
    ctx_len = ctx.shape[1]
    for layer in range(DEPTH):
        last = layer == DEPTH - 1
        p = {
            "w_in": w_in[layer], "m_conv_w": m_conv_w[layer], "m_conv_b": m_conv_b[layer],
            "m_gate_b": m_gate_b[layer], "m_norm_g": m_norm_g[layer],
            "r_w0": r_w0[layer], "r_w_up": r_w_up[layer], "r_a0": r_a0[layer], "r_a_up": r_a_up[layer],
            "r_g_up": r_g_up[layer], "r_k_k": r_k_k[layer], "r_k_a": r_k_a[layer], "r_r_k": r_r_k[layer],
            "r_gn_w": r_gn_w[layer], "r_gn_b": r_gn_b[layer],
            "f_w_up": f_w_up[layer], "f_w_gate": f_w_gate[layer], "f_conv_w": f_conv_w[layer],
            "f_conv_b": f_conv_b[layer], "f_w_down": f_w_down[layer],
        }
        sh1, sc1, gt1, sh2, sc2, gt2 = adaln(c, w_mod[layer], b_mod[layer])
        csh1, csc1, cgt1, csh2, csc2, cgt2 = adaln(c_ctx, w_mod[layer], b_mod[layer])

        hc = modulate(rms_norm(ctx, g_norm1[layer]), csh1, csc1)
        yc, ctx_state = hybrid_mixers(hc, p, zero_state(ctx.shape[0]))
        hx = modulate(rms_norm(x, g_norm1[layer]), sh1, sc1)
        yx, _ = hybrid_mixers(hx, p, ctx_state)
        x = x + gt1 * (yx @ w_out[layer])

        hx = modulate(rms_norm(x, g_norm2[layer]), sh2, sc2)
        x = x + gt2 * conv_ffn(hx, rows, GRID_W, p)

        if not last:
            ctx = ctx + cgt1 * (yc @ w_out[layer])
            hc = modulate(rms_norm(ctx, g_norm2[layer]), csh2, csc2)
            ctx = ctx + cgt2 * conv_ffn(hc, 1, ctx_len, p)
    return rms_norm(x, g_final)
```

```python
import functools

import jax
import jax.numpy as jnp
from jax import lax
from jax.experimental import pallas as pl
from jax.experimental.pallas import tpu as pltpu

F32 = jnp.float32
BF16 = jnp.bfloat16
HIGHEST = lax.Precision.HIGHEST

LANES = 128
GRID_W = 64
M_HEADS = 4
M_DQK = 128
M_DV = 256
R_N = 64
NORM_EPS = 1e-6
GN_EPS = 64e-5
MLSTM_CHUNK = 128
RWKV_CHUNK = 64
VMEM_LIMIT = 56 * 1024 * 1024

SLAB_Q, SLAB_K, SLAB_V, SLAB_O = 0, 4, 8, 16
SLAB_R, SLAB_KR, SLAB_VR = 24, 32, 40
SLAB_LW, SLAB_LA, SLAB_LG0, SLAB_LG1 = 48, 49, 50, 51
N_SLABS = 52
GATE_LANE0 = 32


def _mm(a, b, precision=None):
    return jnp.dot(a, b, preferred_element_type=F32, precision=precision)


def _mm_nt(a, b):
    return lax.dot_general(a, b, (((1,), (1,)), ((), ())), preferred_element_type=F32)


def _softplus(x):
    return jnp.maximum(x, 0.0) + jnp.log1p(jnp.exp(-jnp.abs(x)))


def _log_sigmoid(x):
    return -_softplus(-x)


def _params(n_axes):
    return pltpu.CompilerParams(dimension_semantics=("arbitrary",) * n_axes,
                                vmem_limit_bytes=VMEM_LIMIT)


def _mod_kernel(cv_ref, w_ref, b_ref, o_ref):
    cv = cv_ref[...]
    s = (cv * jax.nn.sigmoid(cv)).astype(BF16)
    o_ref[...] = _mm(s, w_ref[...].astype(BF16)) + b_ref[...]


def _modulation(cv8, w_mod, b_mod):
    d, n = w_mod.shape
    tn = 1024
    return pl.pallas_call(
        _mod_kernel,
        grid=(n // tn,),
        in_specs=[pl.BlockSpec((8, d), lambda j: (0, 0)),
                  pl.BlockSpec((d, tn), lambda j: (0, j)),
                  pl.BlockSpec((1, tn), lambda j: (0, j))],
        out_specs=pl.BlockSpec((8, tn), lambda j: (0, j)),
        out_shape=jax.ShapeDtypeStruct((8, n), F32),
        compiler_params=_params(1),
    )(cv8, w_mod, b_mod.reshape(1, n))


def _inproj_kernel(tiles_per_row, row0, x_ref, g_ref, sh_ref, sc_ref, w_ref, o_ref, hx_ref):
    i = pl.program_id(0)
    j = pl.program_id(1)

    @pl.when(j == 0)
    def _():
        x = x_ref[...]
        ms = jnp.mean(x * x, axis=-1, keepdims=True)
        y = x * lax.rsqrt(ms + NORM_EPS) * g_ref[...]
        r = row0 + i // tiles_per_row
        hx = y * (1.0 + sc_ref[pl.ds(r, 1), :]) + sh_ref[pl.ds(r, 1), :]
        hx_ref[...] = hx.astype(BF16)

    acc = _mm(hx_ref[...], w_ref[...])
    for s in range(acc.shape[1] // LANES):
        o_ref[s] = acc[:, s * LANES:(s + 1) * LANES]


def _inproj(x2d, g, sh, sc, w_p, tm, tiles_per_row, row0):
    t, d = x2d.shape
    n = w_p.shape[1]
    tn = 512
    return pl.pallas_call(
        functools.partial(_inproj_kernel, tiles_per_row, row0),
        grid=(t // tm, n // tn),
        in_specs=[pl.BlockSpec((tm, d), lambda i, j: (i, 0)),
                  pl.BlockSpec((1, d), lambda i, j: (0, 0)),
                  pl.BlockSpec((8, d), lambda i, j: (0, 0)),
                  pl.BlockSpec((8, d), lambda i, j: (0, 0)),
                  pl.BlockSpec((d, tn), lambda i, j: (0, j))],
        out_specs=pl.BlockSpec((tn // LANES, tm, LANES), lambda i, j: (j, i, 0)),
        out_shape=jax.ShapeDtypeStruct((n // LANES, t, LANES), F32),
        scratch_shapes=[pltpu.VMEM((tm, d), BF16)],
        compiler_params=_params(2),
    )(x2d, g, sh, sc, w_p)


def _mlstm_kernel(n_ctx, n_lat,
                  qx_ref, kx_ref, vx_ref, gx_ref, qc_ref, kc_ref, vc_ref, gc_ref,
                  cwq_ref, cwk_ref, cbq_ref, cbk_ref, gb_ref,
                  o_ref,
                  qs_ref, ks_ref, kst_ref, ct_ref, hf_ref, hb_ref):
    c = MLSTM_CHUNK
    head = pl.program_id(1)
    rid = lax.broadcasted_iota(jnp.int32, (c, LANES), 0)
    lane = lax.broadcasted_iota(jnp.int32, (c, LANES), 1)
    r2 = lax.broadcasted_iota(jnp.int32, (c, c), 0)
    c2 = lax.broadcasted_iota(jnp.int32, (c, c), 1)
    tri = ((c2 <= r2).astype(F32), (c2 >= r2).astype(F32))
    keep = (c2 <= r2, c2 >= r2)
    gbias = gb_ref[...]

    def conv_pass(q_ref, k_ref, nchunk, base):
        nrows = nchunk * c

        def body(ci, carry):
            r0 = pl.multiple_of(ci * c, c)
            p0 = pl.multiple_of(jnp.maximum(r0 - 8, 0), 8)
            n0 = pl.multiple_of(jnp.minimum(r0 + c, nrows - 8), 8)
            for src, w_ref, b_ref, scale, is_k in ((q_ref, cwq_ref, cbq_ref, M_DQK ** -0.5, False),
                                                   (k_ref, cwk_ref, cbk_ref, 1.0, True)):
                cur = src[0, pl.ds(r0, c), :]
                prev_row = jnp.where(ci > 0, src[0, pl.ds(p0, 8), :][7:8, :], 0.0)
                next_row = jnp.where(ci < nchunk - 1, src[0, pl.ds(n0, 8), :][0:1, :], 0.0)
                up = jnp.where(rid == 0, prev_row, pltpu.roll(cur, 1, 0))
                dn = jnp.where(rid == c - 1, next_row, pltpu.roll(cur, c - 1, 0))
                w = w_ref[...]
                y = (up * w[0:1, :] + cur * w[1:2, :] + dn * w[2:3, :] + b_ref[...]) * scale
                if is_k:
                    ks_ref[base + ci] = y.astype(BF16)
                    kst_ref[base + ci] = y.T.astype(BF16)
                else:
                    qs_ref[base + ci] = y.astype(BF16)
            return carry

        lax.fori_loop(0, nchunk, body, 0)

    def step(d, ci, base, v_ref, g_ref, n, m, want_h):
        r0 = pl.multiple_of(ci * c, c)
        q = qs_ref[base + ci]
        k = ks_ref[base + ci]
        kt = kst_ref[base + ci]
        v = jnp.concatenate([v_ref[0, pl.ds(r0, c), :], v_ref[1, pl.ds(r0, c), :]], axis=1).astype(BF16)
        gb = g_ref[0, pl.ds(r0, c), :] + gbias
        icol = GATE_LANE0 + d * M_HEADS + head
        fcol = GATE_LANE0 + 2 * M_HEADS + d * M_HEADS + head
        bc = _mm(tri[d], _log_sigmoid(gb), HIGHEST)
        b_col = jnp.sum(jnp.where(lane == fcol, bc, 0.0), axis=1, keepdims=True)
        i_col = jnp.sum(jnp.where(lane == icol, gb, 0.0), axis=1, keepdims=True)
        b_row = jnp.sum(jnp.where(rid == fcol, bc.T, 0.0), axis=0, keepdims=True)
        i_row = jnp.sum(jnp.where(rid == icol, gb.T, 0.0), axis=0, keepdims=True)
        b_last = b_col[0:1, :] if d else b_col[c - 1:c, :]
        ct = ct_ref[d]
        h = None
        if want_h:
            dmat = jnp.where(keep[d], b_col - b_row + i_row, -jnp.inf)
            m_inter = b_col + m
            m_j = jnp.maximum(m_inter, jnp.max(dmat, axis=-1, keepdims=True))
            s = _mm(q, kt) * jnp.exp(dmat - m_j)
            inter = jnp.exp(m_inter - m_j)
            num = _mm(s.astype(BF16), v) + inter * _mm(q, ct.astype(BF16))
            qn = jnp.sum(q.astype(F32) * n, axis=-1, keepdims=True)
            den = jnp.sum(s, axis=-1, keepdims=True) + inter * qn
            h = num / jnp.maximum(jnp.abs(den), jnp.exp(-m_j))
        glog = b_last - b_col + i_col
        m_new = jnp.maximum(b_last + m, jnp.max(glog, axis=0, keepdims=True))
        wk = jnp.exp(glog - m_new)
        decay = jnp.exp(b_last + m - m_new)
        ct_ref[d] = decay * ct + _mm(kt, (wk * v.astype(F32)).astype(BF16))
        n_new = decay * n + jnp.sum(wk * k.astype(F32), axis=0, keepdims=True)
        return n_new, m_new, h

    def run(nchunk, base, v_ref, g_ref, carry, want_h):
        def body(i, carry):
            nf, mf, nb, mb = carry
            ib = nchunk - 1 - i
            nf, mf, h_f = step(0, i, base, v_ref, g_ref, nf, mf, want_h)
            nb, mb, h_b = step(1, ib, base, v_ref, g_ref, nb, mb, want_h)
            if want_h:
                hf_ref[pl.ds(pl.multiple_of(i * c, c), c), :] = h_f
                hb_ref[pl.ds(pl.multiple_of(ib * c, c), c), :] = h_b
            return nf, mf, nb, mb

        return lax.fori_loop(0, nchunk, body, carry)

    conv_pass(qc_ref, kc_ref, n_ctx, 0)
    conv_pass(qx_ref, kx_ref, n_lat, n_ctx)
    ct_ref[...] = jnp.zeros(ct_ref.shape, F32)
    zn = jnp.zeros((1, M_DQK), F32)
    zm = jnp.zeros((1, 1), F32)
    carry = run(n_ctx, 0, vc_ref, gc_ref, (zn, zm, zn, zm), False)
    run(n_lat, n_ctx, vx_ref, gx_ref, carry, True)

    def fin(i, carry):
        rows = pl.ds(pl.multiple_of(i * c, c), c)
        o_ref[rows, :] = (hf_ref[rows, :] + hb_ref[rows, :]).astype(o_ref.dtype)
        return carry

    lax.fori_loop(0, n_lat, fin, 0)


def _mlstm(px, pc, batch, cw, cb, gate_row):
    lx = px.shape[1] // batch
    lc = pc.shape[1] // batch
    n_lat, n_ctx = lx // MLSTM_CHUNK, lc // MLSTM_CHUNK

    def slab(nrows, first, width=1):
        return pl.BlockSpec((width, nrows, LANES), lambda b, h: (first // width + h, b, 0))

    def fixed(nrows, idx):
        return pl.BlockSpec((1, nrows, LANES), lambda b, h: (idx, b, 0))

    in_specs = [slab(lx, SLAB_Q), slab(lx, SLAB_K), slab(lx, SLAB_V, 2), fixed(lx, SLAB_LG1),
                slab(lc, SLAB_Q), slab(lc, SLAB_K), slab(lc, SLAB_V, 2), fixed(lc, SLAB_LG1),
                pl.BlockSpec((3, LANES), lambda b, h: (0, h)),
                pl.BlockSpec((3, LANES), lambda b, h: (0, M_HEADS + h)),
                pl.BlockSpec((1, LANES), lambda b, h: (0, h)),
                pl.BlockSpec((1, LANES), lambda b, h: (0, M_HEADS + h)),
                pl.BlockSpec((1, LANES), lambda b, h: (0, 0))]
    nch = n_ctx + n_lat
    return pl.pallas_call(
        functools.partial(_mlstm_kernel, n_ctx, n_lat),
        grid=(batch, M_HEADS),
        in_specs=in_specs,
        out_specs=pl.BlockSpec((lx, M_DV), lambda b, h: (b, h)),
        out_shape=jax.ShapeDtypeStruct((batch * lx, M_HEADS * M_DV), BF16),
        scratch_shapes=[pltpu.VMEM((nch, MLSTM_CHUNK, LANES), BF16),
                        pltpu.VMEM((nch, MLSTM_CHUNK, LANES), BF16),
                        pltpu.VMEM((nch, LANES, MLSTM_CHUNK), BF16),
                        pltpu.VMEM((2, M_DQK, M_DV), F32),
                        pltpu.VMEM((lx, M_DV), F32),
                        pltpu.VMEM((lx, M_DV), F32)],
        compiler_params=_params(2),
    )(px, px, px, px, pc, pc, pc, pc, cw, cw, cb, cb, gate_row)


def _rwkv_kernel(n_ctx, n_lat,
                 rx_ref, kx_ref, vx_ref, lwx_ref, lax_ref, lg0_ref, lg1_ref,
                 rc_ref, kc_ref, vc_ref, lwc_ref, lac_ref,
                 ww_ref, wa_ref, wg_ref, w0_ref, a0_ref, kk_ref, ka_ref, rk_ref, gnw_ref, gnb_ref,
                 o_ref,
                 yf_ref, yb_ref):
    c = RWKV_CHUNK
    c2 = 2 * c
    lane = lax.broadcasted_iota(jnp.int32, (c, LANES), 1)
    head0 = lane < R_N
    rr = lax.broadcasted_iota(jnp.int32, (c2, c2), 0)
    cc = lax.broadcasted_iota(jnp.int32, (c2, c2), 1)
    same_head = (rr < c) == (cc < c)
    group_ones = same_head.astype(F32)
    tt = rr & (c - 1)
    ss = cc & (c - 1)
    strict = (ss < tt, ss > tt)
    incl = (ss <= tt, ss >= tt)
    eye = (rr == cc).astype(F32)
    tr = lax.broadcasted_iota(jnp.int32, (c, c), 0)
    tc = lax.broadcasted_iota(jnp.int32, (c, c), 1)
    tri = ((tc <= tr).astype(F32), (tc >= tr).astype(F32))
    ww = ww_ref[0]
    wa = wa_ref[0]
    k_k = kk_ref[...]
    k_a = ka_ref[...]

    def stack(x):
        return jnp.concatenate([jnp.where(head0, x, 0.0), jnp.where(head0, 0.0, x)], axis=0)

    def gate_a(la, d):
        return jax.nn.sigmoid(a0_ref[d:d + 1, :] + _mm(la.astype(BF16), wa)[:, d * LANES:(d + 1) * LANES])

    def step(d, ci, refs, z):
        r_ref, k_ref, v_ref, lw_ref, la_ref = refs
        rows = pl.ds(pl.multiple_of(ci * c, c), c)
        r = r_ref[0, rows, :]
        k = k_ref[0, rows, :]
        v = v_ref[0, rows, :]
        lo_w = _mm(jnp.tanh(lw_ref[0, rows, :]).astype(BF16), ww)[:, d * LANES:(d + 1) * LANES]
        wlog = -_softplus(-(w0_ref[d:d + 1, :] + lo_w)) - 0.5
        logw = -jnp.exp(wlog)
        a = gate_a(la_ref[0, rows, :], d)
        kk = k * k_k
        kk = kk * lax.rsqrt(_mm(kk * kk, group_ones, HIGHEST) + 1e-12)
        kd = k * (1.0 + (a - 1.0) * k_a)
        pin = _mm(tri[d], logw, HIGHEST)
        pex = pin - logw
        ptot = pin[0:1, :] if d else pin[c - 1:c, :]
        e_in = jnp.exp(pin)
        e_inv = jnp.exp(-pin)
        e_end = jnp.exp(ptot - pin)
        kka = kk * a
        a_t = stack(-kk * jnp.exp(pex)).astype(BF16)
        r_t = stack(r * e_in).astype(BF16)
        b_t = stack(kka * e_inv).astype(BF16)
        k_t = stack(kd * e_inv).astype(BF16)
        v_s = stack(v).astype(BF16)
        aab = jnp.where(strict[d], _mm_nt(a_t, b_t), 0.0)
        aak = jnp.where(strict[d], _mm_nt(a_t, k_t), 0.0)
        arb = jnp.where(incl[d], _mm_nt(r_t, b_t), 0.0)
        ark = jnp.where(incl[d], _mm_nt(r_t, k_t), 0.0)
        x = eye + aab
        p = aab
        for _ in range(c.bit_length() - 2):
            pb = p.astype(BF16)
            p = _mm(pb, pb)
            x = x + _mm(x.astype(BF16), p.astype(BF16))
        zb = z.astype(BF16)
        rhs = _mm_nt(a_t, zb) + _mm(aak.astype(BF16), v_s)
        u = _mm(x.astype(BF16), rhs.astype(BF16))
        ub = u.astype(BF16)
        y_s = _mm_nt(r_t, zb) + _mm(arb.astype(BF16), ub) + _mm(ark.astype(BF16), v_s)
        y = y_s[0:c, :] + y_s[c:c2, :]
        left = jnp.concatenate([u, stack(v)], axis=0).T.astype(BF16)
        right = jnp.concatenate([stack(kka * e_end), stack(kd * e_end)], axis=0).astype(BF16)
        z_new = z * jnp.exp(ptot) + _mm(left, right)
        return z_new, y

    def run(nchunk, refs, carry, want_y):
        def body(i, carry):
            zf, zb = carry
            ib = nchunk - 1 - i
            zf, y_f = step(0, i, refs, zf)
            zb, y_b = step(1, ib, refs, zb)
            if want_y:
                yf_ref[pl.ds(pl.multiple_of(i * c, c), c), :] = y_f
                yb_ref[pl.ds(pl.multiple_of(ib * c, c), c), :] = y_b
            return zf, zb

        return lax.fori_loop(0, nchunk, body, carry)

    z0 = jnp.zeros((c2, c2), F32)
    carry = run(n_ctx, (rc_ref, kc_ref, vc_ref, lwc_ref, lac_ref), (z0, z0), False)
    run(n_lat, (rx_ref, kx_ref, vx_ref, lwx_ref, lax_ref), carry, True)

    wg = wg_ref[...]
    inv_n = 1.0 / R_N

    def fin(i, carry):
        rows = pl.ds(pl.multiple_of(i * c, c), c)
        r = rx_ref[0, rows, :]
        k = kx_ref[0, rows, :]
        v = vx_ref[0, rows, :]
        la = lax_ref[0, rows, :]
        ksum = k * (2.0 + (gate_a(la, 0) + gate_a(la, 1) - 2.0) * k_a)
        bonus = _mm(r * ksum * rk_ref[...], group_ones, HIGHEST) * v
        lg = jnp.concatenate([lg0_ref[0, rows, :], lg1_ref[0, rows, :]], axis=1)
        g = _mm(jax.nn.sigmoid(lg).astype(BF16), wg)
        y = yf_ref[rows, :] + yb_ref[rows, :]
        mu = _mm(y, group_ones, HIGHEST) * inv_n
        yc = y - mu
        var = _mm(yc * yc, group_ones, HIGHEST) * inv_n
        yn = yc * lax.rsqrt(var + GN_EPS) * gnw_ref[...] + gnb_ref[...]
        o_ref[rows, :] = ((yn + bonus) * g).astype(o_ref.dtype)
        return carry

    lax.fori_loop(0, n_lat, fin, 0)


def _rwkv(px, pc, batch, ww, wa, wg, w0, a0, k_k, k_a, r_k, gn_w, gn_b):
    lx = px.shape[1] // batch
    lc = pc.shape[1] // batch
    n_lat, n_ctx = lx // RWKV_CHUNK, lc // RWKV_CHUNK
    n_pairs = ww.shape[0]

    def slab(nrows, first):
        return pl.BlockSpec((1, nrows, LANES), lambda b, p: (first + p, b, 0))

    def fixed(nrows, idx):
        return pl.BlockSpec((1, nrows, LANES), lambda b, p: (idx, b, 0))

    def vec(nrows):
        return pl.BlockSpec((nrows, LANES), lambda b, p: (0, p))

    in_specs = [slab(lx, SLAB_R), slab(lx, SLAB_KR), slab(lx, SLAB_VR),
                fixed(lx, SLAB_LW), fixed(lx, SLAB_LA), fixed(lx, SLAB_LG0), fixed(lx, SLAB_LG1),
                slab(lc, SLAB_R), slab(lc, SLAB_KR), slab(lc, SLAB_VR),
                fixed(lc, SLAB_LW), fixed(lc, SLAB_LA),
                pl.BlockSpec((1, LANES, 2 * LANES), lambda b, p: (p, 0, 0)),
                pl.BlockSpec((1, LANES, 2 * LANES), lambda b, p: (p, 0, 0)),
                pl.BlockSpec((2 * LANES, LANES), lambda b, p: (0, p)),
                vec(2), vec(2), vec(1), vec(1), vec(1), vec(1), vec(1)]
    return pl.pallas_call(
        functools.partial(_rwkv_kernel, n_ctx, n_lat),
        grid=(batch, n_pairs),
        in_specs=in_specs,
        out_specs=pl.BlockSpec((lx, LANES), lambda b, p: (b, p)),
        out_shape=jax.ShapeDtypeStruct((batch * lx, n_pairs * LANES), BF16),
        scratch_shapes=[pltpu.VMEM((lx, LANES), F32), pltpu.VMEM((lx, LANES), F32)],
        compiler_params=_params(2),
    )(px, px, px, px, px, px, px, pc, pc, pc, pc, pc,
      ww, wa, wg, w0, a0, k_k, k_a, r_k, gn_w, gn_b)


def _out_kernel(tiles_per_batch, mixm_ref, om_ref, mixr_ref, w_ref, x_ref, ng_ref,
                gt1_ref, g2_ref, sh2_ref, sc2_ref, x1_ref, hx2_ref):
    b = pl.program_id(0) // tiles_per_batch
    hm = mixm_ref[...].astype(F32)
    ng = ng_ref[...]
    parts = []
    for h in range(M_HEADS):
        cols = slice(h * M_DV, (h + 1) * M_DV)
        seg = hm[:, cols]
        seg = seg * lax.rsqrt(jnp.mean(seg * seg, axis=-1, keepdims=True) + NORM_EPS)
        og = jnp.concatenate([om_ref[2 * h], om_ref[2 * h + 1]], axis=1)
        parts.append((seg * ng[:, cols] * jax.nn.sigmoid(og)).astype(BF16))
    lhs = jnp.concatenate(parts + [mixr_ref[...]], axis=1)
    x1 = x_ref[...] + gt1_ref[pl.ds(b, 1), :] * _mm(lhs, w_ref[...])
    x1_ref[...] = x1
    y = x1 * lax.rsqrt(jnp.mean(x1 * x1, axis=-1, keepdims=True) + NORM_EPS) * g2_ref[...]
    hx2_ref[...] = (y * (1.0 + sc2_ref[pl.ds(b, 1), :]) + sh2_ref[pl.ds(b, 1), :]).astype(BF16)


def _out_proj(mixm, px, mixr, w_out, x2d, ng, gt1, g2, sh2, sc2, tm, tiles_per_batch):
    t, d = x2d.shape
    dm = mixm.shape[1]
    row = lambda i: (i, 0)
    const = lambda i: (0, 0)
    return pl.pallas_call(
        functools.partial(_out_kernel, tiles_per_batch),
        grid=(t // tm,),
        in_specs=[pl.BlockSpec((tm, dm), row),
                  pl.BlockSpec((8, tm, LANES), lambda i: (SLAB_O // 8, i, 0)),
                  pl.BlockSpec((tm, dm), row),
                  pl.BlockSpec((d, d), const),
                  pl.BlockSpec((tm, d), row),
                  pl.BlockSpec((1, dm), const),
                  pl.BlockSpec((8, d), const),
                  pl.BlockSpec((1, d), const),
                  pl.BlockSpec((8, d), const),
                  pl.BlockSpec((8, d), const)],
        out_specs=[pl.BlockSpec((tm, d), row), pl.BlockSpec((tm, d), row)],
        out_shape=[jax.ShapeDtypeStruct((t, d), F32), jax.ShapeDtypeStruct((t, d), BF16)],
        compiler_params=_params(1),
    )(mixm, px, mixr, w_out, x2d, ng, gt1, g2, sh2, sc2)


def _ffn_kernel(tiles_per_img, hx_ref, top_ref, bot_ref, wu_ref, wg_ref, wd_ref, cw_ref, cb_ref,
                x1_ref, gt2_ref, gf_ref, o_ref, acc_ref):
    i = pl.program_id(0)
    j = pl.program_id(1)
    tm = hx_ref.shape[0]
    ti = i % tiles_per_img

    @pl.when(j == 0)
    def _():
        acc_ref[...] = jnp.zeros(acc_ref.shape, F32)

    hx = hx_ref[...]
    top = jnp.where(ti > 0, top_ref[...], jnp.zeros_like(top_ref[...]))
    bot = jnp.where(ti < tiles_per_img - 1, bot_ref[...], jnp.zeros_like(bot_ref[...]))
    u = _mm(jnp.concatenate([top, hx, bot], axis=0), wu_ref[...])
    nr = u.shape[0]
    col = lax.broadcasted_iota(jnp.int32, u.shape, 0) & (GRID_W - 1)
    ul = jnp.where(col == 0, 0.0, pltpu.roll(u, 1, 0))
    ur = jnp.where(col == GRID_W - 1, 0.0, pltpu.roll(u, nr - 1, 0))
    cw = cw_ref[...]
    conv = cb_ref[...]
    for dy in range(3):
        rows = slice(dy * GRID_W, dy * GRID_W + tm)
        conv = (conv + ul[rows] * cw[3 * dy:3 * dy + 1, :] + u[rows] * cw[3 * dy + 1:3 * dy + 2, :]
                + ur[rows] * cw[3 * dy + 2:3 * dy + 3, :])
    gelu = 0.5 * conv * (1.0 + jnp.tanh(0.7978845608028654 * (conv + 0.044715 * conv * conv * conv)))
    act = (gelu * _mm(hx, wg_ref[...])).astype(BF16)
    acc_ref[...] += _mm(act, wd_ref[...])

    @pl.when(j == pl.num_programs(1) - 1)
    def _():
        b = i // tiles_per_img
        x2 = x1_ref[...] + gt2_ref[pl.ds(b, 1), :] * acc_ref[...]
        o_ref[...] = x2 * lax.rsqrt(jnp.mean(x2 * x2, axis=-1, keepdims=True) + NORM_EPS) * gf_ref[...]


def _conv_ffn(hx2, w_up, w_gate, w_down, cw, cb, x1, gt2, g_final, tm, tiles_per_img):
    t, d = hx2.shape
    f = w_up.shape[1]
    tf = 512
    rows_per_tile = tm // GRID_W
    n_rows = t // GRID_W
    return pl.pallas_call(
        functools.partial(_ffn_kernel, tiles_per_img),
        grid=(t // tm, f // tf),
        in_specs=[pl.BlockSpec((tm, d), lambda i, j: (i, 0)),
                  pl.BlockSpec((GRID_W, d), lambda i, j: (jnp.maximum(i * rows_per_tile - 1, 0), 0)),
                  pl.BlockSpec((GRID_W, d), lambda i, j: (jnp.minimum((i + 1) * rows_per_tile, n_rows - 1), 0)),
                  pl.BlockSpec((d, tf), lambda i, j: (0, j)),
                  pl.BlockSpec((d, tf), lambda i, j: (0, j)),
                  pl.BlockSpec((tf, d), lambda i, j: (j, 0)),
                  pl.BlockSpec((9, tf), lambda i, j: (0, j)),
                  pl.BlockSpec((1, tf), lambda i, j: (0, j)),
                  pl.BlockSpec((tm, d), lambda i, j: (i, 0)),
                  pl.BlockSpec((8, d), lambda i, j: (0, 0)),
                  pl.BlockSpec((1, d), lambda i, j: (0, 0))],
        out_specs=pl.BlockSpec((tm, d), lambda i, j: (i, 0)),
        out_shape=jax.ShapeDtypeStruct((t, d), F32),
        scratch_shapes=[pltpu.VMEM((tm, d), F32)],
        compiler_params=_params(2),
    )(hx2, hx2, hx2, w_up, w_gate, w_down, cw, cb, x1, gt2, g_final)


def _lora_pairs(up):
    _, rank, width = up.shape
    u = up.reshape(2, rank, width // LANES, LANES).transpose(2, 0, 1, 3)
    z = jnp.zeros_like(u[:, 0])
    top = jnp.concatenate([u[:, 0], z], axis=-1)
    bot = jnp.concatenate([z, u[:, 1]], axis=-1)
    return jnp.concatenate([top, bot], axis=1).astype(BF16)


def kernel(x, c, ctx, c_ctx, w_mod, b_mod, g_norm1, g_norm2, w_in, m_conv_w, m_conv_b, m_gate_b, m_norm_g, r_w0, r_w_up, r_a0, r_a_up, r_g_up, r_k_k, r_k_a, r_r_k, r_gn_w, r_gn_b, w_out, f_w_up, f_w_gate, f_conv_w, f_conv_b, f_w_down, g_final):
    batch, seq, d = x.shape
    ctx_len = ctx.shape[1]
    assert w_mod.shape[0] == 1, "single-layer block"
    assert batch + 1 <= 8 and seq % 512 == 0 and ctx_len % MLSTM_CHUNK == 0

    cv8 = jnp.zeros((8, d), F32).at[:batch].set(c).at[batch].set(c_ctx)
    mod = _modulation(cv8, w_mod[0], b_mod[0])
    sh1, sc1, gt1, sh2, sc2, gt2 = (mod[:, k * d:(k + 1) * d] for k in range(6))

    w = w_in[0]
    n_gate = 4 * M_HEADS
    g0 = 3 * 1024
    w_p = jnp.concatenate([w[:, :g0], w[:, g0 + n_gate:], w[:, g0:g0 + n_gate],
                           jnp.zeros((d, N_SLABS * LANES - w.shape[1]), F32)], axis=1).astype(BF16)
    g1 = g_norm1[0].reshape(1, d)
    tm_x = 1024 if seq % 1024 == 0 else 512
    px = _inproj(x.reshape(batch * seq, d), g1, sh1, sc1, w_p, tm_x, seq // tm_x, 0)
    tm_c = ctx_len
    pc = _inproj(ctx.reshape(batch * ctx_len, d), g1, sh1, sc1, w_p, tm_c, 1 << 30, batch)

    gate_row = jnp.zeros((1, LANES), F32).at[0, GATE_LANE0:GATE_LANE0 + n_gate].set(m_gate_b[0].reshape(-1))
    mixm = _mlstm(px, pc, batch, m_conv_w[0], m_conv_b[0].reshape(1, -1), gate_row)

    rw = r_k_k.shape[1]
    wg = jnp.zeros((2 * LANES, rw), F32).at[:r_g_up.shape[1]].set(r_g_up[0]).astype(BF16)
    mixr = _rwkv(px, pc, batch, _lora_pairs(r_w_up[0]), _lora_pairs(r_a_up[0]), wg,
                 r_w0[0], r_a0[0], r_k_k, r_k_a, r_r_k[0].reshape(1, rw), r_gn_w, r_gn_b)

    tm_o = 512
    x1, hx2 = _out_proj(mixm, px, mixr, w_out[0].astype(BF16), x.reshape(batch * seq, d), m_norm_g,
                        gt1, g_norm2[0].reshape(1, d), sh2, sc2, tm_o, seq // tm_o)

    tm_f = 512
    out = _conv_ffn(hx2, f_w_up[0].astype(BF16), f_w_gate[0].astype(BF16), f_w_down[0].astype(BF16),
                    f_conv_w[0].reshape(9, -1), f_conv_b, x1, gt2, g_final.reshape(1, d),
                    tm_f, seq // tm_f)
    return out.reshape(batch, seq, d)
```

```python
import functools

import jax
import jax.numpy as jnp
from jax import lax
from jax.experimental import pallas as pl
from jax.experimental.pallas import tpu as pltpu

F32 = jnp.float32
BF16 = jnp.bfloat16
HIGHEST = lax.Precision.HIGHEST

LANES = 128
GRID_W = 64
M_HEADS = 4
M_DQK = 128
M_DV = 256
R_N = 64
NORM_EPS = 1e-6
GN_EPS = 64e-5
MLSTM_CHUNK = 128
RWKV_CHUNK = 64
RWKV_GROUP = 8
VMEM_LIMIT = 56 * 1024 * 1024

SLAB_Q, SLAB_K, SLAB_V, SLAB_O = 0, 4, 8, 16
SLAB_R, SLAB_KR, SLAB_VR = 24, 32, 40
SLAB_LW, SLAB_LA, SLAB_LG0, SLAB_LG1 = 48, 49, 50, 51
N_SLABS = 52
GATE_LANE0 = 32


def _mm(a, b, precision=None):
    return jnp.dot(a, b, preferred_element_type=F32, precision=precision)


def _mm_nt(a, b):
    return lax.dot_general(a, b, (((1,), (1,)), ((), ())), preferred_element_type=F32)


def _mm_tn(a, b):
    return lax.dot_general(a, b, (((0,), (0,)), ((), ())), preferred_element_type=F32)


def _softplus(x):
    return jnp.maximum(x, 0.0) + jnp.log1p(jnp.exp(-jnp.abs(x)))


def _log_sigmoid(x):
    return -_softplus(-x)


def _params(n_axes):
    return pltpu.CompilerParams(dimension_semantics=("arbitrary",) * n_axes,
                                vmem_limit_bytes=VMEM_LIMIT)


def _mod_kernel(cv_ref, w_ref, b_ref, o_ref):
    cv = cv_ref[...]
    s = (cv * jax.nn.sigmoid(cv)).astype(BF16)
    o_ref[...] = _mm(s, w_ref[...].astype(BF16)) + b_ref[...]


def _modulation(cv8, w_mod, b_mod):
    d, n = w_mod.shape
    tn = 1024
    return pl.pallas_call(
        _mod_kernel,
        grid=(n // tn,),
        in_specs=[pl.BlockSpec((8, d), lambda j: (0, 0)),
                  pl.BlockSpec((d, tn), lambda j: (0, j)),
                  pl.BlockSpec((1, tn), lambda j: (0, j))],
        out_specs=pl.BlockSpec((8, tn), lambda j: (0, j)),
        out_shape=jax.ShapeDtypeStruct((8, n), F32),
        compiler_params=_params(1),
    )(cv8, w_mod, b_mod.reshape(1, n))


def _inproj_kernel(tiles_per_row, row0, x_ref, g_ref, sh_ref, sc_ref, w_ref, o_ref, hx_ref):
    i = pl.program_id(0)
    j = pl.program_id(1)

    @pl.when(j == 0)
    def _():
        x = x_ref[...]
        ms = jnp.mean(x * x, axis=-1, keepdims=True)
        y = x * lax.rsqrt(ms + NORM_EPS) * g_ref[...]
        r = row0 + i // tiles_per_row
        hx = y * (1.0 + sc_ref[pl.ds(r, 1), :]) + sh_ref[pl.ds(r, 1), :]
        hx_ref[...] = hx.astype(BF16)

    acc = _mm(hx_ref[...], w_ref[...])
    for s in range(acc.shape[1] // LANES):
        o_ref[s] = acc[:, s * LANES:(s + 1) * LANES]


def _inproj(x2d, g, sh, sc, w_p, tm, tiles_per_row, row0):
    t, d = x2d.shape
    n = w_p.shape[1]
    tn = 512
    return pl.pallas_call(
        functools.partial(_inproj_kernel, tiles_per_row, row0),
        grid=(t // tm, n // tn),
        in_specs=[pl.BlockSpec((tm, d), lambda i, j: (i, 0)),
                  pl.BlockSpec((1, d), lambda i, j: (0, 0)),
                  pl.BlockSpec((8, d), lambda i, j: (0, 0)),
                  pl.BlockSpec((8, d), lambda i, j: (0, 0)),
                  pl.BlockSpec((d, tn), lambda i, j: (0, j))],
        out_specs=pl.BlockSpec((tn // LANES, tm, LANES), lambda i, j: (j, i, 0)),
        out_shape=jax.ShapeDtypeStruct((n // LANES, t, LANES), F32),
        scratch_shapes=[pltpu.VMEM((tm, d), BF16)],
        compiler_params=_params(2),
    )(x2d, g, sh, sc, w_p)


def _mlstm_kernel(n_ctx, n_lat,
                  qx_ref, kx_ref, vx_ref, gx_ref, qc_ref, kc_ref, vc_ref, gc_ref,
                  cwq_ref, cwk_ref, cbq_ref, cbk_ref, gb_ref,
                  o_ref,
                  qs_ref, ks_ref, kst_ref, ct_ref, hf_ref, hb_ref):
    c = MLSTM_CHUNK
    head = pl.program_id(1)
    rid = lax.broadcasted_iota(jnp.int32, (c, LANES), 0)
    lane = lax.broadcasted_iota(jnp.int32, (c, LANES), 1)
    r2 = lax.broadcasted_iota(jnp.int32, (c, c), 0)
    c2 = lax.broadcasted_iota(jnp.int32, (c, c), 1)
    tri = ((c2 <= r2).astype(F32), (c2 >= r2).astype(F32))
    keep = (c2 <= r2, c2 >= r2)
    gbias = gb_ref[...]

    def conv_pass(q_ref, k_ref, nchunk, base):
        nrows = nchunk * c

        def body(ci, carry):
            r0 = pl.multiple_of(ci * c, c)
            p0 = pl.multiple_of(jnp.maximum(r0 - 8, 0), 8)
            n0 = pl.multiple_of(jnp.minimum(r0 + c, nrows - 8), 8)
            for src, w_ref, b_ref, scale, is_k in ((q_ref, cwq_ref, cbq_ref, M_DQK ** -0.5, False),
                                                   (k_ref, cwk_ref, cbk_ref, 1.0, True)):
                cur = src[0, pl.ds(r0, c), :]
                prev_row = jnp.where(ci > 0, src[0, pl.ds(p0, 8), :][7:8, :], 0.0)
                next_row = jnp.where(ci < nchunk - 1, src[0, pl.ds(n0, 8), :][0:1, :], 0.0)
                up = jnp.where(rid == 0, prev_row, pltpu.roll(cur, 1, 0))
                dn = jnp.where(rid == c - 1, next_row, pltpu.roll(cur, c - 1, 0))
                w = w_ref[...]
                y = (up * w[0:1, :] + cur * w[1:2, :] + dn * w[2:3, :] + b_ref[...]) * scale
                if is_k:
                    ks_ref[base + ci] = y.astype(BF16)
                    kst_ref[base + ci] = y.T.astype(BF16)
                else:
                    qs_ref[base + ci] = y.astype(BF16)
            return carry

        lax.fori_loop(0, nchunk, body, 0)

    def step(d, ci, base, v_ref, g_ref, n, m, want_h):
        r0 = pl.multiple_of(ci * c, c)
        q = qs_ref[base + ci]
        k = ks_ref[base + ci]
        kt = kst_ref[base + ci]
        v = jnp.concatenate([v_ref[0, pl.ds(r0, c), :], v_ref[1, pl.ds(r0, c), :]], axis=1).astype(BF16)
        gb = g_ref[0, pl.ds(r0, c), :] + gbias
        icol = GATE_LANE0 + d * M_HEADS + head
        fcol = GATE_LANE0 + 2 * M_HEADS + d * M_HEADS + head
        bc = _mm(tri[d], _log_sigmoid(gb), HIGHEST)
        b_col = jnp.sum(jnp.where(lane == fcol, bc, 0.0), axis=1, keepdims=True)
        i_col = jnp.sum(jnp.where(lane == icol, gb, 0.0), axis=1, keepdims=True)
        b_row = jnp.sum(jnp.where(rid == fcol, bc.T, 0.0), axis=0, keepdims=True)
        i_row = jnp.sum(jnp.where(rid == icol, gb.T, 0.0), axis=0, keepdims=True)
        b_last = b_col[0:1, :] if d else b_col[c - 1:c, :]
        ct = ct_ref[d]
        h = None
        if want_h:
            dmat = jnp.where(keep[d], b_col - b_row + i_row, -jnp.inf)
            m_inter = b_col + m
            m_j = jnp.maximum(m_inter, jnp.max(dmat, axis=-1, keepdims=True))
            s = _mm(q, kt) * jnp.exp(dmat - m_j)
            inter = jnp.exp(m_inter - m_j)
            num = _mm(s.astype(BF16), v) + inter * _mm(q, ct.astype(BF16))
            qn = jnp.sum(q.astype(F32) * n, axis=-1, keepdims=True)
            den = jnp.sum(s, axis=-1, keepdims=True) + inter * qn
            h = num / jnp.maximum(jnp.abs(den), jnp.exp(-m_j))
        glog = b_last - b_col + i_col
        m_new = jnp.maximum(b_last + m, jnp.max(glog, axis=0, keepdims=True))
        wk = jnp.exp(glog - m_new)
        decay = jnp.exp(b_last + m - m_new)
        ct_ref[d] = decay * ct + _mm(kt, (wk * v.astype(F32)).astype(BF16))
        n_new = decay * n + jnp.sum(wk * k.astype(F32), axis=0, keepdims=True)
        return n_new, m_new, h

    def run(nchunk, base, v_ref, g_ref, carry, want_h):
        def body(i, carry):
            nf, mf, nb, mb = carry
            ib = nchunk - 1 - i
            nf, mf, h_f = step(0, i, base, v_ref, g_ref, nf, mf, want_h)
            nb, mb, h_b = step(1, ib, base, v_ref, g_ref, nb, mb, want_h)
            if want_h:
                hf_ref[pl.ds(pl.multiple_of(i * c, c), c), :] = h_f
                hb_ref[pl.ds(pl.multiple_of(ib * c, c), c), :] = h_b
            return nf, mf, nb, mb

        return lax.fori_loop(0, nchunk, body, carry)

    conv_pass(qc_ref, kc_ref, n_ctx, 0)
    conv_pass(qx_ref, kx_ref, n_lat, n_ctx)
    ct_ref[...] = jnp.zeros(ct_ref.shape, F32)
    zn = jnp.zeros((1, M_DQK), F32)
    zm = jnp.zeros((1, 1), F32)
    carry = run(n_ctx, 0, vc_ref, gc_ref, (zn, zm, zn, zm), False)
    run(n_lat, n_ctx, vx_ref, gx_ref, carry, True)

    def fin(i, carry):
        rows = pl.ds(pl.multiple_of(i * c, c), c)
        o_ref[rows, :] = (hf_ref[rows, :] + hb_ref[rows, :]).astype(o_ref.dtype)
        return carry

    lax.fori_loop(0, n_lat, fin, 0)


def _mlstm(px, pc, batch, cw, cb, gate_row):
    lx = px.shape[1] // batch
    lc = pc.shape[1] // batch
    n_lat, n_ctx = lx // MLSTM_CHUNK, lc // MLSTM_CHUNK

    def slab(nrows, first, width=1):
        return pl.BlockSpec((width, nrows, LANES), lambda b, h: (first // width + h, b, 0))

    def fixed(nrows, idx):
        return pl.BlockSpec((1, nrows, LANES), lambda b, h: (idx, b, 0))

    in_specs = [slab(lx, SLAB_Q), slab(lx, SLAB_K), slab(lx, SLAB_V, 2), fixed(lx, SLAB_LG1),
                slab(lc, SLAB_Q), slab(lc, SLAB_K), slab(lc, SLAB_V, 2), fixed(lc, SLAB_LG1),
                pl.BlockSpec((3, LANES), lambda b, h: (0, h)),
                pl.BlockSpec((3, LANES), lambda b, h: (0, M_HEADS + h)),
                pl.BlockSpec((1, LANES), lambda b, h: (0, h)),
                pl.BlockSpec((1, LANES), lambda b, h: (0, M_HEADS + h)),
                pl.BlockSpec((1, LANES), lambda b, h: (0, 0))]
    nch = n_ctx + n_lat
    return pl.pallas_call(
        functools.partial(_mlstm_kernel, n_ctx, n_lat),
        grid=(batch, M_HEADS),
        in_specs=in_specs,
        out_specs=pl.BlockSpec((lx, M_DV), lambda b, h: (b, h)),
        out_shape=jax.ShapeDtypeStruct((batch * lx, M_HEADS * M_DV), BF16),
        scratch_shapes=[pltpu.VMEM((nch, MLSTM_CHUNK, LANES), BF16),
                        pltpu.VMEM((nch, MLSTM_CHUNK, LANES), BF16),
                        pltpu.VMEM((nch, LANES, MLSTM_CHUNK), BF16),
                        pltpu.VMEM((2, M_DQK, M_DV), F32),
                        pltpu.VMEM((lx, M_DV), F32),
                        pltpu.VMEM((lx, M_DV), F32)],
        compiler_params=_params(2),
    )(px, px, px, px, pc, pc, pc, pc, cw, cw, cb, cb, gate_row)


def _rwkv_stack(x):
    head0 = lax.broadcasted_iota(jnp.int32, x.shape, 1) < R_N
    zero = jnp.zeros_like(x)
    return jnp.concatenate([jnp.where(head0, x, zero), jnp.where(head0, zero, x)], axis=0)


def _rwkv_fold(x):
    half = x.shape[0] // 2
    return x[0:half] + x[half:2 * half]


def _rwkv_gate_a(la, wa, a0_ref, d):
    return jax.nn.sigmoid(a0_ref[d:d + 1, :] + _mm(la.astype(BF16), wa)[:, d * LANES:(d + 1) * LANES])


def _rwkv_prep_kernel(want_y, r_ref, k_ref, v_ref, lw_ref, la_ref,
                      ww_ref, wa_ref, w0_ref, a0_ref, kk_ref, ka_ref, *out_refs):
    c = RWKV_CHUNK
    c2 = 2 * c
    stack = _rwkv_stack
    rr = lax.broadcasted_iota(jnp.int32, (c2, c2), 0)
    cc = lax.broadcasted_iota(jnp.int32, (c2, c2), 1)
    same_head = (rr < c) == (cc < c)
    group_ones = same_head.astype(F32)
    tt = rr & (c - 1)
    ss = cc & (c - 1)
    strict = (ss < tt, ss > tt)
    incl = (ss <= tt, ss >= tt)
    eye = (rr == cc).astype(F32)
    tr = lax.broadcasted_iota(jnp.int32, (c, c), 0)
    tc = lax.broadcasted_iota(jnp.int32, (c, c), 1)
    tri = ((tc <= tr).astype(F32), (tc >= tr).astype(F32))
    ww = ww_ref[0]
    wa = wa_ref[0]
    k_k = kk_ref[...]
    k_a = ka_ref[...]
    m_ref, n_ref, dec_ref = out_refs[0], out_refs[1], out_refs[2]

    def prep_chunk(d, j):
        rows = pl.ds(pl.multiple_of(j * c, c), c)
        slot = (0, 0, d, j)
        r = r_ref[0, rows, :]
        k = k_ref[0, rows, :]
        v = v_ref[0, rows, :]
        lo_w = _mm(jnp.tanh(lw_ref[0, rows, :]).astype(BF16), ww)[:, d * LANES:(d + 1) * LANES]
        wlog = -_softplus(-(w0_ref[d:d + 1, :] + lo_w)) - 0.5
        logw = -jnp.exp(wlog)
        a = _rwkv_gate_a(la_ref[0, rows, :], wa, a0_ref, d)
        kk = k * k_k
        kk = kk * lax.rsqrt(_mm(kk * kk, group_ones, HIGHEST) + 1e-12)
        kd = k * (1.0 + (a - 1.0) * k_a)
        pin = _mm(tri[d], logw, HIGHEST)
        pex = pin - logw
        ptot = pin[0:1, :] if d else pin[c - 1:c, :]
        e_in = jnp.exp(pin)
        e_inv = jnp.exp(-pin)
        e_end = jnp.exp(ptot - pin)
        kka = kk * a
        r_f = stack(r * e_in)
        ar_t = jnp.concatenate([stack(-kk * jnp.exp(pex)), r_f], axis=0).astype(BF16)
        a_t = ar_t[0:c2]
        bk_t = jnp.concatenate([stack(kka * e_inv), stack(kd * e_inv)], axis=0).astype(BF16)
        v_f = stack(v)
        v_s = v_f.astype(BF16)
        aa = _mm_nt(ar_t, bk_t)
        aab = jnp.where(strict[d], aa[0:c2, 0:c2], 0.0)
        aak = jnp.where(strict[d], aa[0:c2, c2:2 * c2], 0.0)
        arb = jnp.where(incl[d], aa[c2:2 * c2, 0:c2], 0.0)
        ark = jnp.where(incl[d], aa[c2:2 * c2, c2:2 * c2], 0.0)
        x = eye + aab
        pb = aab.astype(BF16)
        p = _mm(pb, pb)
        for _ in range(c.bit_length() - 3):
            pb = p.astype(BF16)
            both = _mm(jnp.concatenate([x.astype(BF16), pb], axis=0), pb)
            x = x + both[0:c2]
            p = both[c2:2 * c2]
        x = x + _mm(x.astype(BF16), p.astype(BF16))
        akv = _mm(aak.astype(BF16), v_s)
        xb = x.astype(BF16)
        wt = _mm(xb, a_t).astype(BF16)
        u0 = _mm(xb, akv.astype(BF16)).astype(BF16)
        bk_end = jnp.concatenate([stack(kka * e_end), stack(kd * e_end)], axis=0).astype(BF16)
        m_mat = _mm_tn(wt, bk_end[0:c2])
        n_mat = _mm_tn(jnp.concatenate([u0, v_s], axis=0), bk_end)
        m_ref[slot] = _rwkv_fold(m_mat).astype(BF16)
        n_ref[slot] = _rwkv_fold(n_mat)
        dec_ref[slot] = jnp.broadcast_to(jnp.exp(ptot), (8, LANES))
        if want_y:
            qy = _mm(arb.astype(BF16), jnp.concatenate([wt, u0], axis=1))
            yl = qy[:, c2:2 * c2] + _mm(ark.astype(BF16), v_s)
            out_refs[3][slot] = _rwkv_fold(r_f + qy[:, 0:c2]).astype(BF16)
            out_refs[4][slot] = _rwkv_fold(yl)

    def body(j, carry):
        prep_chunk(0, j)
        prep_chunk(1, j)
        return carry

    lax.fori_loop(0, r_ref.shape[1] // c, body, 0)


def _rwkv_scan_kernel(mc_ref, nc_ref, dc_ref, mf_ref, nf_ref, df_ref, qf_ref, ylf_ref,
                      mb_ref, nb_ref, db_ref, qb_ref, ylb_ref,
                      r_ref, k_ref, v_ref, la_ref, lg0_ref, lg1_ref,
                      wa_ref, wg_ref, a0_ref, ka_ref, rk_ref, gnw_ref, gnb_ref,
                      o_ref, z_ref, yf_ref, yb_ref):
    c = RWKV_CHUNK
    g = pl.program_id(2)
    n_groups = pl.num_programs(2)
    group = mf_ref.shape[3]
    n_ctx = mc_ref.shape[3]

    def advance(z, m_c, n_c, dec):
        return z * dec[0:1, :] + _mm(z.astype(BF16), _rwkv_stack(m_c)) + _rwkv_stack(n_c)

    @pl.when(g == 0)
    def _():
        z_ref[...] = jnp.zeros(z_ref.shape, F32)

        def ctx_body(i, carry):
            for d, ii in ((0, i), (1, n_ctx - 1 - i)):
                z_ref[d] = advance(z_ref[d], mc_ref[0, 0, d, ii], nc_ref[0, 0, d, ii], dc_ref[0, 0, d, ii])
            return carry

        lax.fori_loop(0, n_ctx, ctx_body, 0)

    def lat_body(j, carry):
        for d, jj, first, m_ref, n_ref, d_ref, q_ref, yl_ref, y_ref in (
                (0, j, g * group, mf_ref, nf_ref, df_ref, qf_ref, ylf_ref, yf_ref),
                (1, group - 1 - j, (n_groups - 1 - g) * group, mb_ref, nb_ref, db_ref, qb_ref, ylb_ref, yb_ref)):
            z = z_ref[d]
            ys = _mm_nt(_rwkv_stack(q_ref[0, 0, 0, jj]), z.astype(BF16))
            y_ref[pl.ds(pl.multiple_of((first + jj) * c, c), c), :] = _rwkv_fold(ys) + yl_ref[0, 0, 0, jj]
            z_ref[d] = advance(z, m_ref[0, 0, 0, jj], n_ref[0, 0, 0, jj], d_ref[0, 0, 0, jj])
        return carry

    lax.fori_loop(0, group, lat_body, 0)

    @pl.when(g == n_groups - 1)
    def _():
        fr = 4 * c
        r2 = lax.broadcasted_iota(jnp.int32, (LANES, LANES), 0)
        c2 = lax.broadcasted_iota(jnp.int32, (LANES, LANES), 1)
        group_ones = ((r2 < R_N) == (c2 < R_N)).astype(F32)
        wa = wa_ref[0]
        wg = wg_ref[...]
        k_a = ka_ref[...]
        inv_n = 1.0 / R_N

        def fin(i, carry):
            rows = pl.ds(pl.multiple_of(i * fr, fr), fr)
            r = r_ref[0, rows, :]
            k = k_ref[0, rows, :]
            v = v_ref[0, rows, :]
            la = la_ref[0, rows, :]
            a_sum = _rwkv_gate_a(la, wa, a0_ref, 0) + _rwkv_gate_a(la, wa, a0_ref, 1)
            ksum = k * (2.0 + (a_sum - 2.0) * k_a)
            bonus = _mm(r * ksum * rk_ref[...], group_ones, HIGHEST) * v
            lg = jnp.concatenate([lg0_ref[0, rows, :], lg1_ref[0, rows, :]], axis=1)
            gate = _mm(jax.nn.sigmoid(lg).astype(BF16), wg)
            y = yf_ref[rows, :] + yb_ref[rows, :]
            mu = _mm(y, group_ones, HIGHEST) * inv_n
            yc = y - mu
            var = _mm(yc * yc, group_ones, HIGHEST) * inv_n
            yn = yc * lax.rsqrt(var + GN_EPS) * gnw_ref[...] + gnb_ref[...]
            o_ref[rows, :] = ((yn + bonus) * gate).astype(o_ref.dtype)
            return carry

        lax.fori_loop(0, o_ref.shape[0] // fr, fin, 0)


def _rwkv_prep(p_all, batch, group, want_y, ww, wa, w0, a0, k_k, k_a):
    c = RWKV_CHUNK
    length = p_all.shape[1] // batch
    n_chunks = length // c
    n_groups = n_chunks // group
    n_pairs = ww.shape[0]
    rows = group * c

    def slab(first):
        return pl.BlockSpec((1, rows, LANES), lambda b, p, g: (first + p, b * n_groups + g, 0))

    def fixed(idx):
        return pl.BlockSpec((1, rows, LANES), lambda b, p, g: (idx, b * n_groups + g, 0))

    def vec(nrows):
        return pl.BlockSpec((nrows, LANES), lambda b, p, g: (0, p))

    lora = pl.BlockSpec((1, LANES, 2 * LANES), lambda b, p, g: (p, 0, 0))
    outs = [(c, BF16), (c, F32), (8, F32)] + ([(c, BF16), (c, F32)] if want_y else [])
    return pl.pallas_call(
        functools.partial(_rwkv_prep_kernel, want_y),
        grid=(batch, n_pairs, n_groups),
        in_specs=[slab(SLAB_R), slab(SLAB_KR), slab(SLAB_VR), fixed(SLAB_LW), fixed(SLAB_LA),
                  lora, lora, vec(2), vec(2), vec(1), vec(1)],
        out_specs=[pl.BlockSpec((1, 1, 2, group, nr, LANES), lambda b, p, g: (b, p, 0, g, 0, 0))
                   for nr, _ in outs],
        out_shape=[jax.ShapeDtypeStruct((batch, n_pairs, 2, n_chunks, nr, LANES), dt) for nr, dt in outs],
        compiler_params=_params(3),
    )(p_all, p_all, p_all, p_all, p_all, ww, wa, w0, a0, k_k, k_a)


def _rwkv_scan(ops_c, ops_x, px, batch, group, wa, wg, a0, k_a, r_k, gn_w, gn_b):
    c = RWKV_CHUNK
    lx = px.shape[1] // batch
    n_pairs, n_ctx = ops_c[0].shape[1], ops_c[0].shape[3]
    n_groups = ops_x[0].shape[3] // group

    def ctx_block(a):
        return pl.BlockSpec((1, 1, 2, n_ctx, a.shape[4], LANES), lambda b, p, g: (b, p, 0, 0, 0, 0))

    def fwd_block(a):
        return pl.BlockSpec((1, 1, 1, group, a.shape[4], LANES), lambda b, p, g: (b, p, 0, g, 0, 0))

    def bwd_block(a):
        return pl.BlockSpec((1, 1, 1, group, a.shape[4], LANES),
                            lambda b, p, g: (b, p, 1, n_groups - 1 - g, 0, 0))

    def slab(first):
        return pl.BlockSpec((1, lx, LANES), lambda b, p, g: (first + p, b, 0))

    def fixed(idx):
        return pl.BlockSpec((1, lx, LANES), lambda b, p, g: (idx, b, 0))

    def vec(nrows):
        return pl.BlockSpec((nrows, LANES), lambda b, p, g: (0, p))

    return pl.pallas_call(
        _rwkv_scan_kernel,
        grid=(batch, n_pairs, n_groups),
        in_specs=[ctx_block(a) for a in ops_c] + [fwd_block(a) for a in ops_x] + [bwd_block(a) for a in ops_x]
                 + [slab(SLAB_R), slab(SLAB_KR), slab(SLAB_VR), fixed(SLAB_LA), fixed(SLAB_LG0), fixed(SLAB_LG1),
                    pl.BlockSpec((1, LANES, 2 * LANES), lambda b, p, g: (p, 0, 0)),
                    pl.BlockSpec((2 * LANES, LANES), lambda b, p, g: (0, p)),
                    vec(2), vec(1), vec(1), vec(1), vec(1)],
        out_specs=pl.BlockSpec((lx, LANES), lambda b, p, g: (b, p)),
        out_shape=jax.ShapeDtypeStruct((batch * lx, n_pairs * LANES), BF16),
        scratch_shapes=[pltpu.VMEM((2, 2 * c, LANES), F32),
                        pltpu.VMEM((lx, LANES), F32), pltpu.VMEM((lx, LANES), F32)],
        compiler_params=_params(3),
    )(*ops_c, *ops_x, *ops_x, px, px, px, px, px, px, wa, wg, a0, k_a, r_k, gn_w, gn_b)


def _rwkv(px, pc, batch, ww, wa, wg, w0, a0, k_k, k_a, r_k, gn_w, gn_b):
    c = RWKV_CHUNK
    n_ctx = pc.shape[1] // batch // c
    n_lat = px.shape[1] // batch // c
    group = min(RWKV_GROUP, n_lat)
    assert n_lat % group == 0
    ops_c = _rwkv_prep(pc, batch, n_ctx, False, ww, wa, w0, a0, k_k, k_a)
    ops_x = _rwkv_prep(px, batch, group, True, ww, wa, w0, a0, k_k, k_a)
    return _rwkv_scan(ops_c, ops_x, px, batch, group, wa, wg, a0, k_a, r_k, gn_w, gn_b)


def _out_kernel(tiles_per_batch, mixm_ref, om_ref, mixr_ref, w_ref, x_ref, ng_ref,
                gt1_ref, g2_ref, sh2_ref, sc2_ref, x1_ref, hx2_ref):
    b = pl.program_id(0) // tiles_per_batch
    hm = mixm_ref[...].astype(F32)
    ng = ng_ref[...]
    parts = []
    for h in range(M_HEADS):
        cols = slice(h * M_DV, (h + 1) * M_DV)
        seg = hm[:, cols]
        seg = seg * lax.rsqrt(jnp.mean(seg * seg, axis=-1, keepdims=True) + NORM_EPS)
        og = jnp.concatenate([om_ref[2 * h], om_ref[2 * h + 1]], axis=1)
        parts.append((seg * ng[:, cols] * jax.nn.sigmoid(og)).astype(BF16))
    lhs = jnp.concatenate(parts + [mixr_ref[...]], axis=1)
    x1 = x_ref[...] + gt1_ref[pl.ds(b, 1), :] * _mm(lhs, w_ref[...])
    x1_ref[...] = x1
    y = x1 * lax.rsqrt(jnp.mean(x1 * x1, axis=-1, keepdims=True) + NORM_EPS) * g2_ref[...]
    hx2_ref[...] = (y * (1.0 + sc2_ref[pl.ds(b, 1), :]) + sh2_ref[pl.ds(b, 1), :]).astype(BF16)


def _out_proj(mixm, px, mixr, w_out, x2d, ng, gt1, g2, sh2, sc2, tm, tiles_per_batch):
    t, d = x2d.shape
    dm = mixm.shape[1]
    row = lambda i: (i, 0)
    const = lambda i: (0, 0)
    return pl.pallas_call(
        functools.partial(_out_kernel, tiles_per_batch),
        grid=(t // tm,),
        in_specs=[pl.BlockSpec((tm, dm), row),
                  pl.BlockSpec((8, tm, LANES), lambda i: (SLAB_O // 8, i, 0)),
                  pl.BlockSpec((tm, dm), row),
                  pl.BlockSpec((d, d), const),
                  pl.BlockSpec((tm, d), row),
                  pl.BlockSpec((1, dm), const),
                  pl.BlockSpec((8, d), const),
                  pl.BlockSpec((1, d), const),
                  pl.BlockSpec((8, d), const),
                  pl.BlockSpec((8, d), const)],
        out_specs=[pl.BlockSpec((tm, d), row), pl.BlockSpec((tm, d), row)],
        out_shape=[jax.ShapeDtypeStruct((t, d), F32), jax.ShapeDtypeStruct((t, d), BF16)],
        compiler_params=_params(1),
    )(mixm, px, mixr, w_out, x2d, ng, gt1, g2, sh2, sc2)


def _ffn_kernel(tiles_per_img, hx_ref, top_ref, bot_ref, wu_ref, wg_ref, wd_ref, cw_ref, cb_ref,
                x1_ref, gt2_ref, gf_ref, o_ref, acc_ref):
    i = pl.program_id(0)
    j = pl.program_id(1)
    tm = hx_ref.shape[0]
    ti = i % tiles_per_img

    @pl.when(j == 0)
    def _():
        acc_ref[...] = jnp.zeros(acc_ref.shape, F32)

    hx = hx_ref[...]
    top = jnp.where(ti > 0, top_ref[...], jnp.zeros_like(top_ref[...]))
    bot = jnp.where(ti < tiles_per_img - 1, bot_ref[...], jnp.zeros_like(bot_ref[...]))
    u = _mm(jnp.concatenate([top, hx, bot], axis=0), wu_ref[...])
    nr = u.shape[0]
    col = lax.broadcasted_iota(jnp.int32, u.shape, 0) & (GRID_W - 1)
    ul = jnp.where(col == 0, 0.0, pltpu.roll(u, 1, 0))
    ur = jnp.where(col == GRID_W - 1, 0.0, pltpu.roll(u, nr - 1, 0))
    cw = cw_ref[...]
    conv = cb_ref[...]
    for dy in range(3):
        rows = slice(dy * GRID_W, dy * GRID_W + tm)
        conv = (conv + ul[rows] * cw[3 * dy:3 * dy + 1, :] + u[rows] * cw[3 * dy + 1:3 * dy + 2, :]
                + ur[rows] * cw[3 * dy + 2:3 * dy + 3, :])
    gelu = 0.5 * conv * (1.0 + jnp.tanh(0.7978845608028654 * (conv + 0.044715 * conv * conv * conv)))
    act = (gelu * _mm(hx, wg_ref[...])).astype(BF16)
    acc_ref[...] += _mm(act, wd_ref[...])

    @pl.when(j == pl.num_programs(1) - 1)
    def _():
        b = i // tiles_per_img
        x2 = x1_ref[...] + gt2_ref[pl.ds(b, 1), :] * acc_ref[...]
        o_ref[...] = x2 * lax.rsqrt(jnp.mean(x2 * x2, axis=-1, keepdims=True) + NORM_EPS) * gf_ref[...]


def _conv_ffn(hx2, w_up, w_gate, w_down, cw, cb, x1, gt2, g_final, tm, tiles_per_img):
    t, d = hx2.shape
    f = w_up.shape[1]
    tf = 512
    rows_per_tile = tm // GRID_W
    n_rows = t // GRID_W
    return pl.pallas_call(
        functools.partial(_ffn_kernel, tiles_per_img),
        grid=(t // tm, f // tf),
        in_specs=[pl.BlockSpec((tm, d), lambda i, j: (i, 0)),
                  pl.BlockSpec((GRID_W, d), lambda i, j: (jnp.maximum(i * rows_per_tile - 1, 0), 0)),
                  pl.BlockSpec((GRID_W, d), lambda i, j: (jnp.minimum((i + 1) * rows_per_tile, n_rows - 1), 0)),
                  pl.BlockSpec((d, tf), lambda i, j: (0, j)),
                  pl.BlockSpec((d, tf), lambda i, j: (0, j)),
                  pl.BlockSpec((tf, d), lambda i, j: (j, 0)),
                  pl.BlockSpec((9, tf), lambda i, j: (0, j)),
                  pl.BlockSpec((1, tf), lambda i, j: (0, j)),
                  pl.BlockSpec((tm, d), lambda i, j: (i, 0)),
                  pl.BlockSpec((8, d), lambda i, j: (0, 0)),
                  pl.BlockSpec((1, d), lambda i, j: (0, 0))],
        out_specs=pl.BlockSpec((tm, d), lambda i, j: (i, 0)),
        out_shape=jax.ShapeDtypeStruct((t, d), F32),
        scratch_shapes=[pltpu.VMEM((tm, d), F32)],
        compiler_params=_params(2),
    )(hx2, hx2, hx2, w_up, w_gate, w_down, cw, cb, x1, gt2, g_final)


def _lora_pairs(up):
    _, rank, width = up.shape
    u = up.reshape(2, rank, width // LANES, LANES).transpose(2, 0, 1, 3)
    z = jnp.zeros_like(u[:, 0])
    top = jnp.concatenate([u[:, 0], z], axis=-1)
    bot = jnp.concatenate([z, u[:, 1]], axis=-1)
    return jnp.concatenate([top, bot], axis=1).astype(BF16)


def kernel(x, c, ctx, c_ctx, w_mod, b_mod, g_norm1, g_norm2, w_in, m_conv_w, m_conv_b, m_gate_b, m_norm_g, r_w0, r_w_up, r_a0, r_a_up, r_g_up, r_k_k, r_k_a, r_r_k, r_gn_w, r_gn_b, w_out, f_w_up, f_w_gate, f_conv_w, f_conv_b, f_w_down, g_final):
    batch, seq, d = x.shape
    ctx_len = ctx.shape[1]
    assert w_mod.shape[0] == 1, "single-layer block"
    assert batch + 1 <= 8 and seq % 512 == 0 and ctx_len % MLSTM_CHUNK == 0

    cv8 = jnp.zeros((8, d), F32).at[:batch].set(c).at[batch].set(c_ctx)
    mod = _modulation(cv8, w_mod[0], b_mod[0])
    sh1, sc1, gt1, sh2, sc2, gt2 = (mod[:, k * d:(k + 1) * d] for k in range(6))

    w = w_in[0]
    n_gate = 4 * M_HEADS
    g0 = 3 * 1024
    w_p = jnp.concatenate([w[:, :g0], w[:, g0 + n_gate:], w[:, g0:g0 + n_gate],
                           jnp.zeros((d, N_SLABS * LANES - w.shape[1]), F32)], axis=1).astype(BF16)
    g1 = g_norm1[0].reshape(1, d)
    tm_x = 1024 if seq % 1024 == 0 else 512
    px = _inproj(x.reshape(batch * seq, d), g1, sh1, sc1, w_p, tm_x, seq // tm_x, 0)
    tm_c = ctx_len
    pc = _inproj(ctx.reshape(batch * ctx_len, d), g1, sh1, sc1, w_p, tm_c, 1 << 30, batch)

    gate_row = jnp.zeros((1, LANES), F32).at[0, GATE_LANE0:GATE_LANE0 + n_gate].set(m_gate_b[0].reshape(-1))
    mixm = _mlstm(px, pc, batch, m_conv_w[0], m_conv_b[0].reshape(1, -1), gate_row)

    rw = r_k_k.shape[1]
    wg = jnp.zeros((2 * LANES, rw), F32).at[:r_g_up.shape[1]].set(r_g_up[0]).astype(BF16)
    mixr = _rwkv(px, pc, batch, _lora_pairs(r_w_up[0]), _lora_pairs(r_a_up[0]), wg,
                 r_w0[0], r_a0[0], r_k_k, r_k_a, r_r_k[0].reshape(1, rw), r_gn_w, r_gn_b)

    tm_o = 512
    x1, hx2 = _out_proj(mixm, px, mixr, w_out[0].astype(BF16), x.reshape(batch * seq, d), m_norm_g,
                        gt1, g_norm2[0].reshape(1, d), sh2, sc2, tm_o, seq // tm_o)

    tm_f = 512
    out = _conv_ffn(hx2, f_w_up[0].astype(BF16), f_w_gate[0].astype(BF16), f_w_down[0].astype(BF16),
                    f_conv_w[0].reshape(9, -1), f_conv_b, x1, gt2, g_final.reshape(1, d),
                    tm_f, seq // tm_f)
    return out.reshape(batch, seq, d)
```

```python
import functools

import jax
import jax.numpy as jnp
from jax import lax
from jax.experimental import pallas as pl
from jax.experimental.pallas import tpu as pltpu

F32 = jnp.float32
BF16 = jnp.bfloat16
HIGHEST = lax.Precision.HIGHEST

LANES = 128
GRID_W = 64
M_HEADS = 4
M_DQK = 128
M_DV = 256
R_N = 64
NORM_EPS = 1e-6
GN_EPS = 64e-5
MLSTM_CHUNK = 128
RWKV_CHUNK = 64
RWKV_GROUP = 8
RWKV_PREP_WIDTH = 4
VMEM_LIMIT = 56 * 1024 * 1024

SLAB_Q, SLAB_K, SLAB_V, SLAB_O = 0, 4, 8, 16
SLAB_R, SLAB_KR, SLAB_VR = 24, 32, 40
SLAB_LW, SLAB_LA, SLAB_LG0, SLAB_LG1 = 48, 49, 50, 51
N_SLABS = 52
GATE_LANE0 = 32


def _mm(a, b, precision=None):
    return jnp.dot(a, b, preferred_element_type=F32, precision=precision)


def _mm_nt(a, b):
    return lax.dot_general(a, b, (((1,), (1,)), ((), ())), preferred_element_type=F32)


def _mm_tn(a, b):
    return lax.dot_general(a, b, (((0,), (0,)), ((), ())), preferred_element_type=F32)


def _mm_split(x, ones, ones_first):
    hi = x.astype(BF16)
    rest = x - hi.astype(F32)
    mid = rest.astype(BF16)
    lo = (rest - mid.astype(F32)).astype(BF16)
    if ones_first:
        return _mm(ones, hi) + _mm(ones, mid) + _mm(ones, lo)
    return _mm(hi, ones) + _mm(mid, ones) + _mm(lo, ones)


def _softplus(x):
    return jnp.maximum(x, 0.0) + jnp.log1p(jnp.exp(-jnp.abs(x)))


def _log_sigmoid(x):
    return -_softplus(-x)


def _params(n_axes):
    return pltpu.CompilerParams(dimension_semantics=("arbitrary",) * n_axes,
                                vmem_limit_bytes=VMEM_LIMIT)


def _mod_kernel(cv_ref, w_ref, b_ref, o_ref):
    cv = cv_ref[...]
    s = (cv * jax.nn.sigmoid(cv)).astype(BF16)
    o_ref[...] = _mm(s, w_ref[...].astype(BF16)) + b_ref[...]


def _modulation(cv8, w_mod, b_mod):
    d, n = w_mod.shape
    tn = 1024
    return pl.pallas_call(
        _mod_kernel,
        grid=(n // tn,),
        in_specs=[pl.BlockSpec((8, d), lambda j: (0, 0)),
                  pl.BlockSpec((d, tn), lambda j: (0, j)),
                  pl.BlockSpec((1, tn), lambda j: (0, j))],
        out_specs=pl.BlockSpec((8, tn), lambda j: (0, j)),
        out_shape=jax.ShapeDtypeStruct((8, n), F32),
        compiler_params=_params(1),
    )(cv8, w_mod, b_mod.reshape(1, n))


def _inproj_kernel(tiles_per_row, row0, x_ref, g_ref, sh_ref, sc_ref, w_ref, o_ref, hx_ref):
    i = pl.program_id(0)
    j = pl.program_id(1)

    @pl.when(j == 0)
    def _():
        x = x_ref[...]
        ms = jnp.mean(x * x, axis=-1, keepdims=True)
        y = x * lax.rsqrt(ms + NORM_EPS) * g_ref[...]
        r = row0 + i // tiles_per_row
        hx = y * (1.0 + sc_ref[pl.ds(r, 1), :]) + sh_ref[pl.ds(r, 1), :]
        hx_ref[...] = hx.astype(BF16)

    acc = _mm(hx_ref[...], w_ref[...])
    for s in range(acc.shape[1] // LANES):
        o_ref[s] = acc[:, s * LANES:(s + 1) * LANES]


def _inproj(x2d, g, sh, sc, w_p, tm, tiles_per_row, row0):
    t, d = x2d.shape
    n = w_p.shape[1]
    tn = 512
    return pl.pallas_call(
        functools.partial(_inproj_kernel, tiles_per_row, row0),
        grid=(t // tm, n // tn),
        in_specs=[pl.BlockSpec((tm, d), lambda i, j: (i, 0)),
                  pl.BlockSpec((1, d), lambda i, j: (0, 0)),
                  pl.BlockSpec((8, d), lambda i, j: (0, 0)),
                  pl.BlockSpec((8, d), lambda i, j: (0, 0)),
                  pl.BlockSpec((d, tn), lambda i, j: (0, j))],
        out_specs=pl.BlockSpec((tn // LANES, tm, LANES), lambda i, j: (j, i, 0)),
        out_shape=jax.ShapeDtypeStruct((n // LANES, t, LANES), F32),
        scratch_shapes=[pltpu.VMEM((tm, d), BF16)],
        compiler_params=_params(2),
    )(x2d, g, sh, sc, w_p)


def _mlstm_kernel(n_ctx, n_lat,
                  qx_ref, kx_ref, vx_ref, gx_ref, qc_ref, kc_ref, vc_ref, gc_ref,
                  cwq_ref, cwk_ref, cbq_ref, cbk_ref, gb_ref,
                  o_ref,
                  qs_ref, ks_ref, kst_ref, ct_ref, hf_ref, hb_ref):
    c = MLSTM_CHUNK
    head = pl.program_id(1)
    rid = lax.broadcasted_iota(jnp.int32, (c, LANES), 0)
    lane = lax.broadcasted_iota(jnp.int32, (c, LANES), 1)
    r2 = lax.broadcasted_iota(jnp.int32, (c, c), 0)
    c2 = lax.broadcasted_iota(jnp.int32, (c, c), 1)
    tri = ((c2 <= r2).astype(F32), (c2 >= r2).astype(F32))
    keep = (c2 <= r2, c2 >= r2)
    gbias = gb_ref[...]

    def conv_pass(q_ref, k_ref, nchunk, base):
        nrows = nchunk * c

        def body(ci, carry):
            r0 = pl.multiple_of(ci * c, c)
            p0 = pl.multiple_of(jnp.maximum(r0 - 8, 0), 8)
            n0 = pl.multiple_of(jnp.minimum(r0 + c, nrows - 8), 8)
            for src, w_ref, b_ref, scale, is_k in ((q_ref, cwq_ref, cbq_ref, M_DQK ** -0.5, False),
                                                   (k_ref, cwk_ref, cbk_ref, 1.0, True)):
                cur = src[0, pl.ds(r0, c), :]
                prev_row = jnp.where(ci > 0, src[0, pl.ds(p0, 8), :][7:8, :], 0.0)
                next_row = jnp.where(ci < nchunk - 1, src[0, pl.ds(n0, 8), :][0:1, :], 0.0)
                up = jnp.where(rid == 0, prev_row, pltpu.roll(cur, 1, 0))
                dn = jnp.where(rid == c - 1, next_row, pltpu.roll(cur, c - 1, 0))
                w = w_ref[...]
                y = (up * w[0:1, :] + cur * w[1:2, :] + dn * w[2:3, :] + b_ref[...]) * scale
                if is_k:
                    ks_ref[base + ci] = y.astype(BF16)
                    kst_ref[base + ci] = y.T.astype(BF16)
                else:
                    qs_ref[base + ci] = y.astype(BF16)
            return carry

        lax.fori_loop(0, nchunk, body, 0)

    def step(d, ci, base, v_ref, g_ref, n, m, want_h):
        r0 = pl.multiple_of(ci * c, c)
        q = qs_ref[base + ci]
        k = ks_ref[base + ci]
        kt = kst_ref[base + ci]
        v = jnp.concatenate([v_ref[0, pl.ds(r0, c), :], v_ref[1, pl.ds(r0, c), :]], axis=1).astype(BF16)
        gb = g_ref[0, pl.ds(r0, c), :] + gbias
        icol = GATE_LANE0 + d * M_HEADS + head
        fcol = GATE_LANE0 + 2 * M_HEADS + d * M_HEADS + head
        bc = _mm(tri[d], _log_sigmoid(gb), HIGHEST)
        b_col = jnp.sum(jnp.where(lane == fcol, bc, 0.0), axis=1, keepdims=True)
        i_col = jnp.sum(jnp.where(lane == icol, gb, 0.0), axis=1, keepdims=True)
        b_row = jnp.sum(jnp.where(rid == fcol, bc.T, 0.0), axis=0, keepdims=True)
        i_row = jnp.sum(jnp.where(rid == icol, gb.T, 0.0), axis=0, keepdims=True)
        b_last = b_col[0:1, :] if d else b_col[c - 1:c, :]
        ct = ct_ref[d]
        h = None
        if want_h:
            dmat = jnp.where(keep[d], b_col - b_row + i_row, -jnp.inf)
            m_inter = b_col + m
            m_j = jnp.maximum(m_inter, jnp.max(dmat, axis=-1, keepdims=True))
            s = _mm(q, kt) * jnp.exp(dmat - m_j)
            inter = jnp.exp(m_inter - m_j)
            num = _mm(s.astype(BF16), v) + inter * _mm(q, ct.astype(BF16))
            qn = jnp.sum(q.astype(F32) * n, axis=-1, keepdims=True)
            den = jnp.sum(s, axis=-1, keepdims=True) + inter * qn
            h = num / jnp.maximum(jnp.abs(den), jnp.exp(-m_j))
        glog = b_last - b_col + i_col
        m_new = jnp.maximum(b_last + m, jnp.max(glog, axis=0, keepdims=True))
        wk = jnp.exp(glog - m_new)
        decay = jnp.exp(b_last + m - m_new)
        ct_ref[d] = decay * ct + _mm(kt, (wk * v.astype(F32)).astype(BF16))
        n_new = decay * n + jnp.sum(wk * k.astype(F32), axis=0, keepdims=True)
        return n_new, m_new, h

    def run(nchunk, base, v_ref, g_ref, carry, want_h):
        def body(i, carry):
            nf, mf, nb, mb = carry
            ib = nchunk - 1 - i
            nf, mf, h_f = step(0, i, base, v_ref, g_ref, nf, mf, want_h)
            nb, mb, h_b = step(1, ib, base, v_ref, g_ref, nb, mb, want_h)
            if want_h:
                hf_ref[pl.ds(pl.multiple_of(i * c, c), c), :] = h_f
                hb_ref[pl.ds(pl.multiple_of(ib * c, c), c), :] = h_b
            return nf, mf, nb, mb

        return lax.fori_loop(0, nchunk, body, carry)

    conv_pass(qc_ref, kc_ref, n_ctx, 0)
    conv_pass(qx_ref, kx_ref, n_lat, n_ctx)
    ct_ref[...] = jnp.zeros(ct_ref.shape, F32)
    zn = jnp.zeros((1, M_DQK), F32)
    zm = jnp.zeros((1, 1), F32)
    carry = run(n_ctx, 0, vc_ref, gc_ref, (zn, zm, zn, zm), False)
    run(n_lat, n_ctx, vx_ref, gx_ref, carry, True)

    def fin(i, carry):
        rows = pl.ds(pl.multiple_of(i * c, c), c)
        o_ref[rows, :] = (hf_ref[rows, :] + hb_ref[rows, :]).astype(o_ref.dtype)
        return carry

    lax.fori_loop(0, n_lat, fin, 0)


def _mlstm(px, pc, batch, cw, cb, gate_row):
    lx = px.shape[1] // batch
    lc = pc.shape[1] // batch
    n_lat, n_ctx = lx // MLSTM_CHUNK, lc // MLSTM_CHUNK

    def slab(nrows, first, width=1):
        return pl.BlockSpec((width, nrows, LANES), lambda b, h: (first // width + h, b, 0))

    def fixed(nrows, idx):
        return pl.BlockSpec((1, nrows, LANES), lambda b, h: (idx, b, 0))

    in_specs = [slab(lx, SLAB_Q), slab(lx, SLAB_K), slab(lx, SLAB_V, 2), fixed(lx, SLAB_LG1),
                slab(lc, SLAB_Q), slab(lc, SLAB_K), slab(lc, SLAB_V, 2), fixed(lc, SLAB_LG1),
                pl.BlockSpec((3, LANES), lambda b, h: (0, h)),
                pl.BlockSpec((3, LANES), lambda b, h: (0, M_HEADS + h)),
                pl.BlockSpec((1, LANES), lambda b, h: (0, h)),
                pl.BlockSpec((1, LANES), lambda b, h: (0, M_HEADS + h)),
                pl.BlockSpec((1, LANES), lambda b, h: (0, 0))]
    nch = n_ctx + n_lat
    return pl.pallas_call(
        functools.partial(_mlstm_kernel, n_ctx, n_lat),
        grid=(batch, M_HEADS),
        in_specs=in_specs,
        out_specs=pl.BlockSpec((lx, M_DV), lambda b, h: (b, h)),
        out_shape=jax.ShapeDtypeStruct((batch * lx, M_HEADS * M_DV), BF16),
        scratch_shapes=[pltpu.VMEM((nch, MLSTM_CHUNK, LANES), BF16),
                        pltpu.VMEM((nch, MLSTM_CHUNK, LANES), BF16),
                        pltpu.VMEM((nch, LANES, MLSTM_CHUNK), BF16),
                        pltpu.VMEM((2, M_DQK, M_DV), F32),
                        pltpu.VMEM((lx, M_DV), F32),
                        pltpu.VMEM((lx, M_DV), F32)],
        compiler_params=_params(2),
    )(px, px, px, px, pc, pc, pc, pc, cw, cw, cb, cb, gate_row)


def _rwkv_stack(x):
    head0 = lax.broadcasted_iota(jnp.int32, x.shape, 1) < R_N
    zero = jnp.zeros_like(x)
    return jnp.concatenate([jnp.where(head0, x, zero), jnp.where(head0, zero, x)], axis=0)


def _rwkv_fold(x):
    half = x.shape[0] // 2
    return x[0:half] + x[half:2 * half]


def _rwkv_gate_a(la, wa, a0_ref, d):
    return jax.nn.sigmoid(a0_ref[d:d + 1, :] + _mm(la.astype(BF16), wa)[:, d * LANES:(d + 1) * LANES])


def _rwkv_prep_kernel(want_y, r_ref, k_ref, v_ref, lw_ref, la_ref,
                      ww_ref, wa_ref, w0_ref, a0_ref, kk_ref, ka_ref, *out_refs):
    c = RWKV_CHUNK
    c2 = 2 * c
    stack = _rwkv_stack
    rr = lax.broadcasted_iota(jnp.int32, (c2, c2), 0)
    cc = lax.broadcasted_iota(jnp.int32, (c2, c2), 1)
    same_head = (rr < c) == (cc < c)
    group_ones = same_head.astype(BF16)
    tt = rr & (c - 1)
    ss = cc & (c - 1)
    strict = (ss < tt, ss > tt)
    incl = (ss <= tt, ss >= tt)
    eye = (rr == cc).astype(F32)
    tr = lax.broadcasted_iota(jnp.int32, (c, c), 0)
    tc = lax.broadcasted_iota(jnp.int32, (c, c), 1)
    tri = ((tc <= tr).astype(BF16), (tc >= tr).astype(BF16))
    ww = ww_ref[0]
    wa = wa_ref[0]
    k_k = kk_ref[...]
    k_a = ka_ref[...]
    m_ref, n_ref, dec_ref = out_refs[0], out_refs[1], out_refs[2]

    def prep_chunks(js):
        nj = len(js)
        rows = [pl.ds(pl.multiple_of(j * c, c), c) for j in js]
        ch = [(d, i) for d in (0, 1) for i in range(nj)]
        half = lambda d: slice(d * LANES, (d + 1) * LANES)
        r = [r_ref[0, rw, :] for rw in rows]
        k = [k_ref[0, rw, :] for rw in rows]
        v_s = [stack(v_ref[0, rw, :]).astype(BF16) for rw in rows]
        lo_w = [_mm(jnp.tanh(lw_ref[0, rw, :]).astype(BF16), ww) for rw in rows]
        lo_a = [_mm(la_ref[0, rw, :].astype(BF16), wa) for rw in rows]
        kk = [x * k_k for x in k]
        kk = [x * lax.rsqrt(_mm_split(x * x, group_ones, False) + 1e-12) for x in kk]
        logw = [-jnp.exp(-_softplus(-(w0_ref[d:d + 1, :] + lo_w[i][:, half(d)])) - 0.5) for d, i in ch]
        a = [jax.nn.sigmoid(a0_ref[d:d + 1, :] + lo_a[i][:, half(d)]) for d, i in ch]
        pin = [_mm_split(x, tri[d], True) for (d, i), x in zip(ch, logw)]
        ptot = [x[0:1, :] if d else x[c - 1:c, :] for (d, i), x in zip(ch, pin)]
        kd = [k[i] * (1.0 + (a_ - 1.0) * k_a) for (d, i), a_ in zip(ch, a)]
        kka = [kk[i] * a_ for (d, i), a_ in zip(ch, a)]
        e_inv = [jnp.exp(-x) for x in pin]
        e_end = [jnp.exp(pt - x) for pt, x in zip(ptot, pin)]
        r_f = [stack(r[i] * jnp.exp(x)) for (d, i), x in zip(ch, pin)]
        ar_t = [jnp.concatenate([stack(-kk[i] * jnp.exp(x - lw_)), rf], axis=0).astype(BF16)
                for (d, i), x, lw_, rf in zip(ch, pin, logw, r_f)]
        bk_t = [jnp.concatenate([stack(x * e), stack(y * e)], axis=0).astype(BF16)
                for x, y, e in zip(kka, kd, e_inv)]
        bk_end = [jnp.concatenate([stack(x * e), stack(y * e)], axis=0).astype(BF16)
                  for x, y, e in zip(kka, kd, e_end)]
        aa = [_mm_nt(x, y) for x, y in zip(ar_t, bk_t)]
        aab = [jnp.where(strict[d], x[0:c2, 0:c2], 0.0) for (d, i), x in zip(ch, aa)]
        aak = [jnp.where(strict[d], x[0:c2, c2:2 * c2], 0.0).astype(BF16) for (d, i), x in zip(ch, aa)]
        akv = [_mm(x, v_s[i]).astype(BF16) for (d, i), x in zip(ch, aak)]
        xs = [eye + x for x in aab]
        pb = [x.astype(BF16) for x in aab]
        ps = [_mm(x, x) for x in pb]
        for _ in range(c.bit_length() - 3):
            pb = [x.astype(BF16) for x in ps]
            both = [_mm(jnp.concatenate([x.astype(BF16), p], axis=0), p) for x, p in zip(xs, pb)]
            xs = [x + y[0:c2] for x, y in zip(xs, both)]
            ps = [y[c2:2 * c2] for y in both]
        xs = [(x + _mm(x.astype(BF16), p.astype(BF16))).astype(BF16) for x, p in zip(xs, ps)]
        wt = [_mm(x, y[0:c2]).astype(BF16) for x, y in zip(xs, ar_t)]
        u0 = [_mm(x, y).astype(BF16) for x, y in zip(xs, akv)]
        m_mat = [_mm_tn(x, y[0:c2]) for x, y in zip(wt, bk_end)]
        n_mat = [_mm_tn(jnp.concatenate([x, v_s[i]], axis=0), y) for (d, i), x, y in zip(ch, u0, bk_end)]
        for (d, i), mm_, nn_, pt in zip(ch, m_mat, n_mat, ptot):
            slot = (0, 0, d, js[i])
            m_ref[slot] = _rwkv_fold(mm_).astype(BF16)
            n_ref[slot] = _rwkv_fold(nn_)
            dec_ref[slot] = jnp.broadcast_to(jnp.exp(pt), (8, LANES))
        if want_y:
            arb = [jnp.where(incl[d], x[c2:2 * c2, 0:c2], 0.0).astype(BF16) for (d, i), x in zip(ch, aa)]
            ark = [jnp.where(incl[d], x[c2:2 * c2, c2:2 * c2], 0.0).astype(BF16) for (d, i), x in zip(ch, aa)]
            qy = [_mm(x, jnp.concatenate([w_, u_], axis=1)) for x, w_, u_ in zip(arb, wt, u0)]
            yl = [x[:, c2:2 * c2] + _mm(y, v_s[i]) for (d, i), x, y in zip(ch, qy, ark)]
            for (d, i), rf, x, y in zip(ch, r_f, qy, yl):
                slot = (0, 0, d, js[i])
                out_refs[3][slot] = _rwkv_fold(rf + x[:, 0:c2]).astype(BF16)
                out_refs[4][slot] = _rwkv_fold(y)

    n_chunks = r_ref.shape[1] // c
    width = min(RWKV_PREP_WIDTH, n_chunks)
    assert n_chunks % width == 0

    def body(t, carry):
        prep_chunks([t * width + u for u in range(width)])
        return carry

    lax.fori_loop(0, n_chunks // width, body, 0)


def _rwkv_scan_kernel(mc_ref, nc_ref, dc_ref, mf_ref, nf_ref, df_ref, qf_ref, ylf_ref,
                      mb_ref, nb_ref, db_ref, qb_ref, ylb_ref,
                      r_ref, k_ref, v_ref, la_ref, lg0_ref, lg1_ref,
                      wa_ref, wg_ref, a0_ref, ka_ref, rk_ref, gnw_ref, gnb_ref,
                      o_ref, z_ref, yf_ref, yb_ref):
    c = RWKV_CHUNK
    g = pl.program_id(2)
    n_groups = pl.num_programs(2)
    group = mf_ref.shape[3]
    n_ctx = mc_ref.shape[3]

    def advance(z, m_c, n_c, dec):
        return z * dec[0:1, :] + _mm(z.astype(BF16), _rwkv_stack(m_c)) + _rwkv_stack(n_c)

    @pl.when(g == 0)
    def _():
        z_ref[...] = jnp.zeros(z_ref.shape, F32)

        def ctx_body(i, carry):
            for d, ii in ((0, i), (1, n_ctx - 1 - i)):
                z_ref[d] = advance(z_ref[d], mc_ref[0, 0, d, ii], nc_ref[0, 0, d, ii], dc_ref[0, 0, d, ii])
            return carry

        lax.fori_loop(0, n_ctx, ctx_body, 0)

    def lat_body(j, carry):
        for d, jj, first, m_ref, n_ref, d_ref, q_ref, yl_ref, y_ref in (
                (0, j, g * group, mf_ref, nf_ref, df_ref, qf_ref, ylf_ref, yf_ref),
                (1, group - 1 - j, (n_groups - 1 - g) * group, mb_ref, nb_ref, db_ref, qb_ref, ylb_ref, yb_ref)):
            z = z_ref[d]
            ys = _mm_nt(_rwkv_stack(q_ref[0, 0, 0, jj]), z.astype(BF16))
            y_ref[pl.ds(pl.multiple_of((first + jj) * c, c), c), :] = _rwkv_fold(ys) + yl_ref[0, 0, 0, jj]
            z_ref[d] = advance(z, m_ref[0, 0, 0, jj], n_ref[0, 0, 0, jj], d_ref[0, 0, 0, jj])
        return carry

    lax.fori_loop(0, group, lat_body, 0)

    @pl.when(g == n_groups - 1)
    def _():
        fr = 4 * c
        r2 = lax.broadcasted_iota(jnp.int32, (LANES, LANES), 0)
        c2 = lax.broadcasted_iota(jnp.int32, (LANES, LANES), 1)
        group_ones = ((r2 < R_N) == (c2 < R_N)).astype(F32)
        wa = wa_ref[0]
        wg = wg_ref[...]
        k_a = ka_ref[...]
        inv_n = 1.0 / R_N

        def fin(i, carry):
            rows = pl.ds(pl.multiple_of(i * fr, fr), fr)
            r = r_ref[0, rows, :]
            k = k_ref[0, rows, :]
            v = v_ref[0, rows, :]
            la = la_ref[0, rows, :]
            a_sum = _rwkv_gate_a(la, wa, a0_ref, 0) + _rwkv_gate_a(la, wa, a0_ref, 1)
            ksum = k * (2.0 + (a_sum - 2.0) * k_a)
            bonus = _mm(r * ksum * rk_ref[...], group_ones, HIGHEST) * v
            lg = jnp.concatenate([lg0_ref[0, rows, :], lg1_ref[0, rows, :]], axis=1)
            gate = _mm(jax.nn.sigmoid(lg).astype(BF16), wg)
            y = yf_ref[rows, :] + yb_ref[rows, :]
            mu = _mm(y, group_ones, HIGHEST) * inv_n
            yc = y - mu
            var = _mm(yc * yc, group_ones, HIGHEST) * inv_n
            yn = yc * lax.rsqrt(var + GN_EPS) * gnw_ref[...] + gnb_ref[...]
            o_ref[rows, :] = ((yn + bonus) * gate).astype(o_ref.dtype)
            return carry

        lax.fori_loop(0, o_ref.shape[0] // fr, fin, 0)


def _rwkv_prep(p_all, batch, group, want_y, ww, wa, w0, a0, k_k, k_a):
    c = RWKV_CHUNK
    length = p_all.shape[1] // batch
    n_chunks = length // c
    n_groups = n_chunks // group
    n_pairs = ww.shape[0]
    rows = group * c

    def slab(first):
        return pl.BlockSpec((1, rows, LANES), lambda b, p, g: (first + p, b * n_groups + g, 0))

    def fixed(idx):
        return pl.BlockSpec((1, rows, LANES), lambda b, p, g: (idx, b * n_groups + g, 0))

    def vec(nrows):
        return pl.BlockSpec((nrows, LANES), lambda b, p, g: (0, p))

    lora = pl.BlockSpec((1, LANES, 2 * LANES), lambda b, p, g: (p, 0, 0))
    outs = [(c, BF16), (c, F32), (8, F32)] + ([(c, BF16), (c, F32)] if want_y else [])
    return pl.pallas_call(
        functools.partial(_rwkv_prep_kernel, want_y),
        grid=(batch, n_pairs, n_groups),
        in_specs=[slab(SLAB_R), slab(SLAB_KR), slab(SLAB_VR), fixed(SLAB_LW), fixed(SLAB_LA),
                  lora, lora, vec(2), vec(2), vec(1), vec(1)],
        out_specs=[pl.BlockSpec((1, 1, 2, group, nr, LANES), lambda b, p, g: (b, p, 0, g, 0, 0))
                   for nr, _ in outs],
        out_shape=[jax.ShapeDtypeStruct((batch, n_pairs, 2, n_chunks, nr, LANES), dt) for nr, dt in outs],
        compiler_params=_params(3),
    )(p_all, p_all, p_all, p_all, p_all, ww, wa, w0, a0, k_k, k_a)


def _rwkv_scan(ops_c, ops_x, px, batch, group, wa, wg, a0, k_a, r_k, gn_w, gn_b):
    c = RWKV_CHUNK
    lx = px.shape[1] // batch
    n_pairs, n_ctx = ops_c[0].shape[1], ops_c[0].shape[3]
    n_groups = ops_x[0].shape[3] // group

    def ctx_block(a):
        return pl.BlockSpec((1, 1, 2, n_ctx, a.shape[4], LANES), lambda b, p, g: (b, p, 0, 0, 0, 0))

    def fwd_block(a):
        return pl.BlockSpec((1, 1, 1, group, a.shape[4], LANES), lambda b, p, g: (b, p, 0, g, 0, 0))

    def bwd_block(a):
        return pl.BlockSpec((1, 1, 1, group, a.shape[4], LANES),
                            lambda b, p, g: (b, p, 1, n_groups - 1 - g, 0, 0))

    def slab(first):
        return pl.BlockSpec((1, lx, LANES), lambda b, p, g: (first + p, b, 0))

    def fixed(idx):
        return pl.BlockSpec((1, lx, LANES), lambda b, p, g: (idx, b, 0))

    def vec(nrows):
        return pl.BlockSpec((nrows, LANES), lambda b, p, g: (0, p))

    return pl.pallas_call(
        _rwkv_scan_kernel,
        grid=(batch, n_pairs, n_groups),
        in_specs=[ctx_block(a) for a in ops_c] + [fwd_block(a) for a in ops_x] + [bwd_block(a) for a in ops_x]
                 + [slab(SLAB_R), slab(SLAB_KR), slab(SLAB_VR), fixed(SLAB_LA), fixed(SLAB_LG0), fixed(SLAB_LG1),
                    pl.BlockSpec((1, LANES, 2 * LANES), lambda b, p, g: (p, 0, 0)),
                    pl.BlockSpec((2 * LANES, LANES), lambda b, p, g: (0, p)),
                    vec(2), vec(1), vec(1), vec(1), vec(1)],
        out_specs=pl.BlockSpec((lx, LANES), lambda b, p, g: (b, p)),
        out_shape=jax.ShapeDtypeStruct((batch * lx, n_pairs * LANES), BF16),
        scratch_shapes=[pltpu.VMEM((2, 2 * c, LANES), F32),
                        pltpu.VMEM((lx, LANES), F32), pltpu.VMEM((lx, LANES), F32)],
        compiler_params=_params(3),
    )(*ops_c, *ops_x, *ops_x, px, px, px, px, px, px, wa, wg, a0, k_a, r_k, gn_w, gn_b)


def _rwkv(px, pc, batch, ww, wa, wg, w0, a0, k_k, k_a, r_k, gn_w, gn_b):
    c = RWKV_CHUNK
    n_ctx = pc.shape[1] // batch // c
    n_lat = px.shape[1] // batch // c
    group = min(RWKV_GROUP, n_lat)
    assert n_lat % group == 0
    ops_c = _rwkv_prep(pc, batch, n_ctx, False, ww, wa, w0, a0, k_k, k_a)
    ops_x = _rwkv_prep(px, batch, group, True, ww, wa, w0, a0, k_k, k_a)
    return _rwkv_scan(ops_c, ops_x, px, batch, group, wa, wg, a0, k_a, r_k, gn_w, gn_b)


def _out_kernel(tiles_per_batch, mixm_ref, om_ref, mixr_ref, w_ref, x_ref, ng_ref,
                gt1_ref, g2_ref, sh2_ref, sc2_ref, x1_ref, hx2_ref):
    b = pl.program_id(0) // tiles_per_batch
    hm = mixm_ref[...].astype(F32)
    ng = ng_ref[...]
    parts = []
    for h in range(M_HEADS):
        cols = slice(h * M_DV, (h + 1) * M_DV)
        seg = hm[:, cols]
        seg = seg * lax.rsqrt(jnp.mean(seg * seg, axis=-1, keepdims=True) + NORM_EPS)
        og = jnp.concatenate([om_ref[2 * h], om_ref[2 * h + 1]], axis=1)
        parts.append((seg * ng[:, cols] * jax.nn.sigmoid(og)).astype(BF16))
    lhs = jnp.concatenate(parts + [mixr_ref[...]], axis=1)
    x1 = x_ref[...] + gt1_ref[pl.ds(b, 1), :] * _mm(lhs, w_ref[...])
    x1_ref[...] = x1
    y = x1 * lax.rsqrt(jnp.mean(x1 * x1, axis=-1, keepdims=True) + NORM_EPS) * g2_ref[...]
    hx2_ref[...] = (y * (1.0 + sc2_ref[pl.ds(b, 1), :]) + sh2_ref[pl.ds(b, 1), :]).astype(BF16)


def _out_proj(mixm, px, mixr, w_out, x2d, ng, gt1, g2, sh2, sc2, tm, tiles_per_batch):
    t, d = x2d.shape
    dm = mixm.shape[1]
    row = lambda i: (i, 0)
    const = lambda i: (0, 0)
    return pl.pallas_call(
        functools.partial(_out_kernel, tiles_per_batch),
        grid=(t // tm,),
        in_specs=[pl.BlockSpec((tm, dm), row),
                  pl.BlockSpec((8, tm, LANES), lambda i: (SLAB_O // 8, i, 0)),
                  pl.BlockSpec((tm, dm), row),
                  pl.BlockSpec((d, d), const),
                  pl.BlockSpec((tm, d), row),
                  pl.BlockSpec((1, dm), const),
                  pl.BlockSpec((8, d), const),
                  pl.BlockSpec((1, d), const),
                  pl.BlockSpec((8, d), const),
                  pl.BlockSpec((8, d), const)],
        out_specs=[pl.BlockSpec((tm, d), row), pl.BlockSpec((tm, d), row)],
        out_shape=[jax.ShapeDtypeStruct((t, d), F32), jax.ShapeDtypeStruct((t, d), BF16)],
        compiler_params=_params(1),
    )(mixm, px, mixr, w_out, x2d, ng, gt1, g2, sh2, sc2)


def _ffn_kernel(tiles_per_img, hx_ref, top_ref, bot_ref, wu_ref, wg_ref, wd_ref, cw_ref, cb_ref,
                x1_ref, gt2_ref, gf_ref, o_ref, acc_ref):
    i = pl.program_id(0)
    j = pl.program_id(1)
    tm = hx_ref.shape[0]
    ti = i % tiles_per_img

    @pl.when(j == 0)
    def _():
        acc_ref[...] = jnp.zeros(acc_ref.shape, F32)

    hx = hx_ref[...]
    top = jnp.where(ti > 0, top_ref[...], jnp.zeros_like(top_ref[...]))
    bot = jnp.where(ti < tiles_per_img - 1, bot_ref[...], jnp.zeros_like(bot_ref[...]))
    u = _mm(jnp.concatenate([top, hx, bot], axis=0), wu_ref[...])
    nr = u.shape[0]
    col = lax.broadcasted_iota(jnp.int32, u.shape, 0) & (GRID_W - 1)
    ul = jnp.where(col == 0, 0.0, pltpu.roll(u, 1, 0))
    ur = jnp.where(col == GRID_W - 1, 0.0, pltpu.roll(u, nr - 1, 0))
    cw = cw_ref[...]
    conv = cb_ref[...]
    for dy in range(3):
        rows = slice(dy * GRID_W, dy * GRID_W + tm)
        conv = (conv + ul[rows] * cw[3 * dy:3 * dy + 1, :] + u[rows] * cw[3 * dy + 1:3 * dy + 2, :]
                + ur[rows] * cw[3 * dy + 2:3 * dy + 3, :])
    gelu = 0.5 * conv * (1.0 + jnp.tanh(0.7978845608028654 * (conv + 0.044715 * conv * conv * conv)))
    act = (gelu * _mm(hx, wg_ref[...])).astype(BF16)
    acc_ref[...] += _mm(act, wd_ref[...])

    @pl.when(j == pl.num_programs(1) - 1)
    def _():
        b = i // tiles_per_img
        x2 = x1_ref[...] + gt2_ref[pl.ds(b, 1), :] * acc_ref[...]
        o_ref[...] = x2 * lax.rsqrt(jnp.mean(x2 * x2, axis=-1, keepdims=True) + NORM_EPS) * gf_ref[...]


def _conv_ffn(hx2, w_up, w_gate, w_down, cw, cb, x1, gt2, g_final, tm, tiles_per_img):
    t, d = hx2.shape
    f = w_up.shape[1]
    tf = 512
    rows_per_tile = tm // GRID_W
    n_rows = t // GRID_W
    return pl.pallas_call(
        functools.partial(_ffn_kernel, tiles_per_img),
        grid=(t // tm, f // tf),
        in_specs=[pl.BlockSpec((tm, d), lambda i, j: (i, 0)),
                  pl.BlockSpec((GRID_W, d), lambda i, j: (jnp.maximum(i * rows_per_tile - 1, 0), 0)),
                  pl.BlockSpec((GRID_W, d), lambda i, j: (jnp.minimum((i + 1) * rows_per_tile, n_rows - 1), 0)),
                  pl.BlockSpec((d, tf), lambda i, j: (0, j)),
                  pl.BlockSpec((d, tf), lambda i, j: (0, j)),
                  pl.BlockSpec((tf, d), lambda i, j: (j, 0)),
                  pl.BlockSpec((9, tf), lambda i, j: (0, j)),
                  pl.BlockSpec((1, tf), lambda i, j: (0, j)),
                  pl.BlockSpec((tm, d), lambda i, j: (i, 0)),
                  pl.BlockSpec((8, d), lambda i, j: (0, 0)),
                  pl.BlockSpec((1, d), lambda i, j: (0, 0))],
        out_specs=pl.BlockSpec((tm, d), lambda i, j: (i, 0)),
        out_shape=jax.ShapeDtypeStruct((t, d), F32),
        scratch_shapes=[pltpu.VMEM((tm, d), F32)],
        compiler_params=_params(2),
    )(hx2, hx2, hx2, w_up, w_gate, w_down, cw, cb, x1, gt2, g_final)


def _lora_pairs(up):
    _, rank, width = up.shape
    u = up.reshape(2, rank, width // LANES, LANES).transpose(2, 0, 1, 3)
    z = jnp.zeros_like(u[:, 0])
    top = jnp.concatenate([u[:, 0], z], axis=-1)
    bot = jnp.concatenate([z, u[:, 1]], axis=-1)
    return jnp.concatenate([top, bot], axis=1).astype(BF16)


def kernel(x, c, ctx, c_ctx, w_mod, b_mod, g_norm1, g_norm2, w_in, m_conv_w, m_conv_b, m_gate_b, m_norm_g, r_w0, r_w_up, r_a0, r_a_up, r_g_up, r_k_k, r_k_a, r_r_k, r_gn_w, r_gn_b, w_out, f_w_up, f_w_gate, f_conv_w, f_conv_b, f_w_down, g_final):
    batch, seq, d = x.shape
    ctx_len = ctx.shape[1]
    assert w_mod.shape[0] == 1, "single-layer block"
    assert batch + 1 <= 8 and seq % 512 == 0 and ctx_len % MLSTM_CHUNK == 0

    cv8 = jnp.zeros((8, d), F32).at[:batch].set(c).at[batch].set(c_ctx)
    mod = _modulation(cv8, w_mod[0], b_mod[0])
    sh1, sc1, gt1, sh2, sc2, gt2 = (mod[:, k * d:(k + 1) * d] for k in range(6))

    w = w_in[0]
    n_gate = 4 * M_HEADS
    g0 = 3 * 1024
    w_p = jnp.concatenate([w[:, :g0], w[:, g0 + n_gate:], w[:, g0:g0 + n_gate],
                           jnp.zeros((d, N_SLABS * LANES - w.shape[1]), F32)], axis=1).astype(BF16)
    g1 = g_norm1[0].reshape(1, d)
    tm_x = 1024 if seq % 1024 == 0 else 512
    px = _inproj(x.reshape(batch * seq, d), g1, sh1, sc1, w_p, tm_x, seq // tm_x, 0)
    tm_c = ctx_len
    pc = _inproj(ctx.reshape(batch * ctx_len, d), g1, sh1, sc1, w_p, tm_c, 1 << 30, batch)

    gate_row = jnp.zeros((1, LANES), F32).at[0, GATE_LANE0:GATE_LANE0 + n_gate].set(m_gate_b[0].reshape(-1))
    mixm = _mlstm(px, pc, batch, m_conv_w[0], m_conv_b[0].reshape(1, -1), gate_row)

    rw = r_k_k.shape[1]
    wg = jnp.zeros((2 * LANES, rw), F32).at[:r_g_up.shape[1]].set(r_g_up[0]).astype(BF16)
    mixr = _rwkv(px, pc, batch, _lora_pairs(r_w_up[0]), _lora_pairs(r_a_up[0]), wg,
                 r_w0[0], r_a0[0], r_k_k, r_k_a, r_r_k[0].reshape(1, rw), r_gn_w, r_gn_b)

    tm_o = 512
    x1, hx2 = _out_proj(mixm, px, mixr, w_out[0].astype(BF16), x.reshape(batch * seq, d), m_norm_g,
                        gt1, g_norm2[0].reshape(1, d), sh2, sc2, tm_o, seq // tm_o)

    tm_f = 512
    out = _conv_ffn(hx2, f_w_up[0].astype(BF16), f_w_gate[0].astype(BF16), f_w_down[0].astype(BF16),
                    f_conv_w[0].reshape(9, -1), f_conv_b, x1, gt2, g_final.reshape(1, d),
                    tm_f, seq // tm_f)
    return out.reshape(batch, seq, d)
```

```python
import functools

import jax
import jax.numpy as jnp
from jax import lax
from jax.experimental import pallas as pl
from jax.experimental.pallas import tpu as pltpu

F32 = jnp.float32
BF16 = jnp.bfloat16
HIGHEST = lax.Precision.HIGHEST

LANES = 128
GRID_W = 64
M_HEADS = 4
M_DQK = 128
M_DV = 256
R_N = 64
NORM_EPS = 1e-6
GN_EPS = 64e-5
MLSTM_CHUNK = 128
RWKV_CHUNK = 64
RWKV_GROUP = 8
RWKV_SCAN_PAIRS = 2
RWKV_PREP_WIDTH = 8
VMEM_LIMIT = 56 * 1024 * 1024

SLAB_Q, SLAB_K, SLAB_V, SLAB_O = 0, 4, 8, 16
SLAB_R, SLAB_KR, SLAB_VR = 24, 32, 40
SLAB_LW, SLAB_LA, SLAB_LG0, SLAB_LG1 = 48, 49, 50, 51
N_SLABS = 52
GATE_LANE0 = 32


def _mm(a, b, precision=None):
    return jnp.dot(a, b, preferred_element_type=F32, precision=precision)


def _mm_nt(a, b):
    return lax.dot_general(a, b, (((1,), (1,)), ((), ())), preferred_element_type=F32)


def _mm_tn(a, b):
    return lax.dot_general(a, b, (((0,), (0,)), ((), ())), preferred_element_type=F32)


def _mm_split(x, ones, ones_first):
    hi = x.astype(BF16)
    rest = x - hi.astype(F32)
    mid = rest.astype(BF16)
    lo = (rest - mid.astype(F32)).astype(BF16)
    if ones_first:
        return _mm(ones, hi) + _mm(ones, mid) + _mm(ones, lo)
    return _mm(hi, ones) + _mm(mid, ones) + _mm(lo, ones)


def _softplus(x):
    return jnp.maximum(x, 0.0) + jnp.log1p(jnp.exp(-jnp.abs(x)))


def _log_sigmoid(x):
    return -_softplus(-x)


def _params(n_axes):
    return pltpu.CompilerParams(dimension_semantics=("arbitrary",) * n_axes,
                                vmem_limit_bytes=VMEM_LIMIT)


def _mod_kernel(cv_ref, w_ref, b_ref, o_ref):
    cv = cv_ref[...]
    s = (cv * jax.nn.sigmoid(cv)).astype(BF16)
    o_ref[...] = _mm(s, w_ref[...].astype(BF16)) + b_ref[...]


def _modulation(cv8, w_mod, b_mod):
    d, n = w_mod.shape
    tn = 1024
    return pl.pallas_call(
        _mod_kernel,
        grid=(n // tn,),
        in_specs=[pl.BlockSpec((8, d), lambda j: (0, 0)),
                  pl.BlockSpec((d, tn), lambda j: (0, j)),
                  pl.BlockSpec((1, tn), lambda j: (0, j))],
        out_specs=pl.BlockSpec((8, tn), lambda j: (0, j)),
        out_shape=jax.ShapeDtypeStruct((8, n), F32),
        compiler_params=_params(1),
    )(cv8, w_mod, b_mod.reshape(1, n))


def _inproj_kernel(tiles_per_row, row0, x_ref, g_ref, sh_ref, sc_ref, w_ref, o_ref, hx_ref):
    i = pl.program_id(0)
    j = pl.program_id(1)

    @pl.when(j == 0)
    def _():
        x = x_ref[...]
        ms = jnp.mean(x * x, axis=-1, keepdims=True)
        y = x * lax.rsqrt(ms + NORM_EPS) * g_ref[...]
        r = row0 + i // tiles_per_row
        hx = y * (1.0 + sc_ref[pl.ds(r, 1), :]) + sh_ref[pl.ds(r, 1), :]
        hx_ref[...] = hx.astype(BF16)

    acc = _mm(hx_ref[...], w_ref[...])
    for s in range(acc.shape[1] // LANES):
        o_ref[s] = acc[:, s * LANES:(s + 1) * LANES]


def _inproj(x2d, g, sh, sc, w_p, tm, tiles_per_row, row0):
    t, d = x2d.shape
    n = w_p.shape[1]
    tn = 512
    return pl.pallas_call(
        functools.partial(_inproj_kernel, tiles_per_row, row0),
        grid=(t // tm, n // tn),
        in_specs=[pl.BlockSpec((tm, d), lambda i, j: (i, 0)),
                  pl.BlockSpec((1, d), lambda i, j: (0, 0)),
                  pl.BlockSpec((8, d), lambda i, j: (0, 0)),
                  pl.BlockSpec((8, d), lambda i, j: (0, 0)),
                  pl.BlockSpec((d, tn), lambda i, j: (0, j))],
        out_specs=pl.BlockSpec((tn // LANES, tm, LANES), lambda i, j: (j, i, 0)),
        out_shape=jax.ShapeDtypeStruct((n // LANES, t, LANES), F32),
        scratch_shapes=[pltpu.VMEM((tm, d), BF16)],
        compiler_params=_params(2),
    )(x2d, g, sh, sc, w_p)


def _mlstm_kernel(n_ctx, n_lat,
                  qx_ref, kx_ref, vx_ref, gx_ref, qc_ref, kc_ref, vc_ref, gc_ref,
                  cwq_ref, cwk_ref, cbq_ref, cbk_ref, gb_ref,
                  o_ref,
                  qs_ref, ks_ref, kst_ref, ct_ref, hf_ref, hb_ref):
    c = MLSTM_CHUNK
    head = pl.program_id(1)
    rid = lax.broadcasted_iota(jnp.int32, (c, LANES), 0)
    lane = lax.broadcasted_iota(jnp.int32, (c, LANES), 1)
    r2 = lax.broadcasted_iota(jnp.int32, (c, c), 0)
    c2 = lax.broadcasted_iota(jnp.int32, (c, c), 1)
    tri = ((c2 <= r2).astype(F32), (c2 >= r2).astype(F32))
    keep = (c2 <= r2, c2 >= r2)
    gbias = gb_ref[...]

    def conv_pass(q_ref, k_ref, nchunk, base):
        nrows = nchunk * c

        def body(ci, carry):
            r0 = pl.multiple_of(ci * c, c)
            p0 = pl.multiple_of(jnp.maximum(r0 - 8, 0), 8)
            n0 = pl.multiple_of(jnp.minimum(r0 + c, nrows - 8), 8)
            for src, w_ref, b_ref, scale, is_k in ((q_ref, cwq_ref, cbq_ref, M_DQK ** -0.5, False),
                                                   (k_ref, cwk_ref, cbk_ref, 1.0, True)):
                cur = src[0, pl.ds(r0, c), :]
                prev_row = jnp.where(ci > 0, src[0, pl.ds(p0, 8), :][7:8, :], 0.0)
                next_row = jnp.where(ci < nchunk - 1, src[0, pl.ds(n0, 8), :][0:1, :], 0.0)
                up = jnp.where(rid == 0, prev_row, pltpu.roll(cur, 1, 0))
                dn = jnp.where(rid == c - 1, next_row, pltpu.roll(cur, c - 1, 0))
                w = w_ref[...]
                y = (up * w[0:1, :] + cur * w[1:2, :] + dn * w[2:3, :] + b_ref[...]) * scale
                if is_k:
                    ks_ref[base + ci] = y.astype(BF16)
                    kst_ref[base + ci] = y.T.astype(BF16)
                else:
                    qs_ref[base + ci] = y.astype(BF16)
            return carry

        lax.fori_loop(0, nchunk, body, 0)

    def step(d, ci, base, v_ref, g_ref, n, m, want_h):
        r0 = pl.multiple_of(ci * c, c)
        q = qs_ref[base + ci]
        k = ks_ref[base + ci]
        kt = kst_ref[base + ci]
        v = jnp.concatenate([v_ref[0, pl.ds(r0, c), :], v_ref[1, pl.ds(r0, c), :]], axis=1).astype(BF16)
        gb = g_ref[0, pl.ds(r0, c), :] + gbias
        icol = GATE_LANE0 + d * M_HEADS + head
        fcol = GATE_LANE0 + 2 * M_HEADS + d * M_HEADS + head
        bc = _mm(tri[d], _log_sigmoid(gb), HIGHEST)
        b_col = jnp.sum(jnp.where(lane == fcol, bc, 0.0), axis=1, keepdims=True)
        i_col = jnp.sum(jnp.where(lane == icol, gb, 0.0), axis=1, keepdims=True)
        b_row = jnp.sum(jnp.where(rid == fcol, bc.T, 0.0), axis=0, keepdims=True)
        i_row = jnp.sum(jnp.where(rid == icol, gb.T, 0.0), axis=0, keepdims=True)
        b_last = b_col[0:1, :] if d else b_col[c - 1:c, :]
        ct = ct_ref[d]
        h = None
        if want_h:
            dmat = jnp.where(keep[d], b_col - b_row + i_row, -jnp.inf)
            m_inter = b_col + m
            m_j = jnp.maximum(m_inter, jnp.max(dmat, axis=-1, keepdims=True))
            s = _mm(q, kt) * jnp.exp(dmat - m_j)
            inter = jnp.exp(m_inter - m_j)
            num = _mm(s.astype(BF16), v) + inter * _mm(q, ct.astype(BF16))
            qn = jnp.sum(q.astype(F32) * n, axis=-1, keepdims=True)
            den = jnp.sum(s, axis=-1, keepdims=True) + inter * qn
            h = num / jnp.maximum(jnp.abs(den), jnp.exp(-m_j))
        glog = b_last - b_col + i_col
        m_new = jnp.maximum(b_last + m, jnp.max(glog, axis=0, keepdims=True))
        wk = jnp.exp(glog - m_new)
        decay = jnp.exp(b_last + m - m_new)
        ct_ref[d] = decay * ct + _mm(kt, (wk * v.astype(F32)).astype(BF16))
        n_new = decay * n + jnp.sum(wk * k.astype(F32), axis=0, keepdims=True)
        return n_new, m_new, h

    def run(nchunk, base, v_ref, g_ref, carry, want_h):
        def body(i, carry):
            nf, mf, nb, mb = carry
            ib = nchunk - 1 - i
            nf, mf, h_f = step(0, i, base, v_ref, g_ref, nf, mf, want_h)
            nb, mb, h_b = step(1, ib, base, v_ref, g_ref, nb, mb, want_h)
            if want_h:
                hf_ref[pl.ds(pl.multiple_of(i * c, c), c), :] = h_f
                hb_ref[pl.ds(pl.multiple_of(ib * c, c), c), :] = h_b
            return nf, mf, nb, mb

        return lax.fori_loop(0, nchunk, body, carry)

    conv_pass(qc_ref, kc_ref, n_ctx, 0)
    conv_pass(qx_ref, kx_ref, n_lat, n_ctx)
    ct_ref[...] = jnp.zeros(ct_ref.shape, F32)
    zn = jnp.zeros((1, M_DQK), F32)
    zm = jnp.zeros((1, 1), F32)
    carry = run(n_ctx, 0, vc_ref, gc_ref, (zn, zm, zn, zm), False)
    run(n_lat, n_ctx, vx_ref, gx_ref, carry, True)

    def fin(i, carry):
        rows = pl.ds(pl.multiple_of(i * c, c), c)
        o_ref[rows, :] = (hf_ref[rows, :] + hb_ref[rows, :]).astype(o_ref.dtype)
        return carry

    lax.fori_loop(0, n_lat, fin, 0)


def _mlstm(px, pc, batch, cw, cb, gate_row):
    lx = px.shape[1] // batch
    lc = pc.shape[1] // batch
    n_lat, n_ctx = lx // MLSTM_CHUNK, lc // MLSTM_CHUNK

    def slab(nrows, first, width=1):
        return pl.BlockSpec((width, nrows, LANES), lambda b, h: (first // width + h, b, 0))

    def fixed(nrows, idx):
        return pl.BlockSpec((1, nrows, LANES), lambda b, h: (idx, b, 0))

    in_specs = [slab(lx, SLAB_Q), slab(lx, SLAB_K), slab(lx, SLAB_V, 2), fixed(lx, SLAB_LG1),
                slab(lc, SLAB_Q), slab(lc, SLAB_K), slab(lc, SLAB_V, 2), fixed(lc, SLAB_LG1),
                pl.BlockSpec((3, LANES), lambda b, h: (0, h)),
                pl.BlockSpec((3, LANES), lambda b, h: (0, M_HEADS + h)),
                pl.BlockSpec((1, LANES), lambda b, h: (0, h)),
                pl.BlockSpec((1, LANES), lambda b, h: (0, M_HEADS + h)),
                pl.BlockSpec((1, LANES), lambda b, h: (0, 0))]
    nch = n_ctx + n_lat
    return pl.pallas_call(
        functools.partial(_mlstm_kernel, n_ctx, n_lat),
        grid=(batch, M_HEADS),
        in_specs=in_specs,
        out_specs=pl.BlockSpec((lx, M_DV), lambda b, h: (b, h)),
        out_shape=jax.ShapeDtypeStruct((batch * lx, M_HEADS * M_DV), BF16),
        scratch_shapes=[pltpu.VMEM((nch, MLSTM_CHUNK, LANES), BF16),
                        pltpu.VMEM((nch, MLSTM_CHUNK, LANES), BF16),
                        pltpu.VMEM((nch, LANES, MLSTM_CHUNK), BF16),
                        pltpu.VMEM((2, M_DQK, M_DV), F32),
                        pltpu.VMEM((lx, M_DV), F32),
                        pltpu.VMEM((lx, M_DV), F32)],
        compiler_params=_params(2),
    )(px, px, px, px, pc, pc, pc, pc, cw, cw, cb, cb, gate_row)


def _rwkv_stack(x):
    head0 = lax.broadcasted_iota(jnp.int32, x.shape, 1) < R_N
    zero = jnp.zeros_like(x)
    return jnp.concatenate([jnp.where(head0, x, zero), jnp.where(head0, zero, x)], axis=0)


def _rwkv_fold(x):
    half = x.shape[0] // 2
    return x[0:half] + x[half:2 * half]


def _rwkv_gate_a(la, wa, a0_ref, d):
    return jax.nn.sigmoid(a0_ref[d:d + 1, :] + _mm(la.astype(BF16), wa)[:, d * LANES:(d + 1) * LANES])


def _rwkv_prep_kernel(want_y, r_ref, k_ref, v_ref, lw_ref, la_ref,
                      ww_ref, wa_ref, w0_ref, a0_ref, kk_ref, ka_ref, *rest):
    extra_refs, out_refs = (rest[:4], rest[4:]) if want_y else ((), rest)
    c = RWKV_CHUNK
    c2 = 2 * c
    stack = _rwkv_stack
    rr = lax.broadcasted_iota(jnp.int32, (c2, c2), 0)
    cc = lax.broadcasted_iota(jnp.int32, (c2, c2), 1)
    same_head = (rr < c) == (cc < c)
    group_ones = same_head.astype(BF16)
    tt = rr & (c - 1)
    ss = cc & (c - 1)
    strict = (ss < tt, ss > tt)
    incl = (ss <= tt, ss >= tt)
    eye = (rr == cc).astype(F32)
    tr = lax.broadcasted_iota(jnp.int32, (c, c), 0)
    tc = lax.broadcasted_iota(jnp.int32, (c, c), 1)
    tri = ((tc <= tr).astype(BF16), (tc >= tr).astype(BF16))
    ww = ww_ref[0]
    wa = wa_ref[0]
    k_k = kk_ref[...]
    k_a = ka_ref[...]
    m_ref, n_ref, dec_ref = out_refs[0], out_refs[1], out_refs[2]

    def prep_chunks(js):
        nj = len(js)
        rows = [pl.ds(pl.multiple_of(j * c, c), c) for j in js]
        ch = [(d, i) for d in (0, 1) for i in range(nj)]
        half = lambda d: slice(d * LANES, (d + 1) * LANES)
        r = [r_ref[0, rw, :] for rw in rows]
        k = [k_ref[0, rw, :] for rw in rows]
        v_s = [stack(v_ref[0, rw, :]).astype(BF16) for rw in rows]
        lo_w = [_mm(jnp.tanh(lw_ref[0, rw, :]).astype(BF16), ww) for rw in rows]
        lo_a = [_mm(la_ref[0, rw, :].astype(BF16), wa) for rw in rows]
        kk = [x * k_k for x in k]
        kk = [x * lax.rsqrt(_mm_split(x * x, group_ones, False) + 1e-12) for x in kk]
        logw = [-jnp.exp(-_softplus(-(w0_ref[d:d + 1, :] + lo_w[i][:, half(d)])) - 0.5) for d, i in ch]
        a = [jax.nn.sigmoid(a0_ref[d:d + 1, :] + lo_a[i][:, half(d)]) for d, i in ch]
        pin = [_mm_split(x, tri[d], True) for (d, i), x in zip(ch, logw)]
        ptot = [x[0:1, :] if d else x[c - 1:c, :] for (d, i), x in zip(ch, pin)]
        kd = [k[i] * (1.0 + (a_ - 1.0) * k_a) for (d, i), a_ in zip(ch, a)]
        kka = [kk[i] * a_ for (d, i), a_ in zip(ch, a)]
        e_inv = [jnp.exp(-x) for x in pin]
        e_end = [jnp.exp(pt - x) for pt, x in zip(ptot, pin)]
        r_f = [stack(r[i] * jnp.exp(x)) for (d, i), x in zip(ch, pin)]
        ar_t = [jnp.concatenate([stack(-kk[i] * jnp.exp(x - lw_)), rf], axis=0).astype(BF16)
                for (d, i), x, lw_, rf in zip(ch, pin, logw, r_f)]
        bk_t = [jnp.concatenate([stack(x * e), stack(y * e)], axis=0).astype(BF16)
                for x, y, e in zip(kka, kd, e_inv)]
        bk_end = [jnp.concatenate([stack(x * e), stack(y * e)], axis=0).astype(BF16)
                  for x, y, e in zip(kka, kd, e_end)]
        aa = [_mm_nt(x, y) for x, y in zip(ar_t, bk_t)]
        aab = [jnp.where(strict[d], x[0:c2, 0:c2], 0.0) for (d, i), x in zip(ch, aa)]
        aak = [jnp.where(strict[d], x[0:c2, c2:2 * c2], 0.0).astype(BF16) for (d, i), x in zip(ch, aa)]
        akv = [_mm(x, v_s[i]).astype(BF16) for (d, i), x in zip(ch, aak)]
        xs = [eye + x for x in aab]
        pb = [x.astype(BF16) for x in aab]
        ps = [_mm(x, x) for x in pb]
        for _ in range(c.bit_length() - 3):
            pb = [x.astype(BF16) for x in ps]
            both = [_mm(jnp.concatenate([x.astype(BF16), p], axis=0), p) for x, p in zip(xs, pb)]
            xs = [x + y[0:c2] for x, y in zip(xs, both)]
            ps = [y[c2:2 * c2] for y in both]
        xs = [(x + _mm(x.astype(BF16), p.astype(BF16))).astype(BF16) for x, p in zip(xs, ps)]
        wt = [_mm(x, y[0:c2]).astype(BF16) for x, y in zip(xs, ar_t)]
        u0 = [_mm(x, y).astype(BF16) for x, y in zip(xs, akv)]
        m_mat = [_mm_tn(x, y[0:c2]) for x, y in zip(wt, bk_end)]
        n_mat = [_mm_tn(jnp.concatenate([x, v_s[i]], axis=0), y) for (d, i), x, y in zip(ch, u0, bk_end)]
        for (d, i), mm_, nn_, pt in zip(ch, m_mat, n_mat, ptot):
            slot = (0, 0, d, js[i])
            m_ref[slot] = _rwkv_fold(mm_).astype(BF16)
            n_ref[slot] = _rwkv_fold(nn_)
            dec_ref[slot] = jnp.broadcast_to(jnp.exp(pt), (8, LANES))
        if want_y:
            arb = [jnp.where(incl[d], x[c2:2 * c2, 0:c2], 0.0).astype(BF16) for (d, i), x in zip(ch, aa)]
            ark = [jnp.where(incl[d], x[c2:2 * c2, c2:2 * c2], 0.0).astype(BF16) for (d, i), x in zip(ch, aa)]
            qy = [_mm(x, jnp.concatenate([w_, u_], axis=1)) for x, w_, u_ in zip(arb, wt, u0)]
            yl = [x[:, c2:2 * c2] + _mm(y, v_s[i]) for (d, i), x, y in zip(ch, qy, ark)]
            for (d, i), rf, x, y in zip(ch, r_f, qy, yl):
                slot = (0, 0, d, js[i])
                out_refs[3][slot] = _rwkv_fold(rf + x[:, 0:c2]).astype(BF16)
                out_refs[4][slot] = _rwkv_fold(y)
            lg0_ref, lg1_ref, wg_ref, rk_ref = extra_refs
            ksum = [k[i] * (2.0 + (a[i] + a[nj + i] - 2.0) * k_a) for i in range(nj)]
            bonus = [_mm_split(r[i] * ksum[i] * rk_ref[...], group_ones, False) * v_ref[0, rows[i], :]
                     for i in range(nj)]
            gate = [_mm(jax.nn.sigmoid(jnp.concatenate([lg0_ref[0, rw, :], lg1_ref[0, rw, :]], axis=1)
                                       ).astype(BF16), wg_ref[...]) for rw in rows]
            for rw, gt, bn in zip(rows, gate, bonus):
                out_refs[5][rw, :] = gt.astype(BF16)
                out_refs[6][rw, :] = (bn * gt).astype(BF16)

    n_chunks = r_ref.shape[1] // c
    width = min(RWKV_PREP_WIDTH, n_chunks)
    assert n_chunks % width == 0

    def body(t, carry):
        prep_chunks([t * width + u for u in range(width)])
        return carry

    lax.fori_loop(0, n_chunks // width, body, 0)


def _rwkv_scan_kernel(mc_ref, nc_ref, dc_ref, mf_ref, nf_ref, df_ref, qf_ref, ylf_ref,
                      mb_ref, nb_ref, db_ref, qb_ref, ylb_ref,
                      gate_ref, bg_ref, gnw_ref, gnb_ref,
                      o_ref, z_ref, yf_ref, yb_ref):
    c = RWKV_CHUNK
    g = pl.program_id(2)
    n_groups = pl.num_programs(2)
    group = mf_ref.shape[3]
    n_ctx = mc_ref.shape[3]
    pairs = range(RWKV_SCAN_PAIRS)

    def advance(z, m_c, n_c, dec):
        return z * dec[0:1, :] + _mm(z.astype(BF16), _rwkv_stack(m_c)) + _rwkv_stack(n_c)

    @pl.when(g == 0)
    def _():
        z_ref[...] = jnp.zeros(z_ref.shape, F32)

        def ctx_body(i, carry):
            for pp in pairs:
                for d, ii in ((0, i), (1, n_ctx - 1 - i)):
                    zi = 2 * pp + d
                    z_ref[zi] = advance(z_ref[zi], mc_ref[0, pp, d, ii], nc_ref[0, pp, d, ii],
                                        dc_ref[0, pp, d, ii])
            return carry

        lax.fori_loop(0, n_ctx, ctx_body, 0)

    def lat_body(j, carry):
        chains = [(pp,) + t for pp in pairs for t in (
            (0, j, g * group, mf_ref, nf_ref, df_ref, qf_ref, ylf_ref, yf_ref),
            (1, group - 1 - j, (n_groups - 1 - g) * group, mb_ref, nb_ref, db_ref, qb_ref, ylb_ref, yb_ref))]
        zs = [z_ref[2 * pp + d] for pp, d, *_ in chains]
        zb = [z.astype(BF16) for z in zs]
        ys = [_mm_nt(_rwkv_stack(q_ref[0, pp, 0, jj]), z)
              for (pp, d, jj, first, m_ref, n_ref, d_ref, q_ref, yl_ref, y_ref), z in zip(chains, zb)]
        zn = [z * d_ref[0, pp, 0, jj][0:1, :] + _mm(zh, _rwkv_stack(m_ref[0, pp, 0, jj]))
              + _rwkv_stack(n_ref[0, pp, 0, jj])
              for (pp, d, jj, first, m_ref, n_ref, d_ref, q_ref, yl_ref, y_ref), z, zh in zip(chains, zs, zb)]
        for (pp, d, jj, first, m_ref, n_ref, d_ref, q_ref, yl_ref, y_ref), y, z in zip(chains, ys, zn):
            z_ref[2 * pp + d] = z
            y_ref[pl.ds(pl.multiple_of((first + jj) * c, c), c), pp * LANES:(pp + 1) * LANES] = (
                _rwkv_fold(y) + yl_ref[0, pp, 0, jj])
        return carry

    lax.fori_loop(0, group, lat_body, 0)

    @pl.when(g == n_groups - 1)
    def _():
        fr = 4 * c
        r2 = lax.broadcasted_iota(jnp.int32, (LANES, LANES), 0)
        c2 = lax.broadcasted_iota(jnp.int32, (LANES, LANES), 1)
        group_ones = ((r2 < R_N) == (c2 < R_N)).astype(BF16)
        inv_n = 1.0 / R_N

        def fin(i, carry):
            rows = pl.ds(pl.multiple_of(i * fr, fr), fr)
            for pp in pairs:
                cols = slice(pp * LANES, (pp + 1) * LANES)
                y = yf_ref[rows, cols] + yb_ref[rows, cols]
                sums = _mm_split(jnp.concatenate([y, y * y], axis=0), group_ones, False)
                mu = sums[0:fr] * inv_n
                var = sums[fr:2 * fr] * inv_n - mu * mu
                yn = (y - mu) * lax.rsqrt(var + GN_EPS) * gnw_ref[:, cols] + gnb_ref[:, cols]
                o_ref[rows, cols] = (yn * gate_ref[rows, cols].astype(F32)
                                     + bg_ref[rows, cols].astype(F32)).astype(o_ref.dtype)
            return carry

        lax.fori_loop(0, o_ref.shape[0] // fr, fin, 0)


def _rwkv_prep(p_all, batch, group, want_y, ww, wa, w0, a0, k_k, k_a, wg=None, r_k=None):
    c = RWKV_CHUNK
    length = p_all.shape[1] // batch
    n_chunks = length // c
    n_groups = n_chunks // group
    n_pairs = ww.shape[0]
    rows = group * c

    def slab(first):
        return pl.BlockSpec((1, rows, LANES), lambda b, p, g: (first + p, b * n_groups + g, 0))

    def fixed(idx):
        return pl.BlockSpec((1, rows, LANES), lambda b, p, g: (idx, b * n_groups + g, 0))

    def vec(nrows):
        return pl.BlockSpec((nrows, LANES), lambda b, p, g: (0, p))

    lora = pl.BlockSpec((1, LANES, 2 * LANES), lambda b, p, g: (p, 0, 0))
    outs = [(c, BF16), (c, F32), (8, F32)] + ([(c, BF16), (c, F32)] if want_y else [])
    in_specs = [slab(SLAB_R), slab(SLAB_KR), slab(SLAB_VR), fixed(SLAB_LW), fixed(SLAB_LA),
                lora, lora, vec(2), vec(2), vec(1), vec(1)]
    operands = [p_all, p_all, p_all, p_all, p_all, ww, wa, w0, a0, k_k, k_a]
    out_specs = [pl.BlockSpec((1, 1, 2, group, nr, LANES), lambda b, p, g: (b, p, 0, g, 0, 0)) for nr, _ in outs]
    out_shape = [jax.ShapeDtypeStruct((batch, n_pairs, 2, n_chunks, nr, LANES), dt) for nr, dt in outs]
    if want_y:
        in_specs += [fixed(SLAB_LG0), fixed(SLAB_LG1),
                     pl.BlockSpec((2 * LANES, LANES), lambda b, p, g: (0, p)), vec(1)]
        operands += [p_all, p_all, wg, r_k]
        out_specs += [pl.BlockSpec((rows, LANES), lambda b, p, g: (b * n_groups + g, p))] * 2
        out_shape += [jax.ShapeDtypeStruct((batch * length, n_pairs * LANES), BF16)] * 2
    return pl.pallas_call(
        functools.partial(_rwkv_prep_kernel, want_y),
        grid=(batch, n_pairs, n_groups),
        in_specs=in_specs,
        out_specs=out_specs,
        out_shape=out_shape,
        compiler_params=_params(3),
    )(*operands)


def _rwkv_scan(ops_c, ops_x, gate, bonus_gated, batch, group, gn_w, gn_b):
    c = RWKV_CHUNK
    sp = RWKV_SCAN_PAIRS
    lx = gate.shape[0] // batch
    n_pairs, n_ctx = ops_c[0].shape[1], ops_c[0].shape[3]
    n_groups = ops_x[0].shape[3] // group
    assert n_pairs % sp == 0

    def ctx_block(a):
        return pl.BlockSpec((1, sp, 2, n_ctx, a.shape[4], LANES), lambda b, p, g: (b, p, 0, 0, 0, 0))

    def fwd_block(a):
        return pl.BlockSpec((1, sp, 1, group, a.shape[4], LANES), lambda b, p, g: (b, p, 0, g, 0, 0))

    def bwd_block(a):
        return pl.BlockSpec((1, sp, 1, group, a.shape[4], LANES),
                            lambda b, p, g: (b, p, 1, n_groups - 1 - g, 0, 0))

    tokens = pl.BlockSpec((lx, sp * LANES), lambda b, p, g: (b, p))
    vec = pl.BlockSpec((1, sp * LANES), lambda b, p, g: (0, p))
    return pl.pallas_call(
        _rwkv_scan_kernel,
        grid=(batch, n_pairs // sp, n_groups),
        in_specs=[ctx_block(a) for a in ops_c] + [fwd_block(a) for a in ops_x] + [bwd_block(a) for a in ops_x]
                 + [tokens, tokens, vec, vec],
        out_specs=tokens,
        out_shape=jax.ShapeDtypeStruct((batch * lx, n_pairs * LANES), BF16),
        scratch_shapes=[pltpu.VMEM((2 * sp, 2 * c, LANES), F32),
                        pltpu.VMEM((lx, sp * LANES), F32), pltpu.VMEM((lx, sp * LANES), F32)],
        compiler_params=_params(3),
    )(*ops_c, *ops_x, *ops_x, gate, bonus_gated, gn_w, gn_b)


def _rwkv(px, pc, batch, ww, wa, wg, w0, a0, k_k, k_a, r_k, gn_w, gn_b):
    c = RWKV_CHUNK
    n_ctx = pc.shape[1] // batch // c
    n_lat = px.shape[1] // batch // c
    group = min(RWKV_GROUP, n_lat)
    assert n_lat % group == 0
    ops_c = _rwkv_prep(pc, batch, n_ctx, False, ww, wa, w0, a0, k_k, k_a)
    *ops_x, gate, bonus_gated = _rwkv_prep(px, batch, group, True, ww, wa, w0, a0, k_k, k_a, wg, r_k)
    return _rwkv_scan(ops_c, ops_x, gate, bonus_gated, batch, group, gn_w, gn_b)


def _out_kernel(tiles_per_batch, mixm_ref, om_ref, mixr_ref, w_ref, x_ref, ng_ref,
                gt1_ref, g2_ref, sh2_ref, sc2_ref, x1_ref, hx2_ref):
    b = pl.program_id(0) // tiles_per_batch
    hm = mixm_ref[...].astype(F32)
    ng = ng_ref[...]
    parts = []
    for h in range(M_HEADS):
        cols = slice(h * M_DV, (h + 1) * M_DV)
        seg = hm[:, cols]
        seg = seg * lax.rsqrt(jnp.mean(seg * seg, axis=-1, keepdims=True) + NORM_EPS)
        og = jnp.concatenate([om_ref[2 * h], om_ref[2 * h + 1]], axis=1)
        parts.append((seg * ng[:, cols] * jax.nn.sigmoid(og)).astype(BF16))
    lhs = jnp.concatenate(parts + [mixr_ref[...]], axis=1)
    x1 = x_ref[...] + gt1_ref[pl.ds(b, 1), :] * _mm(lhs, w_ref[...])
    x1_ref[...] = x1
    y = x1 * lax.rsqrt(jnp.mean(x1 * x1, axis=-1, keepdims=True) + NORM_EPS) * g2_ref[...]
    hx2_ref[...] = (y * (1.0 + sc2_ref[pl.ds(b, 1), :]) + sh2_ref[pl.ds(b, 1), :]).astype(BF16)


def _out_proj(mixm, px, mixr, w_out, x2d, ng, gt1, g2, sh2, sc2, tm, tiles_per_batch):
    t, d = x2d.shape
    dm = mixm.shape[1]
    row = lambda i: (i, 0)
    const = lambda i: (0, 0)
    return pl.pallas_call(
        functools.partial(_out_kernel, tiles_per_batch),
        grid=(t // tm,),
        in_specs=[pl.BlockSpec((tm, dm), row),
                  pl.BlockSpec((8, tm, LANES), lambda i: (SLAB_O // 8, i, 0)),
                  pl.BlockSpec((tm, dm), row),
                  pl.BlockSpec((d, d), const),
                  pl.BlockSpec((tm, d), row),
                  pl.BlockSpec((1, dm), const),
                  pl.BlockSpec((8, d), const),
                  pl.BlockSpec((1, d), const),
                  pl.BlockSpec((8, d), const),
                  pl.BlockSpec((8, d), const)],
        out_specs=[pl.BlockSpec((tm, d), row), pl.BlockSpec((tm, d), row)],
        out_shape=[jax.ShapeDtypeStruct((t, d), F32), jax.ShapeDtypeStruct((t, d), BF16)],
        compiler_params=_params(1),
    )(mixm, px, mixr, w_out, x2d, ng, gt1, g2, sh2, sc2)


def _ffn_kernel(tiles_per_img, hx_ref, top_ref, bot_ref, wu_ref, wg_ref, wd_ref, cw_ref, cb_ref,
                x1_ref, gt2_ref, gf_ref, o_ref, acc_ref):
    i = pl.program_id(0)
    j = pl.program_id(1)
    tm = hx_ref.shape[0]
    ti = i % tiles_per_img

    @pl.when(j == 0)
    def _():
        acc_ref[...] = jnp.zeros(acc_ref.shape, F32)

    hx = hx_ref[...]
    top = jnp.where(ti > 0, top_ref[...], jnp.zeros_like(top_ref[...]))
    bot = jnp.where(ti < tiles_per_img - 1, bot_ref[...], jnp.zeros_like(bot_ref[...]))
    u = _mm(jnp.concatenate([top, hx, bot], axis=0), wu_ref[...])
    nr = u.shape[0]
    col = lax.broadcasted_iota(jnp.int32, u.shape, 0) & (GRID_W - 1)
    ul = jnp.where(col == 0, 0.0, pltpu.roll(u, 1, 0))
    ur = jnp.where(col == GRID_W - 1, 0.0, pltpu.roll(u, nr - 1, 0))
    cw = cw_ref[...]
    conv = cb_ref[...]
    for dy in range(3):
        rows = slice(dy * GRID_W, dy * GRID_W + tm)
        conv = (conv + ul[rows] * cw[3 * dy:3 * dy + 1, :] + u[rows] * cw[3 * dy + 1:3 * dy + 2, :]
                + ur[rows] * cw[3 * dy + 2:3 * dy + 3, :])
    gelu = 0.5 * conv * (1.0 + jnp.tanh(0.7978845608028654 * (conv + 0.044715 * conv * conv * conv)))
    act = (gelu * _mm(hx, wg_ref[...])).astype(BF16)
    acc_ref[...] += _mm(act, wd_ref[...])

    @pl.when(j == pl.num_programs(1) - 1)
    def _():
        b = i // tiles_per_img
        x2 = x1_ref[...] + gt2_ref[pl.ds(b, 1), :] * acc_ref[...]
        o_ref[...] = x2 * lax.rsqrt(jnp.mean(x2 * x2, axis=-1, keepdims=True) + NORM_EPS) * gf_ref[...]


def _conv_ffn(hx2, w_up, w_gate, w_down, cw, cb, x1, gt2, g_final, tm, tiles_per_img):
    t, d = hx2.shape
    f = w_up.shape[1]
    tf = 512
    rows_per_tile = tm // GRID_W
    n_rows = t // GRID_W
    return pl.pallas_call(
        functools.partial(_ffn_kernel, tiles_per_img),
        grid=(t // tm, f // tf),
        in_specs=[pl.BlockSpec((tm, d), lambda i, j: (i, 0)),
                  pl.BlockSpec((GRID_W, d), lambda i, j: (jnp.maximum(i * rows_per_tile - 1, 0), 0)),
                  pl.BlockSpec((GRID_W, d), lambda i, j: (jnp.minimum((i + 1) * rows_per_tile, n_rows - 1), 0)),
                  pl.BlockSpec((d, tf), lambda i, j: (0, j)),
                  pl.BlockSpec((d, tf), lambda i, j: (0, j)),
                  pl.BlockSpec((tf, d), lambda i, j: (j, 0)),
                  pl.BlockSpec((9, tf), lambda i, j: (0, j)),
                  pl.BlockSpec((1, tf), lambda i, j: (0, j)),
                  pl.BlockSpec((tm, d), lambda i, j: (i, 0)),
                  pl.BlockSpec((8, d), lambda i, j: (0, 0)),
                  pl.BlockSpec((1, d), lambda i, j: (0, 0))],
        out_specs=pl.BlockSpec((tm, d), lambda i, j: (i, 0)),
        out_shape=jax.ShapeDtypeStruct((t, d), F32),
        scratch_shapes=[pltpu.VMEM((tm, d), F32)],
        compiler_params=_params(2),
    )(hx2, hx2, hx2, w_up, w_gate, w_down, cw, cb, x1, gt2, g_final)


def _lora_pairs(up):
    _, rank, width = up.shape
    u = up.reshape(2, rank, width // LANES, LANES).transpose(2, 0, 1, 3)
    z = jnp.zeros_like(u[:, 0])
    top = jnp.concatenate([u[:, 0], z], axis=-1)
    bot = jnp.concatenate([z, u[:, 1]], axis=-1)
    return jnp.concatenate([top, bot], axis=1).astype(BF16)


def kernel(x, c, ctx, c_ctx, w_mod, b_mod, g_norm1, g_norm2, w_in, m_conv_w, m_conv_b, m_gate_b, m_norm_g, r_w0, r_w_up, r_a0, r_a_up, r_g_up, r_k_k, r_k_a, r_r_k, r_gn_w, r_gn_b, w_out, f_w_up, f_w_gate, f_conv_w, f_conv_b, f_w_down, g_final):
    batch, seq, d = x.shape
    ctx_len = ctx.shape[1]
    assert w_mod.shape[0] == 1, "single-layer block"
    assert batch + 1 <= 8 and seq % 512 == 0 and ctx_len % MLSTM_CHUNK == 0

    cv8 = jnp.zeros((8, d), F32).at[:batch].set(c).at[batch].set(c_ctx)
    mod = _modulation(cv8, w_mod[0], b_mod[0])
    sh1, sc1, gt1, sh2, sc2, gt2 = (mod[:, k * d:(k + 1) * d] for k in range(6))

    w = w_in[0].astype(BF16)
    n_gate = 4 * M_HEADS
    g0 = 3 * 1024
    w_p = jnp.concatenate([w[:, :g0], w[:, g0 + n_gate:], w[:, g0:g0 + n_gate],
                           jnp.zeros((d, N_SLABS * LANES - w.shape[1]), BF16)], axis=1)
    g1 = g_norm1[0].reshape(1, d)
    tm_x = 1024 if seq % 1024 == 0 else 512
    px = _inproj(x.reshape(batch * seq, d), g1, sh1, sc1, w_p, tm_x, seq // tm_x, 0)
    tm_c = ctx_len
    pc = _inproj(ctx.reshape(batch * ctx_len, d), g1, sh1, sc1, w_p, tm_c, 1 << 30, batch)

    gate_row = jnp.zeros((1, LANES), F32).at[0, GATE_LANE0:GATE_LANE0 + n_gate].set(m_gate_b[0].reshape(-1))
    mixm = _mlstm(px, pc, batch, m_conv_w[0], m_conv_b[0].reshape(1, -1), gate_row)

    rw = r_k_k.shape[1]
    wg = jnp.zeros((2 * LANES, rw), F32).at[:r_g_up.shape[1]].set(r_g_up[0]).astype(BF16)
    mixr = _rwkv(px, pc, batch, _lora_pairs(r_w_up[0]), _lora_pairs(r_a_up[0]), wg,
                 r_w0[0], r_a0[0], r_k_k, r_k_a, r_r_k[0].reshape(1, rw), r_gn_w, r_gn_b)

    tm_o = 512
    x1, hx2 = _out_proj(mixm, px, mixr, w_out[0].astype(BF16), x.reshape(batch * seq, d), m_norm_g,
                        gt1, g_norm2[0].reshape(1, d), sh2, sc2, tm_o, seq // tm_o)

    tm_f = 512
    out = _conv_ffn(hx2, f_w_up[0].astype(BF16), f_w_gate[0].astype(BF16), f_w_down[0].astype(BF16),
                    f_conv_w[0].reshape(9, -1), f_conv_b, x1, gt2, g_final.reshape(1, d),
                    tm_f, seq // tm_f)
    return out.reshape(batch, seq, d)
```

```python
import functools

import jax
import jax.numpy as jnp
from jax import lax
from jax.experimental import pallas as pl
from jax.experimental.pallas import tpu as pltpu

F32 = jnp.float32
BF16 = jnp.bfloat16
HIGHEST = lax.Precision.HIGHEST

LANES = 128
GRID_W = 64
M_HEADS = 4
M_DQK = 128
M_DV = 256
R_N = 64
NORM_EPS = 1e-6
GN_EPS = 64e-5
MLSTM_CHUNK = 128
RWKV_CHUNK = 64
RWKV_GROUP = 8
RWKV_SCAN_PAIRS = 2
RWKV_PREP_WIDTH = 8
VMEM_LIMIT = 56 * 1024 * 1024

SLAB_Q, SLAB_K, SLAB_V, SLAB_O = 0, 4, 8, 16
SLAB_R, SLAB_KR, SLAB_VR = 24, 32, 40
SLAB_LW, SLAB_LA, SLAB_LG0, SLAB_LG1 = 48, 49, 50, 51
N_SLABS = 52
GATE_LANE0 = 32


def _mm(a, b, precision=None):
    return jnp.dot(a, b, preferred_element_type=F32, precision=precision)


def _mm_nt(a, b):
    return lax.dot_general(a, b, (((1,), (1,)), ((), ())), preferred_element_type=F32)


def _mm_tn(a, b):
    return lax.dot_general(a, b, (((0,), (0,)), ((), ())), preferred_element_type=F32)


def _mm_split(x, ones, ones_first):
    hi = x.astype(BF16)
    rest = x - hi.astype(F32)
    mid = rest.astype(BF16)
    lo = (rest - mid.astype(F32)).astype(BF16)
    if ones_first:
        return _mm(ones, hi) + _mm(ones, mid) + _mm(ones, lo)
    return _mm(hi, ones) + _mm(mid, ones) + _mm(lo, ones)


def _softplus(x):
    return jnp.maximum(x, 0.0) + jnp.log1p(jnp.exp(-jnp.abs(x)))


def _log_sigmoid(x):
    return -_softplus(-x)


def _params(n_axes):
    return pltpu.CompilerParams(dimension_semantics=("arbitrary",) * n_axes,
                                vmem_limit_bytes=VMEM_LIMIT)


def _mod_kernel(cv_ref, w_ref, b_ref, o_ref):
    cv = cv_ref[...]
    s = (cv * jax.nn.sigmoid(cv)).astype(BF16)
    o_ref[...] = _mm(s, w_ref[...].astype(BF16)) + b_ref[...]


def _modulation(cv8, w_mod, b_mod):
    d, n = w_mod.shape
    tn = 1024
    return pl.pallas_call(
        _mod_kernel,
        grid=(n // tn,),
        in_specs=[pl.BlockSpec((8, d), lambda j: (0, 0)),
                  pl.BlockSpec((d, tn), lambda j: (0, j)),
                  pl.BlockSpec((1, tn), lambda j: (0, j))],
        out_specs=pl.BlockSpec((8, tn), lambda j: (0, j)),
        out_shape=jax.ShapeDtypeStruct((8, n), F32),
        compiler_params=_params(1),
    )(cv8, w_mod, b_mod.reshape(1, n))


def _permute_w_kernel(n_main, n_gate, w_ref, o_ref):
    w = w_ref[...]
    pad = jnp.zeros((w.shape[0], o_ref.shape[1] - w.shape[1]), F32)
    o_ref[...] = jnp.concatenate([w[:, :n_main], w[:, n_main + n_gate:], w[:, n_main:n_main + n_gate], pad],
                                 axis=1).astype(BF16)


def _permute_w(w, n_main, n_gate, n_out):
    d, n = w.shape
    tr = 128
    return pl.pallas_call(
        functools.partial(_permute_w_kernel, n_main, n_gate),
        grid=(d // tr,),
        in_specs=[pl.BlockSpec((tr, n), lambda i: (i, 0))],
        out_specs=pl.BlockSpec((tr, n_out), lambda i: (i, 0)),
        out_shape=jax.ShapeDtypeStruct((d, n_out), BF16),
        compiler_params=_params(1),
    )(w)


def _inproj_kernel(tiles_per_row, row0, x_ref, g_ref, sh_ref, sc_ref, w_ref, o_ref, hx_ref):
    i = pl.program_id(0)
    j = pl.program_id(1)

    @pl.when(j == 0)
    def _():
        x = x_ref[...]
        ms = jnp.mean(x * x, axis=-1, keepdims=True)
        y = x * lax.rsqrt(ms + NORM_EPS) * g_ref[...]
        r = row0 + i // tiles_per_row
        hx = y * (1.0 + sc_ref[pl.ds(r, 1), :]) + sh_ref[pl.ds(r, 1), :]
        hx_ref[...] = hx.astype(BF16)

    acc = _mm(hx_ref[...], w_ref[...])
    for s in range(acc.shape[1] // LANES):
        o_ref[s] = acc[:, s * LANES:(s + 1) * LANES]


def _inproj(x2d, g, sh, sc, w_p, tm, tiles_per_row, row0):
    t, d = x2d.shape
    n = w_p.shape[1]
    tn = 512
    return pl.pallas_call(
        functools.partial(_inproj_kernel, tiles_per_row, row0),
        grid=(t // tm, n // tn),
        in_specs=[pl.BlockSpec((tm, d), lambda i, j: (i, 0)),
                  pl.BlockSpec((1, d), lambda i, j: (0, 0)),
                  pl.BlockSpec((8, d), lambda i, j: (0, 0)),
                  pl.BlockSpec((8, d), lambda i, j: (0, 0)),
                  pl.BlockSpec((d, tn), lambda i, j: (0, j))],
        out_specs=pl.BlockSpec((tn // LANES, tm, LANES), lambda i, j: (j, i, 0)),
        out_shape=jax.ShapeDtypeStruct((n // LANES, t, LANES), F32),
        scratch_shapes=[pltpu.VMEM((tm, d), BF16)],
        compiler_params=_params(2),
    )(x2d, g, sh, sc, w_p)


def _mlstm_kernel(n_ctx, n_lat,
                  qx_ref, kx_ref, vx_ref, gx_ref, qc_ref, kc_ref, vc_ref, gc_ref,
                  cwq_ref, cwk_ref, cbq_ref, cbk_ref, gb_ref,
                  o_ref,
                  qs_ref, ks_ref, kst_ref, ct_ref, hf_ref, hb_ref):
    c = MLSTM_CHUNK
    head = pl.program_id(1)
    rid = lax.broadcasted_iota(jnp.int32, (c, LANES), 0)
    lane = lax.broadcasted_iota(jnp.int32, (c, LANES), 1)
    r2 = lax.broadcasted_iota(jnp.int32, (c, c), 0)
    c2 = lax.broadcasted_iota(jnp.int32, (c, c), 1)
    tri = ((c2 <= r2).astype(F32), (c2 >= r2).astype(F32))
    keep = (c2 <= r2, c2 >= r2)
    gbias = gb_ref[...]

    def conv_pass(q_ref, k_ref, nchunk, base):
        nrows = nchunk * c

        def body(ci, carry):
            r0 = pl.multiple_of(ci * c, c)
            p0 = pl.multiple_of(jnp.maximum(r0 - 8, 0), 8)
            n0 = pl.multiple_of(jnp.minimum(r0 + c, nrows - 8), 8)
            for src, w_ref, b_ref, scale, is_k in ((q_ref, cwq_ref, cbq_ref, M_DQK ** -0.5, False),
                                                   (k_ref, cwk_ref, cbk_ref, 1.0, True)):
                cur = src[0, pl.ds(r0, c), :]
                prev_row = jnp.where(ci > 0, src[0, pl.ds(p0, 8), :][7:8, :], 0.0)
                next_row = jnp.where(ci < nchunk - 1, src[0, pl.ds(n0, 8), :][0:1, :], 0.0)
                up = jnp.where(rid == 0, prev_row, pltpu.roll(cur, 1, 0))
                dn = jnp.where(rid == c - 1, next_row, pltpu.roll(cur, c - 1, 0))
                w = w_ref[...]
                y = (up * w[0:1, :] + cur * w[1:2, :] + dn * w[2:3, :] + b_ref[...]) * scale
                if is_k:
                    ks_ref[base + ci] = y.astype(BF16)
                    kst_ref[base + ci] = y.T.astype(BF16)
                else:
                    qs_ref[base + ci] = y.astype(BF16)
            return carry

        lax.fori_loop(0, nchunk, body, 0)

    def step(d, ci, base, v_ref, g_ref, n, m, want_h):
        r0 = pl.multiple_of(ci * c, c)
        q = qs_ref[base + ci]
        k = ks_ref[base + ci]
        kt = kst_ref[base + ci]
        v = jnp.concatenate([v_ref[0, pl.ds(r0, c), :], v_ref[1, pl.ds(r0, c), :]], axis=1).astype(BF16)
        gb = g_ref[0, pl.ds(r0, c), :] + gbias
        icol = GATE_LANE0 + d * M_HEADS + head
        fcol = GATE_LANE0 + 2 * M_HEADS + d * M_HEADS + head
        bc = _mm(tri[d], _log_sigmoid(gb), HIGHEST)
        b_col = jnp.sum(jnp.where(lane == fcol, bc, 0.0), axis=1, keepdims=True)
        i_col = jnp.sum(jnp.where(lane == icol, gb, 0.0), axis=1, keepdims=True)
        b_row = jnp.sum(jnp.where(rid == fcol, bc.T, 0.0), axis=0, keepdims=True)
        i_row = jnp.sum(jnp.where(rid == icol, gb.T, 0.0), axis=0, keepdims=True)
        b_last = b_col[0:1, :] if d else b_col[c - 1:c, :]
        ct = ct_ref[d]
        h = None
        if want_h:
            dmat = jnp.where(keep[d], b_col - b_row + i_row, -jnp.inf)
            m_inter = b_col + m
            m_j = jnp.maximum(m_inter, jnp.max(dmat, axis=-1, keepdims=True))
            s = _mm(q, kt) * jnp.exp(dmat - m_j)
            inter = jnp.exp(m_inter - m_j)
            num = _mm(s.astype(BF16), v) + inter * _mm(q, ct.astype(BF16))
            qn = jnp.sum(q.astype(F32) * n, axis=-1, keepdims=True)
            den = jnp.sum(s, axis=-1, keepdims=True) + inter * qn
            h = num / jnp.maximum(jnp.abs(den), jnp.exp(-m_j))
        glog = b_last - b_col + i_col
        m_new = jnp.maximum(b_last + m, jnp.max(glog, axis=0, keepdims=True))
        wk = jnp.exp(glog - m_new)
        decay = jnp.exp(b_last + m - m_new)
        ct_ref[d] = decay * ct + _mm(kt, (wk * v.astype(F32)).astype(BF16))
        n_new = decay * n + jnp.sum(wk * k.astype(F32), axis=0, keepdims=True)
        return n_new, m_new, h

    def run(nchunk, base, v_ref, g_ref, carry, want_h):
        def body(i, carry):
            nf, mf, nb, mb = carry
            ib = nchunk - 1 - i
            nf, mf, h_f = step(0, i, base, v_ref, g_ref, nf, mf, want_h)
            nb, mb, h_b = step(1, ib, base, v_ref, g_ref, nb, mb, want_h)
            if want_h:
                hf_ref[pl.ds(pl.multiple_of(i * c, c), c), :] = h_f
                hb_ref[pl.ds(pl.multiple_of(ib * c, c), c), :] = h_b
            return nf, mf, nb, mb

        return lax.fori_loop(0, nchunk, body, carry)

    conv_pass(qc_ref, kc_ref, n_ctx, 0)
    conv_pass(qx_ref, kx_ref, n_lat, n_ctx)
    ct_ref[...] = jnp.zeros(ct_ref.shape, F32)
    zn = jnp.zeros((1, M_DQK), F32)
    zm = jnp.zeros((1, 1), F32)
    carry = run(n_ctx, 0, vc_ref, gc_ref, (zn, zm, zn, zm), False)
    run(n_lat, n_ctx, vx_ref, gx_ref, carry, True)

    def fin(i, carry):
        rows = pl.ds(pl.multiple_of(i * c, c), c)
        o_ref[rows, :] = (hf_ref[rows, :] + hb_ref[rows, :]).astype(o_ref.dtype)
        return carry

    lax.fori_loop(0, n_lat, fin, 0)


def _mlstm(px, pc, batch, cw, cb, gate_row):
    lx = px.shape[1] // batch
    lc = pc.shape[1] // batch
    n_lat, n_ctx = lx // MLSTM_CHUNK, lc // MLSTM_CHUNK

    def slab(nrows, first, width=1):
        return pl.BlockSpec((width, nrows, LANES), lambda b, h: (first // width + h, b, 0))

    def fixed(nrows, idx):
        return pl.BlockSpec((1, nrows, LANES), lambda b, h: (idx, b, 0))

    in_specs = [slab(lx, SLAB_Q), slab(lx, SLAB_K), slab(lx, SLAB_V, 2), fixed(lx, SLAB_LG1),
                slab(lc, SLAB_Q), slab(lc, SLAB_K), slab(lc, SLAB_V, 2), fixed(lc, SLAB_LG1),
                pl.BlockSpec((3, LANES), lambda b, h: (0, h)),
                pl.BlockSpec((3, LANES), lambda b, h: (0, M_HEADS + h)),
                pl.BlockSpec((1, LANES), lambda b, h: (0, h)),
                pl.BlockSpec((1, LANES), lambda b, h: (0, M_HEADS + h)),
                pl.BlockSpec((1, LANES), lambda b, h: (0, 0))]
    nch = n_ctx + n_lat
    return pl.pallas_call(
        functools.partial(_mlstm_kernel, n_ctx, n_lat),
        grid=(batch, M_HEADS),
        in_specs=in_specs,
        out_specs=pl.BlockSpec((lx, M_DV), lambda b, h: (b, h)),
        out_shape=jax.ShapeDtypeStruct((batch * lx, M_HEADS * M_DV), BF16),
        scratch_shapes=[pltpu.VMEM((nch, MLSTM_CHUNK, LANES), BF16),
                        pltpu.VMEM((nch, MLSTM_CHUNK, LANES), BF16),
                        pltpu.VMEM((nch, LANES, MLSTM_CHUNK), BF16),
                        pltpu.VMEM((2, M_DQK, M_DV), F32),
                        pltpu.VMEM((lx, M_DV), F32),
                        pltpu.VMEM((lx, M_DV), F32)],
        compiler_params=_params(2),
    )(px, px, px, px, pc, pc, pc, pc, cw, cw, cb, cb, gate_row)


def _rwkv_stack(x):
    head0 = lax.broadcasted_iota(jnp.int32, x.shape, 1) < R_N
    zero = jnp.zeros_like(x)
    return jnp.concatenate([jnp.where(head0, x, zero), jnp.where(head0, zero, x)], axis=0)


def _rwkv_fold(x):
    half = x.shape[0] // 2
    return x[0:half] + x[half:2 * half]


def _rwkv_gate_a(la, wa, a0_ref, d):
    return jax.nn.sigmoid(a0_ref[d:d + 1, :] + _mm(la.astype(BF16), wa)[:, d * LANES:(d + 1) * LANES])


def _rwkv_prep_kernel(want_y, r_ref, k_ref, v_ref, lw_ref, la_ref,
                      ww_ref, wa_ref, w0_ref, a0_ref, kk_ref, ka_ref, *rest):
    extra_refs, out_refs = (rest[:4], rest[4:]) if want_y else ((), rest)
    c = RWKV_CHUNK
    c2 = 2 * c
    stack = _rwkv_stack
    rr = lax.broadcasted_iota(jnp.int32, (c2, c2), 0)
    cc = lax.broadcasted_iota(jnp.int32, (c2, c2), 1)
    same_head = (rr < c) == (cc < c)
    group_ones = same_head.astype(BF16)
    tt = rr & (c - 1)
    ss = cc & (c - 1)
    strict = (ss < tt, ss > tt)
    incl = (ss <= tt, ss >= tt)
    eye = (rr == cc).astype(F32)
    tr = lax.broadcasted_iota(jnp.int32, (c, c), 0)
    tc = lax.broadcasted_iota(jnp.int32, (c, c), 1)
    tri = ((tc <= tr).astype(BF16), (tc >= tr).astype(BF16))
    ww = ww_ref[0]
    wa = wa_ref[0]
    k_k = kk_ref[...]
    k_a = ka_ref[...]
    m_ref, n_ref, dec_ref = out_refs[0], out_refs[1], out_refs[2]

    def prep_chunks(js):
        nj = len(js)
        rows = [pl.ds(pl.multiple_of(j * c, c), c) for j in js]
        ch = [(d, i) for d in (0, 1) for i in range(nj)]
        half = lambda d: slice(d * LANES, (d + 1) * LANES)
        r = [r_ref[0, rw, :] for rw in rows]
        k = [k_ref[0, rw, :] for rw in rows]
        v_s = [stack(v_ref[0, rw, :]).astype(BF16) for rw in rows]
        lo_w = [_mm(jnp.tanh(lw_ref[0, rw, :]).astype(BF16), ww) for rw in rows]
        lo_a = [_mm(la_ref[0, rw, :].astype(BF16), wa) for rw in rows]
        kk = [x * k_k for x in k]
        kk = [x * lax.rsqrt(_mm_split(x * x, group_ones, False) + 1e-12) for x in kk]
        logw = [-jnp.exp(-_softplus(-(w0_ref[d:d + 1, :] + lo_w[i][:, half(d)])) - 0.5) for d, i in ch]
        a = [jax.nn.sigmoid(a0_ref[d:d + 1, :] + lo_a[i][:, half(d)]) for d, i in ch]
        pin = [_mm_split(x, tri[d], True) for (d, i), x in zip(ch, logw)]
        ptot = [x[0:1, :] if d else x[c - 1:c, :] for (d, i), x in zip(ch, pin)]
        kd = [k[i] * (1.0 + (a_ - 1.0) * k_a) for (d, i), a_ in zip(ch, a)]
        kka = [kk[i] * a_ for (d, i), a_ in zip(ch, a)]
        e_inv = [jnp.exp(-x) for x in pin]
        e_end = [jnp.exp(pt - x) for pt, x in zip(ptot, pin)]
        r_f = [stack(r[i] * jnp.exp(x)) for (d, i), x in zip(ch, pin)]
        ar_t = [jnp.concatenate([stack(-kk[i] * jnp.exp(x - lw_)), rf], axis=0).astype(BF16)
                for (d, i), x, lw_, rf in zip(ch, pin, logw, r_f)]
        bk_t = [jnp.concatenate([stack(x * e), stack(y * e)], axis=0).astype(BF16)
                for x, y, e in zip(kka, kd, e_inv)]
        bk_end = [jnp.concatenate([stack(x * e), stack(y * e)], axis=0).astype(BF16)
                  for x, y, e in zip(kka, kd, e_end)]
        aa = [_mm_nt(x, y) for x, y in zip(ar_t, bk_t)]
        aab = [jnp.where(strict[d], x[0:c2, 0:c2], 0.0) for (d, i), x in zip(ch, aa)]
        aak = [jnp.where(strict[d], x[0:c2, c2:2 * c2], 0.0).astype(BF16) for (d, i), x in zip(ch, aa)]
        akv = [_mm(x, v_s[i]).astype(BF16) for (d, i), x in zip(ch, aak)]
        xs = [eye + x for x in aab]
        pb = [x.astype(BF16) for x in aab]
        ps = [_mm(x, x) for x in pb]
        for _ in range(c.bit_length() - 3):
            pb = [x.astype(BF16) for x in ps]
            both = [_mm(jnp.concatenate([x.astype(BF16), p], axis=0), p) for x, p in zip(xs, pb)]
            xs = [x + y[0:c2] for x, y in zip(xs, both)]
            ps = [y[c2:2 * c2] for y in both]
        xs = [(x + _mm(x.astype(BF16), p.astype(BF16))).astype(BF16) for x, p in zip(xs, ps)]
        wt = [_mm(x, y[0:c2]).astype(BF16) for x, y in zip(xs, ar_t)]
        u0 = [_mm(x, y).astype(BF16) for x, y in zip(xs, akv)]
        m_mat = [_mm_tn(x, y[0:c2]) for x, y in zip(wt, bk_end)]
        n_mat = [_mm_tn(jnp.concatenate([x, v_s[i]], axis=0), y) for (d, i), x, y in zip(ch, u0, bk_end)]
        for (d, i), mm_, nn_, pt in zip(ch, m_mat, n_mat, ptot):
            slot = (0, 0, d, js[i])
            m_ref[slot] = _rwkv_fold(mm_).astype(BF16)
            n_ref[slot] = _rwkv_fold(nn_)
            dec_ref[slot] = jnp.broadcast_to(jnp.exp(pt), (8, LANES))
        if want_y:
            arb = [jnp.where(incl[d], x[c2:2 * c2, 0:c2], 0.0).astype(BF16) for (d, i), x in zip(ch, aa)]
            ark = [jnp.where(incl[d], x[c2:2 * c2, c2:2 * c2], 0.0).astype(BF16) for (d, i), x in zip(ch, aa)]
            qy = [_mm(x, jnp.concatenate([w_, u_], axis=1)) for x, w_, u_ in zip(arb, wt, u0)]
            yl = [x[:, c2:2 * c2] + _mm(y, v_s[i]) for (d, i), x, y in zip(ch, qy, ark)]
            for (d, i), rf, x, y in zip(ch, r_f, qy, yl):
                slot = (0, 0, d, js[i])
                out_refs[3][slot] = _rwkv_fold(rf + x[:, 0:c2]).astype(BF16)
                out_refs[4][slot] = _rwkv_fold(y)
            lg0_ref, lg1_ref, wg_ref, rk_ref = extra_refs
            ksum = [k[i] * (2.0 + (a[i] + a[nj + i] - 2.0) * k_a) for i in range(nj)]
            bonus = [_mm_split(r[i] * ksum[i] * rk_ref[...], group_ones, False) * v_ref[0, rows[i], :]
                     for i in range(nj)]
            gate = [_mm(jax.nn.sigmoid(jnp.concatenate([lg0_ref[0, rw, :], lg1_ref[0, rw, :]], axis=1)
                                       ).astype(BF16), wg_ref[...]) for rw in rows]
            for rw, gt, bn in zip(rows, gate, bonus):
                out_refs[5][rw, :] = gt.astype(BF16)
                out_refs[6][rw, :] = (bn * gt).astype(BF16)

    n_chunks = r_ref.shape[1] // c
    width = min(RWKV_PREP_WIDTH, n_chunks)
    assert n_chunks % width == 0

    def body(t, carry):
        prep_chunks([t * width + u for u in range(width)])
        return carry

    lax.fori_loop(0, n_chunks // width, body, 0)


def _rwkv_scan_kernel(mc_ref, nc_ref, dc_ref, mf_ref, nf_ref, df_ref, qf_ref, ylf_ref,
                      mb_ref, nb_ref, db_ref, qb_ref, ylb_ref,
                      gate_ref, bg_ref, gnw_ref, gnb_ref,
                      o_ref, z_ref, yf_ref, yb_ref):
    c = RWKV_CHUNK
    g = pl.program_id(2)
    n_groups = pl.num_programs(2)
    group = mf_ref.shape[3]
    n_ctx = mc_ref.shape[3]
    pairs = range(RWKV_SCAN_PAIRS)

    def advance(z, m_c, n_c, dec):
        return z * dec[0:1, :] + _mm(z.astype(BF16), _rwkv_stack(m_c)) + _rwkv_stack(n_c)

    @pl.when(g == 0)
    def _():
        z_ref[...] = jnp.zeros(z_ref.shape, F32)

        def ctx_body(i, carry):
            for pp in pairs:
                for d, ii in ((0, i), (1, n_ctx - 1 - i)):
                    zi = 2 * pp + d
                    z_ref[zi] = advance(z_ref[zi], mc_ref[0, pp, d, ii], nc_ref[0, pp, d, ii],
                                        dc_ref[0, pp, d, ii])
            return carry

        lax.fori_loop(0, n_ctx, ctx_body, 0)

    def lat_body(j, carry):
        chains = [(pp,) + t for pp in pairs for t in (
            (0, j, g * group, mf_ref, nf_ref, df_ref, qf_ref, ylf_ref, yf_ref),
            (1, group - 1 - j, (n_groups - 1 - g) * group, mb_ref, nb_ref, db_ref, qb_ref, ylb_ref, yb_ref))]
        zs = [z_ref[2 * pp + d] for pp, d, *_ in chains]
        zb = [z.astype(BF16) for z in zs]
        ys = [_mm_nt(_rwkv_stack(q_ref[0, pp, 0, jj]), z)
              for (pp, d, jj, first, m_ref, n_ref, d_ref, q_ref, yl_ref, y_ref), z in zip(chains, zb)]
        zn = [z * d_ref[0, pp, 0, jj][0:1, :] + _mm(zh, _rwkv_stack(m_ref[0, pp, 0, jj]))
              + _rwkv_stack(n_ref[0, pp, 0, jj])
              for (pp, d, jj, first, m_ref, n_ref, d_ref, q_ref, yl_ref, y_ref), z, zh in zip(chains, zs, zb)]
        for (pp, d, jj, first, m_ref, n_ref, d_ref, q_ref, yl_ref, y_ref), y, z in zip(chains, ys, zn):
            z_ref[2 * pp + d] = z
            y_ref[pl.ds(pl.multiple_of((first + jj) * c, c), c), pp * LANES:(pp + 1) * LANES] = (
                _rwkv_fold(y) + yl_ref[0, pp, 0, jj])
        return carry

    lax.fori_loop(0, group, lat_body, 0)

    @pl.when(g == n_groups - 1)
    def _():
        fr = 4 * c
        r2 = lax.broadcasted_iota(jnp.int32, (LANES, LANES), 0)
        c2 = lax.broadcasted_iota(jnp.int32, (LANES, LANES), 1)
        group_ones = ((r2 < R_N) == (c2 < R_N)).astype(BF16)
        inv_n = 1.0 / R_N

        def fin(i, carry):
            rows = pl.ds(pl.multiple_of(i * fr, fr), fr)
            for pp in pairs:
                cols = slice(pp * LANES, (pp + 1) * LANES)
                y = yf_ref[rows, cols] + yb_ref[rows, cols]
                sums = _mm_split(jnp.concatenate([y, y * y], axis=0), group_ones, False)
                mu = sums[0:fr] * inv_n
                var = sums[fr:2 * fr] * inv_n - mu * mu
                yn = (y - mu) * lax.rsqrt(var + GN_EPS) * gnw_ref[:, cols] + gnb_ref[:, cols]
                o_ref[rows, cols] = (yn * gate_ref[rows, cols].astype(F32)
                                     + bg_ref[rows, cols].astype(F32)).astype(o_ref.dtype)
            return carry

        lax.fori_loop(0, o_ref.shape[0] // fr, fin, 0)


def _rwkv_prep(p_all, batch, group, want_y, ww, wa, w0, a0, k_k, k_a, wg=None, r_k=None):
    c = RWKV_CHUNK
    length = p_all.shape[1] // batch
    n_chunks = length // c
    n_groups = n_chunks // group
    n_pairs = ww.shape[0]
    rows = group * c

    def slab(first):
        return pl.BlockSpec((1, rows, LANES), lambda b, p, g: (first + p, b * n_groups + g, 0))

    def fixed(idx):
        return pl.BlockSpec((1, rows, LANES), lambda b, p, g: (idx, b * n_groups + g, 0))

    def vec(nrows):
        return pl.BlockSpec((nrows, LANES), lambda b, p, g: (0, p))

    lora = pl.BlockSpec((1, LANES, 2 * LANES), lambda b, p, g: (p, 0, 0))
    outs = [(c, BF16), (c, F32), (8, F32)] + ([(c, BF16), (c, F32)] if want_y else [])
    in_specs = [slab(SLAB_R), slab(SLAB_KR), slab(SLAB_VR), fixed(SLAB_LW), fixed(SLAB_LA),
                lora, lora, vec(2), vec(2), vec(1), vec(1)]
    operands = [p_all, p_all, p_all, p_all, p_all, ww, wa, w0, a0, k_k, k_a]
    out_specs = [pl.BlockSpec((1, 1, 2, group, nr, LANES), lambda b, p, g: (b, p, 0, g, 0, 0)) for nr, _ in outs]
    out_shape = [jax.ShapeDtypeStruct((batch, n_pairs, 2, n_chunks, nr, LANES), dt) for nr, dt in outs]
    if want_y:
        in_specs += [fixed(SLAB_LG0), fixed(SLAB_LG1),
                     pl.BlockSpec((2 * LANES, LANES), lambda b, p, g: (0, p)), vec(1)]
        operands += [p_all, p_all, wg, r_k]
        out_specs += [pl.BlockSpec((rows, LANES), lambda b, p, g: (b * n_groups + g, p))] * 2
        out_shape += [jax.ShapeDtypeStruct((batch * length, n_pairs * LANES), BF16)] * 2
    return pl.pallas_call(
        functools.partial(_rwkv_prep_kernel, want_y),
        grid=(batch, n_pairs, n_groups),
        in_specs=in_specs,
        out_specs=out_specs,
        out_shape=out_shape,
        compiler_params=_params(3),
    )(*operands)


def _rwkv_scan(ops_c, ops_x, gate, bonus_gated, batch, group, gn_w, gn_b):
    c = RWKV_CHUNK
    sp = RWKV_SCAN_PAIRS
    lx = gate.shape[0] // batch
    n_pairs, n_ctx = ops_c[0].shape[1], ops_c[0].shape[3]
    n_groups = ops_x[0].shape[3] // group
    assert n_pairs % sp == 0

    def ctx_block(a):
        return pl.BlockSpec((1, sp, 2, n_ctx, a.shape[4], LANES), lambda b, p, g: (b, p, 0, 0, 0, 0))

    def fwd_block(a):
        return pl.BlockSpec((1, sp, 1, group, a.shape[4], LANES), lambda b, p, g: (b, p, 0, g, 0, 0))

    def bwd_block(a):
        return pl.BlockSpec((1, sp, 1, group, a.shape[4], LANES),
                            lambda b, p, g: (b, p, 1, n_groups - 1 - g, 0, 0))

    tokens = pl.BlockSpec((lx, sp * LANES), lambda b, p, g: (b, p))
    vec = pl.BlockSpec((1, sp * LANES), lambda b, p, g: (0, p))
    return pl.pallas_call(
        _rwkv_scan_kernel,
        grid=(batch, n_pairs // sp, n_groups),
        in_specs=[ctx_block(a) for a in ops_c] + [fwd_block(a) for a in ops_x] + [bwd_block(a) for a in ops_x]
                 + [tokens, tokens, vec, vec],
        out_specs=tokens,
        out_shape=jax.ShapeDtypeStruct((batch * lx, n_pairs * LANES), BF16),
        scratch_shapes=[pltpu.VMEM((2 * sp, 2 * c, LANES), F32),
                        pltpu.VMEM((lx, sp * LANES), F32), pltpu.VMEM((lx, sp * LANES), F32)],
        compiler_params=_params(3),
    )(*ops_c, *ops_x, *ops_x, gate, bonus_gated, gn_w, gn_b)


def _rwkv(px, pc, batch, ww, wa, wg, w0, a0, k_k, k_a, r_k, gn_w, gn_b):
    c = RWKV_CHUNK
    n_ctx = pc.shape[1] // batch // c
    n_lat = px.shape[1] // batch // c
    group = min(RWKV_GROUP, n_lat)
    assert n_lat % group == 0
    ops_c = _rwkv_prep(pc, batch, n_ctx, False, ww, wa, w0, a0, k_k, k_a)
    *ops_x, gate, bonus_gated = _rwkv_prep(px, batch, group, True, ww, wa, w0, a0, k_k, k_a, wg, r_k)
    return _rwkv_scan(ops_c, ops_x, gate, bonus_gated, batch, group, gn_w, gn_b)


def _out_kernel(tiles_per_batch, mixm_ref, om_ref, mixr_ref, w_ref, x_ref, ng_ref,
                gt1_ref, g2_ref, sh2_ref, sc2_ref, x1_ref, hx2_ref):
    b = pl.program_id(0) // tiles_per_batch
    hm = mixm_ref[...].astype(F32)
    ng = ng_ref[...]
    parts = []
    for h in range(M_HEADS):
        cols = slice(h * M_DV, (h + 1) * M_DV)
        seg = hm[:, cols]
        seg = seg * lax.rsqrt(jnp.mean(seg * seg, axis=-1, keepdims=True) + NORM_EPS)
        og = jnp.concatenate([om_ref[2 * h], om_ref[2 * h + 1]], axis=1)
        parts.append((seg * ng[:, cols] * jax.nn.sigmoid(og)).astype(BF16))
    lhs = jnp.concatenate(parts + [mixr_ref[...]], axis=1)
    x1 = x_ref[...] + gt1_ref[pl.ds(b, 1), :] * _mm(lhs, w_ref[...])
    x1_ref[...] = x1
    y = x1 * lax.rsqrt(jnp.mean(x1 * x1, axis=-1, keepdims=True) + NORM_EPS) * g2_ref[...]
    hx2_ref[...] = (y * (1.0 + sc2_ref[pl.ds(b, 1), :]) + sh2_ref[pl.ds(b, 1), :]).astype(BF16)


def _out_proj(mixm, px, mixr, w_out, x2d, ng, gt1, g2, sh2, sc2, tm, tiles_per_batch):
    t, d = x2d.shape
    dm = mixm.shape[1]
    row = lambda i: (i, 0)
    const = lambda i: (0, 0)
    return pl.pallas_call(
        functools.partial(_out_kernel, tiles_per_batch),
        grid=(t // tm,),
        in_specs=[pl.BlockSpec((tm, dm), row),
                  pl.BlockSpec((8, tm, LANES), lambda i: (SLAB_O // 8, i, 0)),
                  pl.BlockSpec((tm, dm), row),
                  pl.BlockSpec((d, d), const),
                  pl.BlockSpec((tm, d), row),
                  pl.BlockSpec((1, dm), const),
                  pl.BlockSpec((8, d), const),
                  pl.BlockSpec((1, d), const),
                  pl.BlockSpec((8, d), const),
                  pl.BlockSpec((8, d), const)],
        out_specs=[pl.BlockSpec((tm, d), row), pl.BlockSpec((tm, d), row)],
        out_shape=[jax.ShapeDtypeStruct((t, d), F32), jax.ShapeDtypeStruct((t, d), BF16)],
        compiler_params=_params(1),
    )(mixm, px, mixr, w_out, x2d, ng, gt1, g2, sh2, sc2)


def _ffn_kernel(tiles_per_img, hx_ref, top_ref, bot_ref, wu_ref, wg_ref, wd_ref, cw_ref, cb_ref,
                x1_ref, gt2_ref, gf_ref, o_ref):
    i = pl.program_id(0)
    j = pl.program_id(1)
    tm = hx_ref.shape[0]
    ti = i % tiles_per_img

    @pl.when(j == 0)
    def _():
        o_ref[...] = jnp.zeros(o_ref.shape, F32)

    hx = hx_ref[...]
    top = jnp.where(ti > 0, top_ref[...], jnp.zeros_like(top_ref[...]))
    bot = jnp.where(ti < tiles_per_img - 1, bot_ref[...], jnp.zeros_like(bot_ref[...]))
    u = _mm(jnp.concatenate([top, hx, bot], axis=0), wu_ref[...])
    nr = u.shape[0]
    col = lax.broadcasted_iota(jnp.int32, u.shape, 0) & (GRID_W - 1)
    ul = jnp.where(col == 0, 0.0, pltpu.roll(u, 1, 0))
    ur = jnp.where(col == GRID_W - 1, 0.0, pltpu.roll(u, nr - 1, 0))
    cw = cw_ref[...]
    conv = cb_ref[...]
    for dy in range(3):
        rows = slice(dy * GRID_W, dy * GRID_W + tm)
        conv = (conv + ul[rows] * cw[3 * dy:3 * dy + 1, :] + u[rows] * cw[3 * dy + 1:3 * dy + 2, :]
                + ur[rows] * cw[3 * dy + 2:3 * dy + 3, :])
    gelu = 0.5 * conv * (1.0 + jnp.tanh(0.7978845608028654 * (conv + 0.044715 * conv * conv * conv)))
    act = (gelu * _mm(hx, wg_ref[...])).astype(BF16)
    o_ref[...] += _mm(act, wd_ref[...])

    @pl.when(j == pl.num_programs(1) - 1)
    def _():
        b = i // tiles_per_img
        x2 = x1_ref[...] + gt2_ref[pl.ds(b, 1), :] * o_ref[...]
        o_ref[...] = x2 * lax.rsqrt(jnp.mean(x2 * x2, axis=-1, keepdims=True) + NORM_EPS) * gf_ref[...]


def _conv_ffn(hx2, w_up, w_gate, w_down, cw, cb, x1, gt2, g_final, tm, tiles_per_img):
    t, d = hx2.shape
    f = w_up.shape[1]
    tf = 512
    rows_per_tile = tm // GRID_W
    n_rows = t // GRID_W
    return pl.pallas_call(
        functools.partial(_ffn_kernel, tiles_per_img),
        grid=(t // tm, f // tf),
        in_specs=[pl.BlockSpec((tm, d), lambda i, j: (i, 0)),
                  pl.BlockSpec((GRID_W, d), lambda i, j: (jnp.maximum(i * rows_per_tile - 1, 0), 0)),
                  pl.BlockSpec((GRID_W, d), lambda i, j: (jnp.minimum((i + 1) * rows_per_tile, n_rows - 1), 0)),
                  pl.BlockSpec((d, tf), lambda i, j: (0, j)),
                  pl.BlockSpec((d, tf), lambda i, j: (0, j)),
                  pl.BlockSpec((tf, d), lambda i, j: (j, 0)),
                  pl.BlockSpec((9, tf), lambda i, j: (0, j)),
                  pl.BlockSpec((1, tf), lambda i, j: (0, j)),
                  pl.BlockSpec((tm, d), lambda i, j: (i, 0), pipeline_mode=pl.Buffered(1)),
                  pl.BlockSpec((8, d), lambda i, j: (0, 0)),
                  pl.BlockSpec((1, d), lambda i, j: (0, 0))],
        out_specs=pl.BlockSpec((tm, d), lambda i, j: (i, 0), pipeline_mode=pl.Buffered(1)),
        out_shape=jax.ShapeDtypeStruct((t, d), F32),
        compiler_params=_params(2),
    )(hx2, hx2, hx2, w_up, w_gate, w_down, cw, cb, x1, gt2, g_final)


def _lora_pairs(up):
    _, rank, width = up.shape
    u = up.reshape(2, rank, width // LANES, LANES).transpose(2, 0, 1, 3)
    z = jnp.zeros_like(u[:, 0])
    top = jnp.concatenate([u[:, 0], z], axis=-1)
    bot = jnp.concatenate([z, u[:, 1]], axis=-1)
    return jnp.concatenate([top, bot], axis=1).astype(BF16)


def kernel(x, c, ctx, c_ctx, w_mod, b_mod, g_norm1, g_norm2, w_in, m_conv_w, m_conv_b, m_gate_b, m_norm_g, r_w0, r_w_up, r_a0, r_a_up, r_g_up, r_k_k, r_k_a, r_r_k, r_gn_w, r_gn_b, w_out, f_w_up, f_w_gate, f_conv_w, f_conv_b, f_w_down, g_final):
    batch, seq, d = x.shape
    ctx_len = ctx.shape[1]
    assert w_mod.shape[0] == 1, "single-layer block"
    assert batch + 1 <= 8 and seq % 512 == 0 and ctx_len % MLSTM_CHUNK == 0

    cv8 = jnp.zeros((8, d), F32).at[:batch].set(c).at[batch].set(c_ctx)
    mod = _modulation(cv8, w_mod[0], b_mod[0])
    sh1, sc1, gt1, sh2, sc2, gt2 = (mod[:, k * d:(k + 1) * d] for k in range(6))

    n_gate = 4 * M_HEADS
    w_p = _permute_w(w_in[0], SLAB_R * LANES, n_gate, N_SLABS * LANES)
    g1 = g_norm1[0].reshape(1, d)
    tm_x = 1024 if seq % 1024 == 0 else 512
    px = _inproj(x.reshape(batch * seq, d), g1, sh1, sc1, w_p, tm_x, seq // tm_x, 0)
    tm_c = ctx_len
    pc = _inproj(ctx.reshape(batch * ctx_len, d), g1, sh1, sc1, w_p, tm_c, 1 << 30, batch)

    gate_row = jnp.zeros((1, LANES), F32).at[0, GATE_LANE0:GATE_LANE0 + n_gate].set(m_gate_b[0].reshape(-1))
    mixm = _mlstm(px, pc, batch, m_conv_w[0], m_conv_b[0].reshape(1, -1), gate_row)

    rw = r_k_k.shape[1]
    wg = jnp.zeros((2 * LANES, rw), F32).at[:r_g_up.shape[1]].set(r_g_up[0]).astype(BF16)
    mixr = _rwkv(px, pc, batch, _lora_pairs(r_w_up[0]), _lora_pairs(r_a_up[0]), wg,
                 r_w0[0], r_a0[0], r_k_k, r_k_a, r_r_k[0].reshape(1, rw), r_gn_w, r_gn_b)

    tm_o = 512
    x1, hx2 = _out_proj(mixm, px, mixr, w_out[0].astype(BF16), x.reshape(batch * seq, d), m_norm_g,
                        gt1, g_norm2[0].reshape(1, d), sh2, sc2, tm_o, seq // tm_o)

    tm_f = 1024 if seq % 1024 == 0 else 512
    out = _conv_ffn(hx2, f_w_up[0].astype(BF16), f_w_gate[0].astype(BF16), f_w_down[0].astype(BF16),
                    f_conv_w[0].reshape(9, -1), f_conv_b, x1, gt2, g_final.reshape(1, d),
                    tm_f, seq // tm_f)
    return out.reshape(batch, seq, d)
```

```python
import functools

import jax
import jax.numpy as jnp
from jax import lax
from jax.experimental import pallas as pl
from jax.experimental.pallas import tpu as pltpu

F32 = jnp.float32
BF16 = jnp.bfloat16
HIGHEST = lax.Precision.HIGHEST

LANES = 128
GRID_W = 64
M_HEADS = 4
M_DQK = 128
M_DV = 256
R_N = 64
NORM_EPS = 1e-6
GN_EPS = 64e-5
MLSTM_CHUNK = 128
RWKV_CHUNK = 64
RWKV_GROUP = 16
RWKV_SCAN_PAIRS = 2
RWKV_PREP_WIDTH = 16
VMEM_LIMIT = 56 * 1024 * 1024

SLAB_Q, SLAB_K, SLAB_V, SLAB_O = 0, 4, 8, 16
SLAB_R, SLAB_KR, SLAB_VR = 24, 32, 40
SLAB_LW, SLAB_LA, SLAB_LG0, SLAB_LG1 = 48, 49, 50, 51
N_SLABS = 52
GATE_LANE0 = 32


def _mm(a, b, precision=None):
    return jnp.dot(a, b, preferred_element_type=F32, precision=precision)


def _mm_nt(a, b):
    return lax.dot_general(a, b, (((1,), (1,)), ((), ())), preferred_element_type=F32)


def _mm_tn(a, b):
    return lax.dot_general(a, b, (((0,), (0,)), ((), ())), preferred_element_type=F32)


def _mm_split(x, ones, ones_first):
    hi = x.astype(BF16)
    rest = x - hi.astype(F32)
    mid = rest.astype(BF16)
    lo = (rest - mid.astype(F32)).astype(BF16)
    if ones_first:
        return _mm(ones, hi) + _mm(ones, mid) + _mm(ones, lo)
    return _mm(hi, ones) + _mm(mid, ones) + _mm(lo, ones)


def _softplus(x):
    return jnp.maximum(x, 0.0) + jnp.log1p(jnp.exp(-jnp.abs(x)))


def _log_sigmoid(x):
    return -_softplus(-x)


def _params(n_axes):
    return pltpu.CompilerParams(dimension_semantics=("arbitrary",) * n_axes,
                                vmem_limit_bytes=VMEM_LIMIT)


def _mod_kernel(cv_ref, w_ref, b_ref, o_ref):
    cv = cv_ref[...]
    s = (cv * jax.nn.sigmoid(cv)).astype(BF16)
    o_ref[...] = _mm(s, w_ref[...].astype(BF16)) + b_ref[...]


def _modulation(cv8, w_mod, b_mod):
    d, n = w_mod.shape
    tn = 1024
    return pl.pallas_call(
        _mod_kernel,
        grid=(n // tn,),
        in_specs=[pl.BlockSpec((8, d), lambda j: (0, 0)),
                  pl.BlockSpec((d, tn), lambda j: (0, j)),
                  pl.BlockSpec((1, tn), lambda j: (0, j))],
        out_specs=pl.BlockSpec((8, tn), lambda j: (0, j)),
        out_shape=jax.ShapeDtypeStruct((8, n), F32),
        compiler_params=_params(1),
    )(cv8, w_mod, b_mod.reshape(1, n))


def _permute_w_kernel(n_main, n_gate, w_ref, o_ref):
    w = w_ref[0]
    pad = jnp.zeros((w.shape[0], o_ref.shape[1] - w.shape[1]), F32)
    o_ref[...] = jnp.concatenate([w[:, :n_main], w[:, n_main + n_gate:], w[:, n_main:n_main + n_gate], pad],
                                 axis=1).astype(BF16)


def _permute_w(w, n_main, n_gate, n_out):
    _, d, n = w.shape
    tr = 128
    return pl.pallas_call(
        functools.partial(_permute_w_kernel, n_main, n_gate),
        grid=(d // tr,),
        in_specs=[pl.BlockSpec((1, tr, n), lambda i: (0, i, 0))],
        out_specs=pl.BlockSpec((tr, n_out), lambda i: (i, 0)),
        out_shape=jax.ShapeDtypeStruct((d, n_out), BF16),
        compiler_params=_params(1),
    )(w)


def _inproj_kernel(tiles_per_row, row0, x_ref, g_ref, sh_ref, sc_ref, w_ref, o_ref, hx_ref):
    i = pl.program_id(0)
    j = pl.program_id(1)

    @pl.when(j == 0)
    def _():
        x = x_ref[...]
        ms = jnp.mean(x * x, axis=-1, keepdims=True)
        y = x * lax.rsqrt(ms + NORM_EPS) * g_ref[...]
        r = row0 + i // tiles_per_row
        hx = y * (1.0 + sc_ref[pl.ds(r, 1), :]) + sh_ref[pl.ds(r, 1), :]
        hx_ref[...] = hx.astype(BF16)

    acc = _mm(hx_ref[...], w_ref[...])
    for s in range(acc.shape[1] // LANES):
        o_ref[s] = acc[:, s * LANES:(s + 1) * LANES]


def _inproj(x2d, g, sh, sc, w_p, tm, tiles_per_row, row0):
    t, d = x2d.shape
    n = w_p.shape[1]
    tn = 512
    return pl.pallas_call(
        functools.partial(_inproj_kernel, tiles_per_row, row0),
        grid=(t // tm, n // tn),
        in_specs=[pl.BlockSpec((tm, d), lambda i, j: (i, 0)),
                  pl.BlockSpec((1, d), lambda i, j: (0, 0)),
                  pl.BlockSpec((8, d), lambda i, j: (0, 0)),
                  pl.BlockSpec((8, d), lambda i, j: (0, 0)),
                  pl.BlockSpec((d, tn), lambda i, j: (0, j))],
        out_specs=pl.BlockSpec((tn // LANES, tm, LANES), lambda i, j: (j, i, 0)),
        out_shape=jax.ShapeDtypeStruct((n // LANES, t, LANES), F32),
        scratch_shapes=[pltpu.VMEM((tm, d), BF16)],
        compiler_params=_params(2),
    )(x2d, g, sh, sc, w_p)


def _mlstm_kernel(n_ctx, n_lat,
                  qx_ref, kx_ref, vx_ref, gx_ref, qc_ref, kc_ref, vc_ref, gc_ref,
                  cwq_ref, cwk_ref, cbq_ref, cbk_ref, gb_ref,
                  o_ref,
                  qs_ref, ks_ref, kst_ref, ct_ref, hf_ref, hb_ref):
    c = MLSTM_CHUNK
    head = pl.program_id(1)
    rid = lax.broadcasted_iota(jnp.int32, (c, LANES), 0)
    lane = lax.broadcasted_iota(jnp.int32, (c, LANES), 1)
    r2 = lax.broadcasted_iota(jnp.int32, (c, c), 0)
    c2 = lax.broadcasted_iota(jnp.int32, (c, c), 1)
    tri = ((c2 <= r2).astype(F32), (c2 >= r2).astype(F32))
    keep = (c2 <= r2, c2 >= r2)
    gbias = gb_ref[...]

    def conv_pass(q_ref, k_ref, nchunk, base):
        nrows = nchunk * c

        def body(ci, carry):
            r0 = pl.multiple_of(ci * c, c)
            p0 = pl.multiple_of(jnp.maximum(r0 - 8, 0), 8)
            n0 = pl.multiple_of(jnp.minimum(r0 + c, nrows - 8), 8)
            for src, w_ref, b_ref, scale, is_k in ((q_ref, cwq_ref, cbq_ref, M_DQK ** -0.5, False),
                                                   (k_ref, cwk_ref, cbk_ref, 1.0, True)):
                cur = src[0, pl.ds(r0, c), :]
                prev_row = jnp.where(ci > 0, src[0, pl.ds(p0, 8), :][7:8, :], 0.0)
                next_row = jnp.where(ci < nchunk - 1, src[0, pl.ds(n0, 8), :][0:1, :], 0.0)
                up = jnp.where(rid == 0, prev_row, pltpu.roll(cur, 1, 0))
                dn = jnp.where(rid == c - 1, next_row, pltpu.roll(cur, c - 1, 0))
                w = w_ref[...]
                y = (up * w[0:1, :] + cur * w[1:2, :] + dn * w[2:3, :] + b_ref[...]) * scale
                if is_k:
                    ks_ref[base + ci] = y.astype(BF16)
                    kst_ref[base + ci] = y.T.astype(BF16)
                else:
                    qs_ref[base + ci] = y.astype(BF16)
            return carry

        lax.fori_loop(0, nchunk, body, 0)

    def step(d, ci, base, v_ref, g_ref, n, m, want_h):
        r0 = pl.multiple_of(ci * c, c)
        q = qs_ref[base + ci]
        k = ks_ref[base + ci]
        kt = kst_ref[base + ci]
        v = jnp.concatenate([v_ref[0, pl.ds(r0, c), :], v_ref[1, pl.ds(r0, c), :]], axis=1).astype(BF16)
        gb = g_ref[0, pl.ds(r0, c), :] + gbias
        icol = GATE_LANE0 + d * M_HEADS + head
        fcol = GATE_LANE0 + 2 * M_HEADS + d * M_HEADS + head
        bc = _mm(tri[d], _log_sigmoid(gb), HIGHEST)
        b_col = jnp.sum(jnp.where(lane == fcol, bc, 0.0), axis=1, keepdims=True)
        i_col = jnp.sum(jnp.where(lane == icol, gb, 0.0), axis=1, keepdims=True)
        b_row = jnp.sum(jnp.where(rid == fcol, bc.T, 0.0), axis=0, keepdims=True)
        i_row = jnp.sum(jnp.where(rid == icol, gb.T, 0.0), axis=0, keepdims=True)
        b_last = b_col[0:1, :] if d else b_col[c - 1:c, :]
        ct = ct_ref[d]
        h = None
        if want_h:
            dmat = jnp.where(keep[d], b_col - b_row + i_row, -jnp.inf)
            m_inter = b_col + m
            m_j = jnp.maximum(m_inter, jnp.max(dmat, axis=-1, keepdims=True))
            s = _mm(q, kt) * jnp.exp(dmat - m_j)
            inter = jnp.exp(m_inter - m_j)
            num = _mm(s.astype(BF16), v) + inter * _mm(q, ct.astype(BF16))
            qn = jnp.sum(q.astype(F32) * n, axis=-1, keepdims=True)
            den = jnp.sum(s, axis=-1, keepdims=True) + inter * qn
            h = num / jnp.maximum(jnp.abs(den), jnp.exp(-m_j))
        glog = b_last - b_col + i_col
        m_new = jnp.maximum(b_last + m, jnp.max(glog, axis=0, keepdims=True))
        wk = jnp.exp(glog - m_new)
        decay = jnp.exp(b_last + m - m_new)
        ct_ref[d] = decay * ct + _mm(kt, (wk * v.astype(F32)).astype(BF16))
        n_new = decay * n + jnp.sum(wk * k.astype(F32), axis=0, keepdims=True)
        return n_new, m_new, h

    def run(nchunk, base, v_ref, g_ref, carry, want_h):
        def body(i, carry):
            nf, mf, nb, mb = carry
            ib = nchunk - 1 - i
            nf, mf, h_f = step(0, i, base, v_ref, g_ref, nf, mf, want_h)
            nb, mb, h_b = step(1, ib, base, v_ref, g_ref, nb, mb, want_h)
            if want_h:
                hf_ref[pl.ds(pl.multiple_of(i * c, c), c), :] = h_f
                hb_ref[pl.ds(pl.multiple_of(ib * c, c), c), :] = h_b
            return nf, mf, nb, mb

        return lax.fori_loop(0, nchunk, body, carry)

    conv_pass(qc_ref, kc_ref, n_ctx, 0)
    conv_pass(qx_ref, kx_ref, n_lat, n_ctx)
    ct_ref[...] = jnp.zeros(ct_ref.shape, F32)
    zn = jnp.zeros((1, M_DQK), F32)
    zm = jnp.zeros((1, 1), F32)
    carry = run(n_ctx, 0, vc_ref, gc_ref, (zn, zm, zn, zm), False)
    run(n_lat, n_ctx, vx_ref, gx_ref, carry, True)

    def fin(i, carry):
        rows = pl.ds(pl.multiple_of(i * c, c), c)
        o_ref[rows, :] = (hf_ref[rows, :] + hb_ref[rows, :]).astype(o_ref.dtype)
        return carry

    lax.fori_loop(0, n_lat, fin, 0)


def _mlstm(px, pc, batch, cw, cb, gate_row):
    lx = px.shape[1] // batch
    lc = pc.shape[1] // batch
    n_lat, n_ctx = lx // MLSTM_CHUNK, lc // MLSTM_CHUNK

    def slab(nrows, first, width=1):
        return pl.BlockSpec((width, nrows, LANES), lambda b, h: (first // width + h, b, 0))

    def fixed(nrows, idx):
        return pl.BlockSpec((1, nrows, LANES), lambda b, h: (idx, b, 0))

    in_specs = [slab(lx, SLAB_Q), slab(lx, SLAB_K), slab(lx, SLAB_V, 2), fixed(lx, SLAB_LG1),
                slab(lc, SLAB_Q), slab(lc, SLAB_K), slab(lc, SLAB_V, 2), fixed(lc, SLAB_LG1),
                pl.BlockSpec((3, LANES), lambda b, h: (0, h)),
                pl.BlockSpec((3, LANES), lambda b, h: (0, M_HEADS + h)),
                pl.BlockSpec((1, LANES), lambda b, h: (0, h)),
                pl.BlockSpec((1, LANES), lambda b, h: (0, M_HEADS + h)),
                pl.BlockSpec((1, LANES), lambda b, h: (0, 0))]
    nch = n_ctx + n_lat
    return pl.pallas_call(
        functools.partial(_mlstm_kernel, n_ctx, n_lat),
        grid=(batch, M_HEADS),
        in_specs=in_specs,
        out_specs=pl.BlockSpec((lx, M_DV), lambda b, h: (b, h)),
        out_shape=jax.ShapeDtypeStruct((batch * lx, M_HEADS * M_DV), BF16),
        scratch_shapes=[pltpu.VMEM((nch, MLSTM_CHUNK, LANES), BF16),
                        pltpu.VMEM((nch, MLSTM_CHUNK, LANES), BF16),
                        pltpu.VMEM((nch, LANES, MLSTM_CHUNK), BF16),
                        pltpu.VMEM((2, M_DQK, M_DV), F32),
                        pltpu.VMEM((lx, M_DV), F32),
                        pltpu.VMEM((lx, M_DV), F32)],
        compiler_params=_params(2),
    )(px, px, px, px, pc, pc, pc, pc, cw, cw, cb, cb, gate_row)


def _rwkv_stack(x):
    head0 = lax.broadcasted_iota(jnp.int32, x.shape, 1) < R_N
    zero = jnp.zeros_like(x)
    return jnp.concatenate([jnp.where(head0, x, zero), jnp.where(head0, zero, x)], axis=0)


def _rwkv_fold(x):
    half = x.shape[0] // 2
    return x[0:half] + x[half:2 * half]


def _rwkv_gate_a(la, wa, a0_ref, d):
    return jax.nn.sigmoid(a0_ref[d:d + 1, :] + _mm(la.astype(BF16), wa)[:, d * LANES:(d + 1) * LANES])


def _rwkv_prep_kernel(want_y, r_ref, k_ref, v_ref, lw_ref, la_ref,
                      ww_ref, wa_ref, w0_ref, a0_ref, kk_ref, ka_ref, *rest):
    extra_refs, out_refs = (rest[:4], rest[4:]) if want_y else ((), rest)
    c = RWKV_CHUNK
    c2 = 2 * c
    stack = _rwkv_stack
    rr = lax.broadcasted_iota(jnp.int32, (c2, c2), 0)
    cc = lax.broadcasted_iota(jnp.int32, (c2, c2), 1)
    group_ones = ((rr < c) == (cc < c)).astype(BF16)
    tt = lax.broadcasted_iota(jnp.int32, (c, c2), 0)
    lane = lax.broadcasted_iota(jnp.int32, (c, c2), 1)
    ss = lane & (c - 1)
    head0 = lane < c
    strict = (ss < tt, ss > tt)
    incl = (ss <= tt, ss >= tt)
    eye = (ss == tt).astype(F32)
    tr = lax.broadcasted_iota(jnp.int32, (c, c), 0)
    tc = lax.broadcasted_iota(jnp.int32, (c, c), 1)
    tri = ((tc <= tr).astype(BF16), (tc >= tr).astype(BF16))
    ww = ww_ref[0]
    wa = wa_ref[0]
    k_k = kk_ref[...]
    k_a = ka_ref[...]
    m_ref, n_ref, dec_ref = out_refs[0], out_refs[1], out_refs[2]

    def prep_chunks(js):
        nj = len(js)
        rows = [pl.ds(pl.multiple_of(j * c, c), c) for j in js]
        ch = [(d, i) for d in (0, 1) for i in range(nj)]
        half = lambda d: slice(d * LANES, (d + 1) * LANES)
        r = [r_ref[0, rw, :] for rw in rows]
        k = [k_ref[0, rw, :] for rw in rows]
        v_s = [stack(v_ref[0, rw, :]).astype(BF16) for rw in rows]
        lo_w = [_mm(jnp.tanh(lw_ref[0, rw, :]).astype(BF16), ww) for rw in rows]
        lo_a = [_mm(la_ref[0, rw, :].astype(BF16), wa) for rw in rows]
        kk = [x * k_k for x in k]
        kk = [x * lax.rsqrt(_mm_split(x * x, group_ones, False) + 1e-12) for x in kk]
        logw = [-jnp.exp(-_softplus(-(w0_ref[d:d + 1, :] + lo_w[i][:, half(d)])) - 0.5) for d, i in ch]
        a = [jax.nn.sigmoid(a0_ref[d:d + 1, :] + lo_a[i][:, half(d)]) for d, i in ch]
        pin = [_mm_split(x, tri[d], True) for (d, i), x in zip(ch, logw)]
        ptot = [x[0:1, :] if d else x[c - 1:c, :] for (d, i), x in zip(ch, pin)]
        kd = [k[i] * (1.0 + (a_ - 1.0) * k_a) for (d, i), a_ in zip(ch, a)]
        kka = [kk[i] * a_ for (d, i), a_ in zip(ch, a)]
        e_inv = [jnp.exp(-x) for x in pin]
        e_end = [jnp.exp(pt - x) for pt, x in zip(ptot, pin)]
        r_t = [r[i] * jnp.exp(x) for (d, i), x in zip(ch, pin)]
        a_t = [-kk[i] * jnp.exp(x - lw_) for (d, i), x, lw_ in zip(ch, pin, logw)]
        ar_t = [jnp.concatenate([x, y], axis=0).astype(BF16) for x, y in zip(a_t, r_t)]
        bk_t = [jnp.concatenate([stack(x * e), stack(y * e)], axis=0).astype(BF16)
                for x, y, e in zip(kka, kd, e_inv)]
        bk_end = [jnp.concatenate([x * e, y * e], axis=0).astype(BF16) for x, y, e in zip(kka, kd, e_end)]
        aa = [_mm_nt(x, y) for x, y in zip(ar_t, bk_t)]
        aab = [jnp.where(strict[d], x[0:c, 0:c2], 0.0) for (d, i), x in zip(ch, aa)]
        aak = [jnp.where(strict[d], x[0:c, c2:2 * c2], 0.0).astype(BF16) for (d, i), x in zip(ch, aa)]
        akv = [_mm(x, v_s[i]).astype(BF16) for (d, i), x in zip(ch, aak)]
        xs = [eye + x for x in aab]
        pb = [x.astype(BF16) for x in aab]
        ps = [_mm(x, stack(x)) for x in pb]
        for _ in range(c.bit_length() - 3):
            pb = [x.astype(BF16) for x in ps]
            both = [_mm(jnp.concatenate([x.astype(BF16), p], axis=0), stack(p)) for x, p in zip(xs, pb)]
            xs = [x + y[0:c] for x, y in zip(xs, both)]
            ps = [y[c:c2] for y in both]
        xs = [(x + _mm(x.astype(BF16), stack(p.astype(BF16)))).astype(BF16) for x, p in zip(xs, ps)]
        wu = [_mm(x, jnp.concatenate([stack(y[0:c]), stack(z)], axis=1)).astype(BF16)
              for x, y, z in zip(xs, ar_t, akv)]
        m_mat = [_mm_tn(x[:, 0:c2], y[0:c]) for x, y in zip(wu, bk_end)]
        n_mat = [_mm_tn(jnp.concatenate([x[:, c2:2 * c2], v_ref[0, rows[i], :].astype(BF16)], axis=0), y)
                 for (d, i), x, y in zip(ch, wu, bk_end)]
        for (d, i), mm_, nn_, pt in zip(ch, m_mat, n_mat, ptot):
            slot = (0, 0, d, js[i])
            m_ref[slot] = jnp.where(head0, mm_[0:c], mm_[c:c2]).astype(BF16)
            n_ref[slot] = jnp.where(head0, nn_[0:c], nn_[c:c2])
            dec_ref[slot] = jnp.broadcast_to(jnp.exp(pt), (8, LANES))
        if want_y:
            ark = [jnp.concatenate([jnp.where(incl[d], x[c:c2, 0:c2], 0.0),
                                    jnp.where(incl[d], x[c:c2, c2:2 * c2], 0.0)], axis=1).astype(BF16)
                   for (d, i), x in zip(ch, aa)]
            qy = [_mm(x, jnp.concatenate([
                      jnp.concatenate([stack(w_[:, 0:c2]), stack(w_[:, c2:2 * c2])], axis=1),
                      jnp.concatenate([jnp.zeros((c2, c2), BF16), v_s[i]], axis=1)], axis=0))
                  for (d, i), x, w_ in zip(ch, ark, wu)]
            for (d, i), rt, x in zip(ch, r_t, qy):
                slot = (0, 0, d, js[i])
                out_refs[3][slot] = (rt + x[:, 0:c2]).astype(BF16)
                out_refs[4][slot] = x[:, c2:2 * c2]
            lg0_ref, lg1_ref, wg_ref, rk_ref = extra_refs
            ksum = [k[i] * (2.0 + (a[i] + a[nj + i] - 2.0) * k_a) for i in range(nj)]
            bonus = [_mm_split(r[i] * ksum[i] * rk_ref[...], group_ones, False) * v_ref[0, rows[i], :]
                     for i in range(nj)]
            gate = [_mm(jax.nn.sigmoid(jnp.concatenate([lg0_ref[0, rw, :], lg1_ref[0, rw, :]], axis=1)
                                       ).astype(BF16), wg_ref[...]) for rw in rows]
            for rw, gt, bn in zip(rows, gate, bonus):
                out_refs[5][rw, :] = gt.astype(BF16)
                out_refs[6][rw, :] = (bn * gt).astype(BF16)

    n_chunks = r_ref.shape[1] // c
    width = min(RWKV_PREP_WIDTH, n_chunks)
    assert n_chunks % width == 0

    def body(t, carry):
        prep_chunks([t * width + u for u in range(width)])
        return carry

    lax.fori_loop(0, n_chunks // width, body, 0)


def _rwkv_scan_kernel(mc_ref, nc_ref, dc_ref, mf_ref, nf_ref, df_ref, qf_ref, ylf_ref,
                      mb_ref, nb_ref, db_ref, qb_ref, ylb_ref,
                      gate_ref, bg_ref, gnw_ref, gnb_ref,
                      o_ref, z_ref, yf_ref, yb_ref):
    c = RWKV_CHUNK
    g = pl.program_id(2)
    n_groups = pl.num_programs(2)
    group = mf_ref.shape[3]
    n_ctx = mc_ref.shape[3]
    pairs = range(RWKV_SCAN_PAIRS)

    def advance(z, m_c, n_c, dec):
        return z * dec[0:1, :] + _mm(z.astype(BF16), _rwkv_stack(m_c)) + _rwkv_stack(n_c)

    @pl.when(g == 0)
    def _():
        z_ref[...] = jnp.zeros(z_ref.shape, F32)

        def ctx_body(i, carry):
            for pp in pairs:
                for d, ii in ((0, i), (1, n_ctx - 1 - i)):
                    zi = 2 * pp + d
                    z_ref[zi] = advance(z_ref[zi], mc_ref[0, pp, d, ii], nc_ref[0, pp, d, ii],
                                        dc_ref[0, pp, d, ii])
            return carry

        lax.fori_loop(0, n_ctx, ctx_body, 0)

    def lat_body(j, carry):
        chains = [(pp,) + t for pp in pairs for t in (
            (0, j, g * group, mf_ref, nf_ref, df_ref, qf_ref, ylf_ref, yf_ref),
            (1, group - 1 - j, (n_groups - 1 - g) * group, mb_ref, nb_ref, db_ref, qb_ref, ylb_ref, yb_ref))]
        zs = [z_ref[2 * pp + d] for pp, d, *_ in chains]
        zb = [z.astype(BF16) for z in zs]
        ys = [_mm_nt(_rwkv_stack(q_ref[0, pp, 0, jj]), z)
              for (pp, d, jj, first, m_ref, n_ref, d_ref, q_ref, yl_ref, y_ref), z in zip(chains, zb)]
        zn = [z * d_ref[0, pp, 0, jj][0:1, :] + _mm(zh, _rwkv_stack(m_ref[0, pp, 0, jj]))
              + _rwkv_stack(n_ref[0, pp, 0, jj])
              for (pp, d, jj, first, m_ref, n_ref, d_ref, q_ref, yl_ref, y_ref), z, zh in zip(chains, zs, zb)]
        for (pp, d, jj, first, m_ref, n_ref, d_ref, q_ref, yl_ref, y_ref), y, z in zip(chains, ys, zn):
            z_ref[2 * pp + d] = z
            y_ref[pl.ds(pl.multiple_of((first + jj) * c, c), c), pp * LANES:(pp + 1) * LANES] = (
                _rwkv_fold(y) + yl_ref[0, pp, 0, jj])
        return carry

    lax.fori_loop(0, group, lat_body, 0)

    @pl.when(g == n_groups - 1)
    def _():
        fr = 4 * c
        r2 = lax.broadcasted_iota(jnp.int32, (LANES, LANES), 0)
        c2 = lax.broadcasted_iota(jnp.int32, (LANES, LANES), 1)
        group_ones = ((r2 < R_N) == (c2 < R_N)).astype(BF16)
        inv_n = 1.0 / R_N

        def fin(i, carry):
            rows = pl.ds(pl.multiple_of(i * fr, fr), fr)
            for pp in pairs:
                cols = slice(pp * LANES, (pp + 1) * LANES)
                y = yf_ref[rows, cols] + yb_ref[rows, cols]
                sums = _mm_split(jnp.concatenate([y, y * y], axis=0), group_ones, False)
                mu = sums[0:fr] * inv_n
                var = sums[fr:2 * fr] * inv_n - mu * mu
                yn = (y - mu) * lax.rsqrt(var + GN_EPS) * gnw_ref[:, cols] + gnb_ref[:, cols]
                o_ref[rows, cols] = (yn * gate_ref[rows, cols].astype(F32)
                                     + bg_ref[rows, cols].astype(F32)).astype(o_ref.dtype)
            return carry

        lax.fori_loop(0, o_ref.shape[0] // fr, fin, 0)


def _rwkv_prep(p_all, batch, group, want_y, ww, wa, w0, a0, k_k, k_a, wg=None, r_k=None):
    c = RWKV_CHUNK
    length = p_all.shape[1] // batch
    n_chunks = length // c
    n_groups = n_chunks // group
    n_pairs = ww.shape[0]
    rows = group * c

    def slab(first):
        return pl.BlockSpec((1, rows, LANES), lambda b, p, g: (first + p, b * n_groups + g, 0))

    def fixed(idx):
        return pl.BlockSpec((1, rows, LANES), lambda b, p, g: (idx, b * n_groups + g, 0))

    def vec(nrows):
        return pl.BlockSpec((nrows, LANES), lambda b, p, g: (0, p))

    lora = pl.BlockSpec((1, LANES, 2 * LANES), lambda b, p, g: (p, 0, 0))
    outs = [(c, BF16), (c, F32), (8, F32)] + ([(c, BF16), (c, F32)] if want_y else [])
    in_specs = [slab(SLAB_R), slab(SLAB_KR), slab(SLAB_VR), fixed(SLAB_LW), fixed(SLAB_LA),
                lora, lora, vec(2), vec(2), vec(1), vec(1)]
    operands = [p_all, p_all, p_all, p_all, p_all, ww, wa, w0, a0, k_k, k_a]
    out_specs = [pl.BlockSpec((1, 1, 2, group, nr, LANES), lambda b, p, g: (b, p, 0, g, 0, 0)) for nr, _ in outs]
    out_shape = [jax.ShapeDtypeStruct((batch, n_pairs, 2, n_chunks, nr, LANES), dt) for nr, dt in outs]
    if want_y:
        in_specs += [fixed(SLAB_LG0), fixed(SLAB_LG1),
                     pl.BlockSpec((2 * LANES, LANES), lambda b, p, g: (0, p)), vec(1)]
        operands += [p_all, p_all, wg, r_k]
        out_specs += [pl.BlockSpec((rows, LANES), lambda b, p, g: (b * n_groups + g, p))] * 2
        out_shape += [jax.ShapeDtypeStruct((batch * length, n_pairs * LANES), BF16)] * 2
    return pl.pallas_call(
        functools.partial(_rwkv_prep_kernel, want_y),
        grid=(batch, n_pairs, n_groups),
        in_specs=in_specs,
        out_specs=out_specs,
        out_shape=out_shape,
        compiler_params=_params(3),
    )(*operands)


def _rwkv_scan(ops_c, ops_x, gate, bonus_gated, batch, group, gn_w, gn_b):
    c = RWKV_CHUNK
    sp = RWKV_SCAN_PAIRS
    lx = gate.shape[0] // batch
    n_pairs, n_ctx = ops_c[0].shape[1], ops_c[0].shape[3]
    n_groups = ops_x[0].shape[3] // group
    assert n_pairs % sp == 0

    def ctx_block(a):
        return pl.BlockSpec((1, sp, 2, n_ctx, a.shape[4], LANES), lambda b, p, g: (b, p, 0, 0, 0, 0))

    def fwd_block(a):
        return pl.BlockSpec((1, sp, 1, group, a.shape[4], LANES), lambda b, p, g: (b, p, 0, g, 0, 0))

    def bwd_block(a):
        return pl.BlockSpec((1, sp, 1, group, a.shape[4], LANES),
                            lambda b, p, g: (b, p, 1, n_groups - 1 - g, 0, 0))

    tokens = pl.BlockSpec((lx, sp * LANES), lambda b, p, g: (b, p))
    vec = pl.BlockSpec((1, sp * LANES), lambda b, p, g: (0, p))
    return pl.pallas_call(
        _rwkv_scan_kernel,
        grid=(batch, n_pairs // sp, n_groups),
        in_specs=[ctx_block(a) for a in ops_c] + [fwd_block(a) for a in ops_x] + [bwd_block(a) for a in ops_x]
                 + [tokens, tokens, vec, vec],
        out_specs=tokens,
        out_shape=jax.ShapeDtypeStruct((batch * lx, n_pairs * LANES), BF16),
        scratch_shapes=[pltpu.VMEM((2 * sp, 2 * c, LANES), F32),
                        pltpu.VMEM((lx, sp * LANES), F32), pltpu.VMEM((lx, sp * LANES), F32)],
        compiler_params=_params(3),
    )(*ops_c, *ops_x, *ops_x, gate, bonus_gated, gn_w, gn_b)


def _rwkv(px, pc, batch, ww, wa, wg, w0, a0, k_k, k_a, r_k, gn_w, gn_b):
    c = RWKV_CHUNK
    n_ctx = pc.shape[1] // batch // c
    n_lat = px.shape[1] // batch // c
    group = min(RWKV_GROUP, n_lat)
    assert n_lat % group == 0
    ops_c = _rwkv_prep(pc, batch, n_ctx, False, ww, wa, w0, a0, k_k, k_a)
    *ops_x, gate, bonus_gated = _rwkv_prep(px, batch, group, True, ww, wa, w0, a0, k_k, k_a, wg, r_k)
    return _rwkv_scan(ops_c, ops_x, gate, bonus_gated, batch, group, gn_w, gn_b)


def _out_kernel(tiles_per_batch, mixm_ref, om_ref, mixr_ref, w_ref, x_ref, ng_ref,
                gt1_ref, g2_ref, sh2_ref, sc2_ref, x1_ref, hx2_ref):
    b = pl.program_id(0) // tiles_per_batch
    hm = mixm_ref[...].astype(F32)
    ng = ng_ref[...]
    parts = []
    for h in range(M_HEADS):
        cols = slice(h * M_DV, (h + 1) * M_DV)
        seg = hm[:, cols]
        seg = seg * lax.rsqrt(jnp.mean(seg * seg, axis=-1, keepdims=True) + NORM_EPS)
        og = jnp.concatenate([om_ref[2 * h], om_ref[2 * h + 1]], axis=1)
        parts.append((seg * ng[:, cols] * jax.nn.sigmoid(og)).astype(BF16))
    lhs = jnp.concatenate(parts + [mixr_ref[...]], axis=1)
    x1 = x_ref[...] + gt1_ref[pl.ds(b, 1), :] * _mm(lhs, w_ref[...])
    x1_ref[...] = x1
    y = x1 * lax.rsqrt(jnp.mean(x1 * x1, axis=-1, keepdims=True) + NORM_EPS) * g2_ref[...]
    hx2_ref[...] = (y * (1.0 + sc2_ref[pl.ds(b, 1), :]) + sh2_ref[pl.ds(b, 1), :]).astype(BF16)


def _out_proj(mixm, px, mixr, w_out, x2d, ng, gt1, g2, sh2, sc2, tm, tiles_per_batch):
    t, d = x2d.shape
    dm = mixm.shape[1]
    row = lambda i: (i, 0)
    const = lambda i: (0, 0)
    return pl.pallas_call(
        functools.partial(_out_kernel, tiles_per_batch),
        grid=(t // tm,),
        in_specs=[pl.BlockSpec((tm, dm), row),
                  pl.BlockSpec((8, tm, LANES), lambda i: (SLAB_O // 8, i, 0)),
                  pl.BlockSpec((tm, dm), row),
                  pl.BlockSpec((d, d), const),
                  pl.BlockSpec((tm, d), row),
                  pl.BlockSpec((1, dm), const),
                  pl.BlockSpec((8, d), const),
                  pl.BlockSpec((1, d), const),
                  pl.BlockSpec((8, d), const),
                  pl.BlockSpec((8, d), const)],
        out_specs=[pl.BlockSpec((tm, d), row), pl.BlockSpec((tm, d), row)],
        out_shape=[jax.ShapeDtypeStruct((t, d), F32), jax.ShapeDtypeStruct((t, d), BF16)],
        compiler_params=_params(1),
    )(mixm, px, mixr, w_out, x2d, ng, gt1, g2, sh2, sc2)


def _ffn_kernel(tiles_per_img, hx_ref, top_ref, bot_ref, wu_ref, wg_ref, wd_ref, cw_ref, cb_ref,
                x1_ref, gt2_ref, gf_ref, o_ref):
    i = pl.program_id(0)
    j = pl.program_id(1)
    tm = hx_ref.shape[0]
    ti = i % tiles_per_img

    @pl.when(j == 0)
    def _():
        o_ref[...] = jnp.zeros(o_ref.shape, F32)

    hx = hx_ref[...]
    top = jnp.where(ti > 0, top_ref[...], jnp.zeros_like(top_ref[...]))
    bot = jnp.where(ti < tiles_per_img - 1, bot_ref[...], jnp.zeros_like(bot_ref[...]))
    u = _mm(jnp.concatenate([top, hx, bot], axis=0), wu_ref[...])
    nr = u.shape[0]
    col = lax.broadcasted_iota(jnp.int32, u.shape, 0) & (GRID_W - 1)
    ul = jnp.where(col == 0, 0.0, pltpu.roll(u, 1, 0))
    ur = jnp.where(col == GRID_W - 1, 0.0, pltpu.roll(u, nr - 1, 0))
    cw = cw_ref[...]
    conv = cb_ref[...]
    for dy in range(3):
        rows = slice(dy * GRID_W, dy * GRID_W + tm)
        conv = (conv + ul[rows] * cw[3 * dy:3 * dy + 1, :] + u[rows] * cw[3 * dy + 1:3 * dy + 2, :]
                + ur[rows] * cw[3 * dy + 2:3 * dy + 3, :])
    gelu = 0.5 * conv * (1.0 + jnp.tanh(0.7978845608028654 * (conv + 0.044715 * conv * conv * conv)))
    act = (gelu * _mm(hx, wg_ref[...])).astype(BF16)
    o_ref[...] += _mm(act, wd_ref[...])

    @pl.when(j == pl.num_programs(1) - 1)
    def _():
        b = i // tiles_per_img
        x2 = x1_ref[...] + gt2_ref[pl.ds(b, 1), :] * o_ref[...]
        o_ref[...] = x2 * lax.rsqrt(jnp.mean(x2 * x2, axis=-1, keepdims=True) + NORM_EPS) * gf_ref[...]


def _conv_ffn(hx2, w_up, w_gate, w_down, cw, cb, x1, gt2, g_final, tm, tiles_per_img):
    t, d = hx2.shape
    f = w_up.shape[1]
    tf = 512
    rows_per_tile = tm // GRID_W
    n_rows = t // GRID_W
    return pl.pallas_call(
        functools.partial(_ffn_kernel, tiles_per_img),
        grid=(t // tm, f // tf),
        in_specs=[pl.BlockSpec((tm, d), lambda i, j: (i, 0)),
                  pl.BlockSpec((GRID_W, d), lambda i, j: (jnp.maximum(i * rows_per_tile - 1, 0), 0)),
                  pl.BlockSpec((GRID_W, d), lambda i, j: (jnp.minimum((i + 1) * rows_per_tile, n_rows - 1), 0)),
                  pl.BlockSpec((d, tf), lambda i, j: (0, j)),
                  pl.BlockSpec((d, tf), lambda i, j: (0, j)),
                  pl.BlockSpec((tf, d), lambda i, j: (j, 0)),
                  pl.BlockSpec((9, tf), lambda i, j: (0, j)),
                  pl.BlockSpec((1, tf), lambda i, j: (0, j)),
                  pl.BlockSpec((tm, d), lambda i, j: (i, 0), pipeline_mode=pl.Buffered(1)),
                  pl.BlockSpec((8, d), lambda i, j: (0, 0)),
                  pl.BlockSpec((1, d), lambda i, j: (0, 0))],
        out_specs=pl.BlockSpec((tm, d), lambda i, j: (i, 0), pipeline_mode=pl.Buffered(1)),
        out_shape=jax.ShapeDtypeStruct((t, d), F32),
        compiler_params=_params(2),
    )(hx2, hx2, hx2, w_up, w_gate, w_down, cw, cb, x1, gt2, g_final)


def _lora_pairs(up):
    _, rank, width = up.shape
    u = up.reshape(2, rank, width // LANES, LANES).transpose(2, 0, 1, 3)
    z = jnp.zeros_like(u[:, 0])
    top = jnp.concatenate([u[:, 0], z], axis=-1)
    bot = jnp.concatenate([z, u[:, 1]], axis=-1)
    return jnp.concatenate([top, bot], axis=1).astype(BF16)


def kernel(x, c, ctx, c_ctx, w_mod, b_mod, g_norm1, g_norm2, w_in, m_conv_w, m_conv_b, m_gate_b, m_norm_g, r_w0, r_w_up, r_a0, r_a_up, r_g_up, r_k_k, r_k_a, r_r_k, r_gn_w, r_gn_b, w_out, f_w_up, f_w_gate, f_conv_w, f_conv_b, f_w_down, g_final):
    batch, seq, d = x.shape
    ctx_len = ctx.shape[1]
    assert w_mod.shape[0] == 1, "single-layer block"
    assert batch + 1 <= 8 and seq % 512 == 0 and ctx_len % MLSTM_CHUNK == 0

    cv8 = jnp.zeros((8, d), F32).at[:batch].set(c).at[batch].set(c_ctx)
    mod = _modulation(cv8, w_mod[0], b_mod[0])
    sh1, sc1, gt1, sh2, sc2, gt2 = (mod[:, k * d:(k + 1) * d] for k in range(6))

    n_gate = 4 * M_HEADS
    w_p = _permute_w(w_in, SLAB_R * LANES, n_gate, N_SLABS * LANES)
    g1 = g_norm1[0].reshape(1, d)
    tm_x = 1024 if seq % 1024 == 0 else 512
    px = _inproj(x.reshape(batch * seq, d), g1, sh1, sc1, w_p, tm_x, seq // tm_x, 0)
    tm_c = ctx_len
    pc = _inproj(ctx.reshape(batch * ctx_len, d), g1, sh1, sc1, w_p, tm_c, 1 << 30, batch)

    gate_row = jnp.zeros((1, LANES), F32).at[0, GATE_LANE0:GATE_LANE0 + n_gate].set(m_gate_b[0].reshape(-1))
    mixm = _mlstm(px, pc, batch, m_conv_w[0], m_conv_b[0].reshape(1, -1), gate_row)

    rw = r_k_k.shape[1]
    wg = jnp.zeros((2 * LANES, rw), F32).at[:r_g_up.shape[1]].set(r_g_up[0]).astype(BF16)
    mixr = _rwkv(px, pc, batch, _lora_pairs(r_w_up[0]), _lora_pairs(r_a_up[0]), wg,
                 r_w0[0], r_a0[0], r_k_k, r_k_a, r_r_k[0].reshape(1, rw), r_gn_w, r_gn_b)

    tm_o = 512
    x1, hx2 = _out_proj(mixm, px, mixr, w_out[0].astype(BF16), x.reshape(batch * seq, d), m_norm_g,
                        gt1, g_norm2[0].reshape(1, d), sh2, sc2, tm_o, seq // tm_o)

    tm_f = 1024 if seq % 1024 == 0 else 512
    out = _conv_ffn(hx2, f_w_up[0].astype(BF16), f_w_gate[0].astype(BF16), f_w_down[0].astype(BF16),
                    f_conv_w[0].reshape(9, -1), f_conv_b, x1, gt2, g_final.reshape(1, d),
                    tm_f, seq // tm_f)
    return out.reshape(batch, seq, d)
```

```python
import functools

import jax
import jax.numpy as jnp
from jax import lax
from jax.experimental import pallas as pl
from jax.experimental.pallas import tpu as pltpu

F32 = jnp.float32
BF16 = jnp.bfloat16
HIGHEST = lax.Precision.HIGHEST

LANES = 128
GRID_W = 64
M_HEADS = 4
M_DQK = 128
M_DV = 256
R_N = 64
NORM_EPS = 1e-6
GN_EPS = 64e-5
MLSTM_CHUNK = 128
RWKV_CHUNK = 64
RWKV_GROUP = 16
RWKV_SCAN_PAIRS = 2
RWKV_PREP_WIDTH = 16
VMEM_LIMIT = 56 * 1024 * 1024

SLAB_Q, SLAB_K, SLAB_V, SLAB_O = 0, 4, 8, 16
SLAB_R, SLAB_KR, SLAB_VR = 24, 32, 40
SLAB_LW, SLAB_LA, SLAB_LG0, SLAB_LG1 = 48, 49, 50, 51
N_SLABS = 52
GATE_LANE0 = 32


def _mm(a, b, precision=None):
    return jnp.dot(a, b, preferred_element_type=F32, precision=precision)


def _mm_nt(a, b):
    return lax.dot_general(a, b, (((1,), (1,)), ((), ())), preferred_element_type=F32)


def _mm_tn(a, b):
    return lax.dot_general(a, b, (((0,), (0,)), ((), ())), preferred_element_type=F32)


def _mm_split(x, ones, ones_first):
    hi = x.astype(BF16)
    rest = x - hi.astype(F32)
    mid = rest.astype(BF16)
    lo = (rest - mid.astype(F32)).astype(BF16)
    if ones_first:
        return _mm(ones, hi) + _mm(ones, mid) + _mm(ones, lo)
    return _mm(hi, ones) + _mm(mid, ones) + _mm(lo, ones)


def _softplus(x):
    return jnp.maximum(x, 0.0) + jnp.log1p(jnp.exp(-jnp.abs(x)))


def _log_sigmoid(x):
    return -_softplus(-x)


def _params(n_axes):
    return pltpu.CompilerParams(dimension_semantics=("arbitrary",) * n_axes,
                                vmem_limit_bytes=VMEM_LIMIT)


def _mod_kernel(cv_ref, w_ref, b_ref, o_ref):
    cv = cv_ref[...]
    s = (cv * jax.nn.sigmoid(cv)).astype(BF16)
    o_ref[...] = _mm(s, w_ref[...].astype(BF16)) + b_ref[...]


def _modulation(cv8, w_mod, b_mod):
    d, n = w_mod.shape
    tn = 1024
    return pl.pallas_call(
        _mod_kernel,
        grid=(n // tn,),
        in_specs=[pl.BlockSpec((8, d), lambda j: (0, 0)),
                  pl.BlockSpec((d, tn), lambda j: (0, j)),
                  pl.BlockSpec((1, tn), lambda j: (0, j))],
        out_specs=pl.BlockSpec((8, tn), lambda j: (0, j)),
        out_shape=jax.ShapeDtypeStruct((8, n), F32),
        compiler_params=_params(1),
    )(cv8, w_mod, b_mod.reshape(1, n))


def _permute_w_kernel(n_main, n_gate, w_ref, o_ref):
    w = w_ref[0]
    pad = jnp.zeros((w.shape[0], o_ref.shape[1] - w.shape[1]), F32)
    o_ref[...] = jnp.concatenate([w[:, :n_main], w[:, n_main + n_gate:], w[:, n_main:n_main + n_gate], pad],
                                 axis=1).astype(BF16)


def _permute_w(w, n_main, n_gate, n_out):
    _, d, n = w.shape
    tr = 128
    return pl.pallas_call(
        functools.partial(_permute_w_kernel, n_main, n_gate),
        grid=(d // tr,),
        in_specs=[pl.BlockSpec((1, tr, n), lambda i: (0, i, 0))],
        out_specs=pl.BlockSpec((tr, n_out), lambda i: (i, 0)),
        out_shape=jax.ShapeDtypeStruct((d, n_out), BF16),
        compiler_params=_params(1),
    )(w)


def _inproj_kernel(tiles_per_row, row0, x_ref, g_ref, sh_ref, sc_ref, w_ref, o_ref, hx_ref):
    i = pl.program_id(0)
    j = pl.program_id(1)

    @pl.when(j == 0)
    def _():
        x = x_ref[...]
        ms = jnp.mean(x * x, axis=-1, keepdims=True)
        y = x * lax.rsqrt(ms + NORM_EPS) * g_ref[...]
        r = row0 + i // tiles_per_row
        hx = y * (1.0 + sc_ref[pl.ds(r, 1), :]) + sh_ref[pl.ds(r, 1), :]
        hx_ref[...] = hx.astype(BF16)

    acc = _mm(hx_ref[...], w_ref[...])
    for s in range(acc.shape[1] // LANES):
        o_ref[s] = acc[:, s * LANES:(s + 1) * LANES]


def _inproj(x2d, g, sh, sc, w_p, tm, tiles_per_row, row0):
    t, d = x2d.shape
    n = w_p.shape[1]
    tn = 512
    return pl.pallas_call(
        functools.partial(_inproj_kernel, tiles_per_row, row0),
        grid=(t // tm, n // tn),
        in_specs=[pl.BlockSpec((tm, d), lambda i, j: (i, 0)),
                  pl.BlockSpec((1, d), lambda i, j: (0, 0)),
                  pl.BlockSpec((8, d), lambda i, j: (0, 0)),
                  pl.BlockSpec((8, d), lambda i, j: (0, 0)),
                  pl.BlockSpec((d, tn), lambda i, j: (0, j))],
        out_specs=pl.BlockSpec((tn // LANES, tm, LANES), lambda i, j: (j, i, 0)),
        out_shape=jax.ShapeDtypeStruct((n // LANES, t, LANES), F32),
        scratch_shapes=[pltpu.VMEM((tm, d), BF16)],
        compiler_params=_params(2),
    )(x2d, g, sh, sc, w_p)


def _mlstm_kernel(n_ctx, n_lat,
                  qx_ref, kx_ref, vx_ref, gx_ref, qc_ref, kc_ref, vc_ref, gc_ref,
                  cwq_ref, cwk_ref, cbq_ref, cbk_ref, gb_ref,
                  o_ref,
                  qs_ref, ks_ref, kst_ref, gcol_ref, grow_ref, ct_ref, hf_ref, hb_ref):
    c = MLSTM_CHUNK
    head = pl.program_id(1)
    rid = lax.broadcasted_iota(jnp.int32, (c, LANES), 0)
    lane = lax.broadcasted_iota(jnp.int32, (c, LANES), 1)
    r2 = lax.broadcasted_iota(jnp.int32, (c, c), 0)
    c2 = lax.broadcasted_iota(jnp.int32, (c, c), 1)
    prefix = (c2 <= r2).astype(BF16)
    keep = (c2 <= r2, c2 >= r2)
    gbias = gb_ref[...]

    def conv_pass(q_ref, k_ref, g_ref, nchunk, base):
        nrows = nchunk * c

        def body(ci, carry):
            r0 = pl.multiple_of(ci * c, c)
            p0 = pl.multiple_of(jnp.maximum(r0 - 8, 0), 8)
            n0 = pl.multiple_of(jnp.minimum(r0 + c, nrows - 8), 8)
            gb = g_ref[0, pl.ds(r0, c), :] + gbias
            lf = _log_sigmoid(gb)
            pre = _mm_split(lf, prefix, True)
            suf = pre[c - 1:c, :] - pre + lf
            packed = jnp.zeros((c, LANES), F32)
            for slot, (src, col0) in enumerate(((pre, GATE_LANE0 + 2 * M_HEADS), (suf, GATE_LANE0 + 3 * M_HEADS),
                                                (gb, GATE_LANE0), (gb, GATE_LANE0 + M_HEADS))):
                col = jnp.sum(jnp.where(lane == col0 + head, src, 0.0), axis=1, keepdims=True)
                packed = jnp.where(lane == slot, col, packed)
            gcol_ref[base + ci] = packed
            grow_ref[base + ci] = packed.T[0:8, :]
            for src, w_ref, b_ref, scale, is_k in ((q_ref, cwq_ref, cbq_ref, M_DQK ** -0.5, False),
                                                   (k_ref, cwk_ref, cbk_ref, 1.0, True)):
                cur = src[0, pl.ds(r0, c), :]
                prev_row = jnp.where(ci > 0, src[0, pl.ds(p0, 8), :][7:8, :], 0.0)
                next_row = jnp.where(ci < nchunk - 1, src[0, pl.ds(n0, 8), :][0:1, :], 0.0)
                up = jnp.where(rid == 0, prev_row, pltpu.roll(cur, 1, 0))
                dn = jnp.where(rid == c - 1, next_row, pltpu.roll(cur, c - 1, 0))
                w = w_ref[...]
                y = (up * w[0:1, :] + cur * w[1:2, :] + dn * w[2:3, :] + b_ref[...]) * scale
                if is_k:
                    ks_ref[base + ci] = y.astype(BF16)
                    kst_ref[base + ci] = y.T.astype(BF16)
                else:
                    qs_ref[base + ci] = y.astype(BF16)
            return carry

        lax.fori_loop(0, nchunk, body, 0, unroll=min(4, nchunk))

    def step(d, ci, base, v_ref, n, m, want_h):
        r0 = pl.multiple_of(ci * c, c)
        q = qs_ref[base + ci]
        k = ks_ref[base + ci]
        kt = kst_ref[base + ci]
        v = jnp.concatenate([v_ref[0, pl.ds(r0, c), :], v_ref[1, pl.ds(r0, c), :]], axis=1).astype(BF16)
        gcol = gcol_ref[base + ci]
        grow = grow_ref[base + ci]
        rep = lambda x: jnp.broadcast_to(x, (c, LANES))
        wide = lambda x: jnp.concatenate([x, x], axis=1)
        b_col, i_col = rep(gcol[:, d:d + 1]), rep(gcol[:, 2 + d:3 + d])
        b_row, i_row = grow[d:d + 1, :], grow[2 + d:3 + d, :]
        b_last = b_col[0:1, :] if d else b_col[c - 1:c, :]
        ct = ct_ref[d]
        h = None
        if want_h:
            dmat = jnp.where(keep[d], b_col - b_row + i_row, -jnp.inf)
            m_intra = rep(jnp.max(dmat, axis=-1, keepdims=True))
            s = _mm(q, kt) * jnp.exp(dmat - m_intra)
            num_intra = _mm(s.astype(BF16), v)
            den_intra = rep(jnp.sum(s, axis=-1, keepdims=True))
            m_inter = b_col + m
            m_j = jnp.maximum(m_inter, m_intra)
            intra = jnp.exp(m_intra - m_j)
            inter = jnp.exp(m_inter - m_j)
            num = wide(intra) * num_intra + wide(inter) * _mm(q, ct.astype(BF16))
            qn = rep(jnp.sum(q.astype(F32) * n, axis=-1, keepdims=True))
            den = intra * den_intra + inter * qn
            h = num / wide(jnp.maximum(jnp.abs(den), jnp.exp(-m_j)))
        glog = b_last - b_col + i_col
        b_last = b_last[:, 0:1]
        m_new = jnp.maximum(b_last + m, jnp.max(glog, axis=0, keepdims=True)[:, 0:1])
        wk = jnp.exp(glog - m_new)
        wk_row = jnp.exp(b_last - b_row + i_row - m_new)
        decay = jnp.exp(b_last + m - m_new)
        ct_ref[d] = decay * ct + _mm((kt.astype(F32) * wk_row).astype(BF16), v)
        n_new = decay * n + jnp.sum(wk * k.astype(F32), axis=0, keepdims=True)
        return n_new, m_new, h

    def run(nchunk, base, v_ref, carry, want_h):
        def body(i, carry):
            nf, mf, nb, mb = carry
            ib = nchunk - 1 - i
            nf, mf, h_f = step(0, i, base, v_ref, nf, mf, want_h)
            nb, mb, h_b = step(1, ib, base, v_ref, nb, mb, want_h)
            if want_h:
                hf_ref[pl.ds(pl.multiple_of(i * c, c), c), :] = h_f
                hb_ref[pl.ds(pl.multiple_of(ib * c, c), c), :] = h_b
            return nf, mf, nb, mb

        return lax.fori_loop(0, nchunk, body, carry, unroll=2)

    conv_pass(qc_ref, kc_ref, gc_ref, n_ctx, 0)
    conv_pass(qx_ref, kx_ref, gx_ref, n_lat, n_ctx)
    ct_ref[...] = jnp.zeros(ct_ref.shape, F32)
    zn = jnp.zeros((1, M_DQK), F32)
    zm = jnp.zeros((1, 1), F32)
    carry = run(n_ctx, 0, vc_ref, (zn, zm, zn, zm), False)
    run(n_lat, n_ctx, vx_ref, carry, True)

    def fin(i, carry):
        rows = pl.ds(pl.multiple_of(i * c, c), c)
        o_ref[rows, :] = (hf_ref[rows, :] + hb_ref[rows, :]).astype(o_ref.dtype)
        return carry

    lax.fori_loop(0, n_lat, fin, 0)


def _mlstm(px, pc, batch, cw, cb, gate_row):
    lx = px.shape[1] // batch
    lc = pc.shape[1] // batch
    n_lat, n_ctx = lx // MLSTM_CHUNK, lc // MLSTM_CHUNK

    def slab(nrows, first, width=1):
        return pl.BlockSpec((width, nrows, LANES), lambda b, h: (first // width + h, b, 0))

    def fixed(nrows, idx):
        return pl.BlockSpec((1, nrows, LANES), lambda b, h: (idx, b, 0))

    in_specs = [slab(lx, SLAB_Q), slab(lx, SLAB_K), slab(lx, SLAB_V, 2), fixed(lx, SLAB_LG1),
                slab(lc, SLAB_Q), slab(lc, SLAB_K), slab(lc, SLAB_V, 2), fixed(lc, SLAB_LG1),
                pl.BlockSpec((3, LANES), lambda b, h: (0, h)),
                pl.BlockSpec((3, LANES), lambda b, h: (0, M_HEADS + h)),
                pl.BlockSpec((1, LANES), lambda b, h: (0, h)),
                pl.BlockSpec((1, LANES), lambda b, h: (0, M_HEADS + h)),
                pl.BlockSpec((1, LANES), lambda b, h: (0, 0))]
    nch = n_ctx + n_lat
    return pl.pallas_call(
        functools.partial(_mlstm_kernel, n_ctx, n_lat),
        grid=(batch, M_HEADS),
        in_specs=in_specs,
        out_specs=pl.BlockSpec((lx, M_DV), lambda b, h: (b, h)),
        out_shape=jax.ShapeDtypeStruct((batch * lx, M_HEADS * M_DV), BF16),
        scratch_shapes=[pltpu.VMEM((nch, MLSTM_CHUNK, LANES), BF16),
                        pltpu.VMEM((nch, MLSTM_CHUNK, LANES), BF16),
                        pltpu.VMEM((nch, LANES, MLSTM_CHUNK), BF16),
                        pltpu.VMEM((nch, MLSTM_CHUNK, LANES), F32),
                        pltpu.VMEM((nch, 8, LANES), F32),
                        pltpu.VMEM((2, M_DQK, M_DV), F32),
                        pltpu.VMEM((lx, M_DV), F32),
                        pltpu.VMEM((lx, M_DV), F32)],
        compiler_params=_params(2),
    )(px, px, px, px, pc, pc, pc, pc, cw, cw, cb, cb, gate_row)


def _rwkv_stack(x):
    head0 = lax.broadcasted_iota(jnp.int32, x.shape, 1) < R_N
    zero = jnp.zeros_like(x)
    return jnp.concatenate([jnp.where(head0, x, zero), jnp.where(head0, zero, x)], axis=0)


def _rwkv_fold(x):
    half = x.shape[0] // 2
    return x[0:half] + x[half:2 * half]


def _rwkv_gate_a(la, wa, a0_ref, d):
    return jax.nn.sigmoid(a0_ref[d:d + 1, :] + _mm(la.astype(BF16), wa)[:, d * LANES:(d + 1) * LANES])


def _rwkv_prep_kernel(want_y, r_ref, k_ref, v_ref, lw_ref, la_ref,
                      ww_ref, wa_ref, w0_ref, a0_ref, kk_ref, ka_ref, *rest):
    extra_refs, out_refs = (rest[:4], rest[4:]) if want_y else ((), rest)
    c = RWKV_CHUNK
    c2 = 2 * c
    stack = _rwkv_stack
    rr = lax.broadcasted_iota(jnp.int32, (c2, c2), 0)
    cc = lax.broadcasted_iota(jnp.int32, (c2, c2), 1)
    group_ones = ((rr < c) == (cc < c)).astype(BF16)
    tt = lax.broadcasted_iota(jnp.int32, (c, c2), 0)
    lane = lax.broadcasted_iota(jnp.int32, (c, c2), 1)
    ss = lane & (c - 1)
    head0 = lane < c
    strict = (ss < tt, ss > tt)
    incl = (ss <= tt, ss >= tt)
    eye = (ss == tt).astype(F32)
    tr = lax.broadcasted_iota(jnp.int32, (c, c), 0)
    tc = lax.broadcasted_iota(jnp.int32, (c, c), 1)
    tri = ((tc <= tr).astype(BF16), (tc >= tr).astype(BF16))
    ww = ww_ref[0]
    wa = wa_ref[0]
    k_k = kk_ref[...]
    k_a = ka_ref[...]
    m_ref, n_ref, dec_ref = out_refs[0], out_refs[1], out_refs[2]

    def prep_chunks(js):
        nj = len(js)
        rows = [pl.ds(pl.multiple_of(j * c, c), c) for j in js]
        ch = [(d, i) for d in (0, 1) for i in range(nj)]
        half = lambda d: slice(d * LANES, (d + 1) * LANES)
        r = [r_ref[0, rw, :] for rw in rows]
        k = [k_ref[0, rw, :] for rw in rows]
        v_s = [stack(v_ref[0, rw, :]).astype(BF16) for rw in rows]
        lo_w = [_mm(jnp.tanh(lw_ref[0, rw, :]).astype(BF16), ww) for rw in rows]
        lo_a = [_mm(la_ref[0, rw, :].astype(BF16), wa) for rw in rows]
        kk = [x * k_k for x in k]
        kk = [x * lax.rsqrt(_mm_split(x * x, group_ones, False) + 1e-12) for x in kk]
        logw = [-jnp.exp(-_softplus(-(w0_ref[d:d + 1, :] + lo_w[i][:, half(d)])) - 0.5) for d, i in ch]
        a = [jax.nn.sigmoid(a0_ref[d:d + 1, :] + lo_a[i][:, half(d)]) for d, i in ch]
        pin = [_mm_split(x, tri[d], True) for (d, i), x in zip(ch, logw)]
        ptot = [x[0:1, :] if d else x[c - 1:c, :] for (d, i), x in zip(ch, pin)]
        kd = [k[i] * (1.0 + (a_ - 1.0) * k_a) for (d, i), a_ in zip(ch, a)]
        kka = [kk[i] * a_ for (d, i), a_ in zip(ch, a)]
        e_inv = [jnp.exp(-x) for x in pin]
        e_end = [jnp.exp(pt - x) for pt, x in zip(ptot, pin)]
        r_t = [r[i] * jnp.exp(x) for (d, i), x in zip(ch, pin)]
        a_t = [-kk[i] * jnp.exp(x - lw_) for (d, i), x, lw_ in zip(ch, pin, logw)]
        ar_t = [jnp.concatenate([x, y], axis=0).astype(BF16) for x, y in zip(a_t, r_t)]
        bk_t = [jnp.concatenate([stack(x * e), stack(y * e)], axis=0).astype(BF16)
                for x, y, e in zip(kka, kd, e_inv)]
        bk_end = [jnp.concatenate([x * e, y * e], axis=0).astype(BF16) for x, y, e in zip(kka, kd, e_end)]
        aa = [_mm_nt(x, y) for x, y in zip(ar_t, bk_t)]
        aab = [jnp.where(strict[d], x[0:c, 0:c2], 0.0) for (d, i), x in zip(ch, aa)]
        aak = [jnp.where(strict[d], x[0:c, c2:2 * c2], 0.0).astype(BF16) for (d, i), x in zip(ch, aa)]
        akv = [_mm(x, v_s[i]).astype(BF16) for (d, i), x in zip(ch, aak)]
        xs = [eye + x for x in aab]
        pb = [x.astype(BF16) for x in aab]
        ps = [_mm(x, stack(x)) for x in pb]
        for _ in range(c.bit_length() - 3):
            pb = [x.astype(BF16) for x in ps]
            both = [_mm(jnp.concatenate([x.astype(BF16), p], axis=0), stack(p)) for x, p in zip(xs, pb)]
            xs = [x + y[0:c] for x, y in zip(xs, both)]
            ps = [y[c:c2] for y in both]
        xs = [(x + _mm(x.astype(BF16), stack(p.astype(BF16)))).astype(BF16) for x, p in zip(xs, ps)]
        wu = [_mm(x, jnp.concatenate([stack(y[0:c]), stack(z)], axis=1)).astype(BF16)
              for x, y, z in zip(xs, ar_t, akv)]
        m_mat = [_mm_tn(x[:, 0:c2], y[0:c]) for x, y in zip(wu, bk_end)]
        n_mat = [_mm_tn(jnp.concatenate([x[:, c2:2 * c2], v_ref[0, rows[i], :].astype(BF16)], axis=0), y)
                 for (d, i), x, y in zip(ch, wu, bk_end)]
        for (d, i), mm_, nn_, pt in zip(ch, m_mat, n_mat, ptot):
            slot = (0, 0, d, js[i])
            m_ref[slot] = jnp.where(head0, mm_[0:c], mm_[c:c2]).astype(BF16)
            n_ref[slot] = jnp.where(head0, nn_[0:c], nn_[c:c2])
            dec_ref[slot] = jnp.broadcast_to(jnp.exp(pt), (8, LANES))
        if want_y:
            ark = [jnp.concatenate([jnp.where(incl[d], x[c:c2, 0:c2], 0.0),
                                    jnp.where(incl[d], x[c:c2, c2:2 * c2], 0.0)], axis=1).astype(BF16)
                   for (d, i), x in zip(ch, aa)]
            qy = [_mm(x, jnp.concatenate([
                      jnp.concatenate([stack(w_[:, 0:c2]), stack(w_[:, c2:2 * c2])], axis=1),
                      jnp.concatenate([jnp.zeros((c2, c2), BF16), v_s[i]], axis=1)], axis=0))
                  for (d, i), x, w_ in zip(ch, ark, wu)]
            for (d, i), rt, x in zip(ch, r_t, qy):
                slot = (0, 0, d, js[i])
                out_refs[3][slot] = (rt + x[:, 0:c2]).astype(BF16)
                out_refs[4][slot] = x[:, c2:2 * c2]
            lg0_ref, lg1_ref, wg_ref, rk_ref = extra_refs
            ksum = [k[i] * (2.0 + (a[i] + a[nj + i] - 2.0) * k_a) for i in range(nj)]
            bonus = [_mm_split(r[i] * ksum[i] * rk_ref[...], group_ones, False) * v_ref[0, rows[i], :]
                     for i in range(nj)]
            gate = [_mm(jax.nn.sigmoid(jnp.concatenate([lg0_ref[0, rw, :], lg1_ref[0, rw, :]], axis=1)
                                       ).astype(BF16), wg_ref[...]) for rw in rows]
            for rw, gt, bn in zip(rows, gate, bonus):
                out_refs[5][rw, :] = gt.astype(BF16)
                out_refs[6][rw, :] = (bn * gt).astype(BF16)

    n_chunks = r_ref.shape[1] // c
    width = min(RWKV_PREP_WIDTH, n_chunks)
    assert n_chunks % width == 0

    def body(t, carry):
        prep_chunks([t * width + u for u in range(width)])
        return carry

    lax.fori_loop(0, n_chunks // width, body, 0)


def _rwkv_scan_kernel(mc_ref, nc_ref, dc_ref, mf_ref, nf_ref, df_ref, qf_ref, ylf_ref,
                      mb_ref, nb_ref, db_ref, qb_ref, ylb_ref,
                      gate_ref, bg_ref, gnw_ref, gnb_ref,
                      o_ref, z_ref, yf_ref, yb_ref):
    c = RWKV_CHUNK
    g = pl.program_id(2)
    n_groups = pl.num_programs(2)
    group = mf_ref.shape[3]
    n_ctx = mc_ref.shape[3]
    pairs = range(RWKV_SCAN_PAIRS)

    def advance(z, m_c, n_c, dec):
        return z * dec[0:1, :] + _mm(z.astype(BF16), _rwkv_stack(m_c)) + _rwkv_stack(n_c)

    @pl.when(g == 0)
    def _():
        z_ref[...] = jnp.zeros(z_ref.shape, F32)

        def ctx_body(i, carry):
            for pp in pairs:
                for d, ii in ((0, i), (1, n_ctx - 1 - i)):
                    zi = 2 * pp + d
                    z_ref[zi] = advance(z_ref[zi], mc_ref[0, pp, d, ii], nc_ref[0, pp, d, ii],
                                        dc_ref[0, pp, d, ii])
            return carry

        lax.fori_loop(0, n_ctx, ctx_body, 0)

    def lat_body(j, carry):
        chains = [(pp,) + t for pp in pairs for t in (
            (0, j, g * group, mf_ref, nf_ref, df_ref, qf_ref, ylf_ref, yf_ref),
            (1, group - 1 - j, (n_groups - 1 - g) * group, mb_ref, nb_ref, db_ref, qb_ref, ylb_ref, yb_ref))]
        zs = [z_ref[2 * pp + d] for pp, d, *_ in chains]
        zb = [z.astype(BF16) for z in zs]
        ys = [_mm_nt(_rwkv_stack(q_ref[0, pp, 0, jj]), z)
              for (pp, d, jj, first, m_ref, n_ref, d_ref, q_ref, yl_ref, y_ref), z in zip(chains, zb)]
        zn = [z * d_ref[0, pp, 0, jj][0:1, :] + _mm(zh, _rwkv_stack(m_ref[0, pp, 0, jj]))
              + _rwkv_stack(n_ref[0, pp, 0, jj])
              for (pp, d, jj, first, m_ref, n_ref, d_ref, q_ref, yl_ref, y_ref), z, zh in zip(chains, zs, zb)]
        for (pp, d, jj, first, m_ref, n_ref, d_ref, q_ref, yl_ref, y_ref), y, z in zip(chains, ys, zn):
            z_ref[2 * pp + d] = z
            y_ref[pl.ds(pl.multiple_of((first + jj) * c, c), c), pp * LANES:(pp + 1) * LANES] = (
                _rwkv_fold(y) + yl_ref[0, pp, 0, jj])
        return carry

    lax.fori_loop(0, group, lat_body, 0)

    @pl.when(g == n_groups - 1)
    def _():
        fr = 4 * c
        r2 = lax.broadcasted_iota(jnp.int32, (LANES, LANES), 0)
        c2 = lax.broadcasted_iota(jnp.int32, (LANES, LANES), 1)
        group_ones = ((r2 < R_N) == (c2 < R_N)).astype(BF16)
        inv_n = 1.0 / R_N

        def fin(i, carry):
            rows = pl.ds(pl.multiple_of(i * fr, fr), fr)
            for pp in pairs:
                cols = slice(pp * LANES, (pp + 1) * LANES)
                y = yf_ref[rows, cols] + yb_ref[rows, cols]
                sums = _mm_split(jnp.concatenate([y, y * y], axis=0), group_ones, False)
                mu = sums[0:fr] * inv_n
                var = sums[fr:2 * fr] * inv_n - mu * mu
                yn = (y - mu) * lax.rsqrt(var + GN_EPS) * gnw_ref[:, cols] + gnb_ref[:, cols]
                o_ref[rows, cols] = (yn * gate_ref[rows, cols].astype(F32)
                                     + bg_ref[rows, cols].astype(F32)).astype(o_ref.dtype)
            return carry

        lax.fori_loop(0, o_ref.shape[0] // fr, fin, 0)


def _rwkv_prep(p_all, batch, group, want_y, ww, wa, w0, a0, k_k, k_a, wg=None, r_k=None):
    c = RWKV_CHUNK
    length = p_all.shape[1] // batch
    n_chunks = length // c
    n_groups = n_chunks // group
    n_pairs = ww.shape[0]
    rows = group * c

    def slab(first):
        return pl.BlockSpec((1, rows, LANES), lambda b, p, g: (first + p, b * n_groups + g, 0))

    def fixed(idx):
        return pl.BlockSpec((1, rows, LANES), lambda b, p, g: (idx, b * n_groups + g, 0))

    def vec(nrows):
        return pl.BlockSpec((nrows, LANES), lambda b, p, g: (0, p))

    lora = pl.BlockSpec((1, LANES, 2 * LANES), lambda b, p, g: (p, 0, 0))
    outs = [(c, BF16), (c, F32), (8, F32)] + ([(c, BF16), (c, F32)] if want_y else [])
    in_specs = [slab(SLAB_R), slab(SLAB_KR), slab(SLAB_VR), fixed(SLAB_LW), fixed(SLAB_LA),
                lora, lora, vec(2), vec(2), vec(1), vec(1)]
    operands = [p_all, p_all, p_all, p_all, p_all, ww, wa, w0, a0, k_k, k_a]
    out_specs = [pl.BlockSpec((1, 1, 2, group, nr, LANES), lambda b, p, g: (b, p, 0, g, 0, 0)) for nr, _ in outs]
    out_shape = [jax.ShapeDtypeStruct((batch, n_pairs, 2, n_chunks, nr, LANES), dt) for nr, dt in outs]
    if want_y:
        in_specs += [fixed(SLAB_LG0), fixed(SLAB_LG1),
                     pl.BlockSpec((2 * LANES, LANES), lambda b, p, g: (0, p)), vec(1)]
        operands += [p_all, p_all, wg, r_k]
        out_specs += [pl.BlockSpec((rows, LANES), lambda b, p, g: (b * n_groups + g, p))] * 2
        out_shape += [jax.ShapeDtypeStruct((batch * length, n_pairs * LANES), BF16)] * 2
    return pl.pallas_call(
        functools.partial(_rwkv_prep_kernel, want_y),
        grid=(batch, n_pairs, n_groups),
        in_specs=in_specs,
        out_specs=out_specs,
        out_shape=out_shape,
        compiler_params=_params(3),
    )(*operands)


def _rwkv_scan(ops_c, ops_x, gate, bonus_gated, batch, group, gn_w, gn_b):
    c = RWKV_CHUNK
    sp = RWKV_SCAN_PAIRS
    lx = gate.shape[0] // batch
    n_pairs, n_ctx = ops_c[0].shape[1], ops_c[0].shape[3]
    n_groups = ops_x[0].shape[3] // group
    assert n_pairs % sp == 0

    def ctx_block(a):
        return pl.BlockSpec((1, sp, 2, n_ctx, a.shape[4], LANES), lambda b, p, g: (b, p, 0, 0, 0, 0))

    def fwd_block(a):
        return pl.BlockSpec((1, sp, 1, group, a.shape[4], LANES), lambda b, p, g: (b, p, 0, g, 0, 0))

    def bwd_block(a):
        return pl.BlockSpec((1, sp, 1, group, a.shape[4], LANES),
                            lambda b, p, g: (b, p, 1, n_groups - 1 - g, 0, 0))

    tokens = pl.BlockSpec((lx, sp * LANES), lambda b, p, g: (b, p))
    vec = pl.BlockSpec((1, sp * LANES), lambda b, p, g: (0, p))
    return pl.pallas_call(
        _rwkv_scan_kernel,
        grid=(batch, n_pairs // sp, n_groups),
        in_specs=[ctx_block(a) for a in ops_c] + [fwd_block(a) for a in ops_x] + [bwd_block(a) for a in ops_x]
                 + [tokens, tokens, vec, vec],
        out_specs=tokens,
        out_shape=jax.ShapeDtypeStruct((batch * lx, n_pairs * LANES), BF16),
        scratch_shapes=[pltpu.VMEM((2 * sp, 2 * c, LANES), F32),
                        pltpu.VMEM((lx, sp * LANES), F32), pltpu.VMEM((lx, sp * LANES), F32)],
        compiler_params=_params(3),
    )(*ops_c, *ops_x, *ops_x, gate, bonus_gated, gn_w, gn_b)


def _rwkv(px, pc, batch, ww, wa, wg, w0, a0, k_k, k_a, r_k, gn_w, gn_b):
    c = RWKV_CHUNK
    n_ctx = pc.shape[1] // batch // c
    n_lat = px.shape[1] // batch // c
    group = min(RWKV_GROUP, n_lat)
    assert n_lat % group == 0
    ops_c = _rwkv_prep(pc, batch, n_ctx, False, ww, wa, w0, a0, k_k, k_a)
    *ops_x, gate, bonus_gated = _rwkv_prep(px, batch, group, True, ww, wa, w0, a0, k_k, k_a, wg, r_k)
    return _rwkv_scan(ops_c, ops_x, gate, bonus_gated, batch, group, gn_w, gn_b)


def _out_kernel(tiles_per_batch, mixm_ref, om_ref, mixr_ref, w_ref, x_ref, ng_ref,
                gt1_ref, g2_ref, sh2_ref, sc2_ref, x1_ref, hx2_ref):
    b = pl.program_id(0) // tiles_per_batch
    hm = mixm_ref[...].astype(F32)
    ng = ng_ref[...]
    parts = []
    for h in range(M_HEADS):
        cols = slice(h * M_DV, (h + 1) * M_DV)
        seg = hm[:, cols]
        seg = seg * lax.rsqrt(jnp.mean(seg * seg, axis=-1, keepdims=True) + NORM_EPS)
        og = jnp.concatenate([om_ref[2 * h], om_ref[2 * h + 1]], axis=1)
        parts.append((seg * ng[:, cols] * jax.nn.sigmoid(og)).astype(BF16))
    lhs = jnp.concatenate(parts + [mixr_ref[...]], axis=1)
    x1 = x_ref[...] + gt1_ref[pl.ds(b, 1), :] * _mm(lhs, w_ref[...])
    x1_ref[...] = x1
    y = x1 * lax.rsqrt(jnp.mean(x1 * x1, axis=-1, keepdims=True) + NORM_EPS) * g2_ref[...]
    hx2_ref[...] = (y * (1.0 + sc2_ref[pl.ds(b, 1), :]) + sh2_ref[pl.ds(b, 1), :]).astype(BF16)


def _out_proj(mixm, px, mixr, w_out, x2d, ng, gt1, g2, sh2, sc2, tm, tiles_per_batch):
    t, d = x2d.shape
    dm = mixm.shape[1]
    row = lambda i: (i, 0)
    const = lambda i: (0, 0)
    return pl.pallas_call(
        functools.partial(_out_kernel, tiles_per_batch),
        grid=(t // tm,),
        in_specs=[pl.BlockSpec((tm, dm), row),
                  pl.BlockSpec((8, tm, LANES), lambda i: (SLAB_O // 8, i, 0)),
                  pl.BlockSpec((tm, dm), row),
                  pl.BlockSpec((d, d), const),
                  pl.BlockSpec((tm, d), row),
                  pl.BlockSpec((1, dm), const),
                  pl.BlockSpec((8, d), const),
                  pl.BlockSpec((1, d), const),
                  pl.BlockSpec((8, d), const),
                  pl.BlockSpec((8, d), const)],
        out_specs=[pl.BlockSpec((tm, d), row), pl.BlockSpec((tm, d), row)],
        out_shape=[jax.ShapeDtypeStruct((t, d), F32), jax.ShapeDtypeStruct((t, d), BF16)],
        compiler_params=_params(1),
    )(mixm, px, mixr, w_out, x2d, ng, gt1, g2, sh2, sc2)


def _ffn_kernel(tiles_per_img, hx_ref, top_ref, bot_ref, wu_ref, wg_ref, wd_ref, cw_ref, cb_ref,
                x1_ref, gt2_ref, gf_ref, o_ref):
    i = pl.program_id(0)
    j = pl.program_id(1)
    tm = hx_ref.shape[0]
    ti = i % tiles_per_img

    @pl.when(j == 0)
    def _():
        o_ref[...] = jnp.zeros(o_ref.shape, F32)

    hx = hx_ref[...]
    top = jnp.where(ti > 0, top_ref[...], jnp.zeros_like(top_ref[...]))
    bot = jnp.where(ti < tiles_per_img - 1, bot_ref[...], jnp.zeros_like(bot_ref[...]))
    u = _mm(jnp.concatenate([top, hx, bot], axis=0), wu_ref[...])
    nr = u.shape[0]
    col = lax.broadcasted_iota(jnp.int32, u.shape, 0) & (GRID_W - 1)
    ul = jnp.where(col == 0, 0.0, pltpu.roll(u, 1, 0))
    ur = jnp.where(col == GRID_W - 1, 0.0, pltpu.roll(u, nr - 1, 0))
    cw = cw_ref[...]
    conv = cb_ref[...]
    for dy in range(3):
        rows = slice(dy * GRID_W, dy * GRID_W + tm)
        conv = (conv + ul[rows] * cw[3 * dy:3 * dy + 1, :] + u[rows] * cw[3 * dy + 1:3 * dy + 2, :]
                + ur[rows] * cw[3 * dy + 2:3 * dy + 3, :])
    gelu = 0.5 * conv * (1.0 + jnp.tanh(0.7978845608028654 * (conv + 0.044715 * conv * conv * conv)))
    act = (gelu * _mm(hx, wg_ref[...])).astype(BF16)
    o_ref[...] += _mm(act, wd_ref[...])

    @pl.when(j == pl.num_programs(1) - 1)
    def _():
        b = i // tiles_per_img
        x2 = x1_ref[...] + gt2_ref[pl.ds(b, 1), :] * o_ref[...]
        o_ref[...] = x2 * lax.rsqrt(jnp.mean(x2 * x2, axis=-1, keepdims=True) + NORM_EPS) * gf_ref[...]


def _conv_ffn(hx2, w_up, w_gate, w_down, cw, cb, x1, gt2, g_final, tm, tiles_per_img):
    t, d = hx2.shape
    f = w_up.shape[1]
    tf = 512
    rows_per_tile = tm // GRID_W
    n_rows = t // GRID_W
    return pl.pallas_call(
        functools.partial(_ffn_kernel, tiles_per_img),
        grid=(t // tm, f // tf),
        in_specs=[pl.BlockSpec((tm, d), lambda i, j: (i, 0)),
                  pl.BlockSpec((GRID_W, d), lambda i, j: (jnp.maximum(i * rows_per_tile - 1, 0), 0)),
                  pl.BlockSpec((GRID_W, d), lambda i, j: (jnp.minimum((i + 1) * rows_per_tile, n_rows - 1), 0)),
                  pl.BlockSpec((d, tf), lambda i, j: (0, j)),
                  pl.BlockSpec((d, tf), lambda i, j: (0, j)),
                  pl.BlockSpec((tf, d), lambda i, j: (j, 0)),
                  pl.BlockSpec((9, tf), lambda i, j: (0, j)),
                  pl.BlockSpec((1, tf), lambda i, j: (0, j)),
                  pl.BlockSpec((tm, d), lambda i, j: (i, 0), pipeline_mode=pl.Buffered(1)),
                  pl.BlockSpec((8, d), lambda i, j: (0, 0)),
                  pl.BlockSpec((1, d), lambda i, j: (0, 0))],
        out_specs=pl.BlockSpec((tm, d), lambda i, j: (i, 0), pipeline_mode=pl.Buffered(1)),
        out_shape=jax.ShapeDtypeStruct((t, d), F32),
        compiler_params=_params(2),
    )(hx2, hx2, hx2, w_up, w_gate, w_down, cw, cb, x1, gt2, g_final)


def _lora_pairs(up):
    _, rank, width = up.shape
    u = up.reshape(2, rank, width // LANES, LANES).transpose(2, 0, 1, 3)
    z = jnp.zeros_like(u[:, 0])
    top = jnp.concatenate([u[:, 0], z], axis=-1)
    bot = jnp.concatenate([z, u[:, 1]], axis=-1)
    return jnp.concatenate([top, bot], axis=1).astype(BF16)


def kernel(x, c, ctx, c_ctx, w_mod, b_mod, g_norm1, g_norm2, w_in, m_conv_w, m_conv_b, m_gate_b, m_norm_g, r_w0, r_w_up, r_a0, r_a_up, r_g_up, r_k_k, r_k_a, r_r_k, r_gn_w, r_gn_b, w_out, f_w_up, f_w_gate, f_conv_w, f_conv_b, f_w_down, g_final):
    batch, seq, d = x.shape
    ctx_len = ctx.shape[1]
    assert w_mod.shape[0] == 1, "single-layer block"
    assert batch + 1 <= 8 and seq % 512 == 0 and ctx_len % MLSTM_CHUNK == 0

    cv8 = jnp.zeros((8, d), F32).at[:batch].set(c).at[batch].set(c_ctx)
    mod = _modulation(cv8, w_mod[0], b_mod[0])
    sh1, sc1, gt1, sh2, sc2, gt2 = (mod[:, k * d:(k + 1) * d] for k in range(6))

    n_gate = 4 * M_HEADS
    w_p = _permute_w(w_in, SLAB_R * LANES, n_gate, N_SLABS * LANES)
    g1 = g_norm1[0].reshape(1, d)
    tm_x = 1024 if seq % 1024 == 0 else 512
    px = _inproj(x.reshape(batch * seq, d), g1, sh1, sc1, w_p, tm_x, seq // tm_x, 0)
    tm_c = ctx_len
    pc = _inproj(ctx.reshape(batch * ctx_len, d), g1, sh1, sc1, w_p, tm_c, 1 << 30, batch)

    gate_row = jnp.zeros((1, LANES), F32).at[0, GATE_LANE0:GATE_LANE0 + n_gate].set(m_gate_b[0].reshape(-1))
    mixm = _mlstm(px, pc, batch, m_conv_w[0], m_conv_b[0].reshape(1, -1), gate_row)

    rw = r_k_k.shape[1]
    wg = jnp.zeros((2 * LANES, rw), F32).at[:r_g_up.shape[1]].set(r_g_up[0]).astype(BF16)
    mixr = _rwkv(px, pc, batch, _lora_pairs(r_w_up[0]), _lora_pairs(r_a_up[0]), wg,
                 r_w0[0], r_a0[0], r_k_k, r_k_a, r_r_k[0].reshape(1, rw), r_gn_w, r_gn_b)

    tm_o = 512
    x1, hx2 = _out_proj(mixm, px, mixr, w_out[0].astype(BF16), x.reshape(batch * seq, d), m_norm_g,
                        gt1, g_norm2[0].reshape(1, d), sh2, sc2, tm_o, seq // tm_o)

    tm_f = 1024 if seq % 1024 == 0 else 512
    out = _conv_ffn(hx2, f_w_up[0].astype(BF16), f_w_gate[0].astype(BF16), f_w_down[0].astype(BF16),
                    f_conv_w[0].reshape(9, -1), f_conv_b, x1, gt2, g_final.reshape(1, d),
                    tm_f, seq // tm_f)
    return out.reshape(batch, seq, d)
```

```python
import functools

import jax
import jax.numpy as jnp
from jax import lax
from jax.experimental import pallas as pl
from jax.experimental.pallas import tpu as pltpu

F32 = jnp.float32
BF16 = jnp.bfloat16
HIGHEST = lax.Precision.HIGHEST

LANES = 128
GRID_W = 64
M_HEADS = 4
M_DQK = 128
M_DV = 256
R_N = 64
NORM_EPS = 1e-6
GN_EPS = 64e-5
MLSTM_CHUNK = 128
RWKV_CHUNK = 64
RWKV_GROUP = 16
RWKV_SCAN_GROUP = 8
RWKV_SCAN_PAIRS = 4
RWKV_PREP_WIDTH = 16
VMEM_LIMIT = 56 * 1024 * 1024

SLAB_Q, SLAB_K, SLAB_V, SLAB_O = 0, 4, 8, 16
SLAB_R, SLAB_KR, SLAB_VR = 24, 32, 40
SLAB_LW, SLAB_LA, SLAB_LG0, SLAB_LG1 = 48, 49, 50, 51
N_SLABS = 52
GATE_LANE0 = 32


def _mm(a, b, precision=None):
    return jnp.dot(a, b, preferred_element_type=F32, precision=precision)


def _mm_nt(a, b):
    return lax.dot_general(a, b, (((1,), (1,)), ((), ())), preferred_element_type=F32)


def _mm_tn(a, b):
    return lax.dot_general(a, b, (((0,), (0,)), ((), ())), preferred_element_type=F32)


def _mm_split(x, ones, ones_first, pieces=3):
    dot = (lambda p: _mm(ones, p)) if ones_first else (lambda p: _mm(p, ones))
    piece = x.astype(BF16)
    total = dot(piece)
    for _ in range(pieces - 1):
        x = x - piece.astype(F32)
        piece = x.astype(BF16)
        total = total + dot(piece)
    return total


def _softplus(x):
    return jnp.maximum(x, 0.0) + jnp.log1p(jnp.exp(-jnp.abs(x)))


def _log_sigmoid(x):
    return -_softplus(-x)


def _params(n_axes):
    return pltpu.CompilerParams(dimension_semantics=("arbitrary",) * n_axes,
                                vmem_limit_bytes=VMEM_LIMIT)


def _mod_kernel(cv_ref, w_ref, b_ref, o_ref):
    cv = cv_ref[...]
    s = (cv * jax.nn.sigmoid(cv)).astype(BF16)
    o_ref[...] = _mm(s, w_ref[...].astype(BF16)) + b_ref[...]


def _modulation(cv8, w_mod, b_mod):
    d, n = w_mod.shape
    tn = 1024
    return pl.pallas_call(
        _mod_kernel,
        grid=(n // tn,),
        in_specs=[pl.BlockSpec((8, d), lambda j: (0, 0)),
                  pl.BlockSpec((d, tn), lambda j: (0, j)),
                  pl.BlockSpec((1, tn), lambda j: (0, j))],
        out_specs=pl.BlockSpec((8, tn), lambda j: (0, j)),
        out_shape=jax.ShapeDtypeStruct((8, n), F32),
        compiler_params=_params(1),
    )(cv8, w_mod, b_mod.reshape(1, n))


def _permute_w_kernel(n_main, n_gate, w_ref, o_ref):
    w = w_ref[0]
    pad = jnp.zeros((w.shape[0], o_ref.shape[1] - w.shape[1]), F32)
    o_ref[...] = jnp.concatenate([w[:, :n_main], w[:, n_main + n_gate:], w[:, n_main:n_main + n_gate], pad],
                                 axis=1).astype(BF16)


def _permute_w(w, n_main, n_gate, n_out):
    _, d, n = w.shape
    tr = 128
    return pl.pallas_call(
        functools.partial(_permute_w_kernel, n_main, n_gate),
        grid=(d // tr,),
        in_specs=[pl.BlockSpec((1, tr, n), lambda i: (0, i, 0))],
        out_specs=pl.BlockSpec((tr, n_out), lambda i: (i, 0)),
        out_shape=jax.ShapeDtypeStruct((d, n_out), BF16),
        compiler_params=_params(1),
    )(w)


def _inproj_kernel(tiles_per_row, row0, x_ref, g_ref, sh_ref, sc_ref, w_ref, o_ref, hx_ref):
    i = pl.program_id(0)
    j = pl.program_id(1)

    @pl.when(j == 0)
    def _():
        x = x_ref[...]
        ms = jnp.mean(x * x, axis=-1, keepdims=True)
        y = x * lax.rsqrt(ms + NORM_EPS) * g_ref[...]
        r = row0 + i // tiles_per_row
        hx = y * (1.0 + sc_ref[pl.ds(r, 1), :]) + sh_ref[pl.ds(r, 1), :]
        hx_ref[...] = hx.astype(BF16)

    acc = _mm(hx_ref[...], w_ref[...])
    for s in range(acc.shape[1] // LANES):
        o_ref[s] = acc[:, s * LANES:(s + 1) * LANES]


def _inproj(x2d, g, sh, sc, w_p, tm, tiles_per_row, row0):
    t, d = x2d.shape
    n = w_p.shape[1]
    tn = 512
    return pl.pallas_call(
        functools.partial(_inproj_kernel, tiles_per_row, row0),
        grid=(t // tm, n // tn),
        in_specs=[pl.BlockSpec((tm, d), lambda i, j: (i, 0)),
                  pl.BlockSpec((1, d), lambda i, j: (0, 0)),
                  pl.BlockSpec((8, d), lambda i, j: (0, 0)),
                  pl.BlockSpec((8, d), lambda i, j: (0, 0)),
                  pl.BlockSpec((d, tn), lambda i, j: (0, j))],
        out_specs=pl.BlockSpec((tn // LANES, tm, LANES), lambda i, j: (j, i, 0)),
        out_shape=jax.ShapeDtypeStruct((n // LANES, t, LANES), F32),
        scratch_shapes=[pltpu.VMEM((tm, d), BF16)],
        compiler_params=_params(2),
    )(x2d, g, sh, sc, w_p)


def _mlstm_kernel(n_ctx, n_lat,
                  qx_ref, kx_ref, vx_ref, gx_ref, qc_ref, kc_ref, vc_ref, gc_ref,
                  cwq_ref, cwk_ref, cbq_ref, cbk_ref, gb_ref,
                  o_ref,
                  qs_ref, ks_ref, kst_ref, gcol_ref, grow_ref, ct_ref, hf_ref, hb_ref):
    c = MLSTM_CHUNK
    head = pl.program_id(1)
    rid = lax.broadcasted_iota(jnp.int32, (c, LANES), 0)
    lane = lax.broadcasted_iota(jnp.int32, (c, LANES), 1)
    r2 = lax.broadcasted_iota(jnp.int32, (c, c), 0)
    c2 = lax.broadcasted_iota(jnp.int32, (c, c), 1)
    prefix = (c2 <= r2).astype(BF16)
    keep = (c2 <= r2, c2 >= r2)
    gbias = gb_ref[...]

    def conv_pass(q_ref, k_ref, g_ref, nchunk, base):
        nrows = nchunk * c

        def body(ci, carry):
            r0 = pl.multiple_of(ci * c, c)
            p0 = pl.multiple_of(jnp.maximum(r0 - 8, 0), 8)
            n0 = pl.multiple_of(jnp.minimum(r0 + c, nrows - 8), 8)
            gb = g_ref[0, pl.ds(r0, c), :] + gbias
            lf = _log_sigmoid(gb)
            pre = _mm_split(lf, prefix, True)
            suf = pre[c - 1:c, :] - pre + lf
            packed = jnp.zeros((c, LANES), F32)
            for slot, (src, col0) in enumerate(((pre, GATE_LANE0 + 2 * M_HEADS), (suf, GATE_LANE0 + 3 * M_HEADS),
                                                (gb, GATE_LANE0), (gb, GATE_LANE0 + M_HEADS))):
                col = jnp.sum(jnp.where(lane == col0 + head, src, 0.0), axis=1, keepdims=True)
                packed = jnp.where(lane == slot, col, packed)
            gcol_ref[base + ci] = packed
            grow_ref[base + ci] = packed.T[0:8, :]
            for src, w_ref, b_ref, scale, is_k in ((q_ref, cwq_ref, cbq_ref, M_DQK ** -0.5, False),
                                                   (k_ref, cwk_ref, cbk_ref, 1.0, True)):
                cur = src[0, pl.ds(r0, c), :]
                prev_row = jnp.where(ci > 0, src[0, pl.ds(p0, 8), :][7:8, :], 0.0)
                next_row = jnp.where(ci < nchunk - 1, src[0, pl.ds(n0, 8), :][0:1, :], 0.0)
                up = jnp.where(rid == 0, prev_row, pltpu.roll(cur, 1, 0))
                dn = jnp.where(rid == c - 1, next_row, pltpu.roll(cur, c - 1, 0))
                w = w_ref[...]
                y = (up * w[0:1, :] + cur * w[1:2, :] + dn * w[2:3, :] + b_ref[...]) * scale
                if is_k:
                    ks_ref[base + ci] = y.astype(BF16)
                    kst_ref[base + ci] = y.T.astype(BF16)
                else:
                    qs_ref[base + ci] = y.astype(BF16)
            return carry

        lax.fori_loop(0, nchunk, body, 0, unroll=min(4, nchunk))

    def step(d, ci, base, v_ref, n, m, want_h):
        r0 = pl.multiple_of(ci * c, c)
        q = qs_ref[base + ci]
        k = ks_ref[base + ci]
        kt = kst_ref[base + ci]
        v = jnp.concatenate([v_ref[0, pl.ds(r0, c), :], v_ref[1, pl.ds(r0, c), :]], axis=1).astype(BF16)
        gcol = gcol_ref[base + ci]
        grow = grow_ref[base + ci]
        rep = lambda x: jnp.broadcast_to(x, (c, LANES))
        wide = lambda x: jnp.concatenate([x, x], axis=1)
        b_col, i_col = rep(gcol[:, d:d + 1]), rep(gcol[:, 2 + d:3 + d])
        b_row, i_row = grow[d:d + 1, :], grow[2 + d:3 + d, :]
        b_last = b_col[0:1, :] if d else b_col[c - 1:c, :]
        ct = ct_ref[d]
        h = None
        if want_h:
            dmat = jnp.where(keep[d], b_col - b_row + i_row, -jnp.inf)
            m_intra = rep(jnp.max(dmat, axis=-1, keepdims=True))
            s = _mm(q, kt) * jnp.exp(dmat - m_intra)
            num_intra = _mm(s.astype(BF16), v)
            den_intra = rep(jnp.sum(s, axis=-1, keepdims=True))
            m_inter = b_col + m
            m_j = jnp.maximum(m_inter, m_intra)
            intra = jnp.exp(m_intra - m_j)
            inter = jnp.exp(m_inter - m_j)
            num = wide(intra) * num_intra + wide(inter) * _mm(q, ct.astype(BF16))
            qn = rep(jnp.sum(q.astype(F32) * n, axis=-1, keepdims=True))
            den = intra * den_intra + inter * qn
            h = num / wide(jnp.maximum(jnp.abs(den), jnp.exp(-m_j)))
        glog = b_last - b_col + i_col
        b_last = b_last[:, 0:1]
        m_new = jnp.maximum(b_last + m, jnp.max(glog, axis=0, keepdims=True)[:, 0:1])
        wk = jnp.exp(glog - m_new)
        wk_row = jnp.exp(b_last - b_row + i_row - m_new)
        decay = jnp.exp(b_last + m - m_new)
        ct_ref[d] = decay * ct + _mm((kt.astype(F32) * wk_row).astype(BF16), v)
        n_new = decay * n + jnp.sum(wk * k.astype(F32), axis=0, keepdims=True)
        return n_new, m_new, h

    def run(nchunk, base, v_ref, carry, want_h):
        def body(i, carry):
            nf, mf, nb, mb = carry
            ib = nchunk - 1 - i
            nf, mf, h_f = step(0, i, base, v_ref, nf, mf, want_h)
            nb, mb, h_b = step(1, ib, base, v_ref, nb, mb, want_h)
            if want_h:
                hf_ref[pl.ds(pl.multiple_of(i * c, c), c), :] = h_f
                hb_ref[pl.ds(pl.multiple_of(ib * c, c), c), :] = h_b
            return nf, mf, nb, mb

        return lax.fori_loop(0, nchunk, body, carry, unroll=2)

    conv_pass(qc_ref, kc_ref, gc_ref, n_ctx, 0)
    conv_pass(qx_ref, kx_ref, gx_ref, n_lat, n_ctx)
    ct_ref[...] = jnp.zeros(ct_ref.shape, F32)
    zn = jnp.zeros((1, M_DQK), F32)
    zm = jnp.zeros((1, 1), F32)
    carry = run(n_ctx, 0, vc_ref, (zn, zm, zn, zm), False)
    run(n_lat, n_ctx, vx_ref, carry, True)

    def fin(i, carry):
        rows = pl.ds(pl.multiple_of(i * c, c), c)
        o_ref[rows, :] = (hf_ref[rows, :] + hb_ref[rows, :]).astype(o_ref.dtype)
        return carry

    lax.fori_loop(0, n_lat, fin, 0)


def _mlstm(px, pc, batch, cw, cb, gate_row):
    lx = px.shape[1] // batch
    lc = pc.shape[1] // batch
    n_lat, n_ctx = lx // MLSTM_CHUNK, lc // MLSTM_CHUNK

    def slab(nrows, first, width=1):
        return pl.BlockSpec((width, nrows, LANES), lambda b, h: (first // width + h, b, 0))

    def fixed(nrows, idx):
        return pl.BlockSpec((1, nrows, LANES), lambda b, h: (idx, b, 0))

    in_specs = [slab(lx, SLAB_Q), slab(lx, SLAB_K), slab(lx, SLAB_V, 2), fixed(lx, SLAB_LG1),
                slab(lc, SLAB_Q), slab(lc, SLAB_K), slab(lc, SLAB_V, 2), fixed(lc, SLAB_LG1),
                pl.BlockSpec((3, LANES), lambda b, h: (0, h)),
                pl.BlockSpec((3, LANES), lambda b, h: (0, M_HEADS + h)),
                pl.BlockSpec((1, LANES), lambda b, h: (0, h)),
                pl.BlockSpec((1, LANES), lambda b, h: (0, M_HEADS + h)),
                pl.BlockSpec((1, LANES), lambda b, h: (0, 0))]
    nch = n_ctx + n_lat
    return pl.pallas_call(
        functools.partial(_mlstm_kernel, n_ctx, n_lat),
        grid=(batch, M_HEADS),
        in_specs=in_specs,
        out_specs=pl.BlockSpec((lx, M_DV), lambda b, h: (b, h)),
        out_shape=jax.ShapeDtypeStruct((batch * lx, M_HEADS * M_DV), BF16),
        scratch_shapes=[pltpu.VMEM((nch, MLSTM_CHUNK, LANES), BF16),
                        pltpu.VMEM((nch, MLSTM_CHUNK, LANES), BF16),
                        pltpu.VMEM((nch, LANES, MLSTM_CHUNK), BF16),
                        pltpu.VMEM((nch, MLSTM_CHUNK, LANES), F32),
                        pltpu.VMEM((nch, 8, LANES), F32),
                        pltpu.VMEM((2, M_DQK, M_DV), F32),
                        pltpu.VMEM((lx, M_DV), F32),
                        pltpu.VMEM((lx, M_DV), F32)],
        compiler_params=_params(2),
    )(px, px, px, px, pc, pc, pc, pc, cw, cw, cb, cb, gate_row)


def _rwkv_stack(x):
    head0 = lax.broadcasted_iota(jnp.int32, x.shape, 1) < R_N
    zero = jnp.zeros_like(x)
    return jnp.concatenate([jnp.where(head0, x, zero), jnp.where(head0, zero, x)], axis=0)


def _rwkv_fold(x):
    half = x.shape[0] // 2
    return x[0:half] + x[half:2 * half]


def _rwkv_gate_a(la, wa, a0_ref, d):
    return jax.nn.sigmoid(a0_ref[d:d + 1, :] + _mm(la.astype(BF16), wa)[:, d * LANES:(d + 1) * LANES])


def _rwkv_prep_kernel(want_y, r_ref, k_ref, v_ref, lw_ref, la_ref,
                      ww_ref, wa_ref, w0_ref, a0_ref, kk_ref, ka_ref, *rest):
    extra_refs, out_refs = (rest[:4], rest[4:]) if want_y else ((), rest)
    c = RWKV_CHUNK
    c2 = 2 * c
    stack = _rwkv_stack
    rr = lax.broadcasted_iota(jnp.int32, (c2, c2), 0)
    cc = lax.broadcasted_iota(jnp.int32, (c2, c2), 1)
    group_ones = ((rr < c) == (cc < c)).astype(BF16)
    tt = lax.broadcasted_iota(jnp.int32, (c, c2), 0)
    lane = lax.broadcasted_iota(jnp.int32, (c, c2), 1)
    ss = lane & (c - 1)
    head0 = lane < c
    strict = (ss < tt, ss > tt)
    incl = (ss <= tt, ss >= tt)
    eye = (ss == tt).astype(F32)
    tr = lax.broadcasted_iota(jnp.int32, (c, c), 0)
    tc = lax.broadcasted_iota(jnp.int32, (c, c), 1)
    tri = ((tc <= tr).astype(BF16), (tc >= tr).astype(BF16))
    ww = ww_ref[0]
    wa = wa_ref[0]
    k_k = kk_ref[...]
    k_a = ka_ref[...]
    m_ref, n_ref, dec_ref = out_refs[0], out_refs[1], out_refs[2]

    def prep_chunks(js):
        nj = len(js)
        rows = [pl.ds(pl.multiple_of(j * c, c), c) for j in js]
        ch = [(d, i) for d in (0, 1) for i in range(nj)]
        half = lambda d: slice(d * LANES, (d + 1) * LANES)
        r = [r_ref[0, rw, :] for rw in rows]
        k = [k_ref[0, rw, :] for rw in rows]
        v_s = [stack(v_ref[0, rw, :]).astype(BF16) for rw in rows]
        lo_w = [_mm(jnp.tanh(lw_ref[0, rw, :]).astype(BF16), ww) for rw in rows]
        lo_a = [_mm(la_ref[0, rw, :].astype(BF16), wa) for rw in rows]
        kk = [x * k_k for x in k]
        kk = [x * lax.rsqrt(_mm_split(x * x, group_ones, False) + 1e-12) for x in kk]
        logw = [-jnp.exp(-_softplus(-(w0_ref[d:d + 1, :] + lo_w[i][:, half(d)])) - 0.5) for d, i in ch]
        a = [jax.nn.sigmoid(a0_ref[d:d + 1, :] + lo_a[i][:, half(d)]) for d, i in ch]
        pin = [_mm_split(x, tri[d], True) for (d, i), x in zip(ch, logw)]
        ptot = [x[0:1, :] if d else x[c - 1:c, :] for (d, i), x in zip(ch, pin)]
        kd = [k[i] * (1.0 + (a_ - 1.0) * k_a) for (d, i), a_ in zip(ch, a)]
        kka = [kk[i] * a_ for (d, i), a_ in zip(ch, a)]
        e_inv = [jnp.exp(-x) for x in pin]
        e_end = [jnp.exp(pt - x) for pt, x in zip(ptot, pin)]
        r_t = [r[i] * jnp.exp(x) for (d, i), x in zip(ch, pin)]
        a_t = [-kk[i] * jnp.exp(x - lw_) for (d, i), x, lw_ in zip(ch, pin, logw)]
        ar_t = [jnp.concatenate([x, y], axis=0).astype(BF16) for x, y in zip(a_t, r_t)]
        bk_t = [jnp.concatenate([stack(x * e), stack(y * e)], axis=0).astype(BF16)
                for x, y, e in zip(kka, kd, e_inv)]
        bk_end = [jnp.concatenate([x * e, y * e], axis=0).astype(BF16) for x, y, e in zip(kka, kd, e_end)]
        aa = [_mm_nt(x, y) for x, y in zip(ar_t, bk_t)]
        aab = [jnp.where(strict[d], x[0:c, 0:c2], 0.0) for (d, i), x in zip(ch, aa)]
        aak = [jnp.where(strict[d], x[0:c, c2:2 * c2], 0.0).astype(BF16) for (d, i), x in zip(ch, aa)]
        akv = [_mm(x, v_s[i]).astype(BF16) for (d, i), x in zip(ch, aak)]
        xs = [eye + x for x in aab]
        pb = [x.astype(BF16) for x in aab]
        ps = [_mm(x, stack(x)) for x in pb]
        for _ in range(c.bit_length() - 3):
            pb = [x.astype(BF16) for x in ps]
            both = [_mm(jnp.concatenate([x.astype(BF16), p], axis=0), stack(p)) for x, p in zip(xs, pb)]
            xs = [x + y[0:c] for x, y in zip(xs, both)]
            ps = [y[c:c2] for y in both]
        xs = [(x + _mm(x.astype(BF16), stack(p.astype(BF16)))).astype(BF16) for x, p in zip(xs, ps)]
        wu = [_mm(x, jnp.concatenate([stack(y[0:c]), stack(z)], axis=1)).astype(BF16)
              for x, y, z in zip(xs, ar_t, akv)]
        m_mat = [_mm_tn(x[:, 0:c2], y[0:c]) for x, y in zip(wu, bk_end)]
        n_mat = [_mm_tn(jnp.concatenate([x[:, c2:2 * c2], v_ref[0, rows[i], :].astype(BF16)], axis=0), y)
                 for (d, i), x, y in zip(ch, wu, bk_end)]
        for (d, i), mm_, nn_, pt in zip(ch, m_mat, n_mat, ptot):
            slot = (0, 0, d, js[i])
            m_ref[slot] = jnp.where(head0, mm_[0:c], mm_[c:c2]).astype(BF16)
            n_ref[slot] = jnp.where(head0, nn_[0:c], nn_[c:c2])
            dec_ref[slot] = jnp.broadcast_to(jnp.exp(pt), (8, LANES))
        if want_y:
            ark = [jnp.concatenate([jnp.where(incl[d], x[c:c2, 0:c2], 0.0),
                                    jnp.where(incl[d], x[c:c2, c2:2 * c2], 0.0)], axis=1).astype(BF16)
                   for (d, i), x in zip(ch, aa)]
            qy = [_mm(x, jnp.concatenate([
                      jnp.concatenate([stack(w_[:, 0:c2]), stack(w_[:, c2:2 * c2])], axis=1),
                      jnp.concatenate([jnp.zeros((c2, c2), BF16), v_s[i]], axis=1)], axis=0))
                  for (d, i), x, w_ in zip(ch, ark, wu)]
            for (d, i), rt, x in zip(ch, r_t, qy):
                slot = (0, 0, d, js[i])
                out_refs[3][slot] = (rt + x[:, 0:c2]).astype(BF16)
                out_refs[4][slot] = x[:, c2:2 * c2]
            lg0_ref, lg1_ref, wg_ref, rk_ref = extra_refs
            ksum = [k[i] * (2.0 + (a[i] + a[nj + i] - 2.0) * k_a) for i in range(nj)]
            bonus = [_mm_split(r[i] * ksum[i] * rk_ref[...], group_ones, False) * v_ref[0, rows[i], :]
                     for i in range(nj)]
            gate = [_mm(jax.nn.sigmoid(jnp.concatenate([lg0_ref[0, rw, :], lg1_ref[0, rw, :]], axis=1)
                                       ).astype(BF16), wg_ref[...]) for rw in rows]
            for rw, gt, bn in zip(rows, gate, bonus):
                out_refs[5][rw, :] = gt.astype(BF16)
                out_refs[6][rw, :] = (bn * gt).astype(BF16)

    n_chunks = r_ref.shape[1] // c
    width = min(RWKV_PREP_WIDTH, n_chunks)
    assert n_chunks % width == 0

    def body(t, carry):
        prep_chunks([t * width + u for u in range(width)])
        return carry

    lax.fori_loop(0, n_chunks // width, body, 0)


def _rwkv_scan_kernel(mc_ref, nc_ref, dc_ref, mf_ref, nf_ref, df_ref, qf_ref, ylf_ref,
                      mb_ref, nb_ref, db_ref, qb_ref, ylb_ref,
                      gate_ref, bg_ref, gnw_ref, gnb_ref,
                      o_ref, z_ref, yf_ref, yb_ref):
    c = RWKV_CHUNK
    g = pl.program_id(2)
    n_groups = pl.num_programs(2)
    group = mf_ref.shape[3]
    n_ctx = mc_ref.shape[3]
    pairs = range(RWKV_SCAN_PAIRS)

    def advance(z, m_c, n_c, dec):
        return z * dec[0:1, :] + _mm(z.astype(BF16), _rwkv_stack(m_c)) + _rwkv_stack(n_c)

    @pl.when(g == 0)
    def _():
        z_ref[...] = jnp.zeros(z_ref.shape, F32)

        def ctx_body(i, carry):
            for pp in pairs:
                for d, ii in ((0, i), (1, n_ctx - 1 - i)):
                    zi = 2 * pp + d
                    z_ref[zi] = advance(z_ref[zi], mc_ref[0, pp, d, ii], nc_ref[0, pp, d, ii],
                                        dc_ref[0, pp, d, ii])
            return carry

        lax.fori_loop(0, n_ctx, ctx_body, 0)

    def lat_body(j, carry):
        chains = [(pp,) + t for pp in pairs for t in (
            (0, j, g * group, mf_ref, nf_ref, df_ref, qf_ref, ylf_ref, yf_ref),
            (1, group - 1 - j, (n_groups - 1 - g) * group, mb_ref, nb_ref, db_ref, qb_ref, ylb_ref, yb_ref))]
        zs = [z_ref[2 * pp + d] for pp, d, *_ in chains]
        zb = [z.astype(BF16) for z in zs]
        ys = [_mm_nt(_rwkv_stack(q_ref[0, pp, 0, jj]), z)
              for (pp, d, jj, first, m_ref, n_ref, d_ref, q_ref, yl_ref, y_ref), z in zip(chains, zb)]
        zn = [z * d_ref[0, pp, 0, jj][0:1, :] + _mm(zh, _rwkv_stack(m_ref[0, pp, 0, jj]))
              + _rwkv_stack(n_ref[0, pp, 0, jj])
              for (pp, d, jj, first, m_ref, n_ref, d_ref, q_ref, yl_ref, y_ref), z, zh in zip(chains, zs, zb)]
        for (pp, d, jj, first, m_ref, n_ref, d_ref, q_ref, yl_ref, y_ref), y, z in zip(chains, ys, zn):
            z_ref[2 * pp + d] = z
            y_ref[pl.ds(pl.multiple_of((first + jj) * c, c), c), pp * LANES:(pp + 1) * LANES] = (
                _rwkv_fold(y) + yl_ref[0, pp, 0, jj])
        return carry

    lax.fori_loop(0, group, lat_body, 0)

    @pl.when(g == n_groups - 1)
    def _():
        fr = 4 * c
        r2 = lax.broadcasted_iota(jnp.int32, (LANES, LANES), 0)
        c2 = lax.broadcasted_iota(jnp.int32, (LANES, LANES), 1)
        group_ones = ((r2 < R_N) == (c2 < R_N)).astype(BF16)
        inv_n = 1.0 / R_N

        def fin(i, carry):
            rows = pl.ds(pl.multiple_of(i * fr, fr), fr)
            for pp in pairs:
                cols = slice(pp * LANES, (pp + 1) * LANES)
                y = yf_ref[rows, cols] + yb_ref[rows, cols]
                sums = _mm_split(jnp.concatenate([y, y * y], axis=0), group_ones, False, pieces=2)
                mu = sums[0:fr] * inv_n
                var = sums[fr:2 * fr] * inv_n - mu * mu
                yn = (y - mu) * lax.rsqrt(var + GN_EPS) * gnw_ref[:, cols] + gnb_ref[:, cols]
                o_ref[rows, cols] = (yn * gate_ref[rows, cols].astype(F32)
                                     + bg_ref[rows, cols].astype(F32)).astype(o_ref.dtype)
            return carry

        lax.fori_loop(0, o_ref.shape[0] // fr, fin, 0)


def _rwkv_prep(p_all, batch, group, want_y, ww, wa, w0, a0, k_k, k_a, wg=None, r_k=None):
    c = RWKV_CHUNK
    length = p_all.shape[1] // batch
    n_chunks = length // c
    n_groups = n_chunks // group
    n_pairs = ww.shape[0]
    rows = group * c

    def slab(first):
        return pl.BlockSpec((1, rows, LANES), lambda b, p, g: (first + p, b * n_groups + g, 0))

    def fixed(idx):
        return pl.BlockSpec((1, rows, LANES), lambda b, p, g: (idx, b * n_groups + g, 0))

    def vec(nrows):
        return pl.BlockSpec((nrows, LANES), lambda b, p, g: (0, p))

    lora = pl.BlockSpec((1, LANES, 2 * LANES), lambda b, p, g: (p, 0, 0))
    outs = [(c, BF16), (c, F32), (8, F32)] + ([(c, BF16), (c, F32)] if want_y else [])
    in_specs = [slab(SLAB_R), slab(SLAB_KR), slab(SLAB_VR), fixed(SLAB_LW), fixed(SLAB_LA),
                lora, lora, vec(2), vec(2), vec(1), vec(1)]
    operands = [p_all, p_all, p_all, p_all, p_all, ww, wa, w0, a0, k_k, k_a]
    out_specs = [pl.BlockSpec((1, 1, 2, group, nr, LANES), lambda b, p, g: (b, p, 0, g, 0, 0)) for nr, _ in outs]
    out_shape = [jax.ShapeDtypeStruct((batch, n_pairs, 2, n_chunks, nr, LANES), dt) for nr, dt in outs]
    if want_y:
        in_specs += [fixed(SLAB_LG0), fixed(SLAB_LG1),
                     pl.BlockSpec((2 * LANES, LANES), lambda b, p, g: (0, p)), vec(1)]
        operands += [p_all, p_all, wg, r_k]
        out_specs += [pl.BlockSpec((rows, LANES), lambda b, p, g: (b * n_groups + g, p))] * 2
        out_shape += [jax.ShapeDtypeStruct((batch * length, n_pairs * LANES), BF16)] * 2
    return pl.pallas_call(
        functools.partial(_rwkv_prep_kernel, want_y),
        grid=(batch, n_pairs, n_groups),
        in_specs=in_specs,
        out_specs=out_specs,
        out_shape=out_shape,
        compiler_params=_params(3),
    )(*operands)


def _rwkv_scan(ops_c, ops_x, gate, bonus_gated, batch, group, gn_w, gn_b):
    c = RWKV_CHUNK
    sp = RWKV_SCAN_PAIRS
    lx = gate.shape[0] // batch
    n_pairs, n_ctx = ops_c[0].shape[1], ops_c[0].shape[3]
    n_groups = ops_x[0].shape[3] // group
    assert n_pairs % sp == 0

    def ctx_block(a):
        return pl.BlockSpec((1, sp, 2, n_ctx, a.shape[4], LANES), lambda b, p, g: (b, p, 0, 0, 0, 0))

    def fwd_block(a):
        return pl.BlockSpec((1, sp, 1, group, a.shape[4], LANES), lambda b, p, g: (b, p, 0, g, 0, 0))

    def bwd_block(a):
        return pl.BlockSpec((1, sp, 1, group, a.shape[4], LANES),
                            lambda b, p, g: (b, p, 1, n_groups - 1 - g, 0, 0))

    tokens = pl.BlockSpec((lx, sp * LANES), lambda b, p, g: (b, p), pipeline_mode=pl.Buffered(1))
    vec = pl.BlockSpec((1, sp * LANES), lambda b, p, g: (0, p))
    return pl.pallas_call(
        _rwkv_scan_kernel,
        grid=(batch, n_pairs // sp, n_groups),
        in_specs=[ctx_block(a) for a in ops_c] + [fwd_block(a) for a in ops_x] + [bwd_block(a) for a in ops_x]
                 + [tokens, tokens, vec, vec],
        out_specs=tokens,
        out_shape=jax.ShapeDtypeStruct((batch * lx, n_pairs * LANES), BF16),
        scratch_shapes=[pltpu.VMEM((2 * sp, 2 * c, LANES), F32),
                        pltpu.VMEM((lx, sp * LANES), F32), pltpu.VMEM((lx, sp * LANES), F32)],
        compiler_params=_params(3),
    )(*ops_c, *ops_x, *ops_x, gate, bonus_gated, gn_w, gn_b)


def _rwkv(px, pc, batch, ww, wa, wg, w0, a0, k_k, k_a, r_k, gn_w, gn_b):
    c = RWKV_CHUNK
    n_ctx = pc.shape[1] // batch // c
    n_lat = px.shape[1] // batch // c
    group = min(RWKV_GROUP, n_lat)
    assert n_lat % group == 0
    ops_c = _rwkv_prep(pc, batch, n_ctx, False, ww, wa, w0, a0, k_k, k_a)
    *ops_x, gate, bonus_gated = _rwkv_prep(px, batch, group, True, ww, wa, w0, a0, k_k, k_a, wg, r_k)
    scan_group = min(RWKV_SCAN_GROUP, n_lat)
    assert n_lat % scan_group == 0
    return _rwkv_scan(ops_c, ops_x, gate, bonus_gated, batch, scan_group, gn_w, gn_b)


def _out_kernel(tiles_per_batch, mixm_ref, om_ref, mixr_ref, w_ref, x_ref, ng_ref,
                gt1_ref, g2_ref, sh2_ref, sc2_ref, x1_ref, hx2_ref):
    b = pl.program_id(0) // tiles_per_batch
    hm = mixm_ref[...].astype(F32)
    ng = ng_ref[...]
    parts = []
    for h in range(M_HEADS):
        cols = slice(h * M_DV, (h + 1) * M_DV)
        seg = hm[:, cols]
        seg = seg * lax.rsqrt(jnp.mean(seg * seg, axis=-1, keepdims=True) + NORM_EPS)
        og = jnp.concatenate([om_ref[2 * h], om_ref[2 * h + 1]], axis=1)
        parts.append((seg * ng[:, cols] * jax.nn.sigmoid(og)).astype(BF16))
    lhs = jnp.concatenate(parts + [mixr_ref[...]], axis=1)
    x1 = x_ref[...] + gt1_ref[pl.ds(b, 1), :] * _mm(lhs, w_ref[...])
    x1_ref[...] = x1
    y = x1 * lax.rsqrt(jnp.mean(x1 * x1, axis=-1, keepdims=True) + NORM_EPS) * g2_ref[...]
    hx2_ref[...] = (y * (1.0 + sc2_ref[pl.ds(b, 1), :]) + sh2_ref[pl.ds(b, 1), :]).astype(BF16)


def _out_proj(mixm, px, mixr, w_out, x2d, ng, gt1, g2, sh2, sc2, tm, tiles_per_batch):
    t, d = x2d.shape
    dm = mixm.shape[1]
    row = lambda i: (i, 0)
    const = lambda i: (0, 0)
    return pl.pallas_call(
        functools.partial(_out_kernel, tiles_per_batch),
        grid=(t // tm,),
        in_specs=[pl.BlockSpec((tm, dm), row),
                  pl.BlockSpec((8, tm, LANES), lambda i: (SLAB_O // 8, i, 0)),
                  pl.BlockSpec((tm, dm), row),
                  pl.BlockSpec((d, d), const),
                  pl.BlockSpec((tm, d), row),
                  pl.BlockSpec((1, dm), const),
                  pl.BlockSpec((8, d), const),
                  pl.BlockSpec((1, d), const),
                  pl.BlockSpec((8, d), const),
                  pl.BlockSpec((8, d), const)],
        out_specs=[pl.BlockSpec((tm, d), row), pl.BlockSpec((tm, d), row)],
        out_shape=[jax.ShapeDtypeStruct((t, d), F32), jax.ShapeDtypeStruct((t, d), BF16)],
        compiler_params=_params(1),
    )(mixm, px, mixr, w_out, x2d, ng, gt1, g2, sh2, sc2)


def _ffn_kernel(tiles_per_img, hx_ref, top_ref, bot_ref, wu_ref, wg_ref, wd_ref, cw_ref, cb_ref,
                x1_ref, gt2_ref, gf_ref, o_ref):
    i = pl.program_id(0)
    j = pl.program_id(1)
    tm = hx_ref.shape[0]
    ti = i % tiles_per_img

    @pl.when(j == 0)
    def _():
        o_ref[...] = jnp.zeros(o_ref.shape, F32)

    hx = hx_ref[...]
    top = jnp.where(ti > 0, top_ref[...], jnp.zeros_like(top_ref[...]))
    bot = jnp.where(ti < tiles_per_img - 1, bot_ref[...], jnp.zeros_like(bot_ref[...]))
    u = _mm(jnp.concatenate([top, hx, bot], axis=0), wu_ref[...])
    nr = u.shape[0]
    col = lax.broadcasted_iota(jnp.int32, u.shape, 0) & (GRID_W - 1)
    ul = jnp.where(col == 0, 0.0, pltpu.roll(u, 1, 0))
    ur = jnp.where(col == GRID_W - 1, 0.0, pltpu.roll(u, nr - 1, 0))
    cw = cw_ref[...]
    conv = cb_ref[...]
    for dy in range(3):
        rows = slice(dy * GRID_W, dy * GRID_W + tm)
        conv = (conv + ul[rows] * cw[3 * dy:3 * dy + 1, :] + u[rows] * cw[3 * dy + 1:3 * dy + 2, :]
                + ur[rows] * cw[3 * dy + 2:3 * dy + 3, :])
    gelu = 0.5 * conv * (1.0 + jnp.tanh(0.7978845608028654 * (conv + 0.044715 * conv * conv * conv)))
    act = (gelu * _mm(hx, wg_ref[...])).astype(BF16)
    o_ref[...] += _mm(act, wd_ref[...])

    @pl.when(j == pl.num_programs(1) - 1)
    def _():
        b = i // tiles_per_img
        x2 = x1_ref[...] + gt2_ref[pl.ds(b, 1), :] * o_ref[...]
        o_ref[...] = x2 * lax.rsqrt(jnp.mean(x2 * x2, axis=-1, keepdims=True) + NORM_EPS) * gf_ref[...]


def _conv_ffn(hx2, w_up, w_gate, w_down, cw, cb, x1, gt2, g_final, tm, tiles_per_img):
    t, d = hx2.shape
    f = w_up.shape[1]
    tf = 512
    rows_per_tile = tm // GRID_W
    n_rows = t // GRID_W
    return pl.pallas_call(
        functools.partial(_ffn_kernel, tiles_per_img),
        grid=(t // tm, f // tf),
        in_specs=[pl.BlockSpec((tm, d), lambda i, j: (i, 0)),
                  pl.BlockSpec((GRID_W, d), lambda i, j: (jnp.maximum(i * rows_per_tile - 1, 0), 0)),
                  pl.BlockSpec((GRID_W, d), lambda i, j: (jnp.minimum((i + 1) * rows_per_tile, n_rows - 1), 0)),
                  pl.BlockSpec((d, tf), lambda i, j: (0, j)),
                  pl.BlockSpec((d, tf), lambda i, j: (0, j)),
                  pl.BlockSpec((tf, d), lambda i, j: (j, 0)),
                  pl.BlockSpec((9, tf), lambda i, j: (0, j)),
                  pl.BlockSpec((1, tf), lambda i, j: (0, j)),
                  pl.BlockSpec((tm, d), lambda i, j: (i, 0), pipeline_mode=pl.Buffered(1)),
                  pl.BlockSpec((8, d), lambda i, j: (0, 0)),
                  pl.BlockSpec((1, d), lambda i, j: (0, 0))],
        out_specs=pl.BlockSpec((tm, d), lambda i, j: (i, 0), pipeline_mode=pl.Buffered(1)),
        out_shape=jax.ShapeDtypeStruct((t, d), F32),
        compiler_params=_params(2),
    )(hx2, hx2, hx2, w_up, w_gate, w_down, cw, cb, x1, gt2, g_final)


def _lora_pairs(up):
    _, rank, width = up.shape
    u = up.reshape(2, rank, width // LANES, LANES).transpose(2, 0, 1, 3)
    z = jnp.zeros_like(u[:, 0])
    top = jnp.concatenate([u[:, 0], z], axis=-1)
    bot = jnp.concatenate([z, u[:, 1]], axis=-1)
    return jnp.concatenate([top, bot], axis=1).astype(BF16)


def kernel(x, c, ctx, c_ctx, w_mod, b_mod, g_norm1, g_norm2, w_in, m_conv_w, m_conv_b, m_gate_b, m_norm_g, r_w0, r_w_up, r_a0, r_a_up, r_g_up, r_k_k, r_k_a, r_r_k, r_gn_w, r_gn_b, w_out, f_w_up, f_w_gate, f_conv_w, f_conv_b, f_w_down, g_final):
    batch, seq, d = x.shape
    ctx_len = ctx.shape[1]
    assert w_mod.shape[0] == 1, "single-layer block"
    assert batch + 1 <= 8 and seq % 512 == 0 and ctx_len % MLSTM_CHUNK == 0

    cv8 = jnp.zeros((8, d), F32).at[:batch].set(c).at[batch].set(c_ctx)
    mod = _modulation(cv8, w_mod[0], b_mod[0])
    sh1, sc1, gt1, sh2, sc2, gt2 = (mod[:, k * d:(k + 1) * d] for k in range(6))

    n_gate = 4 * M_HEADS
    w_p = _permute_w(w_in, SLAB_R * LANES, n_gate, N_SLABS * LANES)
    g1 = g_norm1[0].reshape(1, d)
    tm_x = 1024 if seq % 1024 == 0 else 512
    px = _inproj(x.reshape(batch * seq, d), g1, sh1, sc1, w_p, tm_x, seq // tm_x, 0)
    tm_c = batch * ctx_len
    pc = _inproj(ctx.reshape(batch * ctx_len, d), g1, sh1, sc1, w_p, tm_c, 1 << 30, batch)

    gate_row = jnp.zeros((1, LANES), F32).at[0, GATE_LANE0:GATE_LANE0 + n_gate].set(m_gate_b[0].reshape(-1))
    mixm = _mlstm(px, pc, batch, m_conv_w[0], m_conv_b[0].reshape(1, -1), gate_row)

    rw = r_k_k.shape[1]
    wg = jnp.zeros((2 * LANES, rw), F32).at[:r_g_up.shape[1]].set(r_g_up[0]).astype(BF16)
    mixr = _rwkv(px, pc, batch, _lora_pairs(r_w_up[0]), _lora_pairs(r_a_up[0]), wg,
                 r_w0[0], r_a0[0], r_k_k, r_k_a, r_r_k[0].reshape(1, rw), r_gn_w, r_gn_b)

    tm_o = 512
    x1, hx2 = _out_proj(mixm, px, mixr, w_out[0].astype(BF16), x.reshape(batch * seq, d), m_norm_g,
                        gt1, g_norm2[0].reshape(1, d), sh2, sc2, tm_o, seq // tm_o)

    tm_f = 1024 if seq % 1024 == 0 else 512
    out = _conv_ffn(hx2, f_w_up[0].astype(BF16), f_w_gate[0].astype(BF16), f_w_down[0].astype(BF16),
                    f_conv_w[0].reshape(9, -1), f_conv_b, x1, gt2, g_final.reshape(1, d),
                    tm_f, seq // tm_f)
    return out.reshape(batch, seq, d)
```

```python
import functools

import jax
import jax.numpy as jnp
from jax import lax
from jax.experimental import pallas as pl
from jax.experimental.pallas import tpu as pltpu

F32 = jnp.float32
BF16 = jnp.bfloat16
HIGHEST = lax.Precision.HIGHEST

LANES = 128
GRID_W = 64
M_HEADS = 4
M_DQK = 128
M_DV = 256
R_N = 64
NORM_EPS = 1e-6
GN_EPS = 64e-5
MLSTM_CHUNK = 128
RWKV_CHUNK = 64
RWKV_GROUP = 16
RWKV_SCAN_GROUP = 8
RWKV_SCAN_PAIRS = 4
RWKV_PREP_WIDTH = 16
VMEM_LIMIT = 56 * 1024 * 1024

SLAB_Q, SLAB_K, SLAB_V, SLAB_O = 0, 4, 8, 16
SLAB_R, SLAB_KR, SLAB_VR = 24, 32, 40
SLAB_LW, SLAB_LA, SLAB_LG0, SLAB_LG1 = 48, 49, 50, 51
N_SLABS = 52
GATE_LANE0 = 32


def _mm(a, b, precision=None):
    return jnp.dot(a, b, preferred_element_type=F32, precision=precision)


def _mm_nt(a, b):
    return lax.dot_general(a, b, (((1,), (1,)), ((), ())), preferred_element_type=F32)


def _mm_tn(a, b):
    return lax.dot_general(a, b, (((0,), (0,)), ((), ())), preferred_element_type=F32)


def _mm_split(x, ones, ones_first, pieces=3):
    dot = (lambda p: _mm(ones, p)) if ones_first else (lambda p: _mm(p, ones))
    piece = x.astype(BF16)
    total = dot(piece)
    for _ in range(pieces - 1):
        x = x - piece.astype(F32)
        piece = x.astype(BF16)
        total = total + dot(piece)
    return total


def _softplus(x):
    return jnp.maximum(x, 0.0) + jnp.log(1.0 + jnp.exp(-jnp.abs(x)))


def _log_sigmoid(x):
    return -_softplus(-x)


def _params(n_axes):
    return pltpu.CompilerParams(dimension_semantics=("arbitrary",) * n_axes,
                                vmem_limit_bytes=VMEM_LIMIT)


def _mod_kernel(cv_ref, w_ref, b_ref, o_ref):
    cv = cv_ref[...]
    s = (cv * jax.nn.sigmoid(cv)).astype(BF16)
    o_ref[...] = _mm(s, w_ref[...].astype(BF16)) + b_ref[...]


def _modulation(cv8, w_mod, b_mod):
    d, n = w_mod.shape
    tn = 1024
    return pl.pallas_call(
        _mod_kernel,
        grid=(n // tn,),
        in_specs=[pl.BlockSpec((8, d), lambda j: (0, 0)),
                  pl.BlockSpec((d, tn), lambda j: (0, j)),
                  pl.BlockSpec((1, tn), lambda j: (0, j))],
        out_specs=pl.BlockSpec((8, tn), lambda j: (0, j)),
        out_shape=jax.ShapeDtypeStruct((8, n), F32),
        compiler_params=_params(1),
    )(cv8, w_mod, b_mod.reshape(1, n))


def _permute_w_kernel(n_main, n_gate, n_valid, cur_ref, nxt_ref, gate_ref, o_ref):
    j = pl.program_id(0)
    last = pl.num_programs(0) - 1
    tr = o_ref.shape[0]

    @pl.when(j < n_main)
    def _():
        o_ref[...] = cur_ref[0].astype(BF16)

    @pl.when(jnp.logical_and(j >= n_main, j < last))
    def _():
        o_ref[...] = jnp.concatenate([cur_ref[0, n_gate:tr, :], nxt_ref[0, 0:n_gate, :]], axis=0).astype(BF16)

    @pl.when(j == last)
    def _():
        pad = jnp.zeros((tr - n_valid, o_ref.shape[1]), F32)
        o_ref[...] = jnp.concatenate([cur_ref[0, n_gate:n_valid, :], gate_ref[0, 0:n_gate, :], pad],
                                     axis=0).astype(BF16)


def _permute_w(w_t, n_main, n_gate, n_out):
    _, n, d = w_t.shape
    tr = 512
    n_blocks = n_out // tr
    assert n_main % tr == 0 and n_out % tr == 0 and pl.cdiv(n, tr) == n_blocks
    block = lambda index: pl.BlockSpec((1, tr, d), index)
    return pl.pallas_call(
        functools.partial(_permute_w_kernel, n_main // tr, n_gate, n - (n_blocks - 1) * tr),
        grid=(n_blocks,),
        in_specs=[block(lambda j: (0, j, 0)),
                  block(lambda j: (0, jnp.minimum(j + 1, n_blocks - 1), 0)),
                  block(lambda j: (0, n_main // tr, 0))],
        out_specs=pl.BlockSpec((tr, d), lambda j: (j, 0)),
        out_shape=jax.ShapeDtypeStruct((n_out, d), BF16),
        compiler_params=_params(1),
    )(w_t, w_t, w_t)


def _inproj_kernel(tiles_per_row, row0, x_ref, g_ref, sh_ref, sc_ref, w_ref, o_ref, hx_ref):
    i = pl.program_id(0)
    j = pl.program_id(1)

    @pl.when(j == 0)
    def _():
        x = x_ref[...]
        ms = jnp.mean(x * x, axis=-1, keepdims=True)
        y = x * lax.rsqrt(ms + NORM_EPS) * g_ref[...]
        r = row0 + i // tiles_per_row
        hx = y * (1.0 + sc_ref[pl.ds(r, 1), :]) + sh_ref[pl.ds(r, 1), :]
        hx_ref[...] = hx.astype(BF16)

    acc = _mm_nt(hx_ref[...], w_ref[...])
    for s in range(acc.shape[1] // LANES):
        o_ref[s] = acc[:, s * LANES:(s + 1) * LANES]


def _inproj(x2d, g, sh, sc, w_p, tm, tiles_per_row, row0):
    t, d = x2d.shape
    n = w_p.shape[0]
    tn = 512
    return pl.pallas_call(
        functools.partial(_inproj_kernel, tiles_per_row, row0),
        grid=(t // tm, n // tn),
        in_specs=[pl.BlockSpec((tm, d), lambda i, j: (i, 0)),
                  pl.BlockSpec((1, d), lambda i, j: (0, 0)),
                  pl.BlockSpec((8, d), lambda i, j: (0, 0)),
                  pl.BlockSpec((8, d), lambda i, j: (0, 0)),
                  pl.BlockSpec((tn, d), lambda i, j: (j, 0))],
        out_specs=pl.BlockSpec((tn // LANES, tm, LANES), lambda i, j: (j, i, 0)),
        out_shape=jax.ShapeDtypeStruct((n // LANES, t, LANES), F32),
        scratch_shapes=[pltpu.VMEM((tm, d), BF16)],
        compiler_params=_params(2),
    )(x2d, g, sh, sc, w_p)


def _mlstm_kernel(n_ctx, n_lat,
                  qx_ref, kx_ref, vx_ref, gx_ref, qc_ref, kc_ref, vc_ref, gc_ref,
                  cwq_ref, cwk_ref, cbq_ref, cbk_ref, gb_ref,
                  o_ref,
                  qs_ref, ks_ref, kst_ref, gcol_ref, grow_ref, ct_ref, hf_ref, hb_ref):
    c = MLSTM_CHUNK
    head = pl.program_id(1)
    rid = lax.broadcasted_iota(jnp.int32, (c, LANES), 0)
    lane = lax.broadcasted_iota(jnp.int32, (c, LANES), 1)
    r2 = lax.broadcasted_iota(jnp.int32, (c, c), 0)
    c2 = lax.broadcasted_iota(jnp.int32, (c, c), 1)
    prefix = (c2 <= r2).astype(BF16)
    keep = (c2 <= r2, c2 >= r2)
    gbias = gb_ref[...]

    def conv_pass(q_ref, k_ref, g_ref, nchunk, base):
        nrows = nchunk * c

        def body(ci, carry):
            r0 = pl.multiple_of(ci * c, c)
            p0 = pl.multiple_of(jnp.maximum(r0 - 8, 0), 8)
            n0 = pl.multiple_of(jnp.minimum(r0 + c, nrows - 8), 8)
            gb = g_ref[0, pl.ds(r0, c), :] + gbias
            lf = _log_sigmoid(gb)
            pre = _mm_split(lf, prefix, True)
            suf = pre[c - 1:c, :] - pre + lf
            packed = jnp.zeros((c, LANES), F32)
            for slot, (src, col0) in enumerate(((pre, GATE_LANE0 + 2 * M_HEADS), (suf, GATE_LANE0 + 3 * M_HEADS),
                                                (gb, GATE_LANE0), (gb, GATE_LANE0 + M_HEADS))):
                col = jnp.sum(jnp.where(lane == col0 + head, src, 0.0), axis=1, keepdims=True)
                packed = jnp.where(lane == slot, col, packed)
            gcol_ref[base + ci] = packed
            grow_ref[base + ci] = packed.T[0:8, :]
            for src, w_ref, b_ref, scale, is_k in ((q_ref, cwq_ref, cbq_ref, M_DQK ** -0.5, False),
                                                   (k_ref, cwk_ref, cbk_ref, 1.0, True)):
                cur = src[0, pl.ds(r0, c), :]
                prev_row = jnp.where(ci > 0, src[0, pl.ds(p0, 8), :][7:8, :], 0.0)
                next_row = jnp.where(ci < nchunk - 1, src[0, pl.ds(n0, 8), :][0:1, :], 0.0)
                up = jnp.where(rid == 0, prev_row, pltpu.roll(cur, 1, 0))
                dn = jnp.where(rid == c - 1, next_row, pltpu.roll(cur, c - 1, 0))
                w = w_ref[...]
                y = (up * w[0:1, :] + cur * w[1:2, :] + dn * w[2:3, :] + b_ref[...]) * scale
                if is_k:
                    ks_ref[base + ci] = y.astype(BF16)
                    kst_ref[base + ci] = y.T.astype(BF16)
                else:
                    qs_ref[base + ci] = y.astype(BF16)
            return carry

        lax.fori_loop(0, nchunk, body, 0, unroll=min(4, nchunk))

    def step(d, ci, base, v_ref, n, m, want_h):
        r0 = pl.multiple_of(ci * c, c)
        q = qs_ref[base + ci]
        k = ks_ref[base + ci]
        kt = kst_ref[base + ci]
        v = jnp.concatenate([v_ref[0, pl.ds(r0, c), :], v_ref[1, pl.ds(r0, c), :]], axis=1).astype(BF16)
        gcol = gcol_ref[base + ci]
        grow = grow_ref[base + ci]
        rep = lambda x: jnp.broadcast_to(x, (c, LANES))
        wide = lambda x: jnp.concatenate([x, x], axis=1)
        b_col, i_col = rep(gcol[:, d:d + 1]), rep(gcol[:, 2 + d:3 + d])
        b_row, i_row = grow[d:d + 1, :], grow[2 + d:3 + d, :]
        b_last = b_col[0:1, :] if d else b_col[c - 1:c, :]
        ct = ct_ref[d]
        h = None
        if want_h:
            dmat = jnp.where(keep[d], b_col - b_row + i_row, -jnp.inf)
            m_intra = rep(jnp.max(dmat, axis=-1, keepdims=True))
            s = _mm(q, kt) * jnp.exp(dmat - m_intra)
            num_intra = _mm(s.astype(BF16), v)
            den_intra = rep(jnp.sum(s, axis=-1, keepdims=True))
            m_inter = b_col + m
            m_j = jnp.maximum(m_inter, m_intra)
            intra = jnp.exp(m_intra - m_j)
            inter = jnp.exp(m_inter - m_j)
            num = wide(intra) * num_intra + wide(inter) * _mm(q, ct.astype(BF16))
            qn = rep(jnp.sum(q.astype(F32) * n, axis=-1, keepdims=True))
            den = intra * den_intra + inter * qn
            h = num / wide(jnp.maximum(jnp.abs(den), jnp.exp(-m_j)))
        glog = b_last - b_col + i_col
        b_last = b_last[:, 0:1]
        m_new = jnp.maximum(b_last + m, jnp.max(glog, axis=0, keepdims=True)[:, 0:1])
        wk = jnp.exp(glog - m_new)
        wk_row = jnp.exp(b_last - b_row + i_row - m_new)
        decay = jnp.exp(b_last + m - m_new)
        ct_ref[d] = decay * ct + _mm((kt.astype(F32) * wk_row).astype(BF16), v)
        n_new = decay * n + jnp.sum(wk * k.astype(F32), axis=0, keepdims=True)
        return n_new, m_new, h

    def run(nchunk, base, v_ref, carry, want_h):
        def body(i, carry):
            nf, mf, nb, mb = carry
            ib = nchunk - 1 - i
            nf, mf, h_f = step(0, i, base, v_ref, nf, mf, want_h)
            nb, mb, h_b = step(1, ib, base, v_ref, nb, mb, want_h)
            if want_h:
                hf_ref[pl.ds(pl.multiple_of(i * c, c), c), :] = h_f
                hb_ref[pl.ds(pl.multiple_of(ib * c, c), c), :] = h_b
            return nf, mf, nb, mb

        return lax.fori_loop(0, nchunk, body, carry, unroll=2)

    conv_pass(qc_ref, kc_ref, gc_ref, n_ctx, 0)
    conv_pass(qx_ref, kx_ref, gx_ref, n_lat, n_ctx)
    ct_ref[...] = jnp.zeros(ct_ref.shape, F32)
    zn = jnp.zeros((1, M_DQK), F32)
    zm = jnp.zeros((1, 1), F32)
    carry = run(n_ctx, 0, vc_ref, (zn, zm, zn, zm), False)
    run(n_lat, n_ctx, vx_ref, carry, True)

    def fin(i, carry):
        rows = pl.ds(pl.multiple_of(i * c, c), c)
        o_ref[rows, :] = (hf_ref[rows, :] + hb_ref[rows, :]).astype(o_ref.dtype)
        return carry

    lax.fori_loop(0, n_lat, fin, 0)


def _mlstm(px, pc, batch, cw, cb, gate_row):
    lx = px.shape[1] // batch
    lc = pc.shape[1] // batch
    n_lat, n_ctx = lx // MLSTM_CHUNK, lc // MLSTM_CHUNK

    def slab(nrows, first, width=1):
        return pl.BlockSpec((width, nrows, LANES), lambda b, h: (first // width + h, b, 0))

    def fixed(nrows, idx):
        return pl.BlockSpec((1, nrows, LANES), lambda b, h: (idx, b, 0))

    in_specs = [slab(lx, SLAB_Q), slab(lx, SLAB_K), slab(lx, SLAB_V, 2), fixed(lx, SLAB_LG1),
                slab(lc, SLAB_Q), slab(lc, SLAB_K), slab(lc, SLAB_V, 2), fixed(lc, SLAB_LG1),
                pl.BlockSpec((3, LANES), lambda b, h: (0, h)),
                pl.BlockSpec((3, LANES), lambda b, h: (0, M_HEADS + h)),
                pl.BlockSpec((1, LANES), lambda b, h: (0, h)),
                pl.BlockSpec((1, LANES), lambda b, h: (0, M_HEADS + h)),
                pl.BlockSpec((1, LANES), lambda b, h: (0, 0))]
    nch = n_ctx + n_lat
    return pl.pallas_call(
        functools.partial(_mlstm_kernel, n_ctx, n_lat),
        grid=(batch, M_HEADS),
        in_specs=in_specs,
        out_specs=pl.BlockSpec((lx, M_DV), lambda b, h: (b, h)),
        out_shape=jax.ShapeDtypeStruct((batch * lx, M_HEADS * M_DV), BF16),
        scratch_shapes=[pltpu.VMEM((nch, MLSTM_CHUNK, LANES), BF16),
                        pltpu.VMEM((nch, MLSTM_CHUNK, LANES), BF16),
                        pltpu.VMEM((nch, LANES, MLSTM_CHUNK), BF16),
                        pltpu.VMEM((nch, MLSTM_CHUNK, LANES), F32),
                        pltpu.VMEM((nch, 8, LANES), F32),
                        pltpu.VMEM((2, M_DQK, M_DV), F32),
                        pltpu.VMEM((lx, M_DV), F32),
                        pltpu.VMEM((lx, M_DV), F32)],
        compiler_params=_params(2),
    )(px, px, px, px, pc, pc, pc, pc, cw, cw, cb, cb, gate_row)


def _rwkv_stack(x):
    head0 = lax.broadcasted_iota(jnp.int32, x.shape, 1) < R_N
    zero = jnp.zeros_like(x)
    return jnp.concatenate([jnp.where(head0, x, zero), jnp.where(head0, zero, x)], axis=0)


def _rwkv_fold(x):
    half = x.shape[0] // 2
    return x[0:half] + x[half:2 * half]


def _rwkv_gate_a(la, wa, a0_ref, d):
    return jax.nn.sigmoid(a0_ref[d:d + 1, :] + _mm(la.astype(BF16), wa)[:, d * LANES:(d + 1) * LANES])


def _rwkv_prep_kernel(want_y, r_ref, k_ref, v_ref, lw_ref, la_ref,
                      ww_ref, wa_ref, w0_ref, a0_ref, kk_ref, ka_ref, *rest):
    extra_refs, out_refs = (rest[:4], rest[4:]) if want_y else ((), rest)
    c = RWKV_CHUNK
    c2 = 2 * c
    stack = _rwkv_stack
    rr = lax.broadcasted_iota(jnp.int32, (c2, c2), 0)
    cc = lax.broadcasted_iota(jnp.int32, (c2, c2), 1)
    group_ones = ((rr < c) == (cc < c)).astype(BF16)
    tt = lax.broadcasted_iota(jnp.int32, (c, c2), 0)
    lane = lax.broadcasted_iota(jnp.int32, (c, c2), 1)
    ss = lane & (c - 1)
    head0 = lane < c
    strict = (ss < tt, ss > tt)
    incl = (ss <= tt, ss >= tt)
    eye = (ss == tt).astype(F32)
    tr = lax.broadcasted_iota(jnp.int32, (c, c), 0)
    tc = lax.broadcasted_iota(jnp.int32, (c, c), 1)
    tri = ((tc <= tr).astype(BF16), (tc >= tr).astype(BF16))
    ww = ww_ref[0]
    wa = wa_ref[0]
    k_k = kk_ref[...]
    k_a = ka_ref[...]
    m_ref, n_ref, dec_ref = out_refs[0], out_refs[1], out_refs[2]

    def prep_chunks(js):
        nj = len(js)
        rows = [pl.ds(pl.multiple_of(j * c, c), c) for j in js]
        ch = [(d, i) for d in (0, 1) for i in range(nj)]
        half = lambda d: slice(d * LANES, (d + 1) * LANES)
        r = [r_ref[0, rw, :] for rw in rows]
        k = [k_ref[0, rw, :] for rw in rows]
        v_s = [stack(v_ref[0, rw, :]).astype(BF16) for rw in rows]
        lo_w = [_mm(jnp.tanh(lw_ref[0, rw, :]).astype(BF16), ww) for rw in rows]
        lo_a = [_mm(la_ref[0, rw, :].astype(BF16), wa) for rw in rows]
        kk = [x * k_k for x in k]
        kk = [x * lax.rsqrt(_mm_split(x * x, group_ones, False, pieces=2) + 1e-12) for x in kk]
        logw = [-jnp.exp(-_softplus(-(w0_ref[d:d + 1, :] + lo_w[i][:, half(d)])) - 0.5) for d, i in ch]
        a = [jax.nn.sigmoid(a0_ref[d:d + 1, :] + lo_a[i][:, half(d)]) for d, i in ch]
        pin = [_mm_split(x, tri[d], True, pieces=2) for (d, i), x in zip(ch, logw)]
        ptot = [x[0:1, :] if d else x[c - 1:c, :] for (d, i), x in zip(ch, pin)]
        kd = [k[i] * (1.0 + (a_ - 1.0) * k_a) for (d, i), a_ in zip(ch, a)]
        kka = [kk[i] * a_ for (d, i), a_ in zip(ch, a)]
        e_inv = [jnp.exp(-x) for x in pin]
        e_end = [jnp.exp(pt - x) for pt, x in zip(ptot, pin)]
        r_t = [r[i] * jnp.exp(x) for (d, i), x in zip(ch, pin)]
        a_t = [-kk[i] * jnp.exp(x - lw_) for (d, i), x, lw_ in zip(ch, pin, logw)]
        ar_t = [jnp.concatenate([x, y], axis=0).astype(BF16) for x, y in zip(a_t, r_t)]
        bk_t = [jnp.concatenate([stack(x * e), stack(y * e)], axis=0).astype(BF16)
                for x, y, e in zip(kka, kd, e_inv)]
        bk_end = [jnp.concatenate([x * e, y * e], axis=0).astype(BF16) for x, y, e in zip(kka, kd, e_end)]
        aa = [_mm_nt(x, y) for x, y in zip(ar_t, bk_t)]
        aab = [jnp.where(strict[d], x[0:c, 0:c2], 0.0) for (d, i), x in zip(ch, aa)]
        aak = [jnp.where(strict[d], x[0:c, c2:2 * c2], 0.0).astype(BF16) for (d, i), x in zip(ch, aa)]
        akv = [_mm(x, v_s[i]).astype(BF16) for (d, i), x in zip(ch, aak)]
        xs = [eye + x for x in aab]
        pb = [x.astype(BF16) for x in aab]
        ps = [_mm(x, stack(x)) for x in pb]
        for _ in range(c.bit_length() - 3):
            pb = [x.astype(BF16) for x in ps]
            both = [_mm(jnp.concatenate([x.astype(BF16), p], axis=0), stack(p)) for x, p in zip(xs, pb)]
            xs = [x + y[0:c] for x, y in zip(xs, both)]
            ps = [y[c:c2] for y in both]
        xs = [(x + _mm(x.astype(BF16), stack(p.astype(BF16)))).astype(BF16) for x, p in zip(xs, ps)]
        wu = [_mm(x, jnp.concatenate([stack(y[0:c]), stack(z)], axis=1)).astype(BF16)
              for x, y, z in zip(xs, ar_t, akv)]
        m_mat = [_mm_tn(x[:, 0:c2], y[0:c]) for x, y in zip(wu, bk_end)]
        n_mat = [_mm_tn(jnp.concatenate([x[:, c2:2 * c2], v_ref[0, rows[i], :].astype(BF16)], axis=0), y)
                 for (d, i), x, y in zip(ch, wu, bk_end)]
        for (d, i), mm_, nn_, pt in zip(ch, m_mat, n_mat, ptot):
            slot = (0, 0, d, js[i])
            m_ref[slot] = jnp.where(head0, mm_[0:c], mm_[c:c2]).astype(BF16)
            n_ref[slot] = jnp.where(head0, nn_[0:c], nn_[c:c2])
            dec_ref[slot] = jnp.broadcast_to(jnp.exp(pt), (8, LANES))
        if want_y:
            ark = [jnp.concatenate([jnp.where(incl[d], x[c:c2, 0:c2], 0.0),
                                    jnp.where(incl[d], x[c:c2, c2:2 * c2], 0.0)], axis=1).astype(BF16)
                   for (d, i), x in zip(ch, aa)]
            qy = [_mm(x, jnp.concatenate([
                      jnp.concatenate([stack(w_[:, 0:c2]), stack(w_[:, c2:2 * c2])], axis=1),
                      jnp.concatenate([jnp.zeros((c2, c2), BF16), v_s[i]], axis=1)], axis=0))
                  for (d, i), x, w_ in zip(ch, ark, wu)]
            for (d, i), rt, x in zip(ch, r_t, qy):
                slot = (0, 0, d, js[i])
                out_refs[3][slot] = (rt + x[:, 0:c2]).astype(BF16)
                out_refs[4][slot] = x[:, c2:2 * c2]
            lg0_ref, lg1_ref, wg_ref, rk_ref = extra_refs
            ksum = [k[i] * (2.0 + (a[i] + a[nj + i] - 2.0) * k_a) for i in range(nj)]
            bonus = [_mm_split(r[i] * ksum[i] * rk_ref[...], group_ones, False) * v_ref[0, rows[i], :]
                     for i in range(nj)]
            gate = [_mm(jax.nn.sigmoid(jnp.concatenate([lg0_ref[0, rw, :], lg1_ref[0, rw, :]], axis=1)
                                       ).astype(BF16), wg_ref[...]) for rw in rows]
            for rw, gt, bn in zip(rows, gate, bonus):
                out_refs[5][rw, :] = gt.astype(BF16)
                out_refs[6][rw, :] = (bn * gt).astype(BF16)

    n_chunks = r_ref.shape[1] // c
    width = min(RWKV_PREP_WIDTH, n_chunks)
    assert n_chunks % width == 0

    def body(t, carry):
        prep_chunks([t * width + u for u in range(width)])
        return carry

    lax.fori_loop(0, n_chunks // width, body, 0)


def _rwkv_scan_kernel(mc_ref, nc_ref, dc_ref, mf_ref, nf_ref, df_ref, qf_ref, ylf_ref,
                      mb_ref, nb_ref, db_ref, qb_ref, ylb_ref,
                      gate_ref, bg_ref, gnw_ref, gnb_ref,
                      o_ref, z_ref, yf_ref, yb_ref):
    c = RWKV_CHUNK
    g = pl.program_id(2)
    n_groups = pl.num_programs(2)
    group = mf_ref.shape[3]
    n_ctx = mc_ref.shape[3]
    pairs = range(RWKV_SCAN_PAIRS)

    def advance(z, m_c, n_c, dec):
        return z * dec[0:1, :] + _mm(z.astype(BF16), _rwkv_stack(m_c)) + _rwkv_stack(n_c)

    @pl.when(g == 0)
    def _():
        z_ref[...] = jnp.zeros(z_ref.shape, F32)

        def ctx_body(i, carry):
            for pp in pairs:
                for d, ii in ((0, i), (1, n_ctx - 1 - i)):
                    zi = 2 * pp + d
                    z_ref[zi] = advance(z_ref[zi], mc_ref[0, pp, d, ii], nc_ref[0, pp, d, ii],
                                        dc_ref[0, pp, d, ii])
            return carry

        lax.fori_loop(0, n_ctx, ctx_body, 0)

    def lat_body(j, carry):
        chains = [(pp,) + t for pp in pairs for t in (
            (0, j, g * group, mf_ref, nf_ref, df_ref, qf_ref, ylf_ref, yf_ref),
            (1, group - 1 - j, (n_groups - 1 - g) * group, mb_ref, nb_ref, db_ref, qb_ref, ylb_ref, yb_ref))]
        zs = [z_ref[2 * pp + d] for pp, d, *_ in chains]
        zb = [z.astype(BF16) for z in zs]
        ys = [_mm_nt(_rwkv_stack(q_ref[0, pp, 0, jj]), z)
              for (pp, d, jj, first, m_ref, n_ref, d_ref, q_ref, yl_ref, y_ref), z in zip(chains, zb)]
        zn = [z * d_ref[0, pp, 0, jj][0:1, :] + _mm(zh, _rwkv_stack(m_ref[0, pp, 0, jj]))
              + _rwkv_stack(n_ref[0, pp, 0, jj])
              for (pp, d, jj, first, m_ref, n_ref, d_ref, q_ref, yl_ref, y_ref), z, zh in zip(chains, zs, zb)]
        for (pp, d, jj, first, m_ref, n_ref, d_ref, q_ref, yl_ref, y_ref), y, z in zip(chains, ys, zn):
            z_ref[2 * pp + d] = z
            y_ref[pl.ds(pl.multiple_of((first + jj) * c, c), c), pp * LANES:(pp + 1) * LANES] = (
                _rwkv_fold(y) + yl_ref[0, pp, 0, jj])
        return carry

    lax.fori_loop(0, group, lat_body, 0)

    @pl.when(g == n_groups - 1)
    def _():
        fr = 4 * c
        r2 = lax.broadcasted_iota(jnp.int32, (LANES, LANES), 0)
        c2 = lax.broadcasted_iota(jnp.int32, (LANES, LANES), 1)
        group_ones = ((r2 < R_N) == (c2 < R_N)).astype(BF16)
        inv_n = 1.0 / R_N

        def fin(i, carry):
            rows = pl.ds(pl.multiple_of(i * fr, fr), fr)
            for pp in pairs:
                cols = slice(pp * LANES, (pp + 1) * LANES)
                y = yf_ref[rows, cols] + yb_ref[rows, cols]
                sums = _mm_split(jnp.concatenate([y, y * y], axis=0), group_ones, False, pieces=2)
                mu = sums[0:fr] * inv_n
                var = sums[fr:2 * fr] * inv_n - mu * mu
                yn = (y - mu) * lax.rsqrt(var + GN_EPS) * gnw_ref[:, cols] + gnb_ref[:, cols]
                o_ref[rows, cols] = (yn * gate_ref[rows, cols].astype(F32)
                                     + bg_ref[rows, cols].astype(F32)).astype(o_ref.dtype)
            return carry

        lax.fori_loop(0, o_ref.shape[0] // fr, fin, 0)


def _rwkv_prep(p_all, batch, group, want_y, ww, wa, w0, a0, k_k, k_a, wg=None, r_k=None):
    c = RWKV_CHUNK
    length = p_all.shape[1] // batch
    n_chunks = length // c
    n_groups = n_chunks // group
    n_pairs = ww.shape[0]
    rows = group * c

    def slab(first):
        return pl.BlockSpec((1, rows, LANES), lambda b, p, g: (first + p, b * n_groups + g, 0))

    def fixed(idx):
        return pl.BlockSpec((1, rows, LANES), lambda b, p, g: (idx, b * n_groups + g, 0))

    def vec(nrows):
        return pl.BlockSpec((nrows, LANES), lambda b, p, g: (0, p))

    lora = pl.BlockSpec((1, LANES, 2 * LANES), lambda b, p, g: (p, 0, 0))
    outs = [(c, BF16), (c, F32), (8, F32)] + ([(c, BF16), (c, F32)] if want_y else [])
    in_specs = [slab(SLAB_R), slab(SLAB_KR), slab(SLAB_VR), fixed(SLAB_LW), fixed(SLAB_LA),
                lora, lora, vec(2), vec(2), vec(1), vec(1)]
    operands = [p_all, p_all, p_all, p_all, p_all, ww, wa, w0, a0, k_k, k_a]
    out_specs = [pl.BlockSpec((1, 1, 2, group, nr, LANES), lambda b, p, g: (b, p, 0, g, 0, 0)) for nr, _ in outs]
    out_shape = [jax.ShapeDtypeStruct((batch, n_pairs, 2, n_chunks, nr, LANES), dt) for nr, dt in outs]
    if want_y:
        in_specs += [fixed(SLAB_LG0), fixed(SLAB_LG1),
                     pl.BlockSpec((2 * LANES, LANES), lambda b, p, g: (0, p)), vec(1)]
        operands += [p_all, p_all, wg, r_k]
        out_specs += [pl.BlockSpec((rows, LANES), lambda b, p, g: (b * n_groups + g, p))] * 2
        out_shape += [jax.ShapeDtypeStruct((batch * length, n_pairs * LANES), BF16)] * 2
    return pl.pallas_call(
        functools.partial(_rwkv_prep_kernel, want_y),
        grid=(batch, n_pairs, n_groups),
        in_specs=in_specs,
        out_specs=out_specs,
        out_shape=out_shape,
        compiler_params=_params(3),
    )(*operands)


def _rwkv_scan(ops_c, ops_x, gate, bonus_gated, batch, group, gn_w, gn_b):
    c = RWKV_CHUNK
    sp = RWKV_SCAN_PAIRS
    lx = gate.shape[0] // batch
    n_pairs, n_ctx = ops_c[0].shape[1], ops_c[0].shape[3]
    n_groups = ops_x[0].shape[3] // group
    assert n_pairs % sp == 0

    def ctx_block(a):
        return pl.BlockSpec((1, sp, 2, n_ctx, a.shape[4], LANES), lambda b, p, g: (b, p, 0, 0, 0, 0))

    def fwd_block(a):
        return pl.BlockSpec((1, sp, 1, group, a.shape[4], LANES), lambda b, p, g: (b, p, 0, g, 0, 0))

    def bwd_block(a):
        return pl.BlockSpec((1, sp, 1, group, a.shape[4], LANES),
                            lambda b, p, g: (b, p, 1, n_groups - 1 - g, 0, 0))

    tokens = pl.BlockSpec((lx, sp * LANES), lambda b, p, g: (b, p), pipeline_mode=pl.Buffered(1))
    vec = pl.BlockSpec((1, sp * LANES), lambda b, p, g: (0, p))
    return pl.pallas_call(
        _rwkv_scan_kernel,
        grid=(batch, n_pairs // sp, n_groups),
        in_specs=[ctx_block(a) for a in ops_c] + [fwd_block(a) for a in ops_x] + [bwd_block(a) for a in ops_x]
                 + [tokens, tokens, vec, vec],
        out_specs=tokens,
        out_shape=jax.ShapeDtypeStruct((batch * lx, n_pairs * LANES), BF16),
        scratch_shapes=[pltpu.VMEM((2 * sp, 2 * c, LANES), F32),
                        pltpu.VMEM((lx, sp * LANES), F32), pltpu.VMEM((lx, sp * LANES), F32)],
        compiler_params=_params(3),
    )(*ops_c, *ops_x, *ops_x, gate, bonus_gated, gn_w, gn_b)


def _rwkv(px, pc, batch, ww, wa, wg, w0, a0, k_k, k_a, r_k, gn_w, gn_b):
    c = RWKV_CHUNK
    n_ctx = pc.shape[1] // batch // c
    n_lat = px.shape[1] // batch // c
    group = min(RWKV_GROUP, n_lat)
    assert n_lat % group == 0
    ops_c = _rwkv_prep(pc, batch, n_ctx, False, ww, wa, w0, a0, k_k, k_a)
    *ops_x, gate, bonus_gated = _rwkv_prep(px, batch, group, True, ww, wa, w0, a0, k_k, k_a, wg, r_k)
    scan_group = min(RWKV_SCAN_GROUP, n_lat)
    assert n_lat % scan_group == 0
    return _rwkv_scan(ops_c, ops_x, gate, bonus_gated, batch, scan_group, gn_w, gn_b)


def _out_kernel(tiles_per_batch, mixm_ref, om_ref, mixr_ref, w_ref, x_ref, ng_ref,
                gt1_ref, g2_ref, sh2_ref, sc2_ref, x1_ref, hx2_ref):
    b = pl.program_id(0) // tiles_per_batch
    hm = mixm_ref[...].astype(F32)
    ng = ng_ref[...]
    parts = []
    for h in range(M_HEADS):
        cols = slice(h * M_DV, (h + 1) * M_DV)
        seg = hm[:, cols]
        seg = seg * lax.rsqrt(jnp.mean(seg * seg, axis=-1, keepdims=True) + NORM_EPS)
        og = jnp.concatenate([om_ref[2 * h], om_ref[2 * h + 1]], axis=1)
        parts.append((seg * ng[:, cols] * jax.nn.sigmoid(og)).astype(BF16))
    lhs = jnp.concatenate(parts + [mixr_ref[...]], axis=1)
    x1 = x_ref[...] + gt1_ref[pl.ds(b, 1), :] * _mm(lhs, w_ref[...])
    x1_ref[...] = x1
    y = x1 * lax.rsqrt(jnp.mean(x1 * x1, axis=-1, keepdims=True) + NORM_EPS) * g2_ref[...]
    hx2_ref[...] = (y * (1.0 + sc2_ref[pl.ds(b, 1), :]) + sh2_ref[pl.ds(b, 1), :]).astype(BF16)


def _out_proj(mixm, px, mixr, w_out, x2d, ng, gt1, g2, sh2, sc2, tm, tiles_per_batch):
    t, d = x2d.shape
    dm = mixm.shape[1]
    row = lambda i: (i, 0)
    const = lambda i: (0, 0)
    return pl.pallas_call(
        functools.partial(_out_kernel, tiles_per_batch),
        grid=(t // tm,),
        in_specs=[pl.BlockSpec((tm, dm), row),
                  pl.BlockSpec((8, tm, LANES), lambda i: (SLAB_O // 8, i, 0)),
                  pl.BlockSpec((tm, dm), row),
                  pl.BlockSpec((d, d), const),
                  pl.BlockSpec((tm, d), row),
                  pl.BlockSpec((1, dm), const),
                  pl.BlockSpec((8, d), const),
                  pl.BlockSpec((1, d), const),
                  pl.BlockSpec((8, d), const),
                  pl.BlockSpec((8, d), const)],
        out_specs=[pl.BlockSpec((tm, d), row), pl.BlockSpec((tm, d), row)],
        out_shape=[jax.ShapeDtypeStruct((t, d), F32), jax.ShapeDtypeStruct((t, d), BF16)],
        compiler_params=_params(1),
    )(mixm, px, mixr, w_out, x2d, ng, gt1, g2, sh2, sc2)


def _ffn_kernel(tiles_per_img, hx_ref, top_ref, bot_ref, wu_ref, wg_ref, wd_ref, cw_ref, cb_ref,
                x1_ref, gt2_ref, gf_ref, o_ref):
    i = pl.program_id(0)
    j = pl.program_id(1)
    tm = hx_ref.shape[0]
    ti = i % tiles_per_img

    @pl.when(j == 0)
    def _():
        o_ref[...] = jnp.zeros(o_ref.shape, F32)

    hx = hx_ref[...]
    top = jnp.where(ti > 0, top_ref[...], jnp.zeros_like(top_ref[...]))
    bot = jnp.where(ti < tiles_per_img - 1, bot_ref[...], jnp.zeros_like(bot_ref[...]))
    u = _mm(jnp.concatenate([top, hx, bot], axis=0), wu_ref[...])
    nr = u.shape[0]
    col = lax.broadcasted_iota(jnp.int32, u.shape, 0) & (GRID_W - 1)
    ul = jnp.where(col == 0, 0.0, pltpu.roll(u, 1, 0))
    ur = jnp.where(col == GRID_W - 1, 0.0, pltpu.roll(u, nr - 1, 0))
    cw = cw_ref[...]
    conv = cb_ref[...]
    for dy in range(3):
        rows = slice(dy * GRID_W, dy * GRID_W + tm)
        conv = (conv + ul[rows] * cw[3 * dy:3 * dy + 1, :] + u[rows] * cw[3 * dy + 1:3 * dy + 2, :]
                + ur[rows] * cw[3 * dy + 2:3 * dy + 3, :])
    gelu = 0.5 * conv * (1.0 + jnp.tanh(0.7978845608028654 * (conv + 0.044715 * conv * conv * conv)))
    act = (gelu * _mm(hx, wg_ref[...])).astype(BF16)
    o_ref[...] += _mm(act, wd_ref[...])

    @pl.when(j == pl.num_programs(1) - 1)
    def _():
        b = i // tiles_per_img
        x2 = x1_ref[...] + gt2_ref[pl.ds(b, 1), :] * o_ref[...]
        o_ref[...] = x2 * lax.rsqrt(jnp.mean(x2 * x2, axis=-1, keepdims=True) + NORM_EPS) * gf_ref[...]


def _conv_ffn(hx2, w_up, w_gate, w_down, cw, cb, x1, gt2, g_final, tm, tiles_per_img):
    t, d = hx2.shape
    f = w_up.shape[1]
    tf = 512
    rows_per_tile = tm // GRID_W
    n_rows = t // GRID_W
    return pl.pallas_call(
        functools.partial(_ffn_kernel, tiles_per_img),
        grid=(t // tm, f // tf),
        in_specs=[pl.BlockSpec((tm, d), lambda i, j: (i, 0)),
                  pl.BlockSpec((GRID_W, d), lambda i, j: (jnp.maximum(i * rows_per_tile - 1, 0), 0)),
                  pl.BlockSpec((GRID_W, d), lambda i, j: (jnp.minimum((i + 1) * rows_per_tile, n_rows - 1), 0)),
                  pl.BlockSpec((d, tf), lambda i, j: (0, j)),
                  pl.BlockSpec((d, tf), lambda i, j: (0, j)),
                  pl.BlockSpec((tf, d), lambda i, j: (j, 0)),
                  pl.BlockSpec((9, tf), lambda i, j: (0, j)),
                  pl.BlockSpec((1, tf), lambda i, j: (0, j)),
                  pl.BlockSpec((tm, d), lambda i, j: (i, 0), pipeline_mode=pl.Buffered(1)),
                  pl.BlockSpec((8, d), lambda i, j: (0, 0)),
                  pl.BlockSpec((1, d), lambda i, j: (0, 0))],
        out_specs=pl.BlockSpec((tm, d), lambda i, j: (i, 0), pipeline_mode=pl.Buffered(1)),
        out_shape=jax.ShapeDtypeStruct((t, d), F32),
        compiler_params=_params(2),
    )(hx2, hx2, hx2, w_up, w_gate, w_down, cw, cb, x1, gt2, g_final)


def _lora_pairs(up):
    _, rank, width = up.shape
    u = up.reshape(2, rank, width // LANES, LANES).transpose(2, 0, 1, 3)
    z = jnp.zeros_like(u[:, 0])
    top = jnp.concatenate([u[:, 0], z], axis=-1)
    bot = jnp.concatenate([z, u[:, 1]], axis=-1)
    return jnp.concatenate([top, bot], axis=1).astype(BF16)


def kernel(x, c, ctx, c_ctx, w_mod, b_mod, g_norm1, g_norm2, w_in, m_conv_w, m_conv_b, m_gate_b, m_norm_g, r_w0, r_w_up, r_a0, r_a_up, r_g_up, r_k_k, r_k_a, r_r_k, r_gn_w, r_gn_b, w_out, f_w_up, f_w_gate, f_conv_w, f_conv_b, f_w_down, g_final):
    batch, seq, d = x.shape
    ctx_len = ctx.shape[1]
    assert w_mod.shape[0] == 1, "single-layer block"
    assert batch + 1 <= 8 and seq % 512 == 0 and ctx_len % MLSTM_CHUNK == 0

    cv8 = jnp.zeros((8, d), F32).at[:batch].set(c).at[batch].set(c_ctx)
    mod = _modulation(cv8, w_mod[0], b_mod[0])
    sh1, sc1, gt1, sh2, sc2, gt2 = (mod[:, k * d:(k + 1) * d] for k in range(6))

    n_gate = 4 * M_HEADS
    w_p = _permute_w(jnp.swapaxes(w_in, 1, 2), SLAB_R * LANES, n_gate, N_SLABS * LANES)
    g1 = g_norm1[0].reshape(1, d)
    tm_x = 1024 if seq % 1024 == 0 else 512
    px = _inproj(x.reshape(batch * seq, d), g1, sh1, sc1, w_p, tm_x, seq // tm_x, 0)
    tm_c = batch * ctx_len
    pc = _inproj(ctx.reshape(batch * ctx_len, d), g1, sh1, sc1, w_p, tm_c, 1 << 30, batch)

    gate_row = jnp.zeros((1, LANES), F32).at[0, GATE_LANE0:GATE_LANE0 + n_gate].set(m_gate_b[0].reshape(-1))
    mixm = _mlstm(px, pc, batch, m_conv_w[0], m_conv_b[0].reshape(1, -1), gate_row)

    rw = r_k_k.shape[1]
    wg = jnp.zeros((2 * LANES, rw), F32).at[:r_g_up.shape[1]].set(r_g_up[0]).astype(BF16)
    mixr = _rwkv(px, pc, batch, _lora_pairs(r_w_up[0]), _lora_pairs(r_a_up[0]), wg,
                 r_w0[0], r_a0[0], r_k_k, r_k_a, r_r_k[0].reshape(1, rw), r_gn_w, r_gn_b)

    tm_o = 512
    x1, hx2 = _out_proj(mixm, px, mixr, w_out[0].astype(BF16), x.reshape(batch * seq, d), m_norm_g,
                        gt1, g_norm2[0].reshape(1, d), sh2, sc2, tm_o, seq // tm_o)

    tm_f = 1024 if seq % 1024 == 0 else 512
    out = _conv_ffn(hx2, f_w_up[0].astype(BF16), f_w_gate[0].astype(BF16), f_w_down[0].astype(BF16),
                    f_conv_w[0].reshape(9, -1), f_conv_b, x1, gt2, g_final.reshape(1, d),
                    tm_f, seq // tm_f)
    return out.reshape(batch, seq, d)
```

```python
import functools

import jax
import jax.numpy as jnp
from jax import lax
from jax.experimental import pallas as pl
from jax.experimental.pallas import tpu as pltpu

F32 = jnp.float32
BF16 = jnp.bfloat16

LANES = 128
GRID_W = 64
M_HEADS = 4
M_DQK = 128
M_DV = 256
R_N = 64
NORM_EPS = 1e-6
GN_EPS = 64e-5
MLSTM_CHUNK = 128
RWKV_CHUNK = 64
RWKV_GROUP = 16
RWKV_SCAN_GROUP = 8
RWKV_SCAN_PAIRS = 4
RWKV_PREP_WIDTH = 16
VMEM_LIMIT = 56 * 1024 * 1024
MOD_TN = 1024
PERMUTE_ROWS = 512
INPROJ_TM, INPROJ_TN = 1024, 512
OUT_TM = 512
FFN_TM, FFN_TF = 1024, 512

SLAB_Q, SLAB_K, SLAB_V, SLAB_O = 0, 4, 8, 16
SLAB_R, SLAB_KR, SLAB_VR = 24, 32, 40
SLAB_LW, SLAB_LA, SLAB_LG0, SLAB_LG1 = 48, 49, 50, 51
N_SLABS = 52
GATE_LANE0 = 32


def _mm(a, b):
    return jnp.dot(a, b, preferred_element_type=F32)


def _mm_nt(a, b):
    return lax.dot_general(a, b, (((1,), (1,)), ((), ())), preferred_element_type=F32)


def _mm_tn(a, b):
    return lax.dot_general(a, b, (((0,), (0,)), ((), ())), preferred_element_type=F32)


def _mm_split(x, ones, ones_first, pieces=3):
    dot = (lambda p: _mm(ones, p)) if ones_first else (lambda p: _mm(p, ones))
    piece = x.astype(BF16)
    total = dot(piece)
    for _ in range(pieces - 1):
        x = x - piece.astype(F32)
        piece = x.astype(BF16)
        total = total + dot(piece)
    return total


def _softplus(x):
    return jnp.maximum(x, 0.0) + jnp.log(1.0 + jnp.exp(-jnp.abs(x)))


def _log_sigmoid(x):
    return -_softplus(-x)


def _params(n_axes):
    return pltpu.CompilerParams(dimension_semantics=("arbitrary",) * n_axes,
                                vmem_limit_bytes=VMEM_LIMIT)


def _mod_kernel(cv_ref, w_ref, b_ref, o_ref):
    cv = cv_ref[...]
    s = (cv * jax.nn.sigmoid(cv)).astype(BF16)
    o_ref[...] = _mm(s, w_ref[...].astype(BF16)) + b_ref[...]


def _modulation(cv8, w_mod, b_mod):
    d, n = w_mod.shape
    tn = MOD_TN
    return pl.pallas_call(
        _mod_kernel,
        grid=(n // tn,),
        in_specs=[pl.BlockSpec((8, d), lambda j: (0, 0)),
                  pl.BlockSpec((d, tn), lambda j: (0, j)),
                  pl.BlockSpec((1, tn), lambda j: (0, j))],
        out_specs=pl.BlockSpec((8, tn), lambda j: (0, j)),
        out_shape=jax.ShapeDtypeStruct((8, n), F32),
        compiler_params=_params(1),
    )(cv8, w_mod, b_mod.reshape(1, n))


def _permute_w_kernel(n_main, n_gate, n_valid, cur_ref, nxt_ref, gate_ref, o_ref):
    j = pl.program_id(0)
    last = pl.num_programs(0) - 1
    tr = o_ref.shape[0]

    @pl.when(j < n_main)
    def _():
        o_ref[...] = cur_ref[0].astype(BF16)

    @pl.when(jnp.logical_and(j >= n_main, j < last))
    def _():
        o_ref[...] = jnp.concatenate([cur_ref[0, n_gate:tr, :], nxt_ref[0]], axis=0).astype(BF16)

    @pl.when(j == last)
    def _():
        pad = jnp.zeros((tr - n_valid, o_ref.shape[1]), F32)
        o_ref[...] = jnp.concatenate([cur_ref[0, n_gate:n_valid, :], gate_ref[0], pad], axis=0).astype(BF16)


def _permute_w(w_t, n_main, n_gate, n_out):
    _, n, d = w_t.shape
    tr = PERMUTE_ROWS
    n_blocks = n_out // tr
    assert n_main % tr == 0 and n_out % tr == 0 and pl.cdiv(n, tr) == n_blocks and tr % n_gate == 0
    per = tr // n_gate
    return pl.pallas_call(
        functools.partial(_permute_w_kernel, n_main // tr, n_gate, n - (n_blocks - 1) * tr),
        grid=(n_blocks,),
        in_specs=[pl.BlockSpec((1, tr, d), lambda j: (0, j, 0)),
                  pl.BlockSpec((1, n_gate, d), lambda j: (0, jnp.minimum(j + 1, n_blocks - 1) * per, 0)),
                  pl.BlockSpec((1, n_gate, d), lambda j: (0, n_main // n_gate, 0))],
        out_specs=pl.BlockSpec((tr, d), lambda j: (j, 0)),
        out_shape=jax.ShapeDtypeStruct((n_out, d), BF16),
        compiler_params=_params(1),
    )(w_t, w_t, w_t)


def _inproj_kernel(tiles_per_row, row0, x_ref, g_ref, sh_ref, sc_ref, w_ref, o_ref, hx_ref):
    i = pl.program_id(0)
    j = pl.program_id(1)

    @pl.when(j == 0)
    def _():
        x = x_ref[...]
        ms = jnp.mean(x * x, axis=-1, keepdims=True)
        y = x * lax.rsqrt(ms + NORM_EPS) * g_ref[...]
        r = row0 + i // tiles_per_row
        hx = y * (1.0 + sc_ref[pl.ds(r, 1), :]) + sh_ref[pl.ds(r, 1), :]
        hx_ref[...] = hx.astype(BF16)

    acc = _mm_nt(hx_ref[...], w_ref[...])
    for s in range(acc.shape[1] // LANES):
        o_ref[s] = acc[:, s * LANES:(s + 1) * LANES]


def _inproj(x2d, g, sh, sc, w_p, tm, tiles_per_row, row0):
    t, d = x2d.shape
    n = w_p.shape[0]
    tn = INPROJ_TN
    return pl.pallas_call(
        functools.partial(_inproj_kernel, tiles_per_row, row0),
        grid=(t // tm, n // tn),
        in_specs=[pl.BlockSpec((tm, d), lambda i, j: (i, 0)),
                  pl.BlockSpec((1, d), lambda i, j: (0, 0)),
                  pl.BlockSpec((8, d), lambda i, j: (0, 0)),
                  pl.BlockSpec((8, d), lambda i, j: (0, 0)),
                  pl.BlockSpec((tn, d), lambda i, j: (j, 0))],
        out_specs=pl.BlockSpec((tn // LANES, tm, LANES), lambda i, j: (j, i, 0)),
        out_shape=jax.ShapeDtypeStruct((n // LANES, t, LANES), F32),
        scratch_shapes=[pltpu.VMEM((tm, d), BF16)],
        compiler_params=_params(2),
    )(x2d, g, sh, sc, w_p)


def _mlstm_kernel(n_ctx, n_lat,
                  qx_ref, kx_ref, vx_ref, gx_ref, qc_ref, kc_ref, vc_ref, gc_ref,
                  cwq_ref, cwk_ref, cbq_ref, cbk_ref, gb_ref,
                  o_ref,
                  qs_ref, ks_ref, kst_ref, gcol_ref, grow_ref, ct_ref, hf_ref, hb_ref):
    c = MLSTM_CHUNK
    head = pl.program_id(1)
    rid = lax.broadcasted_iota(jnp.int32, (c, LANES), 0)
    lane = lax.broadcasted_iota(jnp.int32, (c, LANES), 1)
    r2 = lax.broadcasted_iota(jnp.int32, (c, c), 0)
    c2 = lax.broadcasted_iota(jnp.int32, (c, c), 1)
    prefix = (c2 <= r2).astype(BF16)
    keep = (c2 <= r2, c2 >= r2)
    gbias = gb_ref[...]

    def conv_pass(q_ref, k_ref, g_ref, nchunk, base):
        nrows = nchunk * c

        def body(ci, carry):
            r0 = pl.multiple_of(ci * c, c)
            p0 = pl.multiple_of(jnp.maximum(r0 - 8, 0), 8)
            n0 = pl.multiple_of(jnp.minimum(r0 + c, nrows - 8), 8)
            gb = g_ref[0, pl.ds(r0, c), :] + gbias
            lf = _log_sigmoid(gb)
            pre = _mm_split(lf, prefix, True)
            suf = pre[c - 1:c, :] - pre + lf
            packed = jnp.zeros((c, LANES), F32)
            for slot, (src, col0) in enumerate(((pre, GATE_LANE0 + 2 * M_HEADS), (suf, GATE_LANE0 + 3 * M_HEADS),
                                                (gb, GATE_LANE0), (gb, GATE_LANE0 + M_HEADS))):
                col = jnp.sum(jnp.where(lane == col0 + head, src, 0.0), axis=1, keepdims=True)
                packed = jnp.where(lane == slot, col, packed)
            gcol_ref[base + ci] = packed
            grow_ref[base + ci] = packed.T[0:8, :]
            for src, w_ref, b_ref, scale, is_k in ((q_ref, cwq_ref, cbq_ref, M_DQK ** -0.5, False),
                                                   (k_ref, cwk_ref, cbk_ref, 1.0, True)):
                cur = src[0, pl.ds(r0, c), :]
                prev_row = jnp.where(ci > 0, src[0, pl.ds(p0, 8), :][7:8, :], 0.0)
                next_row = jnp.where(ci < nchunk - 1, src[0, pl.ds(n0, 8), :][0:1, :], 0.0)
                up = jnp.where(rid == 0, prev_row, pltpu.roll(cur, 1, 0))
                dn = jnp.where(rid == c - 1, next_row, pltpu.roll(cur, c - 1, 0))
                w = w_ref[...]
                y = (up * w[0:1, :] + cur * w[1:2, :] + dn * w[2:3, :] + b_ref[...]) * scale
                if is_k:
                    ks_ref[base + ci] = y.astype(BF16)
                    kst_ref[base + ci] = y.T.astype(BF16)
                else:
                    qs_ref[base + ci] = y.astype(BF16)
            return carry

        lax.fori_loop(0, nchunk, body, 0, unroll=min(4, nchunk))

    def step(d, ci, base, v_ref, n, m, want_h):
        r0 = pl.multiple_of(ci * c, c)
        q = qs_ref[base + ci]
        k = ks_ref[base + ci]
        kt = kst_ref[base + ci]
        v = jnp.concatenate([v_ref[0, pl.ds(r0, c), :], v_ref[1, pl.ds(r0, c), :]], axis=1).astype(BF16)
        gcol = gcol_ref[base + ci]
        grow = grow_ref[base + ci]
        rep = lambda x: jnp.broadcast_to(x, (c, LANES))
        wide = lambda x: jnp.concatenate([x, x], axis=1)
        b_col, i_col = rep(gcol[:, d:d + 1]), rep(gcol[:, 2 + d:3 + d])
        b_row, i_row = grow[d:d + 1, :], grow[2 + d:3 + d, :]
        b_last = b_col[0:1, :] if d else b_col[c - 1:c, :]
        ct = ct_ref[d]
        h = None
        if want_h:
            dmat = jnp.where(keep[d], b_col - b_row + i_row, -jnp.inf)
            m_intra = rep(jnp.max(dmat, axis=-1, keepdims=True))
            s = _mm(q, kt) * jnp.exp(dmat - m_intra)
            num_intra = _mm(s.astype(BF16), v)
            den_intra = rep(jnp.sum(s, axis=-1, keepdims=True))
            m_inter = b_col + m
            m_j = jnp.maximum(m_inter, m_intra)
            intra = jnp.exp(m_intra - m_j)
            inter = jnp.exp(m_inter - m_j)
            num = wide(intra) * num_intra + wide(inter) * _mm(q, ct.astype(BF16))
            qn = rep(jnp.sum(q.astype(F32) * n, axis=-1, keepdims=True))
            den = intra * den_intra + inter * qn
            h = num / wide(jnp.maximum(jnp.abs(den), jnp.exp(-m_j)))
        glog = b_last - b_col + i_col
        b_last = b_last[:, 0:1]
        m_new = jnp.maximum(b_last + m, jnp.max(glog, axis=0, keepdims=True)[:, 0:1])
        wk = jnp.exp(glog - m_new)
        wk_row = jnp.exp(b_last - b_row + i_row - m_new)
        decay = jnp.exp(b_last + m - m_new)
        ct_ref[d] = decay * ct + _mm((kt.astype(F32) * wk_row).astype(BF16), v)
        n_new = decay * n + jnp.sum(wk * k.astype(F32), axis=0, keepdims=True)
        return n_new, m_new, h

    def run(nchunk, base, v_ref, carry, want_h):
        def body(i, carry):
            nf, mf, nb, mb = carry
            ib = nchunk - 1 - i
            nf, mf, h_f = step(0, i, base, v_ref, nf, mf, want_h)
            nb, mb, h_b = step(1, ib, base, v_ref, nb, mb, want_h)
            if want_h:
                hf_ref[pl.ds(pl.multiple_of(i * c, c), c), :] = h_f
                hb_ref[pl.ds(pl.multiple_of(ib * c, c), c), :] = h_b
            return nf, mf, nb, mb

        return lax.fori_loop(0, nchunk, body, carry, unroll=2)

    conv_pass(qc_ref, kc_ref, gc_ref, n_ctx, 0)
    conv_pass(qx_ref, kx_ref, gx_ref, n_lat, n_ctx)
    ct_ref[...] = jnp.zeros(ct_ref.shape, F32)
    zn = jnp.zeros((1, M_DQK), F32)
    zm = jnp.zeros((1, 1), F32)
    carry = run(n_ctx, 0, vc_ref, (zn, zm, zn, zm), False)
    run(n_lat, n_ctx, vx_ref, carry, True)

    def fin(i, carry):
        rows = pl.ds(pl.multiple_of(i * c, c), c)
        o_ref[rows, :] = (hf_ref[rows, :] + hb_ref[rows, :]).astype(o_ref.dtype)
        return carry

    lax.fori_loop(0, n_lat, fin, 0)


def _mlstm(px, pc, batch, cw, cb, gate_row):
    lx = px.shape[1] // batch
    lc = pc.shape[1] // batch
    n_lat, n_ctx = lx // MLSTM_CHUNK, lc // MLSTM_CHUNK

    def slab(nrows, first, width=1):
        return pl.BlockSpec((width, nrows, LANES), lambda b, h: (first // width + h, b, 0))

    def fixed(nrows, idx):
        return pl.BlockSpec((1, nrows, LANES), lambda b, h: (idx, b, 0))

    in_specs = [slab(lx, SLAB_Q), slab(lx, SLAB_K), slab(lx, SLAB_V, 2), fixed(lx, SLAB_LG1),
                slab(lc, SLAB_Q), slab(lc, SLAB_K), slab(lc, SLAB_V, 2), fixed(lc, SLAB_LG1),
                pl.BlockSpec((3, LANES), lambda b, h: (0, h)),
                pl.BlockSpec((3, LANES), lambda b, h: (0, M_HEADS + h)),
                pl.BlockSpec((1, LANES), lambda b, h: (0, h)),
                pl.BlockSpec((1, LANES), lambda b, h: (0, M_HEADS + h)),
                pl.BlockSpec((1, LANES), lambda b, h: (0, 0))]
    nch = n_ctx + n_lat
    return pl.pallas_call(
        functools.partial(_mlstm_kernel, n_ctx, n_lat),
        grid=(batch, M_HEADS),
        in_specs=in_specs,
        out_specs=pl.BlockSpec((lx, M_DV), lambda b, h: (b, h)),
        out_shape=jax.ShapeDtypeStruct((batch * lx, M_HEADS * M_DV), BF16),
        scratch_shapes=[pltpu.VMEM((nch, MLSTM_CHUNK, LANES), BF16),
                        pltpu.VMEM((nch, MLSTM_CHUNK, LANES), BF16),
                        pltpu.VMEM((nch, LANES, MLSTM_CHUNK), BF16),
                        pltpu.VMEM((nch, MLSTM_CHUNK, LANES), F32),
                        pltpu.VMEM((nch, 8, LANES), F32),
                        pltpu.VMEM((2, M_DQK, M_DV), F32),
                        pltpu.VMEM((lx, M_DV), F32),
                        pltpu.VMEM((lx, M_DV), F32)],
        compiler_params=_params(2),
    )(px, px, px, px, pc, pc, pc, pc, cw, cw, cb, cb, gate_row)


def _rwkv_stack(x):
    head0 = lax.broadcasted_iota(jnp.int32, x.shape, 1) < R_N
    zero = jnp.zeros_like(x)
    return jnp.concatenate([jnp.where(head0, x, zero), jnp.where(head0, zero, x)], axis=0)


def _rwkv_fold(x):
    half = x.shape[0] // 2
    return x[0:half] + x[half:2 * half]


def _rwkv_prep_kernel(want_y, r_ref, k_ref, v_ref, lw_ref, la_ref,
                      ww_ref, wa_ref, w0_ref, a0_ref, kk_ref, ka_ref, *rest):
    extra_refs, out_refs = (rest[:4], rest[4:]) if want_y else ((), rest)
    c = RWKV_CHUNK
    c2 = 2 * c
    stack = _rwkv_stack
    rr = lax.broadcasted_iota(jnp.int32, (c2, c2), 0)
    cc = lax.broadcasted_iota(jnp.int32, (c2, c2), 1)
    group_ones = ((rr < c) == (cc < c)).astype(BF16)
    tt = lax.broadcasted_iota(jnp.int32, (c, c2), 0)
    lane = lax.broadcasted_iota(jnp.int32, (c, c2), 1)
    ss = lane & (c - 1)
    head0 = lane < c
    strict = (ss < tt, ss > tt)
    incl = (ss <= tt, ss >= tt)
    eye = (ss == tt).astype(F32)
    tr = lax.broadcasted_iota(jnp.int32, (c, c), 0)
    tc = lax.broadcasted_iota(jnp.int32, (c, c), 1)
    tri = ((tc <= tr).astype(BF16), (tc >= tr).astype(BF16))
    ww = ww_ref[0]
    wa = wa_ref[0]
    k_k = kk_ref[...]
    k_a = ka_ref[...]
    m_ref, n_ref, dec_ref = out_refs[0], out_refs[1], out_refs[2]

    def prep_chunks(js):
        nj = len(js)
        rows = [pl.ds(pl.multiple_of(j * c, c), c) for j in js]
        ch = [(d, i) for d in (0, 1) for i in range(nj)]
        half = lambda d: slice(d * LANES, (d + 1) * LANES)
        r = [r_ref[0, rw, :] for rw in rows]
        k = [k_ref[0, rw, :] for rw in rows]
        v_s = [stack(v_ref[0, rw, :]).astype(BF16) for rw in rows]
        lo_w = [_mm(jnp.tanh(lw_ref[0, rw, :]).astype(BF16), ww) for rw in rows]
        lo_a = [_mm(la_ref[0, rw, :].astype(BF16), wa) for rw in rows]
        kk = [x * k_k for x in k]
        kk = [x * lax.rsqrt(_mm_split(x * x, group_ones, False, pieces=2) + 1e-12) for x in kk]
        logw = [-jnp.exp(-_softplus(-(w0_ref[d:d + 1, :] + lo_w[i][:, half(d)])) - 0.5) for d, i in ch]
        a = [jax.nn.sigmoid(a0_ref[d:d + 1, :] + lo_a[i][:, half(d)]) for d, i in ch]
        pin = [_mm_split(x, tri[d], True, pieces=2) for (d, i), x in zip(ch, logw)]
        ptot = [x[0:1, :] if d else x[c - 1:c, :] for (d, i), x in zip(ch, pin)]
        kd = [k[i] * (1.0 + (a_ - 1.0) * k_a) for (d, i), a_ in zip(ch, a)]
        kka = [kk[i] * a_ for (d, i), a_ in zip(ch, a)]
        e_inv = [jnp.exp(-x) for x in pin]
        e_end = [jnp.exp(pt - x) for pt, x in zip(ptot, pin)]
        r_t = [r[i] * jnp.exp(x) for (d, i), x in zip(ch, pin)]
        a_t = [-kk[i] * jnp.exp(x - lw_) for (d, i), x, lw_ in zip(ch, pin, logw)]
        ar_t = [jnp.concatenate([x, y], axis=0).astype(BF16) for x, y in zip(a_t, r_t)]
        bk_t = [jnp.concatenate([stack(x * e), stack(y * e)], axis=0).astype(BF16)
                for x, y, e in zip(kka, kd, e_inv)]
        bk_end = [jnp.concatenate([x * e, y * e], axis=0).astype(BF16) for x, y, e in zip(kka, kd, e_end)]
        aa = [_mm_nt(x, y) for x, y in zip(ar_t, bk_t)]
        aab = [jnp.where(strict[d], x[0:c, 0:c2], 0.0) for (d, i), x in zip(ch, aa)]
        aak = [jnp.where(strict[d], x[0:c, c2:2 * c2], 0.0).astype(BF16) for (d, i), x in zip(ch, aa)]
        akv = [_mm(x, v_s[i]).astype(BF16) for (d, i), x in zip(ch, aak)]
        xs = [eye + x for x in aab]
        pb = [x.astype(BF16) for x in aab]
        ps = [_mm(x, stack(x)) for x in pb]
        for _ in range(c.bit_length() - 3):
            pb = [x.astype(BF16) for x in ps]
            both = [_mm(jnp.concatenate([x.astype(BF16), p], axis=0), stack(p)) for x, p in zip(xs, pb)]
            xs = [x + y[0:c] for x, y in zip(xs, both)]
            ps = [y[c:c2] for y in both]
        xs = [(x + _mm(x.astype(BF16), stack(p.astype(BF16)))).astype(BF16) for x, p in zip(xs, ps)]
        wu = [_mm(x, jnp.concatenate([stack(y[0:c]), stack(z)], axis=1)).astype(BF16)
              for x, y, z in zip(xs, ar_t, akv)]
        m_mat = [_mm_tn(x[:, 0:c2], y[0:c]) for x, y in zip(wu, bk_end)]
        n_mat = [_mm_tn(jnp.concatenate([x[:, c2:2 * c2], v_ref[0, rows[i], :].astype(BF16)], axis=0), y)
                 for (d, i), x, y in zip(ch, wu, bk_end)]
        for (d, i), mm_, nn_, pt in zip(ch, m_mat, n_mat, ptot):
            slot = (0, 0, d, js[i])
            m_ref[slot] = jnp.where(head0, mm_[0:c], mm_[c:c2]).astype(BF16)
            n_ref[slot] = jnp.where(head0, nn_[0:c], nn_[c:c2])
            dec_ref[slot] = jnp.broadcast_to(jnp.exp(pt), (8, LANES))
        if want_y:
            ark = [jnp.concatenate([jnp.where(incl[d], x[c:c2, 0:c2], 0.0),
                                    jnp.where(incl[d], x[c:c2, c2:2 * c2], 0.0)], axis=1).astype(BF16)
                   for (d, i), x in zip(ch, aa)]
            qy = [_mm(x, jnp.concatenate([
                      jnp.concatenate([stack(w_[:, 0:c2]), stack(w_[:, c2:2 * c2])], axis=1),
                      jnp.concatenate([jnp.zeros((c2, c2), BF16), v_s[i]], axis=1)], axis=0))
                  for (d, i), x, w_ in zip(ch, ark, wu)]
            for (d, i), rt, x in zip(ch, r_t, qy):
                slot = (0, 0, d, js[i])
                out_refs[3][slot] = (rt + x[:, 0:c2]).astype(BF16)
                out_refs[4][slot] = x[:, c2:2 * c2]
            lg0_ref, lg1_ref, wg_ref, rk_ref = extra_refs
            ksum = [k[i] * (2.0 + (a[i] + a[nj + i] - 2.0) * k_a) for i in range(nj)]
            bonus = [_mm_split(r[i] * ksum[i] * rk_ref[...], group_ones, False) * v_ref[0, rows[i], :]
                     for i in range(nj)]
            gate = [_mm(jax.nn.sigmoid(jnp.concatenate([lg0_ref[0, rw, :], lg1_ref[0, rw, :]], axis=1)
                                       ).astype(BF16), wg_ref[...]) for rw in rows]
            for rw, gt, bn in zip(rows, gate, bonus):
                out_refs[5][rw, :] = gt.astype(BF16)
                out_refs[6][rw, :] = (bn * gt).astype(BF16)

    n_chunks = r_ref.shape[1] // c
    width = min(RWKV_PREP_WIDTH, n_chunks)
    assert n_chunks % width == 0

    def body(t, carry):
        prep_chunks([t * width + u for u in range(width)])
        return carry

    lax.fori_loop(0, n_chunks // width, body, 0)


def _rwkv_scan_kernel(mc_ref, nc_ref, dc_ref, mf_ref, nf_ref, df_ref, qf_ref, ylf_ref,
                      mb_ref, nb_ref, db_ref, qb_ref, ylb_ref,
                      gate_ref, bg_ref, gnw_ref, gnb_ref,
                      o_ref, z_ref, yf_ref, yb_ref):
    c = RWKV_CHUNK
    g = pl.program_id(2)
    n_groups = pl.num_programs(2)
    group = mf_ref.shape[3]
    n_ctx = mc_ref.shape[3]
    pairs = range(RWKV_SCAN_PAIRS)

    def advance(z, m_c, n_c, dec):
        return z * dec[0:1, :] + _mm(z.astype(BF16), _rwkv_stack(m_c)) + _rwkv_stack(n_c)

    @pl.when(g == 0)
    def _():
        z_ref[...] = jnp.zeros(z_ref.shape, F32)

        def ctx_body(i, carry):
            for pp in pairs:
                for d, ii in ((0, i), (1, n_ctx - 1 - i)):
                    zi = 2 * pp + d
                    z_ref[zi] = advance(z_ref[zi], mc_ref[0, pp, d, ii], nc_ref[0, pp, d, ii],
                                        dc_ref[0, pp, d, ii])
            return carry

        lax.fori_loop(0, n_ctx, ctx_body, 0)

    def lat_body(j, carry):
        chains = [(pp,) + t for pp in pairs for t in (
            (0, j, g * group, mf_ref, nf_ref, df_ref, qf_ref, ylf_ref, yf_ref),
            (1, group - 1 - j, (n_groups - 1 - g) * group, mb_ref, nb_ref, db_ref, qb_ref, ylb_ref, yb_ref))]
        zs = [z_ref[2 * pp + d] for pp, d, *_ in chains]
        zb = [z.astype(BF16) for z in zs]
        ys = [_mm_nt(_rwkv_stack(q_ref[0, pp, 0, jj]), z)
              for (pp, d, jj, first, m_ref, n_ref, d_ref, q_ref, yl_ref, y_ref), z in zip(chains, zb)]
        zn = [z * d_ref[0, pp, 0, jj][0:1, :] + _mm(zh, _rwkv_stack(m_ref[0, pp, 0, jj]))
              + _rwkv_stack(n_ref[0, pp, 0, jj])
              for (pp, d, jj, first, m_ref, n_ref, d_ref, q_ref, yl_ref, y_ref), z, zh in zip(chains, zs, zb)]
        for (pp, d, jj, first, m_ref, n_ref, d_ref, q_ref, yl_ref, y_ref), y, z in zip(chains, ys, zn):
            z_ref[2 * pp + d] = z
            y_ref[pl.ds(pl.multiple_of((first + jj) * c, c), c), pp * LANES:(pp + 1) * LANES] = (
                _rwkv_fold(y) + yl_ref[0, pp, 0, jj])
        return carry

    lax.fori_loop(0, group, lat_body, 0)

    @pl.when(g == n_groups - 1)
    def _():
        fr = 4 * c
        r2 = lax.broadcasted_iota(jnp.int32, (LANES, LANES), 0)
        c2 = lax.broadcasted_iota(jnp.int32, (LANES, LANES), 1)
        group_ones = ((r2 < R_N) == (c2 < R_N)).astype(BF16)
        inv_n = 1.0 / R_N

        def fin(i, carry):
            rows = pl.ds(pl.multiple_of(i * fr, fr), fr)
            for pp in pairs:
                cols = slice(pp * LANES, (pp + 1) * LANES)
                y = yf_ref[rows, cols] + yb_ref[rows, cols]
                sums = _mm_split(jnp.concatenate([y, y * y], axis=0), group_ones, False, pieces=2)
                mu = sums[0:fr] * inv_n
                var = sums[fr:2 * fr] * inv_n - mu * mu
                yn = (y - mu) * lax.rsqrt(var + GN_EPS) * gnw_ref[:, cols] + gnb_ref[:, cols]
                o_ref[rows, cols] = (yn * gate_ref[rows, cols].astype(F32)
                                     + bg_ref[rows, cols].astype(F32)).astype(o_ref.dtype)
            return carry

        lax.fori_loop(0, o_ref.shape[0] // fr, fin, 0)


def _rwkv_prep(p_all, batch, group, want_y, ww, wa, w0, a0, k_k, k_a, wg=None, r_k=None):
    c = RWKV_CHUNK
    length = p_all.shape[1] // batch
    n_chunks = length // c
    n_groups = n_chunks // group
    n_pairs = ww.shape[0]
    rows = group * c

    def slab(first):
        return pl.BlockSpec((1, rows, LANES), lambda b, p, g: (first + p, b * n_groups + g, 0))

    def fixed(idx):
        return pl.BlockSpec((1, rows, LANES), lambda b, p, g: (idx, b * n_groups + g, 0))

    def vec(nrows):
        return pl.BlockSpec((nrows, LANES), lambda b, p, g: (0, p))

    lora = pl.BlockSpec((1, LANES, 2 * LANES), lambda b, p, g: (p, 0, 0))
    outs = [(c, BF16), (c, F32), (8, F32)] + ([(c, BF16), (c, F32)] if want_y else [])
    in_specs = [slab(SLAB_R), slab(SLAB_KR), slab(SLAB_VR), fixed(SLAB_LW), fixed(SLAB_LA),
                lora, lora, vec(2), vec(2), vec(1), vec(1)]
    operands = [p_all, p_all, p_all, p_all, p_all, ww, wa, w0, a0, k_k, k_a]
    out_specs = [pl.BlockSpec((1, 1, 2, group, nr, LANES), lambda b, p, g: (b, p, 0, g, 0, 0)) for nr, _ in outs]
    out_shape = [jax.ShapeDtypeStruct((batch, n_pairs, 2, n_chunks, nr, LANES), dt) for nr, dt in outs]
    if want_y:
        in_specs += [fixed(SLAB_LG0), fixed(SLAB_LG1),
                     pl.BlockSpec((2 * LANES, LANES), lambda b, p, g: (0, p)), vec(1)]
        operands += [p_all, p_all, wg, r_k]
        out_specs += [pl.BlockSpec((rows, LANES), lambda b, p, g: (b * n_groups + g, p))] * 2
        out_shape += [jax.ShapeDtypeStruct((batch * length, n_pairs * LANES), BF16)] * 2
    return pl.pallas_call(
        functools.partial(_rwkv_prep_kernel, want_y),
        grid=(batch, n_pairs, n_groups),
        in_specs=in_specs,
        out_specs=out_specs,
        out_shape=out_shape,
        compiler_params=_params(3),
    )(*operands)


def _rwkv_scan(ops_c, ops_x, gate, bonus_gated, batch, group, gn_w, gn_b):
    c = RWKV_CHUNK
    sp = RWKV_SCAN_PAIRS
    lx = gate.shape[0] // batch
    n_pairs, n_ctx = ops_c[0].shape[1], ops_c[0].shape[3]
    n_groups = ops_x[0].shape[3] // group
    assert n_pairs % sp == 0

    def ctx_block(a):
        return pl.BlockSpec((1, sp, 2, n_ctx, a.shape[4], LANES), lambda b, p, g: (b, p, 0, 0, 0, 0))

    def fwd_block(a):
        return pl.BlockSpec((1, sp, 1, group, a.shape[4], LANES), lambda b, p, g: (b, p, 0, g, 0, 0))

    def bwd_block(a):
        return pl.BlockSpec((1, sp, 1, group, a.shape[4], LANES),
                            lambda b, p, g: (b, p, 1, n_groups - 1 - g, 0, 0))

    tokens = pl.BlockSpec((lx, sp * LANES), lambda b, p, g: (b, p), pipeline_mode=pl.Buffered(1))
    vec = pl.BlockSpec((1, sp * LANES), lambda b, p, g: (0, p))
    return pl.pallas_call(
        _rwkv_scan_kernel,
        grid=(batch, n_pairs // sp, n_groups),
        in_specs=[ctx_block(a) for a in ops_c] + [fwd_block(a) for a in ops_x] + [bwd_block(a) for a in ops_x]
                 + [tokens, tokens, vec, vec],
        out_specs=tokens,
        out_shape=jax.ShapeDtypeStruct((batch * lx, n_pairs * LANES), BF16),
        scratch_shapes=[pltpu.VMEM((2 * sp, 2 * c, LANES), F32),
                        pltpu.VMEM((lx, sp * LANES), F32), pltpu.VMEM((lx, sp * LANES), F32)],
        compiler_params=_params(3),
    )(*ops_c, *ops_x, *ops_x, gate, bonus_gated, gn_w, gn_b)


def _rwkv(px, pc, batch, ww, wa, wg, w0, a0, k_k, k_a, r_k, gn_w, gn_b):
    c = RWKV_CHUNK
    n_ctx = pc.shape[1] // batch // c
    n_lat = px.shape[1] // batch // c
    group = min(RWKV_GROUP, n_lat)
    assert n_lat % group == 0
    ops_c = _rwkv_prep(pc, batch, n_ctx, False, ww, wa, w0, a0, k_k, k_a)
    *ops_x, gate, bonus_gated = _rwkv_prep(px, batch, group, True, ww, wa, w0, a0, k_k, k_a, wg, r_k)
    scan_group = min(RWKV_SCAN_GROUP, n_lat)
    assert n_lat % scan_group == 0
    return _rwkv_scan(ops_c, ops_x, gate, bonus_gated, batch, scan_group, gn_w, gn_b)


def _out_kernel(tiles_per_batch, mixm_ref, om_ref, mixr_ref, w_ref, x_ref, ng_ref,
                gt1_ref, g2_ref, sh2_ref, sc2_ref, x1_ref, hx2_ref):
    b = pl.program_id(0) // tiles_per_batch
    hm = mixm_ref[...].astype(F32)
    ng = ng_ref[...]
    parts = []
    for h in range(M_HEADS):
        cols = slice(h * M_DV, (h + 1) * M_DV)
        seg = hm[:, cols]
        seg = seg * lax.rsqrt(jnp.mean(seg * seg, axis=-1, keepdims=True) + NORM_EPS)
        og = jnp.concatenate([om_ref[2 * h], om_ref[2 * h + 1]], axis=1)
        parts.append((seg * ng[:, cols] * jax.nn.sigmoid(og)).astype(BF16))
    lhs = jnp.concatenate(parts + [mixr_ref[...]], axis=1)
    x1 = x_ref[...] + gt1_ref[pl.ds(b, 1), :] * _mm(lhs, w_ref[...])
    x1_ref[...] = x1
    y = x1 * lax.rsqrt(jnp.mean(x1 * x1, axis=-1, keepdims=True) + NORM_EPS) * g2_ref[...]
    hx2_ref[...] = (y * (1.0 + sc2_ref[pl.ds(b, 1), :]) + sh2_ref[pl.ds(b, 1), :]).astype(BF16)


def _out_proj(mixm, px, mixr, w_out, x2d, ng, gt1, g2, sh2, sc2, tm, tiles_per_batch):
    t, d = x2d.shape
    dm = mixm.shape[1]
    row = lambda i: (i, 0)
    const = lambda i: (0, 0)
    return pl.pallas_call(
        functools.partial(_out_kernel, tiles_per_batch),
        grid=(t // tm,),
        in_specs=[pl.BlockSpec((tm, dm), row),
                  pl.BlockSpec((8, tm, LANES), lambda i: (SLAB_O // 8, i, 0)),
                  pl.BlockSpec((tm, dm), row),
                  pl.BlockSpec((d, d), const),
                  pl.BlockSpec((tm, d), row),
                  pl.BlockSpec((1, dm), const),
                  pl.BlockSpec((8, d), const),
                  pl.BlockSpec((1, d), const),
                  pl.BlockSpec((8, d), const),
                  pl.BlockSpec((8, d), const)],
        out_specs=[pl.BlockSpec((tm, d), row), pl.BlockSpec((tm, d), row)],
        out_shape=[jax.ShapeDtypeStruct((t, d), F32), jax.ShapeDtypeStruct((t, d), BF16)],
        compiler_params=_params(1),
    )(mixm, px, mixr, w_out, x2d, ng, gt1, g2, sh2, sc2)


def _ffn_kernel(tiles_per_img, hx_ref, top_ref, bot_ref, wu_ref, wg_ref, wd_ref, cw_ref, cb_ref,
                x1_ref, gt2_ref, gf_ref, o_ref):
    i = pl.program_id(0)
    j = pl.program_id(1)
    tm = hx_ref.shape[0]
    ti = i % tiles_per_img

    @pl.when(j == 0)
    def _():
        o_ref[...] = jnp.zeros(o_ref.shape, F32)

    hx = hx_ref[...]
    top = jnp.where(ti > 0, top_ref[...], jnp.zeros_like(top_ref[...]))
    bot = jnp.where(ti < tiles_per_img - 1, bot_ref[...], jnp.zeros_like(bot_ref[...]))
    u = _mm(jnp.concatenate([top, hx, bot], axis=0), wu_ref[...])
    nr = u.shape[0]
    col = lax.broadcasted_iota(jnp.int32, u.shape, 0) & (GRID_W - 1)
    ul = jnp.where(col == 0, 0.0, pltpu.roll(u, 1, 0))
    ur = jnp.where(col == GRID_W - 1, 0.0, pltpu.roll(u, nr - 1, 0))
    cw = cw_ref[...]
    conv = cb_ref[...]
    for dy in range(3):
        rows = slice(dy * GRID_W, dy * GRID_W + tm)
        conv = (conv + ul[rows] * cw[3 * dy:3 * dy + 1, :] + u[rows] * cw[3 * dy + 1:3 * dy + 2, :]
                + ur[rows] * cw[3 * dy + 2:3 * dy + 3, :])
    gelu = 0.5 * conv * (1.0 + jnp.tanh(0.7978845608028654 * (conv + 0.044715 * conv * conv * conv)))
    act = (gelu * _mm(hx, wg_ref[...])).astype(BF16)
    o_ref[...] += _mm(act, wd_ref[...])

    @pl.when(j == pl.num_programs(1) - 1)
    def _():
        b = i // tiles_per_img
        x2 = x1_ref[...] + gt2_ref[pl.ds(b, 1), :] * o_ref[...]
        o_ref[...] = x2 * lax.rsqrt(jnp.mean(x2 * x2, axis=-1, keepdims=True) + NORM_EPS) * gf_ref[...]


def _conv_ffn(hx2, w_up, w_gate, w_down, cw, cb, x1, gt2, g_final, tm, tiles_per_img):
    t, d = hx2.shape
    f = w_up.shape[1]
    tf = FFN_TF
    rows_per_tile = tm // GRID_W
    n_rows = t // GRID_W
    return pl.pallas_call(
        functools.partial(_ffn_kernel, tiles_per_img),
        grid=(t // tm, f // tf),
        in_specs=[pl.BlockSpec((tm, d), lambda i, j: (i, 0)),
                  pl.BlockSpec((GRID_W, d), lambda i, j: (jnp.maximum(i * rows_per_tile - 1, 0), 0)),
                  pl.BlockSpec((GRID_W, d), lambda i, j: (jnp.minimum((i + 1) * rows_per_tile, n_rows - 1), 0)),
                  pl.BlockSpec((d, tf), lambda i, j: (0, j)),
                  pl.BlockSpec((d, tf), lambda i, j: (0, j)),
                  pl.BlockSpec((tf, d), lambda i, j: (j, 0)),
                  pl.BlockSpec((9, tf), lambda i, j: (0, j)),
                  pl.BlockSpec((1, tf), lambda i, j: (0, j)),
                  pl.BlockSpec((tm, d), lambda i, j: (i, 0), pipeline_mode=pl.Buffered(1)),
                  pl.BlockSpec((8, d), lambda i, j: (0, 0)),
                  pl.BlockSpec((1, d), lambda i, j: (0, 0))],
        out_specs=pl.BlockSpec((tm, d), lambda i, j: (i, 0), pipeline_mode=pl.Buffered(1)),
        out_shape=jax.ShapeDtypeStruct((t, d), F32),
        compiler_params=_params(2),
    )(hx2, hx2, hx2, w_up, w_gate, w_down, cw, cb, x1, gt2, g_final)


def _lora_pairs(up):
    _, rank, width = up.shape
    u = up.reshape(2, rank, width // LANES, LANES).transpose(2, 0, 1, 3)
    z = jnp.zeros_like(u[:, 0])
    top = jnp.concatenate([u[:, 0], z], axis=-1)
    bot = jnp.concatenate([z, u[:, 1]], axis=-1)
    return jnp.concatenate([top, bot], axis=1).astype(BF16)


def kernel(x, c, ctx, c_ctx, w_mod, b_mod, g_norm1, g_norm2, w_in, m_conv_w, m_conv_b, m_gate_b, m_norm_g, r_w0, r_w_up, r_a0, r_a_up, r_g_up, r_k_k, r_k_a, r_r_k, r_gn_w, r_gn_b, w_out, f_w_up, f_w_gate, f_conv_w, f_conv_b, f_w_down, g_final):
    batch, seq, d = x.shape
    ctx_len = ctx.shape[1]
    assert w_mod.shape[0] == 1, "single-layer block"
    assert batch + 1 <= 8 and seq % 512 == 0 and ctx_len % MLSTM_CHUNK == 0

    cv8 = jnp.zeros((8, d), F32).at[:batch].set(c).at[batch].set(c_ctx)
    mod = _modulation(cv8, w_mod[0], b_mod[0])
    sh1, sc1, gt1, sh2, sc2, gt2 = (mod[:, k * d:(k + 1) * d] for k in range(6))

    n_gate = 4 * M_HEADS
    w_p = _permute_w(jnp.swapaxes(w_in, 1, 2), SLAB_R * LANES, n_gate, N_SLABS * LANES)
    g1 = g_norm1[0].reshape(1, d)
    tm_x = INPROJ_TM if seq % INPROJ_TM == 0 else OUT_TM
    px = _inproj(x.reshape(batch * seq, d), g1, sh1, sc1, w_p, tm_x, seq // tm_x, 0)
    tm_c = batch * ctx_len
    pc = _inproj(ctx.reshape(batch * ctx_len, d), g1, sh1, sc1, w_p, tm_c, 1 << 30, batch)

    gate_row = jnp.zeros((1, LANES), F32).at[0, GATE_LANE0:GATE_LANE0 + n_gate].set(m_gate_b[0].reshape(-1))
    mixm = _mlstm(px, pc, batch, m_conv_w[0], m_conv_b[0].reshape(1, -1), gate_row)

    rw = r_k_k.shape[1]
    wg = jnp.zeros((2 * LANES, rw), F32).at[:r_g_up.shape[1]].set(r_g_up[0]).astype(BF16)
    mixr = _rwkv(px, pc, batch, _lora_pairs(r_w_up[0]), _lora_pairs(r_a_up[0]), wg,
                 r_w0[0], r_a0[0], r_k_k, r_k_a, r_r_k[0].reshape(1, rw), r_gn_w, r_gn_b)

    tm_o = OUT_TM
    x1, hx2 = _out_proj(mixm, px, mixr, w_out[0].astype(BF16), x.reshape(batch * seq, d), m_norm_g,
                        gt1, g_norm2[0].reshape(1, d), sh2, sc2, tm_o, seq // tm_o)

    tm_f = FFN_TM if seq % FFN_TM == 0 else OUT_TM
    out = _conv_ffn(hx2, f_w_up[0].astype(BF16), f_w_gate[0].astype(BF16), f_w_down[0].astype(BF16),
                    f_conv_w[0].reshape(9, -1), f_conv_b, x1, gt2, g_final.reshape(1, d),
                    tm_f, seq // tm_f)
    return out.reshape(batch, seq, d)
```

```python
import functools

import jax
import jax.numpy as jnp
from jax import lax
from jax.experimental import pallas as pl
from jax.experimental.pallas import tpu as pltpu

F32 = jnp.float32
BF16 = jnp.bfloat16

LANES = 128
GRID_W = 64
M_HEADS = 4
M_DQK = 128
M_DV = 256
R_N = 64
NORM_EPS = 1e-6
GN_EPS = 64e-5
MLSTM_CHUNK = 128
RWKV_CHUNK = 64
RWKV_GROUP = 16
RWKV_SCAN_GROUP = 16
RWKV_SCAN_PAIRS = 4
RWKV_PREP_WIDTH = 16
VMEM_LIMIT = 56 * 1024 * 1024
MOD_TN = 1024
PERMUTE_ROWS = 512
INPROJ_TM, INPROJ_TN = 1024, 512
OUT_TM = 512
FFN_TM, FFN_TF = 1024, 512

SLAB_Q, SLAB_K, SLAB_V, SLAB_O = 0, 4, 8, 16
SLAB_R, SLAB_KR, SLAB_VR = 24, 32, 40
SLAB_LW, SLAB_LA, SLAB_LG0, SLAB_LG1 = 48, 49, 50, 51
N_SLABS = 52
GATE_LANE0 = 32


def _mm(a, b):
    return jnp.dot(a, b, preferred_element_type=F32)


def _mm_nt(a, b):
    return lax.dot_general(a, b, (((1,), (1,)), ((), ())), preferred_element_type=F32)


def _mm_tn(a, b):
    return lax.dot_general(a, b, (((0,), (0,)), ((), ())), preferred_element_type=F32)


def _mm_split(x, ones, ones_first, pieces=3):
    dot = (lambda p: _mm(ones, p)) if ones_first else (lambda p: _mm(p, ones))
    piece = x.astype(BF16)
    total = dot(piece)
    for _ in range(pieces - 1):
        x = x - piece.astype(F32)
        piece = x.astype(BF16)
        total = total + dot(piece)
    return total


def _softplus(x):
    return jnp.maximum(x, 0.0) + jnp.log(1.0 + jnp.exp(-jnp.abs(x)))


def _log_sigmoid(x):
    return -_softplus(-x)


def _params(n_axes):
    return pltpu.CompilerParams(dimension_semantics=("arbitrary",) * n_axes,
                                vmem_limit_bytes=VMEM_LIMIT)


def _mod_kernel(cv_ref, w_ref, b_ref, o_ref):
    cv = cv_ref[...]
    s = (cv * jax.nn.sigmoid(cv)).astype(BF16)
    o_ref[...] = _mm(s, w_ref[...].astype(BF16)) + b_ref[...]


def _modulation(cv8, w_mod, b_mod):
    d, n = w_mod.shape
    tn = MOD_TN
    return pl.pallas_call(
        _mod_kernel,
        grid=(n // tn,),
        in_specs=[pl.BlockSpec((8, d), lambda j: (0, 0)),
                  pl.BlockSpec((d, tn), lambda j: (0, j)),
                  pl.BlockSpec((1, tn), lambda j: (0, j))],
        out_specs=pl.BlockSpec((8, tn), lambda j: (0, j)),
        out_shape=jax.ShapeDtypeStruct((8, n), F32),
        compiler_params=_params(1),
    )(cv8, w_mod, b_mod.reshape(1, n))


def _permute_w_kernel(n_main, n_gate, n_valid, cur_ref, nxt_ref, gate_ref, o_ref):
    j = pl.program_id(0)
    last = pl.num_programs(0) - 1
    tr = o_ref.shape[0]

    @pl.when(j < n_main)
    def _():
        o_ref[...] = cur_ref[0].astype(BF16)

    @pl.when(jnp.logical_and(j >= n_main, j < last))
    def _():
        o_ref[...] = jnp.concatenate([cur_ref[0, n_gate:tr, :], nxt_ref[0]], axis=0).astype(BF16)

    @pl.when(j == last)
    def _():
        pad = jnp.zeros((tr - n_valid, o_ref.shape[1]), F32)
        o_ref[...] = jnp.concatenate([cur_ref[0, n_gate:n_valid, :], gate_ref[0], pad], axis=0).astype(BF16)


def _permute_w(w_t, n_main, n_gate, n_out):
    _, n, d = w_t.shape
    tr = PERMUTE_ROWS
    n_blocks = n_out // tr
    assert n_main % tr == 0 and n_out % tr == 0 and pl.cdiv(n, tr) == n_blocks and tr % n_gate == 0
    per = tr // n_gate
    return pl.pallas_call(
        functools.partial(_permute_w_kernel, n_main // tr, n_gate, n - (n_blocks - 1) * tr),
        grid=(n_blocks,),
        in_specs=[pl.BlockSpec((1, tr, d), lambda j: (0, j, 0)),
                  pl.BlockSpec((1, n_gate, d), lambda j: (0, jnp.minimum(j + 1, n_blocks - 1) * per, 0)),
                  pl.BlockSpec((1, n_gate, d), lambda j: (0, n_main // n_gate, 0))],
        out_specs=pl.BlockSpec((tr, d), lambda j: (j, 0)),
        out_shape=jax.ShapeDtypeStruct((n_out, d), BF16),
        compiler_params=_params(1),
    )(w_t, w_t, w_t)


def _inproj_kernel(tiles_per_row, row0, x_ref, g_ref, sh_ref, sc_ref, w_ref, o_ref, hx_ref):
    i = pl.program_id(0)
    j = pl.program_id(1)

    @pl.when(j == 0)
    def _():
        x = x_ref[...]
        ms = jnp.mean(x * x, axis=-1, keepdims=True)
        y = x * lax.rsqrt(ms + NORM_EPS) * g_ref[...]
        r = row0 + i // tiles_per_row
        hx = y * (1.0 + sc_ref[pl.ds(r, 1), :]) + sh_ref[pl.ds(r, 1), :]
        hx_ref[...] = hx.astype(BF16)

    acc = _mm_nt(hx_ref[...], w_ref[...])
    for s in range(acc.shape[1] // LANES):
        o_ref[s] = acc[:, s * LANES:(s + 1) * LANES]


def _inproj(x2d, g, sh, sc, w_p, tm, tiles_per_row, row0):
    t, d = x2d.shape
    n = w_p.shape[0]
    tn = INPROJ_TN
    return pl.pallas_call(
        functools.partial(_inproj_kernel, tiles_per_row, row0),
        grid=(t // tm, n // tn),
        in_specs=[pl.BlockSpec((tm, d), lambda i, j: (i, 0)),
                  pl.BlockSpec((1, d), lambda i, j: (0, 0)),
                  pl.BlockSpec((8, d), lambda i, j: (0, 0)),
                  pl.BlockSpec((8, d), lambda i, j: (0, 0)),
                  pl.BlockSpec((tn, d), lambda i, j: (j, 0))],
        out_specs=pl.BlockSpec((tn // LANES, tm, LANES), lambda i, j: (j, i, 0)),
        out_shape=jax.ShapeDtypeStruct((n // LANES, t, LANES), F32),
        scratch_shapes=[pltpu.VMEM((tm, d), BF16)],
        compiler_params=_params(2),
    )(x2d, g, sh, sc, w_p)


def _mlstm_kernel(n_ctx, n_lat,
                  qx_ref, kx_ref, vx_ref, gx_ref, qc_ref, kc_ref, vc_ref, gc_ref,
                  cwq_ref, cwk_ref, cbq_ref, cbk_ref, gb_ref,
                  o_ref,
                  qs_ref, ks_ref, kst_ref, gcol_ref, grow_ref, ct_ref, hf_ref, hb_ref):
    c = MLSTM_CHUNK
    head = pl.program_id(1)
    rid = lax.broadcasted_iota(jnp.int32, (c, LANES), 0)
    lane = lax.broadcasted_iota(jnp.int32, (c, LANES), 1)
    r2 = lax.broadcasted_iota(jnp.int32, (c, c), 0)
    c2 = lax.broadcasted_iota(jnp.int32, (c, c), 1)
    prefix = (c2 <= r2).astype(BF16)
    keep = (c2 <= r2, c2 >= r2)
    gbias = gb_ref[...]

    def conv_pass(q_ref, k_ref, g_ref, nchunk, base):
        nrows = nchunk * c

        def body(ci, carry):
            r0 = pl.multiple_of(ci * c, c)
            p0 = pl.multiple_of(jnp.maximum(r0 - 8, 0), 8)
            n0 = pl.multiple_of(jnp.minimum(r0 + c, nrows - 8), 8)
            gb = g_ref[0, pl.ds(r0, c), :] + gbias
            lf = _log_sigmoid(gb)
            pre = _mm_split(lf, prefix, True)
            suf = pre[c - 1:c, :] - pre + lf
            packed = jnp.zeros((c, LANES), F32)
            for slot, (src, col0) in enumerate(((pre, GATE_LANE0 + 2 * M_HEADS), (suf, GATE_LANE0 + 3 * M_HEADS),
                                                (gb, GATE_LANE0), (gb, GATE_LANE0 + M_HEADS))):
                col = jnp.sum(jnp.where(lane == col0 + head, src, 0.0), axis=1, keepdims=True)
                packed = jnp.where(lane == slot, col, packed)
            gcol_ref[base + ci] = packed
            grow_ref[base + ci] = packed.T[0:8, :]
            for src, w_ref, b_ref, scale, is_k in ((q_ref, cwq_ref, cbq_ref, M_DQK ** -0.5, False),
                                                   (k_ref, cwk_ref, cbk_ref, 1.0, True)):
                cur = src[0, pl.ds(r0, c), :]
                prev_row = jnp.where(ci > 0, src[0, pl.ds(p0, 8), :][7:8, :], 0.0)
                next_row = jnp.where(ci < nchunk - 1, src[0, pl.ds(n0, 8), :][0:1, :], 0.0)
                up = jnp.where(rid == 0, prev_row, pltpu.roll(cur, 1, 0))
                dn = jnp.where(rid == c - 1, next_row, pltpu.roll(cur, c - 1, 0))
                w = w_ref[...]
                y = (up * w[0:1, :] + cur * w[1:2, :] + dn * w[2:3, :] + b_ref[...]) * scale
                if is_k:
                    ks_ref[base + ci] = y.astype(BF16)
                    kst_ref[base + ci] = y.T.astype(BF16)
                else:
                    qs_ref[base + ci] = y.astype(BF16)
            return carry

        lax.fori_loop(0, nchunk, body, 0, unroll=min(4, nchunk))

    def step(d, ci, base, v_ref, n, m, want_h):
        r0 = pl.multiple_of(ci * c, c)
        q = qs_ref[base + ci]
        k = ks_ref[base + ci]
        kt = kst_ref[base + ci]
        v = jnp.concatenate([v_ref[0, pl.ds(r0, c), :], v_ref[1, pl.ds(r0, c), :]], axis=1).astype(BF16)
        gcol = gcol_ref[base + ci]
        grow = grow_ref[base + ci]
        rep = lambda x: jnp.broadcast_to(x, (c, LANES))
        wide = lambda x: jnp.concatenate([x, x], axis=1)
        b_col, i_col = rep(gcol[:, d:d + 1]), rep(gcol[:, 2 + d:3 + d])
        b_row, i_row = grow[d:d + 1, :], grow[2 + d:3 + d, :]
        b_last = b_col[0:1, :] if d else b_col[c - 1:c, :]
        ct = ct_ref[d]
        h = None
        if want_h:
            dmat = jnp.where(keep[d], b_col - b_row + i_row, -jnp.inf)
            m_intra = rep(jnp.max(dmat, axis=-1, keepdims=True))
            s = _mm(q, kt) * jnp.exp(dmat - m_intra)
            num_intra = _mm(s.astype(BF16), v)
            den_intra = rep(jnp.sum(s, axis=-1, keepdims=True))
            m_inter = b_col + m
            m_j = jnp.maximum(m_inter, m_intra)
            intra = jnp.exp(m_intra - m_j)
            inter = jnp.exp(m_inter - m_j)
            num = wide(intra) * num_intra + wide(inter) * _mm(q, ct.astype(BF16))
            qn = rep(jnp.sum(q.astype(F32) * n, axis=-1, keepdims=True))
            den = intra * den_intra + inter * qn
            h = num / wide(jnp.maximum(jnp.abs(den), jnp.exp(-m_j)))
        glog = b_last - b_col + i_col
        b_last = b_last[:, 0:1]
        m_new = jnp.maximum(b_last + m, jnp.max(glog, axis=0, keepdims=True)[:, 0:1])
        wk = jnp.exp(glog - m_new)
        wk_row = jnp.exp(b_last - b_row + i_row - m_new)
        decay = jnp.exp(b_last + m - m_new)
        ct_ref[d] = decay * ct + _mm((kt.astype(F32) * wk_row).astype(BF16), v)
        n_new = decay * n + jnp.sum(wk * k.astype(F32), axis=0, keepdims=True)
        return n_new, m_new, h

    def run(nchunk, base, v_ref, carry, want_h):
        def body(i, carry):
            nf, mf, nb, mb = carry
            ib = nchunk - 1 - i
            nf, mf, h_f = step(0, i, base, v_ref, nf, mf, want_h)
            nb, mb, h_b = step(1, ib, base, v_ref, nb, mb, want_h)
            if want_h:
                hf_ref[pl.ds(pl.multiple_of(i * c, c), c), :] = h_f
                hb_ref[pl.ds(pl.multiple_of(ib * c, c), c), :] = h_b
            return nf, mf, nb, mb

        return lax.fori_loop(0, nchunk, body, carry, unroll=2)

    conv_pass(qc_ref, kc_ref, gc_ref, n_ctx, 0)
    conv_pass(qx_ref, kx_ref, gx_ref, n_lat, n_ctx)
    ct_ref[...] = jnp.zeros(ct_ref.shape, F32)
    zn = jnp.zeros((1, M_DQK), F32)
    zm = jnp.zeros((1, 1), F32)
    carry = run(n_ctx, 0, vc_ref, (zn, zm, zn, zm), False)
    run(n_lat, n_ctx, vx_ref, carry, True)

    def fin(i, carry):
        rows = pl.ds(pl.multiple_of(i * c, c), c)
        o_ref[rows, :] = (hf_ref[rows, :] + hb_ref[rows, :]).astype(o_ref.dtype)
        return carry

    lax.fori_loop(0, n_lat, fin, 0)


def _mlstm(px, pc, batch, cw, cb, gate_row):
    lx = px.shape[1] // batch
    lc = pc.shape[1] // batch
    n_lat, n_ctx = lx // MLSTM_CHUNK, lc // MLSTM_CHUNK

    def slab(nrows, first, width=1):
        return pl.BlockSpec((width, nrows, LANES), lambda b, h: (first // width + h, b, 0))

    def fixed(nrows, idx):
        return pl.BlockSpec((1, nrows, LANES), lambda b, h: (idx, b, 0))

    in_specs = [slab(lx, SLAB_Q), slab(lx, SLAB_K), slab(lx, SLAB_V, 2), fixed(lx, SLAB_LG1),
                slab(lc, SLAB_Q), slab(lc, SLAB_K), slab(lc, SLAB_V, 2), fixed(lc, SLAB_LG1),
                pl.BlockSpec((3, LANES), lambda b, h: (0, h)),
                pl.BlockSpec((3, LANES), lambda b, h: (0, M_HEADS + h)),
                pl.BlockSpec((1, LANES), lambda b, h: (0, h)),
                pl.BlockSpec((1, LANES), lambda b, h: (0, M_HEADS + h)),
                pl.BlockSpec((1, LANES), lambda b, h: (0, 0))]
    nch = n_ctx + n_lat
    return pl.pallas_call(
        functools.partial(_mlstm_kernel, n_ctx, n_lat),
        grid=(batch, M_HEADS),
        in_specs=in_specs,
        out_specs=pl.BlockSpec((lx, M_DV), lambda b, h: (b, h)),
        out_shape=jax.ShapeDtypeStruct((batch * lx, M_HEADS * M_DV), BF16),
        scratch_shapes=[pltpu.VMEM((nch, MLSTM_CHUNK, LANES), BF16),
                        pltpu.VMEM((nch, MLSTM_CHUNK, LANES), BF16),
                        pltpu.VMEM((nch, LANES, MLSTM_CHUNK), BF16),
                        pltpu.VMEM((nch, MLSTM_CHUNK, LANES), F32),
                        pltpu.VMEM((nch, 8, LANES), F32),
                        pltpu.VMEM((2, M_DQK, M_DV), F32),
                        pltpu.VMEM((lx, M_DV), F32),
                        pltpu.VMEM((lx, M_DV), F32)],
        compiler_params=_params(2),
    )(px, px, px, px, pc, pc, pc, pc, cw, cw, cb, cb, gate_row)


def _rwkv_stack(x):
    head0 = lax.broadcasted_iota(jnp.int32, x.shape, 1) < R_N
    zero = jnp.zeros_like(x)
    return jnp.concatenate([jnp.where(head0, x, zero), jnp.where(head0, zero, x)], axis=0)


def _rwkv_fold(x):
    half = x.shape[0] // 2
    return x[0:half] + x[half:2 * half]


def _rwkv_prep_kernel(want_y, r_ref, k_ref, v_ref, lw_ref, la_ref,
                      ww_ref, wa_ref, w0_ref, a0_ref, kk_ref, ka_ref, *rest):
    extra_refs, out_refs = (rest[:4], rest[4:]) if want_y else ((), rest)
    c = RWKV_CHUNK
    c2 = 2 * c
    stack = _rwkv_stack
    rr = lax.broadcasted_iota(jnp.int32, (c2, c2), 0)
    cc = lax.broadcasted_iota(jnp.int32, (c2, c2), 1)
    group_ones = ((rr < c) == (cc < c)).astype(BF16)
    tt = lax.broadcasted_iota(jnp.int32, (c, c2), 0)
    lane = lax.broadcasted_iota(jnp.int32, (c, c2), 1)
    ss = lane & (c - 1)
    head0 = lane < c
    strict = (ss < tt, ss > tt)
    incl = (ss <= tt, ss >= tt)
    eye = (ss == tt).astype(F32)
    tr = lax.broadcasted_iota(jnp.int32, (c, c), 0)
    tc = lax.broadcasted_iota(jnp.int32, (c, c), 1)
    tri = ((tc <= tr).astype(BF16), (tc >= tr).astype(BF16))
    ww = ww_ref[0]
    wa = wa_ref[0]
    k_k = kk_ref[...]
    k_a = ka_ref[...]
    m_ref, n_ref, dec_ref = out_refs[0], out_refs[1], out_refs[2]

    def prep_chunks(js):
        nj = len(js)
        rows = [pl.ds(pl.multiple_of(j * c, c), c) for j in js]
        ch = [(d, i) for d in (0, 1) for i in range(nj)]
        half = lambda d: slice(d * LANES, (d + 1) * LANES)
        r = [r_ref[0, rw, :] for rw in rows]
        k = [k_ref[0, rw, :] for rw in rows]
        v_s = [stack(v_ref[0, rw, :]).astype(BF16) for rw in rows]
        lo_w = [_mm(jnp.tanh(lw_ref[0, rw, :]).astype(BF16), ww) for rw in rows]
        lo_a = [_mm(la_ref[0, rw, :].astype(BF16), wa) for rw in rows]
        kk = [x * k_k for x in k]
        kk = [x * lax.rsqrt(_mm_split(x * x, group_ones, False, pieces=2) + 1e-12) for x in kk]
        logw = [-jnp.exp(-_softplus(-(w0_ref[d:d + 1, :] + lo_w[i][:, half(d)])) - 0.5) for d, i in ch]
        a = [jax.nn.sigmoid(a0_ref[d:d + 1, :] + lo_a[i][:, half(d)]) for d, i in ch]
        pin = [_mm_split(x, tri[d], True, pieces=2) for (d, i), x in zip(ch, logw)]
        ptot = [x[0:1, :] if d else x[c - 1:c, :] for (d, i), x in zip(ch, pin)]
        kd = [k[i] * (1.0 + (a_ - 1.0) * k_a) for (d, i), a_ in zip(ch, a)]
        kka = [kk[i] * a_ for (d, i), a_ in zip(ch, a)]
        e_inv = [jnp.exp(-x) for x in pin]
        e_end = [jnp.exp(pt - x) for pt, x in zip(ptot, pin)]
        r_t = [r[i] * jnp.exp(x) for (d, i), x in zip(ch, pin)]
        a_t = [-kk[i] * jnp.exp(x - lw_) for (d, i), x, lw_ in zip(ch, pin, logw)]
        ar_t = [jnp.concatenate([x, y], axis=0).astype(BF16) for x, y in zip(a_t, r_t)]
        bk_t = [jnp.concatenate([stack(x * e), stack(y * e)], axis=0).astype(BF16)
                for x, y, e in zip(kka, kd, e_inv)]
        bk_end = [jnp.concatenate([x * e, y * e], axis=0).astype(BF16) for x, y, e in zip(kka, kd, e_end)]
        aa = [_mm_nt(x, y) for x, y in zip(ar_t, bk_t)]
        aab = [jnp.where(strict[d], x[0:c, 0:c2], 0.0) for (d, i), x in zip(ch, aa)]
        aak = [jnp.where(strict[d], x[0:c, c2:2 * c2], 0.0).astype(BF16) for (d, i), x in zip(ch, aa)]
        akv = [_mm(x, v_s[i]).astype(BF16) for (d, i), x in zip(ch, aak)]
        xs = [eye + x for x in aab]
        pb = [x.astype(BF16) for x in aab]
        ps = [_mm(x, stack(x)) for x in pb]
        for _ in range(c.bit_length() - 3):
            pb = [x.astype(BF16) for x in ps]
            both = [_mm(jnp.concatenate([x.astype(BF16), p], axis=0), stack(p)) for x, p in zip(xs, pb)]
            xs = [x + y[0:c] for x, y in zip(xs, both)]
            ps = [y[c:c2] for y in both]
        xs = [(x + _mm(x.astype(BF16), stack(p.astype(BF16)))).astype(BF16) for x, p in zip(xs, ps)]
        wu = [_mm(x, jnp.concatenate([stack(y[0:c]), stack(z)], axis=1)).astype(BF16)
              for x, y, z in zip(xs, ar_t, akv)]
        m_mat = [_mm_tn(x[:, 0:c2], y[0:c]) for x, y in zip(wu, bk_end)]
        n_mat = [_mm_tn(jnp.concatenate([x[:, c2:2 * c2], v_ref[0, rows[i], :].astype(BF16)], axis=0), y)
                 for (d, i), x, y in zip(ch, wu, bk_end)]
        for (d, i), mm_, nn_, pt in zip(ch, m_mat, n_mat, ptot):
            slot = (0, 0, d, js[i])
            m_ref[slot] = jnp.where(head0, mm_[0:c], mm_[c:c2]).astype(BF16)
            n_ref[slot] = jnp.where(head0, nn_[0:c], nn_[c:c2]).astype(BF16)
            dec_ref[slot] = jnp.broadcast_to(jnp.exp(pt), (8, LANES))
        if want_y:
            ark = [jnp.concatenate([jnp.where(incl[d], x[c:c2, 0:c2], 0.0),
                                    jnp.where(incl[d], x[c:c2, c2:2 * c2], 0.0)], axis=1).astype(BF16)
                   for (d, i), x in zip(ch, aa)]
            qy = [_mm(x, jnp.concatenate([
                      jnp.concatenate([stack(w_[:, 0:c2]), stack(w_[:, c2:2 * c2])], axis=1),
                      jnp.concatenate([jnp.zeros((c2, c2), BF16), v_s[i]], axis=1)], axis=0))
                  for (d, i), x, w_ in zip(ch, ark, wu)]
            for (d, i), rt, x in zip(ch, r_t, qy):
                slot = (0, 0, d, js[i])
                out_refs[3][slot] = (rt + x[:, 0:c2]).astype(BF16)
                out_refs[4][slot] = x[:, c2:2 * c2].astype(BF16)
            lg0_ref, lg1_ref, wg_ref, rk_ref = extra_refs
            ksum = [k[i] * (2.0 + (a[i] + a[nj + i] - 2.0) * k_a) for i in range(nj)]
            bonus = [_mm_split(r[i] * ksum[i] * rk_ref[...], group_ones, False) * v_ref[0, rows[i], :]
                     for i in range(nj)]
            gate = [_mm(jax.nn.sigmoid(jnp.concatenate([lg0_ref[0, rw, :], lg1_ref[0, rw, :]], axis=1)
                                       ).astype(BF16), wg_ref[...]) for rw in rows]
            for rw, gt, bn in zip(rows, gate, bonus):
                out_refs[5][rw, :] = gt.astype(BF16)
                out_refs[6][rw, :] = (bn * gt).astype(BF16)

    n_chunks = r_ref.shape[1] // c
    width = min(RWKV_PREP_WIDTH, n_chunks)
    assert n_chunks % width == 0

    def body(t, carry):
        prep_chunks([t * width + u for u in range(width)])
        return carry

    lax.fori_loop(0, n_chunks // width, body, 0)


def _rwkv_scan_kernel(mc_ref, nc_ref, dc_ref, mf_ref, nf_ref, df_ref, qf_ref, ylf_ref,
                      mb_ref, nb_ref, db_ref, qb_ref, ylb_ref,
                      gate_ref, bg_ref, gnw_ref, gnb_ref,
                      o_ref, z_ref, yf_ref, yb_ref):
    c = RWKV_CHUNK
    g = pl.program_id(2)
    n_groups = pl.num_programs(2)
    group = mf_ref.shape[3]
    n_ctx = mc_ref.shape[3]
    pairs = range(RWKV_SCAN_PAIRS)

    def advance(z, m_c, n_c, dec):
        return z * dec[0:1, :] + _mm(z.astype(BF16), _rwkv_stack(m_c)) + _rwkv_stack(n_c)

    @pl.when(g == 0)
    def _():
        z_ref[...] = jnp.zeros(z_ref.shape, F32)

        def ctx_body(i, carry):
            for pp in pairs:
                for d, ii in ((0, i), (1, n_ctx - 1 - i)):
                    zi = 2 * pp + d
                    z_ref[zi] = advance(z_ref[zi], mc_ref[0, pp, d, ii], nc_ref[0, pp, d, ii],
                                        dc_ref[0, pp, d, ii])
            return carry

        lax.fori_loop(0, n_ctx, ctx_body, 0)

    def lat_body(j, carry):
        chains = [(pp,) + t for pp in pairs for t in (
            (0, j, g * group, mf_ref, nf_ref, df_ref, qf_ref, ylf_ref, yf_ref),
            (1, group - 1 - j, (n_groups - 1 - g) * group, mb_ref, nb_ref, db_ref, qb_ref, ylb_ref, yb_ref))]
        zs = [z_ref[2 * pp + d] for pp, d, *_ in chains]
        zb = [z.astype(BF16) for z in zs]
        ys = [_mm_nt(_rwkv_stack(q_ref[0, pp, 0, jj]), z)
              for (pp, d, jj, first, m_ref, n_ref, d_ref, q_ref, yl_ref, y_ref), z in zip(chains, zb)]
        zn = [z * d_ref[0, pp, 0, jj][0:1, :] + _mm(zh, _rwkv_stack(m_ref[0, pp, 0, jj]))
              + _rwkv_stack(n_ref[0, pp, 0, jj])
              for (pp, d, jj, first, m_ref, n_ref, d_ref, q_ref, yl_ref, y_ref), z, zh in zip(chains, zs, zb)]
        for (pp, d, jj, first, m_ref, n_ref, d_ref, q_ref, yl_ref, y_ref), y, z in zip(chains, ys, zn):
            z_ref[2 * pp + d] = z
            y_ref[pl.ds(pl.multiple_of((first + jj) * c, c), c), pp * LANES:(pp + 1) * LANES] = (
                _rwkv_fold(y) + yl_ref[0, pp, 0, jj])
        return carry

    lax.fori_loop(0, group, lat_body, 0)

    @pl.when(g == n_groups - 1)
    def _():
        fr = 4 * c
        r2 = lax.broadcasted_iota(jnp.int32, (LANES, LANES), 0)
        c2 = lax.broadcasted_iota(jnp.int32, (LANES, LANES), 1)
        group_ones = ((r2 < R_N) == (c2 < R_N)).astype(BF16)
        inv_n = 1.0 / R_N

        def fin(i, carry):
            rows = pl.ds(pl.multiple_of(i * fr, fr), fr)
            for pp in pairs:
                cols = slice(pp * LANES, (pp + 1) * LANES)
                y = yf_ref[rows, cols] + yb_ref[rows, cols]
                sums = _mm_split(jnp.concatenate([y, y * y], axis=0), group_ones, False, pieces=2)
                mu = sums[0:fr] * inv_n
                var = sums[fr:2 * fr] * inv_n - mu * mu
                yn = (y - mu) * lax.rsqrt(var + GN_EPS) * gnw_ref[:, cols] + gnb_ref[:, cols]
                o_ref[rows, cols] = (yn * gate_ref[rows, cols].astype(F32)
                                     + bg_ref[rows, cols].astype(F32)).astype(o_ref.dtype)
            return carry

        lax.fori_loop(0, o_ref.shape[0] // fr, fin, 0)


def _rwkv_prep(p_all, batch, group, want_y, ww, wa, w0, a0, k_k, k_a, wg=None, r_k=None):
    c = RWKV_CHUNK
    length = p_all.shape[1] // batch
    n_chunks = length // c
    n_groups = n_chunks // group
    n_pairs = ww.shape[0]
    rows = group * c

    def slab(first):
        return pl.BlockSpec((1, rows, LANES), lambda b, p, g: (first + p, b * n_groups + g, 0))

    def fixed(idx):
        return pl.BlockSpec((1, rows, LANES), lambda b, p, g: (idx, b * n_groups + g, 0))

    def vec(nrows):
        return pl.BlockSpec((nrows, LANES), lambda b, p, g: (0, p))

    lora = pl.BlockSpec((1, LANES, 2 * LANES), lambda b, p, g: (p, 0, 0))
    outs = [(c, BF16), (c, BF16), (8, F32)] + ([(c, BF16), (c, BF16)] if want_y else [])
    in_specs = [slab(SLAB_R), slab(SLAB_KR), slab(SLAB_VR), fixed(SLAB_LW), fixed(SLAB_LA),
                lora, lora, vec(2), vec(2), vec(1), vec(1)]
    operands = [p_all, p_all, p_all, p_all, p_all, ww, wa, w0, a0, k_k, k_a]
    out_specs = [pl.BlockSpec((1, 1, 2, group, nr, LANES), lambda b, p, g: (b, p, 0, g, 0, 0)) for nr, _ in outs]
    out_shape = [jax.ShapeDtypeStruct((batch, n_pairs, 2, n_chunks, nr, LANES), dt) for nr, dt in outs]
    if want_y:
        in_specs += [fixed(SLAB_LG0), fixed(SLAB_LG1),
                     pl.BlockSpec((2 * LANES, LANES), lambda b, p, g: (0, p)), vec(1)]
        operands += [p_all, p_all, wg, r_k]
        out_specs += [pl.BlockSpec((rows, LANES), lambda b, p, g: (b * n_groups + g, p))] * 2
        out_shape += [jax.ShapeDtypeStruct((batch * length, n_pairs * LANES), BF16)] * 2
    return pl.pallas_call(
        functools.partial(_rwkv_prep_kernel, want_y),
        grid=(batch, n_pairs, n_groups),
        in_specs=in_specs,
        out_specs=out_specs,
        out_shape=out_shape,
        compiler_params=_params(3),
    )(*operands)


def _rwkv_scan(ops_c, ops_x, gate, bonus_gated, batch, group, gn_w, gn_b):
    c = RWKV_CHUNK
    sp = RWKV_SCAN_PAIRS
    lx = gate.shape[0] // batch
    n_pairs, n_ctx = ops_c[0].shape[1], ops_c[0].shape[3]
    n_groups = ops_x[0].shape[3] // group
    assert n_pairs % sp == 0

    def ctx_block(a):
        return pl.BlockSpec((1, sp, 2, n_ctx, a.shape[4], LANES), lambda b, p, g: (b, p, 0, 0, 0, 0))

    def fwd_block(a):
        return pl.BlockSpec((1, sp, 1, group, a.shape[4], LANES), lambda b, p, g: (b, p, 0, g, 0, 0))

    def bwd_block(a):
        return pl.BlockSpec((1, sp, 1, group, a.shape[4], LANES),
                            lambda b, p, g: (b, p, 1, n_groups - 1 - g, 0, 0))

    tokens = pl.BlockSpec((lx, sp * LANES), lambda b, p, g: (b, p), pipeline_mode=pl.Buffered(1))
    vec = pl.BlockSpec((1, sp * LANES), lambda b, p, g: (0, p))
    return pl.pallas_call(
        _rwkv_scan_kernel,
        grid=(batch, n_pairs // sp, n_groups),
        in_specs=[ctx_block(a) for a in ops_c] + [fwd_block(a) for a in ops_x] + [bwd_block(a) for a in ops_x]
                 + [tokens, tokens, vec, vec],
        out_specs=tokens,
        out_shape=jax.ShapeDtypeStruct((batch * lx, n_pairs * LANES), BF16),
        scratch_shapes=[pltpu.VMEM((2 * sp, 2 * c, LANES), F32),
                        pltpu.VMEM((lx, sp * LANES), F32), pltpu.VMEM((lx, sp * LANES), F32)],
        compiler_params=_params(3),
    )(*ops_c, *ops_x, *ops_x, gate, bonus_gated, gn_w, gn_b)


def _rwkv(px, pc, batch, ww, wa, wg, w0, a0, k_k, k_a, r_k, gn_w, gn_b):
    c = RWKV_CHUNK
    n_ctx = pc.shape[1] // batch // c
    n_lat = px.shape[1] // batch // c
    group = min(RWKV_GROUP, n_lat)
    assert n_lat % group == 0
    ops_c = _rwkv_prep(pc, batch, n_ctx, False, ww, wa, w0, a0, k_k, k_a)
    *ops_x, gate, bonus_gated = _rwkv_prep(px, batch, group, True, ww, wa, w0, a0, k_k, k_a, wg, r_k)
    scan_group = min(RWKV_SCAN_GROUP, n_lat)
    assert n_lat % scan_group == 0
    return _rwkv_scan(ops_c, ops_x, gate, bonus_gated, batch, scan_group, gn_w, gn_b)


def _out_kernel(tiles_per_batch, mixm_ref, om_ref, mixr_ref, w_ref, x_ref, ng_ref,
                gt1_ref, g2_ref, sh2_ref, sc2_ref, x1_ref, hx2_ref):
    b = pl.program_id(0) // tiles_per_batch
    hm = mixm_ref[...].astype(F32)
    ng = ng_ref[...]
    parts = []
    for h in range(M_HEADS):
        cols = slice(h * M_DV, (h + 1) * M_DV)
        seg = hm[:, cols]
        seg = seg * lax.rsqrt(jnp.mean(seg * seg, axis=-1, keepdims=True) + NORM_EPS)
        og = jnp.concatenate([om_ref[2 * h], om_ref[2 * h + 1]], axis=1)
        parts.append((seg * ng[:, cols] * jax.nn.sigmoid(og)).astype(BF16))
    lhs = jnp.concatenate(parts + [mixr_ref[...]], axis=1)
    x1 = x_ref[...] + gt1_ref[pl.ds(b, 1), :] * _mm(lhs, w_ref[...])
    x1_ref[...] = x1
    y = x1 * lax.rsqrt(jnp.mean(x1 * x1, axis=-1, keepdims=True) + NORM_EPS) * g2_ref[...]
    hx2_ref[...] = (y * (1.0 + sc2_ref[pl.ds(b, 1), :]) + sh2_ref[pl.ds(b, 1), :]).astype(BF16)


def _out_proj(mixm, px, mixr, w_out, x2d, ng, gt1, g2, sh2, sc2, tm, tiles_per_batch):
    t, d = x2d.shape
    dm = mixm.shape[1]
    row = lambda i: (i, 0)
    const = lambda i: (0, 0)
    return pl.pallas_call(
        functools.partial(_out_kernel, tiles_per_batch),
        grid=(t // tm,),
        in_specs=[pl.BlockSpec((tm, dm), row),
                  pl.BlockSpec((8, tm, LANES), lambda i: (SLAB_O // 8, i, 0)),
                  pl.BlockSpec((tm, dm), row),
                  pl.BlockSpec((d, d), const),
                  pl.BlockSpec((tm, d), row),
                  pl.BlockSpec((1, dm), const),
                  pl.BlockSpec((8, d), const),
                  pl.BlockSpec((1, d), const),
                  pl.BlockSpec((8, d), const),
                  pl.BlockSpec((8, d), const)],
        out_specs=[pl.BlockSpec((tm, d), row), pl.BlockSpec((tm, d), row)],
        out_shape=[jax.ShapeDtypeStruct((t, d), F32), jax.ShapeDtypeStruct((t, d), BF16)],
        compiler_params=_params(1),
    )(mixm, px, mixr, w_out, x2d, ng, gt1, g2, sh2, sc2)


def _ffn_kernel(tiles_per_img, hx_ref, top_ref, bot_ref, wu_ref, wg_ref, wd_ref, cw_ref, cb_ref,
                x1_ref, gt2_ref, gf_ref, o_ref):
    i = pl.program_id(0)
    j = pl.program_id(1)
    tm = hx_ref.shape[0]
    ti = i % tiles_per_img

    @pl.when(j == 0)
    def _():
        o_ref[...] = jnp.zeros(o_ref.shape, F32)

    hx = hx_ref[...]
    top = jnp.where(ti > 0, top_ref[...], jnp.zeros_like(top_ref[...]))
    bot = jnp.where(ti < tiles_per_img - 1, bot_ref[...], jnp.zeros_like(bot_ref[...]))
    u = _mm(jnp.concatenate([top, hx, bot], axis=0), wu_ref[...])
    nr = u.shape[0]
    col = lax.broadcasted_iota(jnp.int32, u.shape, 0) & (GRID_W - 1)
    ul = jnp.where(col == 0, 0.0, pltpu.roll(u, 1, 0))
    ur = jnp.where(col == GRID_W - 1, 0.0, pltpu.roll(u, nr - 1, 0))
    cw = cw_ref[...]
    conv = cb_ref[...]
    for dy in range(3):
        rows = slice(dy * GRID_W, dy * GRID_W + tm)
        conv = (conv + ul[rows] * cw[3 * dy:3 * dy + 1, :] + u[rows] * cw[3 * dy + 1:3 * dy + 2, :]
                + ur[rows] * cw[3 * dy + 2:3 * dy + 3, :])
    gelu = 0.5 * conv * (1.0 + jnp.tanh(0.7978845608028654 * (conv + 0.044715 * conv * conv * conv)))
    act = (gelu * _mm(hx, wg_ref[...])).astype(BF16)
    o_ref[...] += _mm(act, wd_ref[...])

    @pl.when(j == pl.num_programs(1) - 1)
    def _():
        b = i // tiles_per_img
        x2 = x1_ref[...] + gt2_ref[pl.ds(b, 1), :] * o_ref[...]
        o_ref[...] = x2 * lax.rsqrt(jnp.mean(x2 * x2, axis=-1, keepdims=True) + NORM_EPS) * gf_ref[...]


def _conv_ffn(hx2, w_up, w_gate, w_down, cw, cb, x1, gt2, g_final, tm, tiles_per_img):
    t, d = hx2.shape
    f = w_up.shape[1]
    tf = FFN_TF
    rows_per_tile = tm // GRID_W
    n_rows = t // GRID_W
    return pl.pallas_call(
        functools.partial(_ffn_kernel, tiles_per_img),
        grid=(t // tm, f // tf),
        in_specs=[pl.BlockSpec((tm, d), lambda i, j: (i, 0)),
                  pl.BlockSpec((GRID_W, d), lambda i, j: (jnp.maximum(i * rows_per_tile - 1, 0), 0)),
                  pl.BlockSpec((GRID_W, d), lambda i, j: (jnp.minimum((i + 1) * rows_per_tile, n_rows - 1), 0)),
                  pl.BlockSpec((d, tf), lambda i, j: (0, j)),
                  pl.BlockSpec((d, tf), lambda i, j: (0, j)),
                  pl.BlockSpec((tf, d), lambda i, j: (j, 0)),
                  pl.BlockSpec((9, tf), lambda i, j: (0, j)),
                  pl.BlockSpec((1, tf), lambda i, j: (0, j)),
                  pl.BlockSpec((tm, d), lambda i, j: (i, 0), pipeline_mode=pl.Buffered(1)),
                  pl.BlockSpec((8, d), lambda i, j: (0, 0)),
                  pl.BlockSpec((1, d), lambda i, j: (0, 0))],
        out_specs=pl.BlockSpec((tm, d), lambda i, j: (i, 0), pipeline_mode=pl.Buffered(1)),
        out_shape=jax.ShapeDtypeStruct((t, d), F32),
        compiler_params=_params(2),
    )(hx2, hx2, hx2, w_up, w_gate, w_down, cw, cb, x1, gt2, g_final)


def _lora_pairs(up):
    _, rank, width = up.shape
    u = up.reshape(2, rank, width // LANES, LANES).transpose(2, 0, 1, 3)
    z = jnp.zeros_like(u[:, 0])
    top = jnp.concatenate([u[:, 0], z], axis=-1)
    bot = jnp.concatenate([z, u[:, 1]], axis=-1)
    return jnp.concatenate([top, bot], axis=1).astype(BF16)


def kernel(x, c, ctx, c_ctx, w_mod, b_mod, g_norm1, g_norm2, w_in, m_conv_w, m_conv_b, m_gate_b, m_norm_g, r_w0, r_w_up, r_a0, r_a_up, r_g_up, r_k_k, r_k_a, r_r_k, r_gn_w, r_gn_b, w_out, f_w_up, f_w_gate, f_conv_w, f_conv_b, f_w_down, g_final):
    batch, seq, d = x.shape
    ctx_len = ctx.shape[1]
    assert w_mod.shape[0] == 1, "single-layer block"
    assert batch + 1 <= 8 and seq % 512 == 0 and ctx_len % MLSTM_CHUNK == 0

    cv8 = jnp.zeros((8, d), F32).at[:batch].set(c).at[batch].set(c_ctx)
    mod = _modulation(cv8, w_mod[0], b_mod[0])
    sh1, sc1, gt1, sh2, sc2, gt2 = (mod[:, k * d:(k + 1) * d] for k in range(6))

    n_gate = 4 * M_HEADS
    w_p = _permute_w(jnp.swapaxes(w_in, 1, 2), SLAB_R * LANES, n_gate, N_SLABS * LANES)
    g1 = g_norm1[0].reshape(1, d)
    tm_x = INPROJ_TM if seq % INPROJ_TM == 0 else OUT_TM
    px = _inproj(x.reshape(batch * seq, d), g1, sh1, sc1, w_p, tm_x, seq // tm_x, 0)
    tm_c = batch * ctx_len
    pc = _inproj(ctx.reshape(batch * ctx_len, d), g1, sh1, sc1, w_p, tm_c, 1 << 30, batch)

    gate_row = jnp.zeros((1, LANES), F32).at[0, GATE_LANE0:GATE_LANE0 + n_gate].set(m_gate_b[0].reshape(-1))
    mixm = _mlstm(px, pc, batch, m_conv_w[0], m_conv_b[0].reshape(1, -1), gate_row)

    rw = r_k_k.shape[1]
    wg = jnp.zeros((2 * LANES, rw), F32).at[:r_g_up.shape[1]].set(r_g_up[0]).astype(BF16)
    mixr = _rwkv(px, pc, batch, _lora_pairs(r_w_up[0]), _lora_pairs(r_a_up[0]), wg,
                 r_w0[0], r_a0[0], r_k_k, r_k_a, r_r_k[0].reshape(1, rw), r_gn_w, r_gn_b)

    tm_o = OUT_TM
    x1, hx2 = _out_proj(mixm, px, mixr, w_out[0].astype(BF16), x.reshape(batch * seq, d), m_norm_g,
                        gt1, g_norm2[0].reshape(1, d), sh2, sc2, tm_o, seq // tm_o)

    tm_f = FFN_TM if seq % FFN_TM == 0 else OUT_TM
    out = _conv_ffn(hx2, f_w_up[0].astype(BF16), f_w_gate[0].astype(BF16), f_w_down[0].astype(BF16),
                    f_conv_w[0].reshape(9, -1), f_conv_b, x1, gt2, g_final.reshape(1, d),
                    tm_f, seq // tm_f)
    return out.reshape(batch, seq, d)
```

```python
import functools

import jax
import jax.numpy as jnp
from jax import lax
from jax.experimental import pallas as pl
from jax.experimental.pallas import tpu as pltpu

F32 = jnp.float32
BF16 = jnp.bfloat16

LANES = 128
GRID_W = 64
M_HEADS = 4
M_DQK = 128
M_DV = 256
R_N = 64
NORM_EPS = 1e-6
GN_EPS = 64e-5
MLSTM_CHUNK = 128
RWKV_CHUNK = 64
RWKV_GROUP = 16
RWKV_SCAN_GROUP = 8
RWKV_SCAN_PAIRS = 4
RWKV_PREP_WIDTH = 16
VMEM_LIMIT = 56 * 1024 * 1024
MOD_TN = 1024
PERMUTE_ROWS = 512
INPROJ_TM, INPROJ_TN = 1024, 512
OUT_TM = 512
FFN_TM, FFN_TF = 1024, 512

SLAB_Q, SLAB_K, SLAB_V, SLAB_O = 0, 4, 8, 16
SLAB_R, SLAB_KR, SLAB_VR = 24, 32, 40
SLAB_LW, SLAB_LA, SLAB_LG0, SLAB_LG1 = 48, 49, 50, 51
N_SLABS = 52
GATE_LANE0 = 32


def _mm(a, b):
    return jnp.dot(a, b, preferred_element_type=F32)


def _mm_nt(a, b):
    return lax.dot_general(a, b, (((1,), (1,)), ((), ())), preferred_element_type=F32)


def _mm_tn(a, b):
    return lax.dot_general(a, b, (((0,), (0,)), ((), ())), preferred_element_type=F32)


def _mm_split(x, ones, ones_first, pieces=3):
    dot = (lambda p: _mm(ones, p)) if ones_first else (lambda p: _mm(p, ones))
    piece = x.astype(BF16)
    total = dot(piece)
    for _ in range(pieces - 1):
        x = x - piece.astype(F32)
        piece = x.astype(BF16)
        total = total + dot(piece)
    return total


def _softplus(x):
    return jnp.maximum(x, 0.0) + jnp.log(1.0 + jnp.exp(-jnp.abs(x)))


def _log_sigmoid(x):
    return -_softplus(-x)


def _params(n_axes):
    return pltpu.CompilerParams(dimension_semantics=("arbitrary",) * n_axes,
                                vmem_limit_bytes=VMEM_LIMIT)


def _mod_kernel(cv_ref, w_ref, b_ref, o_ref):
    cv = cv_ref[...]
    s = (cv * jax.nn.sigmoid(cv)).astype(BF16)
    o_ref[...] = _mm(s, w_ref[...].astype(BF16)) + b_ref[...]


def _modulation(cv8, w_mod, b_mod):
    d, n = w_mod.shape
    tn = MOD_TN
    return pl.pallas_call(
        _mod_kernel,
        grid=(n // tn,),
        in_specs=[pl.BlockSpec((8, d), lambda j: (0, 0)),
                  pl.BlockSpec((d, tn), lambda j: (0, j)),
                  pl.BlockSpec((1, tn), lambda j: (0, j))],
        out_specs=pl.BlockSpec((8, tn), lambda j: (0, j)),
        out_shape=jax.ShapeDtypeStruct((8, n), F32),
        compiler_params=_params(1),
    )(cv8, w_mod, b_mod.reshape(1, n))


def _permute_w_kernel(n_main, n_gate, n_valid, cur_ref, nxt_ref, gate_ref, o_ref):
    j = pl.program_id(0)
    last = pl.num_programs(0) - 1
    tr = o_ref.shape[0]

    @pl.when(j < n_main)
    def _():
        o_ref[...] = cur_ref[0].astype(BF16)

    @pl.when(jnp.logical_and(j >= n_main, j < last))
    def _():
        o_ref[...] = jnp.concatenate([cur_ref[0, n_gate:tr, :], nxt_ref[0]], axis=0).astype(BF16)

    @pl.when(j == last)
    def _():
        pad = jnp.zeros((tr - n_valid, o_ref.shape[1]), F32)
        o_ref[...] = jnp.concatenate([cur_ref[0, n_gate:n_valid, :], gate_ref[0], pad], axis=0).astype(BF16)


def _permute_w(w_t, n_main, n_gate, n_out):
    _, n, d = w_t.shape
    tr = PERMUTE_ROWS
    n_blocks = n_out // tr
    assert n_main % tr == 0 and n_out % tr == 0 and pl.cdiv(n, tr) == n_blocks and tr % n_gate == 0
    per = tr // n_gate
    return pl.pallas_call(
        functools.partial(_permute_w_kernel, n_main // tr, n_gate, n - (n_blocks - 1) * tr),
        grid=(n_blocks,),
        in_specs=[pl.BlockSpec((1, tr, d), lambda j: (0, j, 0)),
                  pl.BlockSpec((1, n_gate, d), lambda j: (0, jnp.minimum(j + 1, n_blocks - 1) * per, 0)),
                  pl.BlockSpec((1, n_gate, d), lambda j: (0, n_main // n_gate, 0))],
        out_specs=pl.BlockSpec((tr, d), lambda j: (j, 0)),
        out_shape=jax.ShapeDtypeStruct((n_out, d), BF16),
        compiler_params=_params(1),
    )(w_t, w_t, w_t)


def _inproj_kernel(tiles_per_row, row0, x_ref, g_ref, sh_ref, sc_ref, w_ref, o_ref, hx_ref):
    i = pl.program_id(0)
    j = pl.program_id(1)

    @pl.when(j == 0)
    def _():
        r = row0 + i // tiles_per_row
        scale = g_ref[...] * (1.0 + sc_ref[pl.ds(r, 1), :])
        shift = sh_ref[pl.ds(r, 1), :]
        nb = 256

        def norm_rows(k, carry):
            rows = pl.ds(pl.multiple_of(k * nb, nb), nb)
            x = x_ref[rows, :]
            inv = lax.rsqrt(jnp.mean(x * x, axis=-1, keepdims=True) + NORM_EPS)
            hx_ref[rows, :] = (x * inv * scale + shift).astype(BF16)
            return carry

        lax.fori_loop(0, x_ref.shape[0] // nb, norm_rows, 0)

    acc = _mm_nt(hx_ref[...], w_ref[...])
    for s in range(acc.shape[1] // LANES):
        o_ref[s] = acc[:, s * LANES:(s + 1) * LANES]


def _inproj(x2d, g, sh, sc, w_p, tm, tiles_per_row, row0):
    t, d = x2d.shape
    n = w_p.shape[0]
    tn = INPROJ_TN
    return pl.pallas_call(
        functools.partial(_inproj_kernel, tiles_per_row, row0),
        grid=(t // tm, n // tn),
        in_specs=[pl.BlockSpec((tm, d), lambda i, j: (i, 0)),
                  pl.BlockSpec((1, d), lambda i, j: (0, 0)),
                  pl.BlockSpec((8, d), lambda i, j: (0, 0)),
                  pl.BlockSpec((8, d), lambda i, j: (0, 0)),
                  pl.BlockSpec((tn, d), lambda i, j: (j, 0))],
        out_specs=pl.BlockSpec((tn // LANES, tm, LANES), lambda i, j: (j, i, 0)),
        out_shape=jax.ShapeDtypeStruct((n // LANES, t, LANES), F32),
        scratch_shapes=[pltpu.VMEM((tm, d), BF16)],
        compiler_params=_params(2),
    )(x2d, g, sh, sc, w_p)


def _mlstm_kernel(n_ctx, n_lat,
                  qx_ref, kx_ref, vx_ref, gx_ref, qc_ref, kc_ref, vc_ref, gc_ref,
                  cwq_ref, cwk_ref, cbq_ref, cbk_ref, gb_ref,
                  o_ref,
                  qs_ref, ks_ref, kst_ref, gcol_ref, grow_ref, ct_ref, hf_ref, hb_ref):
    c = MLSTM_CHUNK
    head = pl.program_id(1)
    rid = lax.broadcasted_iota(jnp.int32, (c, LANES), 0)
    lane = lax.broadcasted_iota(jnp.int32, (c, LANES), 1)
    r2 = lax.broadcasted_iota(jnp.int32, (c, c), 0)
    c2 = lax.broadcasted_iota(jnp.int32, (c, c), 1)
    prefix = (c2 <= r2).astype(BF16)
    keep = (c2 <= r2, c2 >= r2)
    gbias = gb_ref[...]

    def conv_pass(q_ref, k_ref, g_ref, nchunk, base):
        nrows = nchunk * c

        def body(ci, carry):
            r0 = pl.multiple_of(ci * c, c)
            p0 = pl.multiple_of(jnp.maximum(r0 - 8, 0), 8)
            n0 = pl.multiple_of(jnp.minimum(r0 + c, nrows - 8), 8)
            gb = g_ref[0, pl.ds(r0, c), :] + gbias
            lf = _log_sigmoid(gb)
            pre = _mm_split(lf, prefix, True)
            suf = pre[c - 1:c, :] - pre + lf
            packed = jnp.zeros((c, LANES), F32)
            for slot, (src, col0) in enumerate(((pre, GATE_LANE0 + 2 * M_HEADS), (suf, GATE_LANE0 + 3 * M_HEADS),
                                                (gb, GATE_LANE0), (gb, GATE_LANE0 + M_HEADS))):
                col = jnp.sum(jnp.where(lane == col0 + head, src, 0.0), axis=1, keepdims=True)
                packed = jnp.where(lane == slot, col, packed)
            gcol_ref[base + ci] = packed
            grow_ref[base + ci] = packed.T[0:8, :]
            for src, w_ref, b_ref, scale, is_k in ((q_ref, cwq_ref, cbq_ref, M_DQK ** -0.5, False),
                                                   (k_ref, cwk_ref, cbk_ref, 1.0, True)):
                cur = src[0, pl.ds(r0, c), :]
                prev_row = jnp.where(ci > 0, src[0, pl.ds(p0, 8), :][7:8, :], 0.0)
                next_row = jnp.where(ci < nchunk - 1, src[0, pl.ds(n0, 8), :][0:1, :], 0.0)
                up = jnp.where(rid == 0, prev_row, pltpu.roll(cur, 1, 0))
                dn = jnp.where(rid == c - 1, next_row, pltpu.roll(cur, c - 1, 0))
                w = w_ref[...]
                y = (up * w[0:1, :] + cur * w[1:2, :] + dn * w[2:3, :] + b_ref[...]) * scale
                if is_k:
                    ks_ref[base + ci] = y.astype(BF16)
                    kst_ref[base + ci] = y.T.astype(BF16)
                else:
                    qs_ref[base + ci] = y.astype(BF16)
            return carry

        lax.fori_loop(0, nchunk, body, 0, unroll=min(4, nchunk))

    def step(d, ci, base, v_ref, n, m, want_h):
        r0 = pl.multiple_of(ci * c, c)
        q = qs_ref[base + ci]
        k = ks_ref[base + ci]
        kt = kst_ref[base + ci]
        v = jnp.concatenate([v_ref[0, pl.ds(r0, c), :], v_ref[1, pl.ds(r0, c), :]], axis=1).astype(BF16)
        gcol = gcol_ref[base + ci]
        grow = grow_ref[base + ci]
        rep = lambda x: jnp.broadcast_to(x, (c, LANES))
        wide = lambda x: jnp.concatenate([x, x], axis=1)
        b_col, i_col = rep(gcol[:, d:d + 1]), rep(gcol[:, 2 + d:3 + d])
        b_row, i_row = grow[d:d + 1, :], grow[2 + d:3 + d, :]
        b_last = b_col[0:1, :] if d else b_col[c - 1:c, :]
        ct = ct_ref[d]
        h = None
        if want_h:
            dmat = jnp.where(keep[d], b_col - b_row + i_row, -jnp.inf)
            m_intra = rep(jnp.max(dmat, axis=-1, keepdims=True))
            s = _mm(q, kt) * jnp.exp(dmat - m_intra)
            num_intra = _mm(s.astype(BF16), v)
            den_intra = rep(jnp.sum(s, axis=-1, keepdims=True))
            m_inter = b_col + m
            m_j = jnp.maximum(m_inter, m_intra)
            intra = jnp.exp(m_intra - m_j)
            inter = jnp.exp(m_inter - m_j)
            num = wide(intra) * num_intra + wide(inter) * _mm(q, ct.astype(BF16))
            qn = rep(jnp.sum(q.astype(F32) * n, axis=-1, keepdims=True))
            den = intra * den_intra + inter * qn
            h = num / wide(jnp.maximum(jnp.abs(den), jnp.exp(-m_j)))
        glog = b_last - b_col + i_col
        b_last = b_last[:, 0:1]
        m_new = jnp.maximum(b_last + m, jnp.max(glog, axis=0, keepdims=True)[:, 0:1])
        wk = jnp.exp(glog - m_new)
        wk_row = jnp.exp(b_last - b_row + i_row - m_new)
        decay = jnp.exp(b_last + m - m_new)
        ct_ref[d] = decay * ct + _mm((kt.astype(F32) * wk_row).astype(BF16), v)
        n_new = decay * n + jnp.sum(wk * k.astype(F32), axis=0, keepdims=True)
        return n_new, m_new, h

    def run(nchunk, base, v_ref, carry, want_h):
        def body(i, carry):
            nf, mf, nb, mb = carry
            ib = nchunk - 1 - i
            nf, mf, h_f = step(0, i, base, v_ref, nf, mf, want_h)
            nb, mb, h_b = step(1, ib, base, v_ref, nb, mb, want_h)
            if want_h:
                hf_ref[pl.ds(pl.multiple_of(i * c, c), c), :] = h_f
                hb_ref[pl.ds(pl.multiple_of(ib * c, c), c), :] = h_b
            return nf, mf, nb, mb

        return lax.fori_loop(0, nchunk, body, carry, unroll=2)

    conv_pass(qc_ref, kc_ref, gc_ref, n_ctx, 0)
    conv_pass(qx_ref, kx_ref, gx_ref, n_lat, n_ctx)
    ct_ref[...] = jnp.zeros(ct_ref.shape, F32)
    zn = jnp.zeros((1, M_DQK), F32)
    zm = jnp.zeros((1, 1), F32)
    carry = run(n_ctx, 0, vc_ref, (zn, zm, zn, zm), False)
    run(n_lat, n_ctx, vx_ref, carry, True)

    def fin(i, carry):
        rows = pl.ds(pl.multiple_of(i * c, c), c)
        o_ref[rows, :] = (hf_ref[rows, :] + hb_ref[rows, :]).astype(o_ref.dtype)
        return carry

    lax.fori_loop(0, n_lat, fin, 0)


def _mlstm(px, pc, batch, cw, cb, gate_row):
    lx = px.shape[1] // batch
    lc = pc.shape[1] // batch
    n_lat, n_ctx = lx // MLSTM_CHUNK, lc // MLSTM_CHUNK

    def slab(nrows, first, width=1):
        return pl.BlockSpec((width, nrows, LANES), lambda b, h: (first // width + h, b, 0))

    def fixed(nrows, idx):
        return pl.BlockSpec((1, nrows, LANES), lambda b, h: (idx, b, 0))

    in_specs = [slab(lx, SLAB_Q), slab(lx, SLAB_K), slab(lx, SLAB_V, 2), fixed(lx, SLAB_LG1),
                slab(lc, SLAB_Q), slab(lc, SLAB_K), slab(lc, SLAB_V, 2), fixed(lc, SLAB_LG1),
                pl.BlockSpec((3, LANES), lambda b, h: (0, h)),
                pl.BlockSpec((3, LANES), lambda b, h: (0, M_HEADS + h)),
                pl.BlockSpec((1, LANES), lambda b, h: (0, h)),
                pl.BlockSpec((1, LANES), lambda b, h: (0, M_HEADS + h)),
                pl.BlockSpec((1, LANES), lambda b, h: (0, 0))]
    nch = n_ctx + n_lat
    return pl.pallas_call(
        functools.partial(_mlstm_kernel, n_ctx, n_lat),
        grid=(batch, M_HEADS),
        in_specs=in_specs,
        out_specs=pl.BlockSpec((lx, M_DV), lambda b, h: (b, h)),
        out_shape=jax.ShapeDtypeStruct((batch * lx, M_HEADS * M_DV), BF16),
        scratch_shapes=[pltpu.VMEM((nch, MLSTM_CHUNK, LANES), BF16),
                        pltpu.VMEM((nch, MLSTM_CHUNK, LANES), BF16),
                        pltpu.VMEM((nch, LANES, MLSTM_CHUNK), BF16),
                        pltpu.VMEM((nch, MLSTM_CHUNK, LANES), F32),
                        pltpu.VMEM((nch, 8, LANES), F32),
                        pltpu.VMEM((2, M_DQK, M_DV), F32),
                        pltpu.VMEM((lx, M_DV), F32),
                        pltpu.VMEM((lx, M_DV), F32)],
        compiler_params=_params(2),
    )(px, px, px, px, pc, pc, pc, pc, cw, cw, cb, cb, gate_row)


def _rwkv_stack(x):
    head0 = lax.broadcasted_iota(jnp.int32, x.shape, 1) < R_N
    zero = jnp.zeros_like(x)
    return jnp.concatenate([jnp.where(head0, x, zero), jnp.where(head0, zero, x)], axis=0)


def _rwkv_fold(x):
    half = x.shape[0] // 2
    return x[0:half] + x[half:2 * half]


def _rwkv_prep_kernel(want_y, r_ref, k_ref, v_ref, lw_ref, la_ref,
                      ww_ref, wa_ref, w0_ref, a0_ref, kk_ref, ka_ref, *rest):
    extra_refs, out_refs = (rest[:4], rest[4:]) if want_y else ((), rest)
    c = RWKV_CHUNK
    c2 = 2 * c
    stack = _rwkv_stack
    rr = lax.broadcasted_iota(jnp.int32, (c2, c2), 0)
    cc = lax.broadcasted_iota(jnp.int32, (c2, c2), 1)
    group_ones = ((rr < c) == (cc < c)).astype(BF16)
    tt = lax.broadcasted_iota(jnp.int32, (c, c2), 0)
    lane = lax.broadcasted_iota(jnp.int32, (c, c2), 1)
    ss = lane & (c - 1)
    head0 = lane < c
    strict = (ss < tt, ss > tt)
    incl = (ss <= tt, ss >= tt)
    eye = (ss == tt).astype(F32)
    tr = lax.broadcasted_iota(jnp.int32, (c, c), 0)
    tc = lax.broadcasted_iota(jnp.int32, (c, c), 1)
    tri = ((tc <= tr).astype(BF16), (tc >= tr).astype(BF16))
    ww = ww_ref[0]
    wa = wa_ref[0]
    k_k = kk_ref[...]
    k_a = ka_ref[...]
    m_ref, n_ref, dec_ref = out_refs[0], out_refs[1], out_refs[2]

    def prep_chunks(js):
        nj = len(js)
        rows = [pl.ds(pl.multiple_of(j * c, c), c) for j in js]
        ch = [(d, i) for d in (0, 1) for i in range(nj)]
        half = lambda d: slice(d * LANES, (d + 1) * LANES)
        r = [r_ref[0, rw, :] for rw in rows]
        k = [k_ref[0, rw, :] for rw in rows]
        v_s = [stack(v_ref[0, rw, :]).astype(BF16) for rw in rows]
        lo_w = [_mm(jnp.tanh(lw_ref[0, rw, :]).astype(BF16), ww) for rw in rows]
        lo_a = [_mm(la_ref[0, rw, :].astype(BF16), wa) for rw in rows]
        kk = [x * k_k for x in k]
        kk = [x * lax.rsqrt(_mm_split(x * x, group_ones, False, pieces=2) + 1e-12) for x in kk]
        logw = [-jnp.exp(-_softplus(-(w0_ref[d:d + 1, :] + lo_w[i][:, half(d)])) - 0.5) for d, i in ch]
        a = [jax.nn.sigmoid(a0_ref[d:d + 1, :] + lo_a[i][:, half(d)]) for d, i in ch]
        pin = [_mm_split(x, tri[d], True, pieces=2) for (d, i), x in zip(ch, logw)]
        ptot = [x[0:1, :] if d else x[c - 1:c, :] for (d, i), x in zip(ch, pin)]
        kd = [k[i] * (1.0 + (a_ - 1.0) * k_a) for (d, i), a_ in zip(ch, a)]
        kka = [kk[i] * a_ for (d, i), a_ in zip(ch, a)]
        e_inv = [jnp.exp(-x) for x in pin]
        e_end = [jnp.exp(pt - x) for pt, x in zip(ptot, pin)]
        r_t = [r[i] * jnp.exp(x) for (d, i), x in zip(ch, pin)]
        a_t = [-kk[i] * jnp.exp(x - lw_) for (d, i), x, lw_ in zip(ch, pin, logw)]
        ar_t = [jnp.concatenate([x, y], axis=0).astype(BF16) for x, y in zip(a_t, r_t)]
        bk_t = [jnp.concatenate([stack(x * e), stack(y * e)], axis=0).astype(BF16)
                for x, y, e in zip(kka, kd, e_inv)]
        bk_end = [jnp.concatenate([x * e, y * e], axis=0).astype(BF16) for x, y, e in zip(kka, kd, e_end)]
        aa = [_mm_nt(x, y) for x, y in zip(ar_t, bk_t)]
        aab = [jnp.where(strict[d], x[0:c, 0:c2], 0.0) for (d, i), x in zip(ch, aa)]
        aak = [jnp.where(strict[d], x[0:c, c2:2 * c2], 0.0).astype(BF16) for (d, i), x in zip(ch, aa)]
        akv = [_mm(x, v_s[i]).astype(BF16) for (d, i), x in zip(ch, aak)]
        xs = [eye + x for x in aab]
        pb = [x.astype(BF16) for x in aab]
        ps = [_mm(x, stack(x)) for x in pb]
        for _ in range(c.bit_length() - 3):
            pb = [x.astype(BF16) for x in ps]
            both = [_mm(jnp.concatenate([x.astype(BF16), p], axis=0), stack(p)) for x, p in zip(xs, pb)]
            xs = [x + y[0:c] for x, y in zip(xs, both)]
            ps = [y[c:c2] for y in both]
        xs = [(x + _mm(x.astype(BF16), stack(p.astype(BF16)))).astype(BF16) for x, p in zip(xs, ps)]
        wu = [_mm(x, jnp.concatenate([stack(y[0:c]), stack(z)], axis=1)).astype(BF16)
              for x, y, z in zip(xs, ar_t, akv)]
        m_mat = [_mm_tn(x[:, 0:c2], y[0:c]) for x, y in zip(wu, bk_end)]
        n_mat = [_mm_tn(jnp.concatenate([x[:, c2:2 * c2], v_ref[0, rows[i], :].astype(BF16)], axis=0), y)
                 for (d, i), x, y in zip(ch, wu, bk_end)]
        for (d, i), mm_, nn_, pt in zip(ch, m_mat, n_mat, ptot):
            slot = (0, 0, d, js[i])
            m_ref[slot] = jnp.where(head0, mm_[0:c], mm_[c:c2]).astype(BF16)
            n_ref[slot] = jnp.where(head0, nn_[0:c], nn_[c:c2])
            dec_ref[slot] = jnp.broadcast_to(jnp.exp(pt), (8, LANES))
        if want_y:
            ark = [jnp.concatenate([jnp.where(incl[d], x[c:c2, 0:c2], 0.0),
                                    jnp.where(incl[d], x[c:c2, c2:2 * c2], 0.0)], axis=1).astype(BF16)
                   for (d, i), x in zip(ch, aa)]
            qy = [_mm(x, jnp.concatenate([
                      jnp.concatenate([stack(w_[:, 0:c2]), stack(w_[:, c2:2 * c2])], axis=1),
                      jnp.concatenate([jnp.zeros((c2, c2), BF16), v_s[i]], axis=1)], axis=0))
                  for (d, i), x, w_ in zip(ch, ark, wu)]
            for (d, i), rt, x in zip(ch, r_t, qy):
                slot = (0, 0, d, js[i])
                out_refs[3][slot] = (rt + x[:, 0:c2]).astype(BF16)
                out_refs[4][slot] = x[:, c2:2 * c2]
            lg0_ref, lg1_ref, wg_ref, rk_ref = extra_refs
            ksum = [k[i] * (2.0 + (a[i] + a[nj + i] - 2.0) * k_a) for i in range(nj)]
            bonus = [_mm_split(r[i] * ksum[i] * rk_ref[...], group_ones, False) * v_ref[0, rows[i], :]
                     for i in range(nj)]
            gate = [_mm(jax.nn.sigmoid(jnp.concatenate([lg0_ref[0, rw, :], lg1_ref[0, rw, :]], axis=1)
                                       ).astype(BF16), wg_ref[...]) for rw in rows]
            for rw, gt, bn in zip(rows, gate, bonus):
                out_refs[5][rw, :] = gt.astype(BF16)
                out_refs[6][rw, :] = (bn * gt).astype(BF16)

    n_chunks = r_ref.shape[1] // c
    width = min(RWKV_PREP_WIDTH, n_chunks)
    assert n_chunks % width == 0

    def body(t, carry):
        prep_chunks([t * width + u for u in range(width)])
        return carry

    lax.fori_loop(0, n_chunks // width, body, 0)


def _rwkv_scan_kernel(mc_ref, nc_ref, dc_ref, mf_ref, nf_ref, df_ref, qf_ref, ylf_ref,
                      mb_ref, nb_ref, db_ref, qb_ref, ylb_ref,
                      gate_ref, bg_ref, gnw_ref, gnb_ref,
                      o_ref, z_ref, yf_ref, yb_ref):
    c = RWKV_CHUNK
    g = pl.program_id(2)
    n_groups = pl.num_programs(2)
    group = mf_ref.shape[3]
    n_ctx = mc_ref.shape[3]
    pairs = range(RWKV_SCAN_PAIRS)

    def advance(z, m_c, n_c, dec):
        return z * dec[0:1, :] + _mm(z.astype(BF16), _rwkv_stack(m_c)) + _rwkv_stack(n_c)

    @pl.when(g == 0)
    def _():
        z_ref[...] = jnp.zeros(z_ref.shape, F32)

        def ctx_body(i, carry):
            for pp in pairs:
                for d, ii in ((0, i), (1, n_ctx - 1 - i)):
                    zi = 2 * pp + d
                    z_ref[zi] = advance(z_ref[zi], mc_ref[0, pp, d, ii], nc_ref[0, pp, d, ii],
                                        dc_ref[0, pp, d, ii])
            return carry

        lax.fori_loop(0, n_ctx, ctx_body, 0)

    def lat_body(j, carry):
        chains = [(pp,) + t for pp in pairs for t in (
            (0, j, g * group, mf_ref, nf_ref, df_ref, qf_ref, ylf_ref, yf_ref),
            (1, group - 1 - j, (n_groups - 1 - g) * group, mb_ref, nb_ref, db_ref, qb_ref, ylb_ref, yb_ref))]
        zs = [z_ref[2 * pp + d] for pp, d, *_ in chains]
        zb = [z.astype(BF16) for z in zs]
        ys = [_mm_nt(_rwkv_stack(q_ref[0, pp, 0, jj]), z)
              for (pp, d, jj, first, m_ref, n_ref, d_ref, q_ref, yl_ref, y_ref), z in zip(chains, zb)]
        zn = [z * d_ref[0, pp, 0, jj][0:1, :] + _mm(zh, _rwkv_stack(m_ref[0, pp, 0, jj]))
              + _rwkv_stack(n_ref[0, pp, 0, jj])
              for (pp, d, jj, first, m_ref, n_ref, d_ref, q_ref, yl_ref, y_ref), z, zh in zip(chains, zs, zb)]
        for (pp, d, jj, first, m_ref, n_ref, d_ref, q_ref, yl_ref, y_ref), y, z in zip(chains, ys, zn):
            z_ref[2 * pp + d] = z
            y_ref[pl.ds(pl.multiple_of((first + jj) * c, c), c), pp * LANES:(pp + 1) * LANES] = (
                _rwkv_fold(y) + yl_ref[0, pp, 0, jj])
        return carry

    lax.fori_loop(0, group, lat_body, 0)

    @pl.when(g == n_groups - 1)
    def _():
        fr = 4 * c
        r2 = lax.broadcasted_iota(jnp.int32, (LANES, LANES), 0)
        c2 = lax.broadcasted_iota(jnp.int32, (LANES, LANES), 1)
        group_ones = ((r2 < R_N) == (c2 < R_N)).astype(BF16)
        inv_n = 1.0 / R_N

        def fin(i, carry):
            rows = pl.ds(pl.multiple_of(i * fr, fr), fr)
            for pp in pairs:
                cols = slice(pp * LANES, (pp + 1) * LANES)
                y = yf_ref[rows, cols] + yb_ref[rows, cols]
                sums = _mm_split(jnp.concatenate([y, y * y], axis=0), group_ones, False, pieces=2)
                mu = sums[0:fr] * inv_n
                var = sums[fr:2 * fr] * inv_n - mu * mu
                yn = (y - mu) * lax.rsqrt(var + GN_EPS) * gnw_ref[:, cols] + gnb_ref[:, cols]
                o_ref[rows, cols] = (yn * gate_ref[rows, cols].astype(F32)
                                     + bg_ref[rows, cols].astype(F32)).astype(o_ref.dtype)
            return carry

        lax.fori_loop(0, o_ref.shape[0] // fr, fin, 0)


def _rwkv_prep(p_all, batch, group, want_y, ww, wa, w0, a0, k_k, k_a, wg=None, r_k=None):
    c = RWKV_CHUNK
    length = p_all.shape[1] // batch
    n_chunks = length // c
    n_groups = n_chunks // group
    n_pairs = ww.shape[0]
    rows = group * c

    def slab(first):
        return pl.BlockSpec((1, rows, LANES), lambda b, p, g: (first + p, b * n_groups + g, 0))

    def fixed(idx):
        return pl.BlockSpec((1, rows, LANES), lambda b, p, g: (idx, b * n_groups + g, 0))

    def vec(nrows):
        return pl.BlockSpec((nrows, LANES), lambda b, p, g: (0, p))

    lora = pl.BlockSpec((1, LANES, 2 * LANES), lambda b, p, g: (p, 0, 0))
    outs = [(c, BF16), (c, F32), (8, F32)] + ([(c, BF16), (c, F32)] if want_y else [])
    in_specs = [slab(SLAB_R), slab(SLAB_KR), slab(SLAB_VR), fixed(SLAB_LW), fixed(SLAB_LA),
                lora, lora, vec(2), vec(2), vec(1), vec(1)]
    operands = [p_all, p_all, p_all, p_all, p_all, ww, wa, w0, a0, k_k, k_a]
    out_specs = [pl.BlockSpec((1, 1, 2, group, nr, LANES), lambda b, p, g: (b, p, 0, g, 0, 0)) for nr, _ in outs]
    out_shape = [jax.ShapeDtypeStruct((batch, n_pairs, 2, n_chunks, nr, LANES), dt) for nr, dt in outs]
    if want_y:
        in_specs += [fixed(SLAB_LG0), fixed(SLAB_LG1),
                     pl.BlockSpec((2 * LANES, LANES), lambda b, p, g: (0, p)), vec(1)]
        operands += [p_all, p_all, wg, r_k]
        out_specs += [pl.BlockSpec((rows, LANES), lambda b, p, g: (b * n_groups + g, p))] * 2
        out_shape += [jax.ShapeDtypeStruct((batch * length, n_pairs * LANES), BF16)] * 2
    return pl.pallas_call(
        functools.partial(_rwkv_prep_kernel, want_y),
        grid=(batch, n_pairs, n_groups),
        in_specs=in_specs,
        out_specs=out_specs,
        out_shape=out_shape,
        compiler_params=_params(3),
    )(*operands)


def _rwkv_scan(ops_c, ops_x, gate, bonus_gated, batch, group, gn_w, gn_b):
    c = RWKV_CHUNK
    sp = RWKV_SCAN_PAIRS
    lx = gate.shape[0] // batch
    n_pairs, n_ctx = ops_c[0].shape[1], ops_c[0].shape[3]
    n_groups = ops_x[0].shape[3] // group
    assert n_pairs % sp == 0

    def ctx_block(a):
        return pl.BlockSpec((1, sp, 2, n_ctx, a.shape[4], LANES), lambda b, p, g: (b, p, 0, 0, 0, 0))

    def fwd_block(a):
        return pl.BlockSpec((1, sp, 1, group, a.shape[4], LANES), lambda b, p, g: (b, p, 0, g, 0, 0))

    def bwd_block(a):
        return pl.BlockSpec((1, sp, 1, group, a.shape[4], LANES),
                            lambda b, p, g: (b, p, 1, n_groups - 1 - g, 0, 0))

    tokens = pl.BlockSpec((lx, sp * LANES), lambda b, p, g: (b, p), pipeline_mode=pl.Buffered(1))
    vec = pl.BlockSpec((1, sp * LANES), lambda b, p, g: (0, p))
    return pl.pallas_call(
        _rwkv_scan_kernel,
        grid=(batch, n_pairs // sp, n_groups),
        in_specs=[ctx_block(a) for a in ops_c] + [fwd_block(a) for a in ops_x] + [bwd_block(a) for a in ops_x]
                 + [tokens, tokens, vec, vec],
        out_specs=tokens,
        out_shape=jax.ShapeDtypeStruct((batch * lx, n_pairs * LANES), BF16),
        scratch_shapes=[pltpu.VMEM((2 * sp, 2 * c, LANES), F32),
                        pltpu.VMEM((lx, sp * LANES), F32), pltpu.VMEM((lx, sp * LANES), F32)],
        compiler_params=_params(3),
    )(*ops_c, *ops_x, *ops_x, gate, bonus_gated, gn_w, gn_b)


def _rwkv(px, pc, batch, ww, wa, wg, w0, a0, k_k, k_a, r_k, gn_w, gn_b):
    c = RWKV_CHUNK
    n_ctx = pc.shape[1] // batch // c
    n_lat = px.shape[1] // batch // c
    group = min(RWKV_GROUP, n_lat)
    assert n_lat % group == 0
    ops_c = _rwkv_prep(pc, batch, n_ctx, False, ww, wa, w0, a0, k_k, k_a)
    *ops_x, gate, bonus_gated = _rwkv_prep(px, batch, group, True, ww, wa, w0, a0, k_k, k_a, wg, r_k)
    scan_group = min(RWKV_SCAN_GROUP, n_lat)
    assert n_lat % scan_group == 0
    return _rwkv_scan(ops_c, ops_x, gate, bonus_gated, batch, scan_group, gn_w, gn_b)


def _out_kernel(tiles_per_batch, mixm_ref, om_ref, mixr_ref, w_ref, x_ref, ng_ref,
                gt1_ref, g2_ref, sh2_ref, sc2_ref, x1_ref, hx2_ref):
    b = pl.program_id(0) // tiles_per_batch
    hm = mixm_ref[...].astype(F32)
    ng = ng_ref[...]
    parts = []
    for h in range(M_HEADS):
        cols = slice(h * M_DV, (h + 1) * M_DV)
        seg = hm[:, cols]
        seg = seg * lax.rsqrt(jnp.mean(seg * seg, axis=-1, keepdims=True) + NORM_EPS)
        og = jnp.concatenate([om_ref[2 * h], om_ref[2 * h + 1]], axis=1)
        parts.append((seg * ng[:, cols] * jax.nn.sigmoid(og)).astype(BF16))
    lhs = jnp.concatenate(parts + [mixr_ref[...]], axis=1)
    x1 = x_ref[...] + gt1_ref[pl.ds(b, 1), :] * _mm(lhs, w_ref[...])
    x1_ref[...] = x1
    y = x1 * lax.rsqrt(jnp.mean(x1 * x1, axis=-1, keepdims=True) + NORM_EPS) * g2_ref[...]
    hx2_ref[...] = (y * (1.0 + sc2_ref[pl.ds(b, 1), :]) + sh2_ref[pl.ds(b, 1), :]).astype(BF16)


def _out_proj(mixm, px, mixr, w_out, x2d, ng, gt1, g2, sh2, sc2, tm, tiles_per_batch):
    t, d = x2d.shape
    dm = mixm.shape[1]
    row = lambda i: (i, 0)
    const = lambda i: (0, 0)
    return pl.pallas_call(
        functools.partial(_out_kernel, tiles_per_batch),
        grid=(t // tm,),
        in_specs=[pl.BlockSpec((tm, dm), row),
                  pl.BlockSpec((8, tm, LANES), lambda i: (SLAB_O // 8, i, 0)),
                  pl.BlockSpec((tm, dm), row),
                  pl.BlockSpec((d, d), const),
                  pl.BlockSpec((tm, d), row),
                  pl.BlockSpec((1, dm), const),
                  pl.BlockSpec((8, d), const),
                  pl.BlockSpec((1, d), const),
                  pl.BlockSpec((8, d), const),
                  pl.BlockSpec((8, d), const)],
        out_specs=[pl.BlockSpec((tm, d), row), pl.BlockSpec((tm, d), row)],
        out_shape=[jax.ShapeDtypeStruct((t, d), F32), jax.ShapeDtypeStruct((t, d), BF16)],
        compiler_params=_params(1),
    )(mixm, px, mixr, w_out, x2d, ng, gt1, g2, sh2, sc2)


def _ffn_kernel(tiles_per_img, hx_ref, top_ref, bot_ref, wu_ref, wg_ref, wd_ref, cw_ref, cb_ref,
                x1_ref, gt2_ref, gf_ref, o_ref):
    i = pl.program_id(0)
    j = pl.program_id(1)
    tm = hx_ref.shape[0]
    ti = i % tiles_per_img

    @pl.when(j == 0)
    def _():
        o_ref[...] = jnp.zeros(o_ref.shape, F32)

    hx = hx_ref[...]
    top = jnp.where(ti > 0, top_ref[...], jnp.zeros_like(top_ref[...]))
    bot = jnp.where(ti < tiles_per_img - 1, bot_ref[...], jnp.zeros_like(bot_ref[...]))
    u = _mm(jnp.concatenate([top, hx, bot], axis=0), wu_ref[...])
    nr = u.shape[0]
    col = lax.broadcasted_iota(jnp.int32, u.shape, 0) & (GRID_W - 1)
    ul = jnp.where(col == 0, 0.0, pltpu.roll(u, 1, 0))
    ur = jnp.where(col == GRID_W - 1, 0.0, pltpu.roll(u, nr - 1, 0))
    cw = cw_ref[...]
    conv = cb_ref[...]
    for dy in range(3):
        rows = slice(dy * GRID_W, dy * GRID_W + tm)
        conv = (conv + ul[rows] * cw[3 * dy:3 * dy + 1, :] + u[rows] * cw[3 * dy + 1:3 * dy + 2, :]
                + ur[rows] * cw[3 * dy + 2:3 * dy + 3, :])
    gelu = 0.5 * conv * (1.0 + jnp.tanh(0.7978845608028654 * (conv + 0.044715 * conv * conv * conv)))
    act = (gelu * _mm(hx, wg_ref[...])).astype(BF16)
    o_ref[...] += _mm(act, wd_ref[...])

    @pl.when(j == pl.num_programs(1) - 1)
    def _():
        gate = gt2_ref[pl.ds(i // tiles_per_img, 1), :]
        nb = 256

        def finish_rows(k, carry):
            rows = pl.ds(pl.multiple_of(k * nb, nb), nb)
            x2 = x1_ref[rows, :] + gate * o_ref[rows, :]
            o_ref[rows, :] = x2 * lax.rsqrt(jnp.mean(x2 * x2, axis=-1, keepdims=True) + NORM_EPS) * gf_ref[...]
            return carry

        lax.fori_loop(0, tm // nb, finish_rows, 0)


def _conv_ffn(hx2, w_up, w_gate, w_down, cw, cb, x1, gt2, g_final, tm, tiles_per_img):
    t, d = hx2.shape
    f = w_up.shape[1]
    tf = FFN_TF
    rows_per_tile = tm // GRID_W
    n_rows = t // GRID_W
    return pl.pallas_call(
        functools.partial(_ffn_kernel, tiles_per_img),
        grid=(t // tm, f // tf),
        in_specs=[pl.BlockSpec((tm, d), lambda i, j: (i, 0)),
                  pl.BlockSpec((GRID_W, d), lambda i, j: (jnp.maximum(i * rows_per_tile - 1, 0), 0)),
                  pl.BlockSpec((GRID_W, d), lambda i, j: (jnp.minimum((i + 1) * rows_per_tile, n_rows - 1), 0)),
                  pl.BlockSpec((d, tf), lambda i, j: (0, j)),
                  pl.BlockSpec((d, tf), lambda i, j: (0, j)),
                  pl.BlockSpec((tf, d), lambda i, j: (j, 0)),
                  pl.BlockSpec((9, tf), lambda i, j: (0, j)),
                  pl.BlockSpec((1, tf), lambda i, j: (0, j)),
                  pl.BlockSpec((tm, d), lambda i, j: (i, 0), pipeline_mode=pl.Buffered(1)),
                  pl.BlockSpec((8, d), lambda i, j: (0, 0)),
                  pl.BlockSpec((1, d), lambda i, j: (0, 0))],
        out_specs=pl.BlockSpec((tm, d), lambda i, j: (i, 0), pipeline_mode=pl.Buffered(1)),
        out_shape=jax.ShapeDtypeStruct((t, d), F32),
        compiler_params=_params(2),
    )(hx2, hx2, hx2, w_up, w_gate, w_down, cw, cb, x1, gt2, g_final)


def _lora_pairs(up):
    _, rank, width = up.shape
    u = up.reshape(2, rank, width // LANES, LANES).transpose(2, 0, 1, 3)
    z = jnp.zeros_like(u[:, 0])
    top = jnp.concatenate([u[:, 0], z], axis=-1)
    bot = jnp.concatenate([z, u[:, 1]], axis=-1)
    return jnp.concatenate([top, bot], axis=1).astype(BF16)


def kernel(x, c, ctx, c_ctx, w_mod, b_mod, g_norm1, g_norm2, w_in, m_conv_w, m_conv_b, m_gate_b, m_norm_g, r_w0, r_w_up, r_a0, r_a_up, r_g_up, r_k_k, r_k_a, r_r_k, r_gn_w, r_gn_b, w_out, f_w_up, f_w_gate, f_conv_w, f_conv_b, f_w_down, g_final):
    batch, seq, d = x.shape
    ctx_len = ctx.shape[1]
    assert w_mod.shape[0] == 1, "single-layer block"
    assert batch + 1 <= 8 and seq % 512 == 0 and ctx_len % MLSTM_CHUNK == 0

    cv8 = jnp.zeros((8, d), F32).at[:batch].set(c).at[batch].set(c_ctx)
    mod = _modulation(cv8, w_mod[0], b_mod[0])
    sh1, sc1, gt1, sh2, sc2, gt2 = (mod[:, k * d:(k + 1) * d] for k in range(6))

    n_gate = 4 * M_HEADS
    w_p = _permute_w(jnp.swapaxes(w_in, 1, 2), SLAB_R * LANES, n_gate, N_SLABS * LANES)
    g1 = g_norm1[0].reshape(1, d)
    tm_x = INPROJ_TM if seq % INPROJ_TM == 0 else OUT_TM
    px = _inproj(x.reshape(batch * seq, d), g1, sh1, sc1, w_p, tm_x, seq // tm_x, 0)
    tm_c = batch * ctx_len
    pc = _inproj(ctx.reshape(batch * ctx_len, d), g1, sh1, sc1, w_p, tm_c, 1 << 30, batch)

    gate_row = jnp.zeros((1, LANES), F32).at[0, GATE_LANE0:GATE_LANE0 + n_gate].set(m_gate_b[0].reshape(-1))
    mixm = _mlstm(px, pc, batch, m_conv_w[0], m_conv_b[0].reshape(1, -1), gate_row)

    rw = r_k_k.shape[1]
    wg = jnp.zeros((2 * LANES, rw), F32).at[:r_g_up.shape[1]].set(r_g_up[0]).astype(BF16)
    mixr = _rwkv(px, pc, batch, _lora_pairs(r_w_up[0]), _lora_pairs(r_a_up[0]), wg,
                 r_w0[0], r_a0[0], r_k_k, r_k_a, r_r_k[0].reshape(1, rw), r_gn_w, r_gn_b)

    tm_o = OUT_TM
    x1, hx2 = _out_proj(mixm, px, mixr, w_out[0].astype(BF16), x.reshape(batch * seq, d), m_norm_g,
                        gt1, g_norm2[0].reshape(1, d), sh2, sc2, tm_o, seq // tm_o)

    tm_f = FFN_TM if seq % FFN_TM == 0 else OUT_TM
    out = _conv_ffn(hx2, f_w_up[0].astype(BF16), f_w_gate[0].astype(BF16), f_w_down[0].astype(BF16),
                    f_conv_w[0].reshape(9, -1), f_conv_b, x1, gt2, g_final.reshape(1, d),
                    tm_f, seq // tm_f)
    return out.reshape(batch, seq, d)
```

```python
import functools

import jax
import jax.numpy as jnp
from jax import lax
from jax.experimental import pallas as pl
from jax.experimental.pallas import tpu as pltpu

F32 = jnp.float32
BF16 = jnp.bfloat16

LANES = 128
GRID_W = 64
M_HEADS = 4
M_DQK = 128
M_DV = 256
R_N = 64
NORM_EPS = 1e-6
GN_EPS = 64e-5
MLSTM_CHUNK = 128
RWKV_CHUNK = 64
RWKV_GROUP = 16
RWKV_SCAN_GROUP = 8
RWKV_SCAN_PAIRS = 4
RWKV_PREP_WIDTH = 16
VMEM_LIMIT = 56 * 1024 * 1024
MOD_TN = 1024
PERMUTE_ROWS = 512
INPROJ_TM, INPROJ_TN = 1024, 512
OUT_TM = 512
FFN_TM, FFN_TF = 1024, 512

SLAB_Q, SLAB_K, SLAB_V, SLAB_O = 0, 4, 8, 16
SLAB_R, SLAB_KR, SLAB_VR = 24, 32, 40
SLAB_LW, SLAB_LA, SLAB_LG0, SLAB_LG1 = 48, 49, 50, 51
N_SLABS = 52
GATE_LANE0 = 32


def _mm(a, b):
    return jnp.dot(a, b, preferred_element_type=F32)


def _mm_nt(a, b):
    return lax.dot_general(a, b, (((1,), (1,)), ((), ())), preferred_element_type=F32)


def _mm_tn(a, b):
    return lax.dot_general(a, b, (((0,), (0,)), ((), ())), preferred_element_type=F32)


def _mm_split(x, ones, ones_first, pieces=3):
    dot = (lambda p: _mm(ones, p)) if ones_first else (lambda p: _mm(p, ones))
    piece = x.astype(BF16)
    total = dot(piece)
    for _ in range(pieces - 1):
        x = x - piece.astype(F32)
        piece = x.astype(BF16)
        total = total + dot(piece)
    return total


def _softplus(x):
    return jnp.maximum(x, 0.0) + jnp.log(1.0 + jnp.exp(-jnp.abs(x)))


def _log_sigmoid(x):
    return -_softplus(-x)


def _params(n_axes):
    return pltpu.CompilerParams(dimension_semantics=("arbitrary",) * n_axes,
                                vmem_limit_bytes=VMEM_LIMIT)


def _mod_kernel(cv_ref, w_ref, b_ref, o_ref):
    cv = cv_ref[...]
    s = (cv * jax.nn.sigmoid(cv)).astype(BF16)
    o_ref[...] = _mm(s, w_ref[...].astype(BF16)) + b_ref[...]


def _modulation(cv8, w_mod, b_mod):
    d, n = w_mod.shape
    tn = MOD_TN
    return pl.pallas_call(
        _mod_kernel,
        grid=(n // tn,),
        in_specs=[pl.BlockSpec((8, d), lambda j: (0, 0)),
                  pl.BlockSpec((d, tn), lambda j: (0, j)),
                  pl.BlockSpec((1, tn), lambda j: (0, j))],
        out_specs=pl.BlockSpec((8, tn), lambda j: (0, j)),
        out_shape=jax.ShapeDtypeStruct((8, n), F32),
        compiler_params=_params(1),
    )(cv8, w_mod, b_mod.reshape(1, n))


def _permute_w_kernel(n_main, n_gate, n_valid, cur_ref, nxt_ref, gate_ref, o_ref):
    j = pl.program_id(0)
    last = pl.num_programs(0) - 1
    tr = o_ref.shape[0]

    @pl.when(j < n_main)
    def _():
        o_ref[...] = cur_ref[0].astype(BF16)

    @pl.when(jnp.logical_and(j >= n_main, j < last))
    def _():
        o_ref[...] = jnp.concatenate([cur_ref[0, n_gate:tr, :], nxt_ref[0]], axis=0).astype(BF16)

    @pl.when(j == last)
    def _():
        pad = jnp.zeros((tr - n_valid, o_ref.shape[1]), F32)
        o_ref[...] = jnp.concatenate([cur_ref[0, n_gate:n_valid, :], gate_ref[0], pad], axis=0).astype(BF16)


def _permute_w(w_t, n_main, n_gate, n_out):
    _, n, d = w_t.shape
    tr = PERMUTE_ROWS
    n_blocks = n_out // tr
    assert n_main % tr == 0 and n_out % tr == 0 and pl.cdiv(n, tr) == n_blocks and tr % n_gate == 0
    per = tr // n_gate
    return pl.pallas_call(
        functools.partial(_permute_w_kernel, n_main // tr, n_gate, n - (n_blocks - 1) * tr),
        grid=(n_blocks,),
        in_specs=[pl.BlockSpec((1, tr, d), lambda j: (0, j, 0)),
                  pl.BlockSpec((1, n_gate, d), lambda j: (0, jnp.minimum(j + 1, n_blocks - 1) * per, 0)),
                  pl.BlockSpec((1, n_gate, d), lambda j: (0, n_main // n_gate, 0))],
        out_specs=pl.BlockSpec((tr, d), lambda j: (j, 0)),
        out_shape=jax.ShapeDtypeStruct((n_out, d), BF16),
        compiler_params=_params(1),
    )(w_t, w_t, w_t)


def _inproj_kernel(tiles_per_row, row0, x_ref, g_ref, sh_ref, sc_ref, w_ref, o_ref, hx_ref):
    i = pl.program_id(0)
    j = pl.program_id(1)

    @pl.when(j == 0)
    def _():
        r = row0 + i // tiles_per_row
        scale = g_ref[...] * (1.0 + sc_ref[pl.ds(r, 1), :])
        shift = sh_ref[pl.ds(r, 1), :]
        nb = 256

        def norm_rows(k, carry):
            rows = pl.ds(pl.multiple_of(k * nb, nb), nb)
            x = x_ref[rows, :]
            inv = lax.rsqrt(jnp.mean(x * x, axis=-1, keepdims=True) + NORM_EPS)
            hx_ref[rows, :] = (x * inv * scale + shift).astype(BF16)
            return carry

        lax.fori_loop(0, x_ref.shape[0] // nb, norm_rows, 0)

    acc = _mm_nt(hx_ref[...], w_ref[...])
    for s in range(acc.shape[1] // LANES):
        o_ref[s] = acc[:, s * LANES:(s + 1) * LANES]


def _inproj(x2d, g, sh, sc, w_p, tm, tiles_per_row, row0):
    t, d = x2d.shape
    n = w_p.shape[0]
    tn = INPROJ_TN
    return pl.pallas_call(
        functools.partial(_inproj_kernel, tiles_per_row, row0),
        grid=(t // tm, n // tn),
        in_specs=[pl.BlockSpec((tm, d), lambda i, j: (i, 0)),
                  pl.BlockSpec((1, d), lambda i, j: (0, 0)),
                  pl.BlockSpec((8, d), lambda i, j: (0, 0)),
                  pl.BlockSpec((8, d), lambda i, j: (0, 0)),
                  pl.BlockSpec((tn, d), lambda i, j: (j, 0))],
        out_specs=pl.BlockSpec((tn // LANES, tm, LANES), lambda i, j: (j, i, 0)),
        out_shape=jax.ShapeDtypeStruct((n // LANES, t, LANES), F32),
        scratch_shapes=[pltpu.VMEM((tm, d), BF16)],
        compiler_params=_params(2),
    )(x2d, g, sh, sc, w_p)


def _mlstm_kernel(n_ctx, n_lat,
                  qx_ref, kx_ref, vx_ref, gx_ref, qc_ref, kc_ref, vc_ref, gc_ref,
                  cwq_ref, cwk_ref, cbq_ref, cbk_ref, gb_ref,
                  o_ref,
                  qs_ref, ks_ref, kst_ref, gcol_ref, grow_ref, ct_ref, hf_ref, hb_ref):
    c = MLSTM_CHUNK
    head = pl.program_id(1)
    rid = lax.broadcasted_iota(jnp.int32, (c, LANES), 0)
    lane = lax.broadcasted_iota(jnp.int32, (c, LANES), 1)
    r2 = lax.broadcasted_iota(jnp.int32, (c, c), 0)
    c2 = lax.broadcasted_iota(jnp.int32, (c, c), 1)
    prefix = (c2 <= r2).astype(BF16)
    keep = (c2 <= r2, c2 >= r2)
    gbias = gb_ref[...]

    def conv_pass(q_ref, k_ref, g_ref, nchunk, base):
        nrows = nchunk * c

        def body(ci, carry):
            r0 = pl.multiple_of(ci * c, c)
            p0 = pl.multiple_of(jnp.maximum(r0 - 8, 0), 8)
            n0 = pl.multiple_of(jnp.minimum(r0 + c, nrows - 8), 8)
            gb = g_ref[0, pl.ds(r0, c), :] + gbias
            lf = _log_sigmoid(gb)
            pre = _mm_split(lf, prefix, True)
            suf = pre[c - 1:c, :] - pre + lf
            packed = jnp.zeros((c, LANES), F32)
            for slot, (src, col0) in enumerate(((pre, GATE_LANE0 + 2 * M_HEADS), (suf, GATE_LANE0 + 3 * M_HEADS),
                                                (gb, GATE_LANE0), (gb, GATE_LANE0 + M_HEADS))):
                col = jnp.sum(jnp.where(lane == col0 + head, src, 0.0), axis=1, keepdims=True)
                packed = jnp.where(lane == slot, col, packed)
            gcol_ref[base + ci] = packed
            grow_ref[base + ci] = packed.T[0:8, :]
            for src, w_ref, b_ref, scale, is_k in ((q_ref, cwq_ref, cbq_ref, M_DQK ** -0.5, False),
                                                   (k_ref, cwk_ref, cbk_ref, 1.0, True)):
                cur = src[0, pl.ds(r0, c), :]
                prev_row = jnp.where(ci > 0, src[0, pl.ds(p0, 8), :][7:8, :], 0.0)
                next_row = jnp.where(ci < nchunk - 1, src[0, pl.ds(n0, 8), :][0:1, :], 0.0)
                up = jnp.where(rid == 0, prev_row, pltpu.roll(cur, 1, 0))
                dn = jnp.where(rid == c - 1, next_row, pltpu.roll(cur, c - 1, 0))
                w = w_ref[...]
                y = (up * w[0:1, :] + cur * w[1:2, :] + dn * w[2:3, :] + b_ref[...]) * scale
                if is_k:
                    ks_ref[base + ci] = y.astype(BF16)
                    kst_ref[base + ci] = y.T.astype(BF16)
                else:
                    qs_ref[base + ci] = y.astype(BF16)
            return carry

        lax.fori_loop(0, nchunk, body, 0, unroll=min(4, nchunk))

    def step(d, ci, base, v_ref, n, m, want_h):
        r0 = pl.multiple_of(ci * c, c)
        q = qs_ref[base + ci]
        k = ks_ref[base + ci]
        kt = kst_ref[base + ci]
        v = jnp.concatenate([v_ref[0, pl.ds(r0, c), :], v_ref[1, pl.ds(r0, c), :]], axis=1).astype(BF16)
        gcol = gcol_ref[base + ci]
        grow = grow_ref[base + ci]
        rep = lambda x: jnp.broadcast_to(x, (c, LANES))
        wide = lambda x: jnp.concatenate([x, x], axis=1)
        b_col, i_col = rep(gcol[:, d:d + 1]), rep(gcol[:, 2 + d:3 + d])
        b_row, i_row = grow[d:d + 1, :], grow[2 + d:3 + d, :]
        b_last = b_col[0:1, :] if d else b_col[c - 1:c, :]
        ct = ct_ref[d]
        h = None
        if want_h:
            dmat = jnp.where(keep[d], b_col - b_row + i_row, -jnp.inf)
            m_intra = rep(jnp.max(dmat, axis=-1, keepdims=True))
            s = _mm(q, kt) * jnp.exp(dmat - m_intra)
            num_intra = _mm(s.astype(BF16), v)
            den_intra = rep(jnp.sum(s, axis=-1, keepdims=True))
            m_inter = b_col + m
            m_j = jnp.maximum(m_inter, m_intra)
            intra = jnp.exp(m_intra - m_j)
            inter = jnp.exp(m_inter - m_j)
            num = wide(intra) * num_intra + wide(inter) * _mm(q, ct.astype(BF16))
            qn = rep(jnp.sum(q.astype(F32) * n, axis=-1, keepdims=True))
            den = intra * den_intra + inter * qn
            h = num / wide(jnp.maximum(jnp.abs(den), jnp.exp(-m_j)))
        glog = b_last - b_col + i_col
        b_last = b_last[:, 0:1]
        m_new = jnp.maximum(b_last + m, jnp.max(glog, axis=0, keepdims=True)[:, 0:1])
        wk = jnp.exp(glog - m_new)
        wk_row = jnp.exp(b_last - b_row + i_row - m_new)
        decay = jnp.exp(b_last + m - m_new)
        ct_ref[d] = decay * ct + _mm((kt.astype(F32) * wk_row).astype(BF16), v)
        n_new = decay * n + jnp.sum(wk * k.astype(F32), axis=0, keepdims=True)
        return n_new, m_new, h

    def run(nchunk, base, v_ref, carry, want_h):
        def body(i, carry):
            nf, mf, nb, mb = carry
            ib = nchunk - 1 - i
            nf, mf, h_f = step(0, i, base, v_ref, nf, mf, want_h)
            nb, mb, h_b = step(1, ib, base, v_ref, nb, mb, want_h)
            if want_h:
                hf_ref[pl.ds(pl.multiple_of(i * c, c), c), :] = h_f
                hb_ref[pl.ds(pl.multiple_of(ib * c, c), c), :] = h_b
            return nf, mf, nb, mb

        return lax.fori_loop(0, nchunk, body, carry, unroll=2)

    conv_pass(qc_ref, kc_ref, gc_ref, n_ctx, 0)
    conv_pass(qx_ref, kx_ref, gx_ref, n_lat, n_ctx)
    ct_ref[...] = jnp.zeros(ct_ref.shape, F32)
    zn = jnp.zeros((1, M_DQK), F32)
    zm = jnp.zeros((1, 1), F32)
    carry = run(n_ctx, 0, vc_ref, (zn, zm, zn, zm), False)
    run(n_lat, n_ctx, vx_ref, carry, True)

    def fin(i, carry):
        rows = pl.ds(pl.multiple_of(i * c, c), c)
        o_ref[rows, :] = (hf_ref[rows, :] + hb_ref[rows, :]).astype(o_ref.dtype)
        return carry

    lax.fori_loop(0, n_lat, fin, 0)


def _mlstm(px, pc, batch, cw, cb, gate_row):
    lx = px.shape[1] // batch
    lc = pc.shape[1] // batch
    n_lat, n_ctx = lx // MLSTM_CHUNK, lc // MLSTM_CHUNK

    def slab(nrows, first, width=1):
        return pl.BlockSpec((width, nrows, LANES), lambda b, h: (first // width + h, b, 0))

    def fixed(nrows, idx):
        return pl.BlockSpec((1, nrows, LANES), lambda b, h: (idx, b, 0))

    in_specs = [slab(lx, SLAB_Q), slab(lx, SLAB_K), slab(lx, SLAB_V, 2), fixed(lx, SLAB_LG1),
                slab(lc, SLAB_Q), slab(lc, SLAB_K), slab(lc, SLAB_V, 2), fixed(lc, SLAB_LG1),
                pl.BlockSpec((3, LANES), lambda b, h: (0, h)),
                pl.BlockSpec((3, LANES), lambda b, h: (0, M_HEADS + h)),
                pl.BlockSpec((1, LANES), lambda b, h: (0, h)),
                pl.BlockSpec((1, LANES), lambda b, h: (0, M_HEADS + h)),
                pl.BlockSpec((1, LANES), lambda b, h: (0, 0))]
    nch = n_ctx + n_lat
    return pl.pallas_call(
        functools.partial(_mlstm_kernel, n_ctx, n_lat),
        grid=(batch, M_HEADS),
        in_specs=in_specs,
        out_specs=pl.BlockSpec((lx, M_DV), lambda b, h: (b, h)),
        out_shape=jax.ShapeDtypeStruct((batch * lx, M_HEADS * M_DV), BF16),
        scratch_shapes=[pltpu.VMEM((nch, MLSTM_CHUNK, LANES), BF16),
                        pltpu.VMEM((nch, MLSTM_CHUNK, LANES), BF16),
                        pltpu.VMEM((nch, LANES, MLSTM_CHUNK), BF16),
                        pltpu.VMEM((nch, MLSTM_CHUNK, LANES), F32),
                        pltpu.VMEM((nch, 8, LANES), F32),
                        pltpu.VMEM((2, M_DQK, M_DV), F32),
                        pltpu.VMEM((lx, M_DV), F32),
                        pltpu.VMEM((lx, M_DV), F32)],
        compiler_params=_params(2),
    )(px, px, px, px, pc, pc, pc, pc, cw, cw, cb, cb, gate_row)


def _rwkv_stack(x):
    head0 = lax.broadcasted_iota(jnp.int32, x.shape, 1) < R_N
    zero = jnp.zeros_like(x)
    return jnp.concatenate([jnp.where(head0, x, zero), jnp.where(head0, zero, x)], axis=0)


def _rwkv_fold(x):
    half = x.shape[0] // 2
    return x[0:half] + x[half:2 * half]


def _rwkv_prep_kernel(want_y, r_ref, k_ref, v_ref, lw_ref, la_ref,
                      ww_ref, wa_ref, w0_ref, a0_ref, kk_ref, ka_ref, *rest):
    extra_refs, out_refs = (rest[:4], rest[4:]) if want_y else ((), rest)
    c = RWKV_CHUNK
    c2 = 2 * c
    stack = _rwkv_stack
    rr = lax.broadcasted_iota(jnp.int32, (c2, c2), 0)
    cc = lax.broadcasted_iota(jnp.int32, (c2, c2), 1)
    group_ones = ((rr < c) == (cc < c)).astype(BF16)
    tt = lax.broadcasted_iota(jnp.int32, (c, c2), 0)
    lane = lax.broadcasted_iota(jnp.int32, (c, c2), 1)
    ss = lane & (c - 1)
    head0 = lane < c
    strict = (ss < tt, ss > tt)
    incl = (ss <= tt, ss >= tt)
    eye = (ss == tt).astype(F32)
    tr = lax.broadcasted_iota(jnp.int32, (c, c), 0)
    tc = lax.broadcasted_iota(jnp.int32, (c, c), 1)
    tri = ((tc <= tr).astype(BF16), (tc >= tr).astype(BF16))
    ww = ww_ref[0]
    wa = wa_ref[0]
    k_k = kk_ref[...]
    k_a = ka_ref[...]
    m_ref, n_ref, dec_ref = out_refs[0], out_refs[1], out_refs[2]

    def prep_chunks(js):
        nj = len(js)
        rows = [pl.ds(pl.multiple_of(j * c, c), c) for j in js]
        ch = [(d, i) for d in (0, 1) for i in range(nj)]
        half = lambda d: slice(d * LANES, (d + 1) * LANES)
        r = [r_ref[0, rw, :] for rw in rows]
        k = [k_ref[0, rw, :] for rw in rows]
        v_s = [stack(v_ref[0, rw, :]).astype(BF16) for rw in rows]
        lo_w = [_mm(jnp.tanh(lw_ref[0, rw, :]).astype(BF16), ww) for rw in rows]
        lo_a = [_mm(la_ref[0, rw, :].astype(BF16), wa) for rw in rows]
        kk = [x * k_k for x in k]
        kk = [x * lax.rsqrt(_mm_split(x * x, group_ones, False, pieces=2) + 1e-12) for x in kk]
        logw = [-jnp.exp(-_softplus(-(w0_ref[d:d + 1, :] + lo_w[i][:, half(d)])) - 0.5) for d, i in ch]
        a = [jax.nn.sigmoid(a0_ref[d:d + 1, :] + lo_a[i][:, half(d)]) for d, i in ch]
        pin = [_mm_split(x, tri[d], True, pieces=2) for (d, i), x in zip(ch, logw)]
        ptot = [x[0:1, :] if d else x[c - 1:c, :] for (d, i), x in zip(ch, pin)]
        kd = [k[i] * (1.0 + (a_ - 1.0) * k_a) for (d, i), a_ in zip(ch, a)]
        kka = [kk[i] * a_ for (d, i), a_ in zip(ch, a)]
        e_inv = [jnp.exp(-x) for x in pin]
        e_end = [jnp.exp(pt - x) for pt, x in zip(ptot, pin)]
        r_t = [r[i] * jnp.exp(x) for (d, i), x in zip(ch, pin)]
        a_t = [-kk[i] * jnp.exp(x - lw_) for (d, i), x, lw_ in zip(ch, pin, logw)]
        ar_t = [jnp.concatenate([x, y], axis=0).astype(BF16) for x, y in zip(a_t, r_t)]
        bk_t = [jnp.concatenate([stack(x * e), stack(y * e)], axis=0).astype(BF16)
                for x, y, e in zip(kka, kd, e_inv)]
        bk_end = [jnp.concatenate([x * e, y * e], axis=0).astype(BF16) for x, y, e in zip(kka, kd, e_end)]
        aa = [_mm_nt(x, y) for x, y in zip(ar_t, bk_t)]
        aab = [jnp.where(strict[d], x[0:c, 0:c2], 0.0) for (d, i), x in zip(ch, aa)]
        aak = [jnp.where(strict[d], x[0:c, c2:2 * c2], 0.0).astype(BF16) for (d, i), x in zip(ch, aa)]
        akv = [_mm(x, v_s[i]).astype(BF16) for (d, i), x in zip(ch, aak)]
        xs = [eye + x for x in aab]
        pb = [x.astype(BF16) for x in aab]
        ps = [_mm(x, stack(x)) for x in pb]
        for _ in range(c.bit_length() - 3):
            pb = [x.astype(BF16) for x in ps]
            both = [_mm(jnp.concatenate([x.astype(BF16), p], axis=0), stack(p)) for x, p in zip(xs, pb)]
            xs = [x + y[0:c] for x, y in zip(xs, both)]
            ps = [y[c:c2] for y in both]
        xs = [(x + _mm(x.astype(BF16), stack(p.astype(BF16)))).astype(BF16) for x, p in zip(xs, ps)]
        wu = [_mm(x, jnp.concatenate([stack(y[0:c]), stack(z)], axis=1)).astype(BF16)
              for x, y, z in zip(xs, ar_t, akv)]
        m_mat = [_mm_tn(x[:, 0:c2], y[0:c]) for x, y in zip(wu, bk_end)]
        n_mat = [_mm_tn(jnp.concatenate([x[:, c2:2 * c2], v_ref[0, rows[i], :].astype(BF16)], axis=0), y)
                 for (d, i), x, y in zip(ch, wu, bk_end)]
        for (d, i), mm_, nn_, pt in zip(ch, m_mat, n_mat, ptot):
            slot = (0, 0, d, js[i])
            m_ref[slot] = jnp.where(head0, mm_[0:c], mm_[c:c2]).astype(BF16)
            n_ref[slot] = jnp.where(head0, nn_[0:c], nn_[c:c2])
            dec_ref[slot] = jnp.broadcast_to(jnp.exp(pt), (8, LANES))
        if want_y:
            ark = [jnp.concatenate([jnp.where(incl[d], x[c:c2, 0:c2], 0.0),
                                    jnp.where(incl[d], x[c:c2, c2:2 * c2], 0.0)], axis=1).astype(BF16)
                   for (d, i), x in zip(ch, aa)]
            qy = [_mm(x, jnp.concatenate([
                      jnp.concatenate([stack(w_[:, 0:c2]), stack(w_[:, c2:2 * c2])], axis=1),
                      jnp.concatenate([jnp.zeros((c2, c2), BF16), v_s[i]], axis=1)], axis=0))
                  for (d, i), x, w_ in zip(ch, ark, wu)]
            for (d, i), rt, x in zip(ch, r_t, qy):
                slot = (0, 0, d, js[i])
                out_refs[3][slot] = (rt + x[:, 0:c2]).astype(BF16)
                out_refs[4][slot] = x[:, c2:2 * c2]
            lg0_ref, lg1_ref, wg_ref, rk_ref = extra_refs
            ksum = [k[i] * (2.0 + (a[i] + a[nj + i] - 2.0) * k_a) for i in range(nj)]
            bonus = [_mm_split(r[i] * ksum[i] * rk_ref[...], group_ones, False) * v_ref[0, rows[i], :]
                     for i in range(nj)]
            gate = [_mm(jax.nn.sigmoid(jnp.concatenate([lg0_ref[0, rw, :], lg1_ref[0, rw, :]], axis=1)
                                       ).astype(BF16), wg_ref[...]) for rw in rows]
            for rw, gt, bn in zip(rows, gate, bonus):
                out_refs[5][rw, :] = gt.astype(BF16)
                out_refs[6][rw, :] = (bn * gt).astype(BF16)

    n_chunks = r_ref.shape[1] // c
    width = min(RWKV_PREP_WIDTH, n_chunks)
    assert n_chunks % width == 0

    def body(t, carry):
        prep_chunks([t * width + u for u in range(width)])
        return carry

    lax.fori_loop(0, n_chunks // width, body, 0)


def _rwkv_scan_kernel(mc_ref, nc_ref, dc_ref, mf_ref, nf_ref, df_ref, qf_ref, ylf_ref,
                      mb_ref, nb_ref, db_ref, qb_ref, ylb_ref,
                      gate_ref, bg_ref, gnw_ref, gnb_ref,
                      o_ref, z_ref, yf_ref, yb_ref):
    c = RWKV_CHUNK
    g = pl.program_id(2)
    n_groups = pl.num_programs(2)
    group = mf_ref.shape[3]
    n_ctx = mc_ref.shape[3]
    pairs = range(RWKV_SCAN_PAIRS)

    def advance(z, m_c, n_c, dec):
        return z * dec[0:1, :] + _mm(z.astype(BF16), _rwkv_stack(m_c)) + _rwkv_stack(n_c)

    @pl.when(g == 0)
    def _():
        z_ref[...] = jnp.zeros(z_ref.shape, F32)

        def ctx_body(i, carry):
            for pp in pairs:
                for d, ii in ((0, i), (1, n_ctx - 1 - i)):
                    zi = 2 * pp + d
                    z_ref[zi] = advance(z_ref[zi], mc_ref[0, pp, d, ii], nc_ref[0, pp, d, ii],
                                        dc_ref[0, pp, d, ii])
            return carry

        lax.fori_loop(0, n_ctx, ctx_body, 0)

    def lat_body(j, carry):
        chains = [(pp,) + t for pp in pairs for t in (
            (0, j, g * group, mf_ref, nf_ref, df_ref, qf_ref, ylf_ref, yf_ref),
            (1, group - 1 - j, (n_groups - 1 - g) * group, mb_ref, nb_ref, db_ref, qb_ref, ylb_ref, yb_ref))]
        zs = [z_ref[2 * pp + d] for pp, d, *_ in chains]
        zb = [z.astype(BF16) for z in zs]
        ys = [_mm_nt(_rwkv_stack(q_ref[0, pp, 0, jj]), z)
              for (pp, d, jj, first, m_ref, n_ref, d_ref, q_ref, yl_ref, y_ref), z in zip(chains, zb)]
        zn = [z * d_ref[0, pp, 0, jj][0:1, :] + _mm(zh, _rwkv_stack(m_ref[0, pp, 0, jj]))
              + _rwkv_stack(n_ref[0, pp, 0, jj])
              for (pp, d, jj, first, m_ref, n_ref, d_ref, q_ref, yl_ref, y_ref), z, zh in zip(chains, zs, zb)]
        for (pp, d, jj, first, m_ref, n_ref, d_ref, q_ref, yl_ref, y_ref), y, z in zip(chains, ys, zn):
            z_ref[2 * pp + d] = z
            y_ref[pl.ds(pl.multiple_of((first + jj) * c, c), c), pp * LANES:(pp + 1) * LANES] = (
                _rwkv_fold(y) + yl_ref[0, pp, 0, jj])
        return carry

    lax.fori_loop(0, group, lat_body, 0)

    @pl.when(g == n_groups - 1)
    def _():
        fr = 4 * c
        r2 = lax.broadcasted_iota(jnp.int32, (LANES, LANES), 0)
        c2 = lax.broadcasted_iota(jnp.int32, (LANES, LANES), 1)
        group_ones = ((r2 < R_N) == (c2 < R_N)).astype(BF16)
        inv_n = 1.0 / R_N

        def fin(i, carry):
            rows = pl.ds(pl.multiple_of(i * fr, fr), fr)
            for pp in pairs:
                cols = slice(pp * LANES, (pp + 1) * LANES)
                y = yf_ref[rows, cols] + yb_ref[rows, cols]
                sums = _mm_split(jnp.concatenate([y, y * y], axis=0), group_ones, False, pieces=2)
                mu = sums[0:fr] * inv_n
                var = sums[fr:2 * fr] * inv_n - mu * mu
                yn = (y - mu) * lax.rsqrt(var + GN_EPS) * gnw_ref[:, cols] + gnb_ref[:, cols]
                o_ref[rows, cols] = (yn * gate_ref[rows, cols].astype(F32)
                                     + bg_ref[rows, cols].astype(F32)).astype(o_ref.dtype)
            return carry

        lax.fori_loop(0, o_ref.shape[0] // fr, fin, 0)


def _rwkv_prep(p_all, batch, group, want_y, ww, wa, w0, a0, k_k, k_a, wg=None, r_k=None):
    c = RWKV_CHUNK
    length = p_all.shape[1] // batch
    n_chunks = length // c
    n_groups = n_chunks // group
    n_pairs = ww.shape[0]
    rows = group * c

    def slab(first):
        return pl.BlockSpec((1, rows, LANES), lambda b, p, g: (first + p, b * n_groups + g, 0))

    def fixed(idx):
        return pl.BlockSpec((1, rows, LANES), lambda b, p, g: (idx, b * n_groups + g, 0))

    def vec(nrows):
        return pl.BlockSpec((nrows, LANES), lambda b, p, g: (0, p))

    lora = pl.BlockSpec((1, LANES, 2 * LANES), lambda b, p, g: (p, 0, 0))
    outs = [(c, BF16), (c, F32), (8, F32)] + ([(c, BF16), (c, F32)] if want_y else [])
    in_specs = [slab(SLAB_R), slab(SLAB_KR), slab(SLAB_VR), fixed(SLAB_LW), fixed(SLAB_LA),
                lora, lora, vec(2), vec(2), vec(1), vec(1)]
    operands = [p_all, p_all, p_all, p_all, p_all, ww, wa, w0, a0, k_k, k_a]
    out_specs = [pl.BlockSpec((1, 1, 2, group, nr, LANES), lambda b, p, g: (b, p, 0, g, 0, 0)) for nr, _ in outs]
    out_shape = [jax.ShapeDtypeStruct((batch, n_pairs, 2, n_chunks, nr, LANES), dt) for nr, dt in outs]
    if want_y:
        in_specs += [fixed(SLAB_LG0), fixed(SLAB_LG1),
                     pl.BlockSpec((2 * LANES, LANES), lambda b, p, g: (0, p)), vec(1)]
        operands += [p_all, p_all, wg, r_k]
        out_specs += [pl.BlockSpec((rows, LANES), lambda b, p, g: (b * n_groups + g, p))] * 2
        out_shape += [jax.ShapeDtypeStruct((batch * length, n_pairs * LANES), BF16)] * 2
    return pl.pallas_call(
        functools.partial(_rwkv_prep_kernel, want_y),
        grid=(batch, n_pairs, n_groups),
        in_specs=in_specs,
        out_specs=out_specs,
        out_shape=out_shape,
        compiler_params=_params(3),
    )(*operands)


def _rwkv_scan(ops_c, ops_x, gate, bonus_gated, batch, group, gn_w, gn_b):
    c = RWKV_CHUNK
    sp = RWKV_SCAN_PAIRS
    lx = gate.shape[0] // batch
    n_pairs, n_ctx = ops_c[0].shape[1], ops_c[0].shape[3]
    n_groups = ops_x[0].shape[3] // group
    assert n_pairs % sp == 0

    def ctx_block(a):
        return pl.BlockSpec((1, sp, 2, n_ctx, a.shape[4], LANES), lambda b, p, g: (b, p, 0, 0, 0, 0))

    def fwd_block(a):
        return pl.BlockSpec((1, sp, 1, group, a.shape[4], LANES), lambda b, p, g: (b, p, 0, g, 0, 0))

    def bwd_block(a):
        return pl.BlockSpec((1, sp, 1, group, a.shape[4], LANES),
                            lambda b, p, g: (b, p, 1, n_groups - 1 - g, 0, 0))

    tokens = pl.BlockSpec((lx, sp * LANES), lambda b, p, g: (b, p), pipeline_mode=pl.Buffered(1))
    vec = pl.BlockSpec((1, sp * LANES), lambda b, p, g: (0, p))
    return pl.pallas_call(
        _rwkv_scan_kernel,
        grid=(batch, n_pairs // sp, n_groups),
        in_specs=[ctx_block(a) for a in ops_c] + [fwd_block(a) for a in ops_x] + [bwd_block(a) for a in ops_x]
                 + [tokens, tokens, vec, vec],
        out_specs=tokens,
        out_shape=jax.ShapeDtypeStruct((batch * lx, n_pairs * LANES), BF16),
        scratch_shapes=[pltpu.VMEM((2 * sp, 2 * c, LANES), F32),
                        pltpu.VMEM((lx, sp * LANES), F32), pltpu.VMEM((lx, sp * LANES), F32)],
        compiler_params=_params(3),
    )(*ops_c, *ops_x, *ops_x, gate, bonus_gated, gn_w, gn_b)


def _rwkv(px, pc, batch, ww, wa, wg, w0, a0, k_k, k_a, r_k, gn_w, gn_b):
    c = RWKV_CHUNK
    n_ctx = pc.shape[1] // batch // c
    n_lat = px.shape[1] // batch // c
    group = min(RWKV_GROUP, n_lat)
    assert n_lat % group == 0
    ops_c = _rwkv_prep(pc, batch, n_ctx, False, ww, wa, w0, a0, k_k, k_a)
    *ops_x, gate, bonus_gated = _rwkv_prep(px, batch, group, True, ww, wa, w0, a0, k_k, k_a, wg, r_k)
    scan_group = min(RWKV_SCAN_GROUP, n_lat)
    assert n_lat % scan_group == 0
    return _rwkv_scan(ops_c, ops_x, gate, bonus_gated, batch, scan_group, gn_w, gn_b)


def _out_kernel(tiles_per_batch, mixm_ref, om_ref, mixr_ref, w_ref, x_ref, ng_ref,
                gt1_ref, g2_ref, sh2_ref, sc2_ref, x1_ref, hx2_ref):
    b = pl.program_id(0) // tiles_per_batch
    hm = mixm_ref[...].astype(F32)
    ng = ng_ref[...]
    parts = []
    for h in range(M_HEADS):
        cols = slice(h * M_DV, (h + 1) * M_DV)
        seg = hm[:, cols]
        seg = seg * lax.rsqrt(jnp.mean(seg * seg, axis=-1, keepdims=True) + NORM_EPS)
        og = jnp.concatenate([om_ref[2 * h], om_ref[2 * h + 1]], axis=1)
        parts.append((seg * ng[:, cols] * jax.nn.sigmoid(og)).astype(BF16))
    lhs = jnp.concatenate(parts + [mixr_ref[...]], axis=1)
    x1 = x_ref[...] + gt1_ref[pl.ds(b, 1), :] * _mm(lhs, w_ref[...])
    x1_ref[...] = x1
    y = x1 * lax.rsqrt(jnp.mean(x1 * x1, axis=-1, keepdims=True) + NORM_EPS) * g2_ref[...]
    hx2_ref[...] = (y * (1.0 + sc2_ref[pl.ds(b, 1), :]) + sh2_ref[pl.ds(b, 1), :]).astype(BF16)


def _out_proj(mixm, px, mixr, w_out, x2d, ng, gt1, g2, sh2, sc2, tm, tiles_per_batch):
    t, d = x2d.shape
    dm = mixm.shape[1]
    row = lambda i: (i, 0)
    const = lambda i: (0, 0)
    return pl.pallas_call(
        functools.partial(_out_kernel, tiles_per_batch),
        grid=(t // tm,),
        in_specs=[pl.BlockSpec((tm, dm), row),
                  pl.BlockSpec((8, tm, LANES), lambda i: (SLAB_O // 8, i, 0)),
                  pl.BlockSpec((tm, dm), row),
                  pl.BlockSpec((d, d), const),
                  pl.BlockSpec((tm, d), row),
                  pl.BlockSpec((1, dm), const),
                  pl.BlockSpec((8, d), const),
                  pl.BlockSpec((1, d), const),
                  pl.BlockSpec((8, d), const),
                  pl.BlockSpec((8, d), const)],
        out_specs=[pl.BlockSpec((tm, d), row), pl.BlockSpec((tm, d), row)],
        out_shape=[jax.ShapeDtypeStruct((t, d), F32), jax.ShapeDtypeStruct((t, d), BF16)],
        compiler_params=_params(1),
    )(mixm, px, mixr, w_out, x2d, ng, gt1, g2, sh2, sc2)


def _ffn_kernel(tiles_per_img, hx_ref, top_ref, bot_ref, wu_ref, wg_ref, wd_ref, cw_ref, cb_ref,
                x1_ref, gt2_ref, gf_ref, o_ref):
    i = pl.program_id(0)
    j = pl.program_id(1)
    tm = hx_ref.shape[0]
    ti = i % tiles_per_img

    @pl.when(j == 0)
    def _():
        o_ref[...] = jnp.zeros(o_ref.shape, F32)

    hx = hx_ref[...]
    top = jnp.where(ti > 0, top_ref[...], jnp.zeros_like(top_ref[...]))
    bot = jnp.where(ti < tiles_per_img - 1, bot_ref[...], jnp.zeros_like(bot_ref[...]))
    u = _mm(jnp.concatenate([top, hx, bot], axis=0), wu_ref[...])
    nr = u.shape[0]
    col = lax.broadcasted_iota(jnp.int32, u.shape, 0) & (GRID_W - 1)
    ul = jnp.where(col == 0, 0.0, pltpu.roll(u, 1, 0))
    ur = jnp.where(col == GRID_W - 1, 0.0, pltpu.roll(u, nr - 1, 0))
    cw = cw_ref[...]
    conv = cb_ref[...]
    for dy in range(3):
        rows = slice(dy * GRID_W, dy * GRID_W + tm)
        conv = (conv + ul[rows] * cw[3 * dy:3 * dy + 1, :] + u[rows] * cw[3 * dy + 1:3 * dy + 2, :]
                + ur[rows] * cw[3 * dy + 2:3 * dy + 3, :])
    gelu = 0.5 * conv * (1.0 + jnp.tanh(0.7978845608028654 * (conv + 0.044715 * conv * conv * conv)))
    act = (gelu * _mm(hx, wg_ref[...])).astype(BF16)
    o_ref[...] += _mm(act, wd_ref[...])

    @pl.when(j == pl.num_programs(1) - 1)
    def _():
        gate = gt2_ref[pl.ds(i // tiles_per_img, 1), :]
        nb = 256

        def finish_rows(k, carry):
            rows = pl.ds(pl.multiple_of(k * nb, nb), nb)
            x2 = x1_ref[rows, :] + gate * o_ref[rows, :]
            o_ref[rows, :] = x2 * lax.rsqrt(jnp.mean(x2 * x2, axis=-1, keepdims=True) + NORM_EPS) * gf_ref[...]
            return carry

        lax.fori_loop(0, tm // nb, finish_rows, 0)


def _conv_ffn(hx2, w_up, w_gate, w_down, cw, cb, x1, gt2, g_final, tm, tiles_per_img):
    t, d = hx2.shape
    f = w_up.shape[1]
    tf = FFN_TF
    rows_per_tile = tm // GRID_W
    n_rows = t // GRID_W
    return pl.pallas_call(
        functools.partial(_ffn_kernel, tiles_per_img),
        grid=(t // tm, f // tf),
        in_specs=[pl.BlockSpec((tm, d), lambda i, j: (i, 0)),
                  pl.BlockSpec((GRID_W, d), lambda i, j: (jnp.maximum(i * rows_per_tile - 1, 0), 0)),
                  pl.BlockSpec((GRID_W, d), lambda i, j: (jnp.minimum((i + 1) * rows_per_tile, n_rows - 1), 0)),
                  pl.BlockSpec((d, tf), lambda i, j: (0, j)),
                  pl.BlockSpec((d, tf), lambda i, j: (0, j)),
                  pl.BlockSpec((tf, d), lambda i, j: (j, 0)),
                  pl.BlockSpec((9, tf), lambda i, j: (0, j)),
                  pl.BlockSpec((1, tf), lambda i, j: (0, j)),
                  pl.BlockSpec((tm, d), lambda i, j: (i, 0)),
                  pl.BlockSpec((8, d), lambda i, j: (0, 0)),
                  pl.BlockSpec((1, d), lambda i, j: (0, 0))],
        out_specs=pl.BlockSpec((tm, d), lambda i, j: (i, 0), pipeline_mode=pl.Buffered(1)),
        out_shape=jax.ShapeDtypeStruct((t, d), F32),
        compiler_params=_params(2),
    )(hx2, hx2, hx2, w_up, w_gate, w_down, cw, cb, x1, gt2, g_final)


def _lora_pairs(up):
    _, rank, width = up.shape
    u = up.reshape(2, rank, width // LANES, LANES).transpose(2, 0, 1, 3)
    z = jnp.zeros_like(u[:, 0])
    top = jnp.concatenate([u[:, 0], z], axis=-1)
    bot = jnp.concatenate([z, u[:, 1]], axis=-1)
    return jnp.concatenate([top, bot], axis=1).astype(BF16)


def kernel(x, c, ctx, c_ctx, w_mod, b_mod, g_norm1, g_norm2, w_in, m_conv_w, m_conv_b, m_gate_b, m_norm_g, r_w0, r_w_up, r_a0, r_a_up, r_g_up, r_k_k, r_k_a, r_r_k, r_gn_w, r_gn_b, w_out, f_w_up, f_w_gate, f_conv_w, f_conv_b, f_w_down, g_final):
    batch, seq, d = x.shape
    ctx_len = ctx.shape[1]
    assert w_mod.shape[0] == 1, "single-layer block"
    assert batch + 1 <= 8 and seq % 512 == 0 and ctx_len % MLSTM_CHUNK == 0

    cv8 = jnp.zeros((8, d), F32).at[:batch].set(c).at[batch].set(c_ctx)
    mod = _modulation(cv8, w_mod[0], b_mod[0])
    sh1, sc1, gt1, sh2, sc2, gt2 = (mod[:, k * d:(k + 1) * d] for k in range(6))

    n_gate = 4 * M_HEADS
    w_p = _permute_w(jnp.swapaxes(w_in, 1, 2), SLAB_R * LANES, n_gate, N_SLABS * LANES)
    g1 = g_norm1[0].reshape(1, d)
    tm_x = INPROJ_TM if seq % INPROJ_TM == 0 else OUT_TM
    px = _inproj(x.reshape(batch * seq, d), g1, sh1, sc1, w_p, tm_x, seq // tm_x, 0)
    tm_c = batch * ctx_len
    pc = _inproj(ctx.reshape(batch * ctx_len, d), g1, sh1, sc1, w_p, tm_c, 1 << 30, batch)

    gate_row = jnp.zeros((1, LANES), F32).at[0, GATE_LANE0:GATE_LANE0 + n_gate].set(m_gate_b[0].reshape(-1))
    mixm = _mlstm(px, pc, batch, m_conv_w[0], m_conv_b[0].reshape(1, -1), gate_row)

    rw = r_k_k.shape[1]
    wg = jnp.zeros((2 * LANES, rw), F32).at[:r_g_up.shape[1]].set(r_g_up[0]).astype(BF16)
    mixr = _rwkv(px, pc, batch, _lora_pairs(r_w_up[0]), _lora_pairs(r_a_up[0]), wg,
                 r_w0[0], r_a0[0], r_k_k, r_k_a, r_r_k[0].reshape(1, rw), r_gn_w, r_gn_b)

    tm_o = OUT_TM
    x1, hx2 = _out_proj(mixm, px, mixr, w_out[0].astype(BF16), x.reshape(batch * seq, d), m_norm_g,
                        gt1, g_norm2[0].reshape(1, d), sh2, sc2, tm_o, seq // tm_o)

    tm_f = FFN_TM if seq % FFN_TM == 0 else OUT_TM
    out = _conv_ffn(hx2, f_w_up[0].astype(BF16), f_w_gate[0].astype(BF16), f_w_down[0].astype(BF16),
                    f_conv_w[0].reshape(9, -1), f_conv_b, x1, gt2, g_final.reshape(1, d),
                    tm_f, seq // tm_f)
    return out.reshape(batch, seq, d)
```

```python
import functools

import jax
import jax.numpy as jnp
from jax import lax
from jax.experimental import pallas as pl
from jax.experimental.pallas import tpu as pltpu

F32 = jnp.float32
BF16 = jnp.bfloat16

LANES = 128
GRID_W = 64
M_HEADS = 4
M_DQK = 128
M_DV = 256
R_N = 64
NORM_EPS = 1e-6
GN_EPS = 64e-5
MLSTM_CHUNK = 128
RWKV_CHUNK = 64
RWKV_GROUP = 16
RWKV_SCAN_GROUP = 8
RWKV_SCAN_PAIRS = 4
RWKV_PREP_WIDTH = 16
VMEM_LIMIT = 56 * 1024 * 1024
MOD_TN = 1024
PERMUTE_ROWS = 512
INPROJ_TM, INPROJ_TN = 1024, 512
OUT_TM = 512
FFN_TM, FFN_TF = 1024, 512

SLAB_Q, SLAB_K, SLAB_V, SLAB_O = 0, 4, 8, 16
SLAB_R, SLAB_KR, SLAB_VR = 24, 32, 40
SLAB_LW, SLAB_LA, SLAB_LG0, SLAB_LG1 = 48, 49, 50, 51
N_SLABS = 52
GATE_LANE0 = 32


def _mm(a, b):
    return jnp.dot(a, b, preferred_element_type=F32)


def _mm_nt(a, b):
    return lax.dot_general(a, b, (((1,), (1,)), ((), ())), preferred_element_type=F32)


def _mm_tn(a, b):
    return lax.dot_general(a, b, (((0,), (0,)), ((), ())), preferred_element_type=F32)


def _mm_split(x, ones, ones_first, pieces=3):
    dot = (lambda p: _mm(ones, p)) if ones_first else (lambda p: _mm(p, ones))
    piece = x.astype(BF16)
    total = dot(piece)
    for _ in range(pieces - 1):
        x = x - piece.astype(F32)
        piece = x.astype(BF16)
        total = total + dot(piece)
    return total


def _softplus(x):
    return jnp.maximum(x, 0.0) + jnp.log(1.0 + jnp.exp(-jnp.abs(x)))


def _log_sigmoid(x):
    return -_softplus(-x)


def _params(n_axes):
    return pltpu.CompilerParams(dimension_semantics=("arbitrary",) * n_axes,
                                vmem_limit_bytes=VMEM_LIMIT)


def _mod_kernel(cv_ref, w_ref, b_ref, o_ref):
    cv = cv_ref[...]
    s = (cv * jax.nn.sigmoid(cv)).astype(BF16)
    o_ref[...] = _mm(s, w_ref[...].astype(BF16)) + b_ref[...]


def _modulation(cv8, w_mod, b_mod):
    d, n = w_mod.shape
    tn = MOD_TN
    return pl.pallas_call(
        _mod_kernel,
        grid=(n // tn,),
        in_specs=[pl.BlockSpec((8, d), lambda j: (0, 0)),
                  pl.BlockSpec((d, tn), lambda j: (0, j)),
                  pl.BlockSpec((1, tn), lambda j: (0, j))],
        out_specs=pl.BlockSpec((8, tn), lambda j: (0, j)),
        out_shape=jax.ShapeDtypeStruct((8, n), F32),
        compiler_params=_params(1),
    )(cv8, w_mod, b_mod.reshape(1, n))


def _permute_w_kernel(n_main, n_gate, n_valid, cur_ref, nxt_ref, gate_ref, o_ref):
    j = pl.program_id(0)
    last = pl.num_programs(0) - 1
    tr = o_ref.shape[0]

    @pl.when(j < n_main)
    def _():
        o_ref[...] = cur_ref[0].astype(BF16)

    @pl.when(jnp.logical_and(j >= n_main, j < last))
    def _():
        o_ref[...] = jnp.concatenate([cur_ref[0, n_gate:tr, :], nxt_ref[0]], axis=0).astype(BF16)

    @pl.when(j == last)
    def _():
        pad = jnp.zeros((tr - n_valid, o_ref.shape[1]), F32)
        o_ref[...] = jnp.concatenate([cur_ref[0, n_gate:n_valid, :], gate_ref[0], pad], axis=0).astype(BF16)


def _permute_w(w_t, n_main, n_gate, n_out):
    _, n, d = w_t.shape
    tr = PERMUTE_ROWS
    n_blocks = n_out // tr
    assert n_main % tr == 0 and n_out % tr == 0 and pl.cdiv(n, tr) == n_blocks and tr % n_gate == 0
    per = tr // n_gate
    return pl.pallas_call(
        functools.partial(_permute_w_kernel, n_main // tr, n_gate, n - (n_blocks - 1) * tr),
        grid=(n_blocks,),
        in_specs=[pl.BlockSpec((1, tr, d), lambda j: (0, j, 0)),
                  pl.BlockSpec((1, n_gate, d), lambda j: (0, jnp.minimum(j + 1, n_blocks - 1) * per, 0)),
                  pl.BlockSpec((1, n_gate, d), lambda j: (0, n_main // n_gate, 0))],
        out_specs=pl.BlockSpec((tr, d), lambda j: (j, 0)),
        out_shape=jax.ShapeDtypeStruct((n_out, d), BF16),
        compiler_params=_params(1),
    )(w_t, w_t, w_t)


def _inproj_kernel(tiles_per_row, row0, x_ref, g_ref, sh_ref, sc_ref, w_ref, o_ref, hx_ref):
    i = pl.program_id(0)
    j = pl.program_id(1)

    @pl.when(j == 0)
    def _():
        r = row0 + i // tiles_per_row
        scale = g_ref[...] * (1.0 + sc_ref[pl.ds(r, 1), :])
        shift = sh_ref[pl.ds(r, 1), :]
        nb = 256

        def norm_rows(k, carry):
            rows = pl.ds(pl.multiple_of(k * nb, nb), nb)
            x = x_ref[rows, :]
            inv = lax.rsqrt(jnp.mean(x * x, axis=-1, keepdims=True) + NORM_EPS)
            hx_ref[rows, :] = (x * inv * scale + shift).astype(BF16)
            return carry

        lax.fori_loop(0, x_ref.shape[0] // nb, norm_rows, 0)

    acc = _mm_nt(hx_ref[...], w_ref[...])
    for s in range(acc.shape[1] // LANES):
        o_ref[s] = acc[:, s * LANES:(s + 1) * LANES]


def _inproj(x2d, g, sh, sc, w_p, tm, tiles_per_row, row0):
    t, d = x2d.shape
    n = w_p.shape[0]
    tn = INPROJ_TN
    return pl.pallas_call(
        functools.partial(_inproj_kernel, tiles_per_row, row0),
        grid=(t // tm, n // tn),
        in_specs=[pl.BlockSpec((tm, d), lambda i, j: (i, 0)),
                  pl.BlockSpec((1, d), lambda i, j: (0, 0)),
                  pl.BlockSpec((8, d), lambda i, j: (0, 0)),
                  pl.BlockSpec((8, d), lambda i, j: (0, 0)),
                  pl.BlockSpec((tn, d), lambda i, j: (j, 0))],
        out_specs=pl.BlockSpec((tn // LANES, tm, LANES), lambda i, j: (j, i, 0)),
        out_shape=jax.ShapeDtypeStruct((n // LANES, t, LANES), F32),
        scratch_shapes=[pltpu.VMEM((tm, d), BF16)],
        compiler_params=_params(2),
    )(x2d, g, sh, sc, w_p)


def _mlstm_kernel(n_ctx, n_lat,
                  qx_ref, kx_ref, vx_ref, gx_ref, qc_ref, kc_ref, vc_ref, gc_ref,
                  cwq_ref, cwk_ref, cbq_ref, cbk_ref, gb_ref,
                  o_ref,
                  qs_ref, ks_ref, kst_ref, gcol_ref, grow_ref, ct_ref, hf_ref, hb_ref):
    c = MLSTM_CHUNK
    head = pl.program_id(1)
    rid = lax.broadcasted_iota(jnp.int32, (c, LANES), 0)
    lane = lax.broadcasted_iota(jnp.int32, (c, LANES), 1)
    r2 = lax.broadcasted_iota(jnp.int32, (c, c), 0)
    c2 = lax.broadcasted_iota(jnp.int32, (c, c), 1)
    prefix = (c2 <= r2).astype(BF16)
    keep = (c2 <= r2, c2 >= r2)
    gbias = gb_ref[...]

    def conv_pass(q_ref, k_ref, g_ref, nchunk, base):
        nrows = nchunk * c

        def body(ci, carry):
            r0 = pl.multiple_of(ci * c, c)
            p0 = pl.multiple_of(jnp.maximum(r0 - 8, 0), 8)
            n0 = pl.multiple_of(jnp.minimum(r0 + c, nrows - 8), 8)
            gb = g_ref[0, pl.ds(r0, c), :] + gbias
            lf = _log_sigmoid(gb)
            pre = _mm_split(lf, prefix, True)
            suf = pre[c - 1:c, :] - pre + lf
            packed = jnp.zeros((c, LANES), F32)
            for slot, (src, col0) in enumerate(((pre, GATE_LANE0 + 2 * M_HEADS), (suf, GATE_LANE0 + 3 * M_HEADS),
                                                (gb, GATE_LANE0), (gb, GATE_LANE0 + M_HEADS))):
                col = jnp.sum(jnp.where(lane == col0 + head, src, 0.0), axis=1, keepdims=True)
                packed = jnp.where(lane == slot, col, packed)
            gcol_ref[base + ci] = packed
            grow_ref[base + ci] = packed.T[0:8, :]
            for src, w_ref, b_ref, scale, is_k in ((q_ref, cwq_ref, cbq_ref, M_DQK ** -0.5, False),
                                                   (k_ref, cwk_ref, cbk_ref, 1.0, True)):
                cur = src[0, pl.ds(r0, c), :]
                prev_row = jnp.where(ci > 0, src[0, pl.ds(p0, 8), :][7:8, :], 0.0)
                next_row = jnp.where(ci < nchunk - 1, src[0, pl.ds(n0, 8), :][0:1, :], 0.0)
                up = jnp.where(rid == 0, prev_row, pltpu.roll(cur, 1, 0))
                dn = jnp.where(rid == c - 1, next_row, pltpu.roll(cur, c - 1, 0))
                w = w_ref[...]
                y = (up * w[0:1, :] + cur * w[1:2, :] + dn * w[2:3, :] + b_ref[...]) * scale
                if is_k:
                    ks_ref[base + ci] = y.astype(BF16)
                    kst_ref[base + ci] = y.T.astype(BF16)
                else:
                    qs_ref[base + ci] = y.astype(BF16)
            return carry

        lax.fori_loop(0, nchunk, body, 0, unroll=min(4, nchunk))

    def step(d, ci, base, v_ref, n, m, want_h):
        r0 = pl.multiple_of(ci * c, c)
        q = qs_ref[base + ci]
        k = ks_ref[base + ci]
        kt = kst_ref[base + ci]
        v = jnp.concatenate([v_ref[0, pl.ds(r0, c), :], v_ref[1, pl.ds(r0, c), :]], axis=1).astype(BF16)
        gcol = gcol_ref[base + ci]
        grow = grow_ref[base + ci]
        rep = lambda x: jnp.broadcast_to(x, (c, LANES))
        wide = lambda x: jnp.concatenate([x, x], axis=1)
        b_col, i_col = rep(gcol[:, d:d + 1]), rep(gcol[:, 2 + d:3 + d])
        b_row, i_row = grow[d:d + 1, :], grow[2 + d:3 + d, :]
        b_last = b_col[0:1, :] if d else b_col[c - 1:c, :]
        ct = ct_ref[d]
        h = None
        if want_h:
            dmat = jnp.where(keep[d], b_col - b_row + i_row, -jnp.inf)
            m_intra = rep(jnp.max(dmat, axis=-1, keepdims=True))
            s = _mm(q, kt) * jnp.exp(dmat - m_intra)
            num_intra = _mm(s.astype(BF16), v)
            den_intra = rep(jnp.sum(s, axis=-1, keepdims=True))
            m_inter = b_col + m
            m_j = jnp.maximum(m_inter, m_intra)
            intra = jnp.exp(m_intra - m_j)
            inter = jnp.exp(m_inter - m_j)
            num = wide(intra) * num_intra + wide(inter) * _mm(q, ct.astype(BF16))
            qn = rep(jnp.sum(q.astype(F32) * n, axis=-1, keepdims=True))
            den = intra * den_intra + inter * qn
            h = num / wide(jnp.maximum(jnp.abs(den), jnp.exp(-m_j)))
        glog = b_last - b_col + i_col
        b_last = b_last[:, 0:1]
        m_new = jnp.maximum(b_last + m, jnp.max(glog, axis=0, keepdims=True)[:, 0:1])
        wk = jnp.exp(glog - m_new)
        wk_row = jnp.exp(b_last - b_row + i_row - m_new)
        decay = jnp.exp(b_last + m - m_new)
        ct_ref[d] = decay * ct + _mm((kt.astype(F32) * wk_row).astype(BF16), v)
        n_new = decay * n + jnp.sum(wk * k.astype(F32), axis=0, keepdims=True)
        return n_new, m_new, h

    def run(nchunk, base, v_ref, carry, want_h):
        def body(i, carry):
            nf, mf, nb, mb = carry
            ib = nchunk - 1 - i
            nf, mf, h_f = step(0, i, base, v_ref, nf, mf, want_h)
            nb, mb, h_b = step(1, ib, base, v_ref, nb, mb, want_h)
            if want_h:
                hf_ref[pl.ds(pl.multiple_of(i * c, c), c), :] = h_f
                hb_ref[pl.ds(pl.multiple_of(ib * c, c), c), :] = h_b
            return nf, mf, nb, mb

        return lax.fori_loop(0, nchunk, body, carry, unroll=2)

    conv_pass(qc_ref, kc_ref, gc_ref, n_ctx, 0)
    conv_pass(qx_ref, kx_ref, gx_ref, n_lat, n_ctx)
    ct_ref[...] = jnp.zeros(ct_ref.shape, F32)
    zn = jnp.zeros((1, M_DQK), F32)
    zm = jnp.zeros((1, 1), F32)
    carry = run(n_ctx, 0, vc_ref, (zn, zm, zn, zm), False)
    run(n_lat, n_ctx, vx_ref, carry, True)

    def fin(i, carry):
        rows = pl.ds(pl.multiple_of(i * c, c), c)
        o_ref[rows, :] = (hf_ref[rows, :] + hb_ref[rows, :]).astype(o_ref.dtype)
        return carry

    lax.fori_loop(0, n_lat, fin, 0)


def _mlstm(px, pc, batch, cw, cb, gate_row):
    lx = px.shape[1] // batch
    lc = pc.shape[1] // batch
    n_lat, n_ctx = lx // MLSTM_CHUNK, lc // MLSTM_CHUNK

    def slab(nrows, first, width=1):
        return pl.BlockSpec((width, nrows, LANES), lambda b, h: (first // width + h, b, 0))

    def fixed(nrows, idx):
        return pl.BlockSpec((1, nrows, LANES), lambda b, h: (idx, b, 0))

    in_specs = [slab(lx, SLAB_Q), slab(lx, SLAB_K), slab(lx, SLAB_V, 2), fixed(lx, SLAB_LG1),
                slab(lc, SLAB_Q), slab(lc, SLAB_K), slab(lc, SLAB_V, 2), fixed(lc, SLAB_LG1),
                pl.BlockSpec((3, LANES), lambda b, h: (0, h)),
                pl.BlockSpec((3, LANES), lambda b, h: (0, M_HEADS + h)),
                pl.BlockSpec((1, LANES), lambda b, h: (0, h)),
                pl.BlockSpec((1, LANES), lambda b, h: (0, M_HEADS + h)),
                pl.BlockSpec((1, LANES), lambda b, h: (0, 0))]
    nch = n_ctx + n_lat
    return pl.pallas_call(
        functools.partial(_mlstm_kernel, n_ctx, n_lat),
        grid=(batch, M_HEADS),
        in_specs=in_specs,
        out_specs=pl.BlockSpec((lx, M_DV), lambda b, h: (b, h)),
        out_shape=jax.ShapeDtypeStruct((batch * lx, M_HEADS * M_DV), BF16),
        scratch_shapes=[pltpu.VMEM((nch, MLSTM_CHUNK, LANES), BF16),
                        pltpu.VMEM((nch, MLSTM_CHUNK, LANES), BF16),
                        pltpu.VMEM((nch, LANES, MLSTM_CHUNK), BF16),
                        pltpu.VMEM((nch, MLSTM_CHUNK, LANES), F32),
                        pltpu.VMEM((nch, 8, LANES), F32),
                        pltpu.VMEM((2, M_DQK, M_DV), F32),
                        pltpu.VMEM((lx, M_DV), F32),
                        pltpu.VMEM((lx, M_DV), F32)],
        compiler_params=_params(2),
    )(px, px, px, px, pc, pc, pc, pc, cw, cw, cb, cb, gate_row)


def _rwkv_stack(x):
    head0 = lax.broadcasted_iota(jnp.int32, x.shape, 1) < R_N
    zero = jnp.zeros_like(x)
    return jnp.concatenate([jnp.where(head0, x, zero), jnp.where(head0, zero, x)], axis=0)


def _rwkv_fold(x):
    half = x.shape[0] // 2
    return x[0:half] + x[half:2 * half]


def _rwkv_prep_kernel(want_y, r_ref, k_ref, v_ref, lw_ref, la_ref,
                      ww_ref, wa_ref, w0_ref, a0_ref, kk_ref, ka_ref, *rest):
    extra_refs, out_refs = (rest[:4], rest[4:]) if want_y else ((), rest)
    c = RWKV_CHUNK
    c2 = 2 * c
    stack = _rwkv_stack
    rr = lax.broadcasted_iota(jnp.int32, (c2, c2), 0)
    cc = lax.broadcasted_iota(jnp.int32, (c2, c2), 1)
    group_ones = ((rr < c) == (cc < c)).astype(BF16)
    tt = lax.broadcasted_iota(jnp.int32, (c, c2), 0)
    lane = lax.broadcasted_iota(jnp.int32, (c, c2), 1)
    ss = lane & (c - 1)
    head0 = lane < c
    strict = (ss < tt, ss > tt)
    incl = (ss <= tt, ss >= tt)
    eye = (ss == tt).astype(F32)
    tr = lax.broadcasted_iota(jnp.int32, (c, c), 0)
    tc = lax.broadcasted_iota(jnp.int32, (c, c), 1)
    tri = ((tc <= tr).astype(BF16), (tc >= tr).astype(BF16))
    ww = ww_ref[0]
    wa = wa_ref[0]
    k_k = kk_ref[...]
    k_a = ka_ref[...]
    m_ref, n_ref, dec_ref = out_refs[0], out_refs[1], out_refs[2]

    def prep_chunks(js):
        nj = len(js)
        rows = [pl.ds(pl.multiple_of(j * c, c), c) for j in js]
        ch = [(d, i) for d in (0, 1) for i in range(nj)]
        half = lambda d: slice(d * LANES, (d + 1) * LANES)
        r = [r_ref[0, rw, :] for rw in rows]
        k = [k_ref[0, rw, :] for rw in rows]
        v_s = [stack(v_ref[0, rw, :]).astype(BF16) for rw in rows]
        lo_w = [_mm(jnp.tanh(lw_ref[0, rw, :]).astype(BF16), ww) for rw in rows]
        lo_a = [_mm(la_ref[0, rw, :].astype(BF16), wa) for rw in rows]
        kk = [x * k_k for x in k]
        kk = [x * lax.rsqrt(_mm_split(x * x, group_ones, False, pieces=2) + 1e-12) for x in kk]
        logw = [-jnp.exp(-_softplus(-(w0_ref[d:d + 1, :] + lo_w[i][:, half(d)])) - 0.5) for d, i in ch]
        a = [jax.nn.sigmoid(a0_ref[d:d + 1, :] + lo_a[i][:, half(d)]) for d, i in ch]
        pin = [_mm_split(x, tri[d], True, pieces=2) for (d, i), x in zip(ch, logw)]
        ptot = [x[0:1, :] if d else x[c - 1:c, :] for (d, i), x in zip(ch, pin)]
        kd = [k[i] * (1.0 + (a_ - 1.0) * k_a) for (d, i), a_ in zip(ch, a)]
        kka = [kk[i] * a_ for (d, i), a_ in zip(ch, a)]
        e_inv = [jnp.exp(-x) for x in pin]
        e_end = [jnp.exp(pt - x) for pt, x in zip(ptot, pin)]
        r_t = [r[i] * jnp.exp(x) for (d, i), x in zip(ch, pin)]
        a_t = [-kk[i] * jnp.exp(x - lw_) for (d, i), x, lw_ in zip(ch, pin, logw)]
        ar_t = [jnp.concatenate([x, y], axis=0).astype(BF16) for x, y in zip(a_t, r_t)]
        bk_t = [jnp.concatenate([stack(x * e), stack(y * e)], axis=0).astype(BF16)
                for x, y, e in zip(kka, kd, e_inv)]
        bk_end = [jnp.concatenate([x * e, y * e], axis=0).astype(BF16) for x, y, e in zip(kka, kd, e_end)]
        aa = [_mm_nt(x, y) for x, y in zip(ar_t, bk_t)]
        aab = [jnp.where(strict[d], x[0:c, 0:c2], 0.0) for (d, i), x in zip(ch, aa)]
        aak = [jnp.where(strict[d], x[0:c, c2:2 * c2], 0.0).astype(BF16) for (d, i), x in zip(ch, aa)]
        akv = [_mm(x, v_s[i]).astype(BF16) for (d, i), x in zip(ch, aak)]
        xs = [eye + x for x in aab]
        pb = [x.astype(BF16) for x in aab]
        ps = [_mm(x, stack(x)) for x in pb]
        for _ in range(c.bit_length() - 3):
            pb = [x.astype(BF16) for x in ps]
            both = [_mm(jnp.concatenate([x.astype(BF16), p], axis=0), stack(p)) for x, p in zip(xs, pb)]
            xs = [x + y[0:c] for x, y in zip(xs, both)]
            ps = [y[c:c2] for y in both]
        xs = [(x + _mm(x.astype(BF16), stack(p.astype(BF16)))).astype(BF16) for x, p in zip(xs, ps)]
        wu = [_mm(x, jnp.concatenate([stack(y[0:c]), stack(z)], axis=1)).astype(BF16)
              for x, y, z in zip(xs, ar_t, akv)]
        m_mat = [_mm_tn(x[:, 0:c2], y[0:c]) for x, y in zip(wu, bk_end)]
        n_mat = [_mm_tn(jnp.concatenate([x[:, c2:2 * c2], v_ref[0, rows[i], :].astype(BF16)], axis=0), y)
                 for (d, i), x, y in zip(ch, wu, bk_end)]
        for (d, i), mm_, nn_, pt in zip(ch, m_mat, n_mat, ptot):
            slot = (0, 0, d, js[i])
            m_ref[slot] = jnp.where(head0, mm_[0:c], mm_[c:c2]).astype(BF16)
            n_ref[slot] = jnp.where(head0, nn_[0:c], nn_[c:c2])
            dec_ref[slot] = jnp.broadcast_to(jnp.exp(pt), (8, LANES))
        if want_y:
            ark = [jnp.concatenate([jnp.where(incl[d], x[c:c2, 0:c2], 0.0),
                                    jnp.where(incl[d], x[c:c2, c2:2 * c2], 0.0)], axis=1).astype(BF16)
                   for (d, i), x in zip(ch, aa)]
            qy = [_mm(x, jnp.concatenate([
                      jnp.concatenate([stack(w_[:, 0:c2]), stack(w_[:, c2:2 * c2])], axis=1),
                      jnp.concatenate([jnp.zeros((c2, c2), BF16), v_s[i]], axis=1)], axis=0))
                  for (d, i), x, w_ in zip(ch, ark, wu)]
            for (d, i), rt, x in zip(ch, r_t, qy):
                slot = (0, 0, d, js[i])
                out_refs[3][slot] = (rt + x[:, 0:c2]).astype(BF16)
                out_refs[4][slot] = x[:, c2:2 * c2]
            lg0_ref, lg1_ref, wg_ref, rk_ref = extra_refs
            ksum = [k[i] * (2.0 + (a[i] + a[nj + i] - 2.0) * k_a) for i in range(nj)]
            bonus = [_mm_split(r[i] * ksum[i] * rk_ref[...], group_ones, False) * v_ref[0, rows[i], :]
                     for i in range(nj)]
            gate = [_mm(jax.nn.sigmoid(jnp.concatenate([lg0_ref[0, rw, :], lg1_ref[0, rw, :]], axis=1)
                                       ).astype(BF16), wg_ref[...]) for rw in rows]
            for rw, gt, bn in zip(rows, gate, bonus):
                out_refs[5][rw, :] = gt.astype(BF16)
                out_refs[6][rw, :] = (bn * gt).astype(BF16)

    n_chunks = r_ref.shape[1] // c
    width = min(RWKV_PREP_WIDTH, n_chunks)
    assert n_chunks % width == 0

    def body(t, carry):
        prep_chunks([t * width + u for u in range(width)])
        return carry

    lax.fori_loop(0, n_chunks // width, body, 0)


def _rwkv_scan_kernel(mc_ref, nc_ref, dc_ref, mf_ref, nf_ref, df_ref, qf_ref, ylf_ref,
                      mb_ref, nb_ref, db_ref, qb_ref, ylb_ref,
                      gate_ref, bg_ref, gnw_ref, gnb_ref,
                      o_ref, z_ref, yf_ref, yb_ref):
    c = RWKV_CHUNK
    g = pl.program_id(2)
    n_groups = pl.num_programs(2)
    group = mf_ref.shape[3]
    n_ctx = mc_ref.shape[3]
    pairs = range(RWKV_SCAN_PAIRS)

    def advance(z, m_c, n_c, dec):
        return z * dec[0:1, :] + _mm(z.astype(BF16), _rwkv_stack(m_c)) + _rwkv_stack(n_c)

    @pl.when(g == 0)
    def _():
        z_ref[...] = jnp.zeros(z_ref.shape, F32)

        def ctx_body(i, carry):
            for pp in pairs:
                for d, ii in ((0, i), (1, n_ctx - 1 - i)):
                    zi = 2 * pp + d
                    z_ref[zi] = advance(z_ref[zi], mc_ref[0, pp, d, ii], nc_ref[0, pp, d, ii],
                                        dc_ref[0, pp, d, ii])
            return carry

        lax.fori_loop(0, n_ctx, ctx_body, 0)

    def lat_body(j, carry):
        chains = [(pp,) + t for pp in pairs for t in (
            (0, j, g * group, mf_ref, nf_ref, df_ref, qf_ref, ylf_ref, yf_ref),
            (1, group - 1 - j, (n_groups - 1 - g) * group, mb_ref, nb_ref, db_ref, qb_ref, ylb_ref, yb_ref))]
        zs = [z_ref[2 * pp + d] for pp, d, *_ in chains]
        zb = [z.astype(BF16) for z in zs]
        ys = [_mm_nt(_rwkv_stack(q_ref[0, pp, 0, jj]), z)
              for (pp, d, jj, first, m_ref, n_ref, d_ref, q_ref, yl_ref, y_ref), z in zip(chains, zb)]
        zn = [z * d_ref[0, pp, 0, jj][0:1, :] + _mm(zh, _rwkv_stack(m_ref[0, pp, 0, jj]))
              + _rwkv_stack(n_ref[0, pp, 0, jj])
              for (pp, d, jj, first, m_ref, n_ref, d_ref, q_ref, yl_ref, y_ref), z, zh in zip(chains, zs, zb)]
        for (pp, d, jj, first, m_ref, n_ref, d_ref, q_ref, yl_ref, y_ref), y, z in zip(chains, ys, zn):
            z_ref[2 * pp + d] = z
            y_ref[pl.ds(pl.multiple_of((first + jj) * c, c), c), pp * LANES:(pp + 1) * LANES] = (
                _rwkv_fold(y) + yl_ref[0, pp, 0, jj])
        return carry

    lax.fori_loop(0, group, lat_body, 0)

    @pl.when(g == n_groups - 1)
    def _():
        fr = 4 * c
        r2 = lax.broadcasted_iota(jnp.int32, (LANES, LANES), 0)
        c2 = lax.broadcasted_iota(jnp.int32, (LANES, LANES), 1)
        group_ones = ((r2 < R_N) == (c2 < R_N)).astype(BF16)
        inv_n = 1.0 / R_N

        def fin(i, carry):
            rows = pl.ds(pl.multiple_of(i * fr, fr), fr)
            for pp in pairs:
                cols = slice(pp * LANES, (pp + 1) * LANES)
                y = yf_ref[rows, cols] + yb_ref[rows, cols]
                sums = _mm_split(jnp.concatenate([y, y * y], axis=0), group_ones, False, pieces=2)
                mu = sums[0:fr] * inv_n
                var = sums[fr:2 * fr] * inv_n - mu * mu
                yn = (y - mu) * lax.rsqrt(var + GN_EPS) * gnw_ref[:, cols] + gnb_ref[:, cols]
                o_ref[rows, cols] = (yn * gate_ref[rows, cols].astype(F32)
                                     + bg_ref[rows, cols].astype(F32)).astype(o_ref.dtype)
            return carry

        lax.fori_loop(0, o_ref.shape[0] // fr, fin, 0)


def _rwkv_prep(p_all, batch, group, want_y, ww, wa, w0, a0, k_k, k_a, wg=None, r_k=None):
    c = RWKV_CHUNK
    length = p_all.shape[1] // batch
    n_chunks = length // c
    n_groups = n_chunks // group
    n_pairs = ww.shape[0]
    rows = group * c

    def slab(first):
        return pl.BlockSpec((1, rows, LANES), lambda b, p, g: (first + p, b * n_groups + g, 0))

    def fixed(idx):
        return pl.BlockSpec((1, rows, LANES), lambda b, p, g: (idx, b * n_groups + g, 0))

    def vec(nrows):
        return pl.BlockSpec((nrows, LANES), lambda b, p, g: (0, p))

    lora = pl.BlockSpec((1, LANES, 2 * LANES), lambda b, p, g: (p, 0, 0))
    outs = [(c, BF16), (c, F32), (8, F32)] + ([(c, BF16), (c, F32)] if want_y else [])
    in_specs = [slab(SLAB_R), slab(SLAB_KR), slab(SLAB_VR), fixed(SLAB_LW), fixed(SLAB_LA),
                lora, lora, vec(2), vec(2), vec(1), vec(1)]
    operands = [p_all, p_all, p_all, p_all, p_all, ww, wa, w0, a0, k_k, k_a]
    out_specs = [pl.BlockSpec((1, 1, 2, group, nr, LANES), lambda b, p, g: (b, p, 0, g, 0, 0)) for nr, _ in outs]
    out_shape = [jax.ShapeDtypeStruct((batch, n_pairs, 2, n_chunks, nr, LANES), dt) for nr, dt in outs]
    if want_y:
        in_specs += [fixed(SLAB_LG0), fixed(SLAB_LG1),
                     pl.BlockSpec((2 * LANES, LANES), lambda b, p, g: (0, p)), vec(1)]
        operands += [p_all, p_all, wg, r_k]
        out_specs += [pl.BlockSpec((rows, LANES), lambda b, p, g: (b * n_groups + g, p))] * 2
        out_shape += [jax.ShapeDtypeStruct((batch * length, n_pairs * LANES), BF16)] * 2
    return pl.pallas_call(
        functools.partial(_rwkv_prep_kernel, want_y),
        grid=(batch, n_pairs, n_groups),
        in_specs=in_specs,
        out_specs=out_specs,
        out_shape=out_shape,
        compiler_params=_params(3),
    )(*operands)


def _rwkv_scan(ops_c, ops_x, gate, bonus_gated, batch, group, gn_w, gn_b):
    c = RWKV_CHUNK
    sp = RWKV_SCAN_PAIRS
    lx = gate.shape[0] // batch
    n_pairs, n_ctx = ops_c[0].shape[1], ops_c[0].shape[3]
    n_groups = ops_x[0].shape[3] // group
    assert n_pairs % sp == 0

    def ctx_block(a):
        return pl.BlockSpec((1, sp, 2, n_ctx, a.shape[4], LANES), lambda b, p, g: (b, p, 0, 0, 0, 0))

    def fwd_block(a):
        return pl.BlockSpec((1, sp, 1, group, a.shape[4], LANES), lambda b, p, g: (b, p, 0, g, 0, 0))

    def bwd_block(a):
        return pl.BlockSpec((1, sp, 1, group, a.shape[4], LANES),
                            lambda b, p, g: (b, p, 1, n_groups - 1 - g, 0, 0))

    tokens = pl.BlockSpec((lx, sp * LANES), lambda b, p, g: (b, p))
    tokens_out = pl.BlockSpec((lx, sp * LANES), lambda b, p, g: (b, p), pipeline_mode=pl.Buffered(1))
    vec = pl.BlockSpec((1, sp * LANES), lambda b, p, g: (0, p))
    return pl.pallas_call(
        _rwkv_scan_kernel,
        grid=(batch, n_pairs // sp, n_groups),
        in_specs=[ctx_block(a) for a in ops_c] + [fwd_block(a) for a in ops_x] + [bwd_block(a) for a in ops_x]
                 + [tokens, tokens, vec, vec],
        out_specs=tokens_out,
        out_shape=jax.ShapeDtypeStruct((batch * lx, n_pairs * LANES), BF16),
        scratch_shapes=[pltpu.VMEM((2 * sp, 2 * c, LANES), F32),
                        pltpu.VMEM((lx, sp * LANES), F32), pltpu.VMEM((lx, sp * LANES), F32)],
        compiler_params=_params(3),
    )(*ops_c, *ops_x, *ops_x, gate, bonus_gated, gn_w, gn_b)


def _rwkv(px, pc, batch, ww, wa, wg, w0, a0, k_k, k_a, r_k, gn_w, gn_b):
    c = RWKV_CHUNK
    n_ctx = pc.shape[1] // batch // c
    n_lat = px.shape[1] // batch // c
    group = min(RWKV_GROUP, n_lat)
    assert n_lat % group == 0
    ops_c = _rwkv_prep(pc, batch, n_ctx, False, ww, wa, w0, a0, k_k, k_a)
    *ops_x, gate, bonus_gated = _rwkv_prep(px, batch, group, True, ww, wa, w0, a0, k_k, k_a, wg, r_k)
    scan_group = min(RWKV_SCAN_GROUP, n_lat)
    assert n_lat % scan_group == 0
    return _rwkv_scan(ops_c, ops_x, gate, bonus_gated, batch, scan_group, gn_w, gn_b)


def _out_kernel(tiles_per_batch, mixm_ref, om_ref, mixr_ref, w_ref, x_ref, ng_ref,
                gt1_ref, g2_ref, sh2_ref, sc2_ref, x1_ref, hx2_ref):
    b = pl.program_id(0) // tiles_per_batch
    hm = mixm_ref[...].astype(F32)
    ng = ng_ref[...]
    parts = []
    for h in range(M_HEADS):
        cols = slice(h * M_DV, (h + 1) * M_DV)
        seg = hm[:, cols]
        seg = seg * lax.rsqrt(jnp.mean(seg * seg, axis=-1, keepdims=True) + NORM_EPS)
        og = jnp.concatenate([om_ref[2 * h], om_ref[2 * h + 1]], axis=1)
        parts.append((seg * ng[:, cols] * jax.nn.sigmoid(og)).astype(BF16))
    lhs = jnp.concatenate(parts + [mixr_ref[...]], axis=1)
    x1 = x_ref[...] + gt1_ref[pl.ds(b, 1), :] * _mm(lhs, w_ref[...])
    x1_ref[...] = x1
    y = x1 * lax.rsqrt(jnp.mean(x1 * x1, axis=-1, keepdims=True) + NORM_EPS) * g2_ref[...]
    hx2_ref[...] = (y * (1.0 + sc2_ref[pl.ds(b, 1), :]) + sh2_ref[pl.ds(b, 1), :]).astype(BF16)


def _out_proj(mixm, px, mixr, w_out, x2d, ng, gt1, g2, sh2, sc2, tm, tiles_per_batch):
    t, d = x2d.shape
    dm = mixm.shape[1]
    row = lambda i: (i, 0)
    const = lambda i: (0, 0)
    return pl.pallas_call(
        functools.partial(_out_kernel, tiles_per_batch),
        grid=(t // tm,),
        in_specs=[pl.BlockSpec((tm, dm), row),
                  pl.BlockSpec((8, tm, LANES), lambda i: (SLAB_O // 8, i, 0)),
                  pl.BlockSpec((tm, dm), row),
                  pl.BlockSpec((d, d), const),
                  pl.BlockSpec((tm, d), row),
                  pl.BlockSpec((1, dm), const),
                  pl.BlockSpec((8, d), const),
                  pl.BlockSpec((1, d), const),
                  pl.BlockSpec((8, d), const),
                  pl.BlockSpec((8, d), const)],
        out_specs=[pl.BlockSpec((tm, d), row), pl.BlockSpec((tm, d), row)],
        out_shape=[jax.ShapeDtypeStruct((t, d), F32), jax.ShapeDtypeStruct((t, d), BF16)],
        compiler_params=_params(1),
    )(mixm, px, mixr, w_out, x2d, ng, gt1, g2, sh2, sc2)


def _ffn_kernel(tiles_per_img, hx_ref, top_ref, bot_ref, wu_ref, wg_ref, wd_ref, cw_ref, cb_ref,
                x1_ref, gt2_ref, gf_ref, o_ref):
    i = pl.program_id(0)
    j = pl.program_id(1)
    tm = hx_ref.shape[0]
    ti = i % tiles_per_img

    @pl.when(j == 0)
    def _():
        o_ref[...] = jnp.zeros(o_ref.shape, F32)

    hx = hx_ref[...]
    top = jnp.where(ti > 0, top_ref[...], jnp.zeros_like(top_ref[...]))
    bot = jnp.where(ti < tiles_per_img - 1, bot_ref[...], jnp.zeros_like(bot_ref[...]))
    u = _mm(jnp.concatenate([top, hx, bot], axis=0), wu_ref[...])
    nr = u.shape[0]
    col = lax.broadcasted_iota(jnp.int32, u.shape, 0) & (GRID_W - 1)
    ul = jnp.where(col == 0, 0.0, pltpu.roll(u, 1, 0))
    ur = jnp.where(col == GRID_W - 1, 0.0, pltpu.roll(u, nr - 1, 0))
    cw = cw_ref[...]
    conv = cb_ref[...]
    for dy in range(3):
        rows = slice(dy * GRID_W, dy * GRID_W + tm)
        conv = (conv + ul[rows] * cw[3 * dy:3 * dy + 1, :] + u[rows] * cw[3 * dy + 1:3 * dy + 2, :]
                + ur[rows] * cw[3 * dy + 2:3 * dy + 3, :])
    gelu = 0.5 * conv * (1.0 + jnp.tanh(0.7978845608028654 * (conv + 0.044715 * conv * conv * conv)))
    act = (gelu * _mm(hx, wg_ref[...])).astype(BF16)
    o_ref[...] += _mm(act, wd_ref[...])

    @pl.when(j == pl.num_programs(1) - 1)
    def _():
        gate = gt2_ref[pl.ds(i // tiles_per_img, 1), :]
        nb = 256

        def finish_rows(k, carry):
            rows = pl.ds(pl.multiple_of(k * nb, nb), nb)
            x2 = x1_ref[rows, :] + gate * o_ref[rows, :]
            o_ref[rows, :] = x2 * lax.rsqrt(jnp.mean(x2 * x2, axis=-1, keepdims=True) + NORM_EPS) * gf_ref[...]
            return carry

        lax.fori_loop(0, tm // nb, finish_rows, 0)


def _conv_ffn(hx2, w_up, w_gate, w_down, cw, cb, x1, gt2, g_final, tm, tiles_per_img):
    t, d = hx2.shape
    f = w_up.shape[1]
    tf = FFN_TF
    rows_per_tile = tm // GRID_W
    n_rows = t // GRID_W
    return pl.pallas_call(
        functools.partial(_ffn_kernel, tiles_per_img),
        grid=(t // tm, f // tf),
        in_specs=[pl.BlockSpec((tm, d), lambda i, j: (i, 0)),
                  pl.BlockSpec((GRID_W, d), lambda i, j: (jnp.maximum(i * rows_per_tile - 1, 0), 0)),
                  pl.BlockSpec((GRID_W, d), lambda i, j: (jnp.minimum((i + 1) * rows_per_tile, n_rows - 1), 0)),
                  pl.BlockSpec((d, tf), lambda i, j: (0, j)),
                  pl.BlockSpec((d, tf), lambda i, j: (0, j)),
                  pl.BlockSpec((tf, d), lambda i, j: (j, 0)),
                  pl.BlockSpec((9, tf), lambda i, j: (0, j)),
                  pl.BlockSpec((1, tf), lambda i, j: (0, j)),
                  pl.BlockSpec((tm, d), lambda i, j: (i, 0)),
                  pl.BlockSpec((8, d), lambda i, j: (0, 0)),
                  pl.BlockSpec((1, d), lambda i, j: (0, 0))],
        out_specs=pl.BlockSpec((tm, d), lambda i, j: (i, 0), pipeline_mode=pl.Buffered(1)),
        out_shape=jax.ShapeDtypeStruct((t, d), F32),
        compiler_params=_params(2),
    )(hx2, hx2, hx2, w_up, w_gate, w_down, cw, cb, x1, gt2, g_final)


def _lora_pairs(up):
    _, rank, width = up.shape
    u = up.reshape(2, rank, width // LANES, LANES).transpose(2, 0, 1, 3)
    z = jnp.zeros_like(u[:, 0])
    top = jnp.concatenate([u[:, 0], z], axis=-1)
    bot = jnp.concatenate([z, u[:, 1]], axis=-1)
    return jnp.concatenate([top, bot], axis=1).astype(BF16)


def kernel(x, c, ctx, c_ctx, w_mod, b_mod, g_norm1, g_norm2, w_in, m_conv_w, m_conv_b, m_gate_b, m_norm_g, r_w0, r_w_up, r_a0, r_a_up, r_g_up, r_k_k, r_k_a, r_r_k, r_gn_w, r_gn_b, w_out, f_w_up, f_w_gate, f_conv_w, f_conv_b, f_w_down, g_final):
    batch, seq, d = x.shape
    ctx_len = ctx.shape[1]
    assert w_mod.shape[0] == 1, "single-layer block"
    assert batch + 1 <= 8 and seq % 512 == 0 and ctx_len % MLSTM_CHUNK == 0

    cv8 = jnp.zeros((8, d), F32).at[:batch].set(c).at[batch].set(c_ctx)
    mod = _modulation(cv8, w_mod[0], b_mod[0])
    sh1, sc1, gt1, sh2, sc2, gt2 = (mod[:, k * d:(k + 1) * d] for k in range(6))

    n_gate = 4 * M_HEADS
    w_p = _permute_w(jnp.swapaxes(w_in, 1, 2), SLAB_R * LANES, n_gate, N_SLABS * LANES)
    g1 = g_norm1[0].reshape(1, d)
    tm_x = INPROJ_TM if seq % INPROJ_TM == 0 else OUT_TM
    px = _inproj(x.reshape(batch * seq, d), g1, sh1, sc1, w_p, tm_x, seq // tm_x, 0)
    tm_c = batch * ctx_len
    pc = _inproj(ctx.reshape(batch * ctx_len, d), g1, sh1, sc1, w_p, tm_c, 1 << 30, batch)

    gate_row = jnp.zeros((1, LANES), F32).at[0, GATE_LANE0:GATE_LANE0 + n_gate].set(m_gate_b[0].reshape(-1))
    mixm = _mlstm(px, pc, batch, m_conv_w[0], m_conv_b[0].reshape(1, -1), gate_row)

    rw = r_k_k.shape[1]
    wg = jnp.zeros((2 * LANES, rw), F32).at[:r_g_up.shape[1]].set(r_g_up[0]).astype(BF16)
    mixr = _rwkv(px, pc, batch, _lora_pairs(r_w_up[0]), _lora_pairs(r_a_up[0]), wg,
                 r_w0[0], r_a0[0], r_k_k, r_k_a, r_r_k[0].reshape(1, rw), r_gn_w, r_gn_b)

    tm_o = OUT_TM
    x1, hx2 = _out_proj(mixm, px, mixr, w_out[0].astype(BF16), x.reshape(batch * seq, d), m_norm_g,
                        gt1, g_norm2[0].reshape(1, d), sh2, sc2, tm_o, seq // tm_o)

    tm_f = FFN_TM if seq % FFN_TM == 0 else OUT_TM
    out = _conv_ffn(hx2, f_w_up[0].astype(BF16), f_w_gate[0].astype(BF16), f_w_down[0].astype(BF16),
                    f_conv_w[0].reshape(9, -1), f_conv_b, x1, gt2, g_final.reshape(1, d),
                    tm_f, seq // tm_f)
    return out.reshape(batch, seq, d)
```

```python
import functools

import jax
import jax.numpy as jnp
from jax import lax
from jax.experimental import pallas as pl
from jax.experimental.pallas import tpu as pltpu

F32 = jnp.float32
BF16 = jnp.bfloat16

LANES = 128
GRID_W = 64
M_HEADS = 4
M_DQK = 128
M_DV = 256
R_N = 64
NORM_EPS = 1e-6
GN_EPS = 64e-5
MLSTM_CHUNK = 128
RWKV_CHUNK = 64
RWKV_GROUP = 16
RWKV_SCAN_GROUP = 8
RWKV_SCAN_PAIRS = 4
RWKV_PREP_WIDTH = 16
VMEM_LIMIT = 56 * 1024 * 1024
MOD_TN = 1024
PERMUTE_ROWS = 512
INPROJ_TM, INPROJ_TN = 1024, 512
OUT_TM = 512
FFN_TM, FFN_TF = 1024, 512

SLAB_Q, SLAB_K, SLAB_V, SLAB_O = 0, 4, 8, 16
SLAB_R, SLAB_KR, SLAB_VR = 24, 32, 40
SLAB_LW, SLAB_LA, SLAB_LG0, SLAB_LG1 = 48, 49, 50, 51
N_SLABS = 52
GATE_LANE0 = 32


def _mm(a, b):
    return jnp.dot(a, b, preferred_element_type=F32)


def _mm_nt(a, b):
    return lax.dot_general(a, b, (((1,), (1,)), ((), ())), preferred_element_type=F32)


def _mm_tn(a, b):
    return lax.dot_general(a, b, (((0,), (0,)), ((), ())), preferred_element_type=F32)


def _mm_split(x, ones, ones_first, pieces=3):
    dot = (lambda p: _mm(ones, p)) if ones_first else (lambda p: _mm(p, ones))
    piece = x.astype(BF16)
    total = dot(piece)
    for _ in range(pieces - 1):
        x = x - piece.astype(F32)
        piece = x.astype(BF16)
        total = total + dot(piece)
    return total


def _softplus(x):
    return jnp.maximum(x, 0.0) + jnp.log(1.0 + jnp.exp(-jnp.abs(x)))


def _log_sigmoid(x):
    return -_softplus(-x)


def _params(n_axes, vmem_limit=VMEM_LIMIT):
    return pltpu.CompilerParams(dimension_semantics=("arbitrary",) * n_axes,
                                vmem_limit_bytes=vmem_limit)


def _mod_kernel(cv_ref, w_ref, b_ref, o_ref):
    cv = cv_ref[...]
    s = (cv * jax.nn.sigmoid(cv)).astype(BF16)
    o_ref[...] = _mm(s, w_ref[...].astype(BF16)) + b_ref[...]


def _modulation(cv8, w_mod, b_mod):
    d, n = w_mod.shape
    tn = MOD_TN
    return pl.pallas_call(
        _mod_kernel,
        grid=(n // tn,),
        in_specs=[pl.BlockSpec((8, d), lambda j: (0, 0)),
                  pl.BlockSpec((d, tn), lambda j: (0, j)),
                  pl.BlockSpec((1, tn), lambda j: (0, j))],
        out_specs=pl.BlockSpec((8, tn), lambda j: (0, j)),
        out_shape=jax.ShapeDtypeStruct((8, n), F32),
        compiler_params=_params(1),
    )(cv8, w_mod, b_mod.reshape(1, n))


def _permute_w_kernel(n_main, n_gate, n_valid, cur_ref, nxt_ref, gate_ref, o_ref):
    j = pl.program_id(0)
    last = pl.num_programs(0) - 1
    tr = o_ref.shape[0]

    @pl.when(j < n_main)
    def _():
        o_ref[...] = cur_ref[0].astype(BF16)

    @pl.when(jnp.logical_and(j >= n_main, j < last))
    def _():
        o_ref[...] = jnp.concatenate([cur_ref[0, n_gate:tr, :], nxt_ref[0]], axis=0).astype(BF16)

    @pl.when(j == last)
    def _():
        pad = jnp.zeros((tr - n_valid, o_ref.shape[1]), F32)
        o_ref[...] = jnp.concatenate([cur_ref[0, n_gate:n_valid, :], gate_ref[0], pad], axis=0).astype(BF16)


def _permute_w(w_t, n_main, n_gate, n_out):
    _, n, d = w_t.shape
    tr = PERMUTE_ROWS
    n_blocks = n_out // tr
    assert n_main % tr == 0 and n_out % tr == 0 and pl.cdiv(n, tr) == n_blocks and tr % n_gate == 0
    per = tr // n_gate
    return pl.pallas_call(
        functools.partial(_permute_w_kernel, n_main // tr, n_gate, n - (n_blocks - 1) * tr),
        grid=(n_blocks,),
        in_specs=[pl.BlockSpec((1, tr, d), lambda j: (0, j, 0)),
                  pl.BlockSpec((1, n_gate, d), lambda j: (0, jnp.minimum(j + 1, n_blocks - 1) * per, 0)),
                  pl.BlockSpec((1, n_gate, d), lambda j: (0, n_main // n_gate, 0))],
        out_specs=pl.BlockSpec((tr, d), lambda j: (j, 0)),
        out_shape=jax.ShapeDtypeStruct((n_out, d), BF16),
        compiler_params=_params(1),
    )(w_t, w_t, w_t)


def _inproj_kernel(tiles_per_row, row0, x_ref, g_ref, sh_ref, sc_ref, w_ref, o_ref, hx_ref):
    i = pl.program_id(0)
    j = pl.program_id(1)

    @pl.when(j == 0)
    def _():
        r = row0 + i // tiles_per_row
        scale = g_ref[...] * (1.0 + sc_ref[pl.ds(r, 1), :])
        shift = sh_ref[pl.ds(r, 1), :]
        nb = 256

        def norm_rows(k, carry):
            rows = pl.ds(pl.multiple_of(k * nb, nb), nb)
            x = x_ref[rows, :]
            inv = lax.rsqrt(jnp.mean(x * x, axis=-1, keepdims=True) + NORM_EPS)
            hx_ref[rows, :] = (x * inv * scale + shift).astype(BF16)
            return carry

        lax.fori_loop(0, x_ref.shape[0] // nb, norm_rows, 0)

    acc = _mm_nt(hx_ref[...], w_ref[...])
    for s in range(acc.shape[1] // LANES):
        o_ref[s] = acc[:, s * LANES:(s + 1) * LANES]


def _inproj(x2d, g, sh, sc, w_p, tm, tiles_per_row, row0):
    t, d = x2d.shape
    n = w_p.shape[0]
    tn = INPROJ_TN
    return pl.pallas_call(
        functools.partial(_inproj_kernel, tiles_per_row, row0),
        grid=(t // tm, n // tn),
        in_specs=[pl.BlockSpec((tm, d), lambda i, j: (i, 0)),
                  pl.BlockSpec((1, d), lambda i, j: (0, 0)),
                  pl.BlockSpec((8, d), lambda i, j: (0, 0)),
                  pl.BlockSpec((8, d), lambda i, j: (0, 0)),
                  pl.BlockSpec((tn, d), lambda i, j: (j, 0))],
        out_specs=pl.BlockSpec((tn // LANES, tm, LANES), lambda i, j: (j, i, 0)),
        out_shape=jax.ShapeDtypeStruct((n // LANES, t, LANES), F32),
        scratch_shapes=[pltpu.VMEM((tm, d), BF16)],
        compiler_params=_params(2),
    )(x2d, g, sh, sc, w_p)


def _mlstm_kernel(n_ctx, n_lat,
                  qx_ref, kx_ref, vx_ref, gx_ref, qc_ref, kc_ref, vc_ref, gc_ref,
                  cwq_ref, cwk_ref, cbq_ref, cbk_ref, gb_ref,
                  o_ref,
                  qs_ref, ks_ref, kst_ref, gcol_ref, grow_ref, ct_ref, hf_ref, hb_ref):
    c = MLSTM_CHUNK
    head = pl.program_id(1)
    rid = lax.broadcasted_iota(jnp.int32, (c, LANES), 0)
    lane = lax.broadcasted_iota(jnp.int32, (c, LANES), 1)
    r2 = lax.broadcasted_iota(jnp.int32, (c, c), 0)
    c2 = lax.broadcasted_iota(jnp.int32, (c, c), 1)
    prefix = (c2 <= r2).astype(BF16)
    keep = (c2 <= r2, c2 >= r2)
    gbias = gb_ref[...]

    def conv_pass(q_ref, k_ref, g_ref, nchunk, base):
        nrows = nchunk * c

        def body(ci, carry):
            r0 = pl.multiple_of(ci * c, c)
            p0 = pl.multiple_of(jnp.maximum(r0 - 8, 0), 8)
            n0 = pl.multiple_of(jnp.minimum(r0 + c, nrows - 8), 8)
            gb = g_ref[0, pl.ds(r0, c), :] + gbias
            lf = _log_sigmoid(gb)
            pre = _mm_split(lf, prefix, True)
            suf = pre[c - 1:c, :] - pre + lf
            packed = jnp.zeros((c, LANES), F32)
            for slot, (src, col0) in enumerate(((pre, GATE_LANE0 + 2 * M_HEADS), (suf, GATE_LANE0 + 3 * M_HEADS),
                                                (gb, GATE_LANE0), (gb, GATE_LANE0 + M_HEADS))):
                col = jnp.sum(jnp.where(lane == col0 + head, src, 0.0), axis=1, keepdims=True)
                packed = jnp.where(lane == slot, col, packed)
            gcol_ref[base + ci] = packed
            grow_ref[base + ci] = packed.T[0:8, :]
            for src, w_ref, b_ref, scale, is_k in ((q_ref, cwq_ref, cbq_ref, M_DQK ** -0.5, False),
                                                   (k_ref, cwk_ref, cbk_ref, 1.0, True)):
                cur = src[0, pl.ds(r0, c), :]
                prev_row = jnp.where(ci > 0, src[0, pl.ds(p0, 8), :][7:8, :], 0.0)
                next_row = jnp.where(ci < nchunk - 1, src[0, pl.ds(n0, 8), :][0:1, :], 0.0)
                up = jnp.where(rid == 0, prev_row, pltpu.roll(cur, 1, 0))
                dn = jnp.where(rid == c - 1, next_row, pltpu.roll(cur, c - 1, 0))
                w = w_ref[...]
                y = (up * w[0:1, :] + cur * w[1:2, :] + dn * w[2:3, :] + b_ref[...]) * scale
                if is_k:
                    ks_ref[base + ci] = y.astype(BF16)
                    kst_ref[base + ci] = y.T.astype(BF16)
                else:
                    qs_ref[base + ci] = y.astype(BF16)
            return carry

        lax.fori_loop(0, nchunk, body, 0, unroll=min(4, nchunk))

    def step(d, ci, base, v_ref, n, m, want_h):
        r0 = pl.multiple_of(ci * c, c)
        q = qs_ref[base + ci]
        k = ks_ref[base + ci]
        kt = kst_ref[base + ci]
        v = jnp.concatenate([v_ref[0, pl.ds(r0, c), :], v_ref[1, pl.ds(r0, c), :]], axis=1).astype(BF16)
        gcol = gcol_ref[base + ci]
        grow = grow_ref[base + ci]
        rep = lambda x: jnp.broadcast_to(x, (c, LANES))
        wide = lambda x: jnp.concatenate([x, x], axis=1)
        b_col, i_col = rep(gcol[:, d:d + 1]), rep(gcol[:, 2 + d:3 + d])
        b_row, i_row = grow[d:d + 1, :], grow[2 + d:3 + d, :]
        b_last = b_col[0:1, :] if d else b_col[c - 1:c, :]
        ct = ct_ref[d]
        h = None
        if want_h:
            dmat = jnp.where(keep[d], b_col - b_row + i_row, -jnp.inf)
            m_intra = rep(jnp.max(dmat, axis=-1, keepdims=True))
            s = _mm(q, kt) * jnp.exp(dmat - m_intra)
            num_intra = _mm(s.astype(BF16), v)
            den_intra = rep(jnp.sum(s, axis=-1, keepdims=True))
            m_inter = b_col + m
            m_j = jnp.maximum(m_inter, m_intra)
            intra = jnp.exp(m_intra - m_j)
            inter = jnp.exp(m_inter - m_j)
            num = wide(intra) * num_intra + wide(inter) * _mm(q, ct.astype(BF16))
            qn = rep(jnp.sum(q.astype(F32) * n, axis=-1, keepdims=True))
            den = intra * den_intra + inter * qn
            h = num / wide(jnp.maximum(jnp.abs(den), jnp.exp(-m_j)))
        glog = b_last - b_col + i_col
        b_last = b_last[:, 0:1]
        m_new = jnp.maximum(b_last + m, jnp.max(glog, axis=0, keepdims=True)[:, 0:1])
        wk = jnp.exp(glog - m_new)
        wk_row = jnp.exp(b_last - b_row + i_row - m_new)
        decay = jnp.exp(b_last + m - m_new)
        ct_ref[d] = decay * ct + _mm((kt.astype(F32) * wk_row).astype(BF16), v)
        n_new = decay * n + jnp.sum(wk * k.astype(F32), axis=0, keepdims=True)
        return n_new, m_new, h

    def run(nchunk, base, v_ref, carry, want_h):
        def body(i, carry):
            nf, mf, nb, mb = carry
            ib = nchunk - 1 - i
            nf, mf, h_f = step(0, i, base, v_ref, nf, mf, want_h)
            nb, mb, h_b = step(1, ib, base, v_ref, nb, mb, want_h)
            if want_h:
                hf_ref[pl.ds(pl.multiple_of(i * c, c), c), :] = h_f
                hb_ref[pl.ds(pl.multiple_of(ib * c, c), c), :] = h_b
            return nf, mf, nb, mb

        return lax.fori_loop(0, nchunk, body, carry, unroll=2)

    conv_pass(qc_ref, kc_ref, gc_ref, n_ctx, 0)
    conv_pass(qx_ref, kx_ref, gx_ref, n_lat, n_ctx)
    ct_ref[...] = jnp.zeros(ct_ref.shape, F32)
    zn = jnp.zeros((1, M_DQK), F32)
    zm = jnp.zeros((1, 1), F32)
    carry = run(n_ctx, 0, vc_ref, (zn, zm, zn, zm), False)
    run(n_lat, n_ctx, vx_ref, carry, True)

    def fin(i, carry):
        rows = pl.ds(pl.multiple_of(i * c, c), c)
        o_ref[rows, :] = (hf_ref[rows, :] + hb_ref[rows, :]).astype(o_ref.dtype)
        return carry

    lax.fori_loop(0, n_lat, fin, 0)


def _mlstm(px, pc, batch, cw, cb, gate_row):
    lx = px.shape[1] // batch
    lc = pc.shape[1] // batch
    n_lat, n_ctx = lx // MLSTM_CHUNK, lc // MLSTM_CHUNK

    def slab(nrows, first, width=1):
        return pl.BlockSpec((width, nrows, LANES), lambda b, h: (first // width + h, b, 0))

    def fixed(nrows, idx):
        return pl.BlockSpec((1, nrows, LANES), lambda b, h: (idx, b, 0))

    in_specs = [slab(lx, SLAB_Q), slab(lx, SLAB_K), slab(lx, SLAB_V, 2), fixed(lx, SLAB_LG1),
                slab(lc, SLAB_Q), slab(lc, SLAB_K), slab(lc, SLAB_V, 2), fixed(lc, SLAB_LG1),
                pl.BlockSpec((3, LANES), lambda b, h: (0, h)),
                pl.BlockSpec((3, LANES), lambda b, h: (0, M_HEADS + h)),
                pl.BlockSpec((1, LANES), lambda b, h: (0, h)),
                pl.BlockSpec((1, LANES), lambda b, h: (0, M_HEADS + h)),
                pl.BlockSpec((1, LANES), lambda b, h: (0, 0))]
    nch = n_ctx + n_lat
    return pl.pallas_call(
        functools.partial(_mlstm_kernel, n_ctx, n_lat),
        grid=(batch, M_HEADS),
        in_specs=in_specs,
        out_specs=pl.BlockSpec((lx, M_DV), lambda b, h: (b, h)),
        out_shape=jax.ShapeDtypeStruct((batch * lx, M_HEADS * M_DV), BF16),
        scratch_shapes=[pltpu.VMEM((nch, MLSTM_CHUNK, LANES), BF16),
                        pltpu.VMEM((nch, MLSTM_CHUNK, LANES), BF16),
                        pltpu.VMEM((nch, LANES, MLSTM_CHUNK), BF16),
                        pltpu.VMEM((nch, MLSTM_CHUNK, LANES), F32),
                        pltpu.VMEM((nch, 8, LANES), F32),
                        pltpu.VMEM((2, M_DQK, M_DV), F32),
                        pltpu.VMEM((lx, M_DV), F32),
                        pltpu.VMEM((lx, M_DV), F32)],
        compiler_params=_params(2),
    )(px, px, px, px, pc, pc, pc, pc, cw, cw, cb, cb, gate_row)


def _rwkv_stack(x):
    head0 = lax.broadcasted_iota(jnp.int32, x.shape, 1) < R_N
    zero = jnp.zeros_like(x)
    return jnp.concatenate([jnp.where(head0, x, zero), jnp.where(head0, zero, x)], axis=0)


def _rwkv_fold(x):
    half = x.shape[0] // 2
    return x[0:half] + x[half:2 * half]


def _rwkv_prep_kernel(want_y, r_ref, k_ref, v_ref, lw_ref, la_ref,
                      ww_ref, wa_ref, w0_ref, a0_ref, kk_ref, ka_ref, *rest):
    extra_refs, out_refs = (rest[:4], rest[4:]) if want_y else ((), rest)
    c = RWKV_CHUNK
    c2 = 2 * c
    stack = _rwkv_stack
    rr = lax.broadcasted_iota(jnp.int32, (c2, c2), 0)
    cc = lax.broadcasted_iota(jnp.int32, (c2, c2), 1)
    group_ones = ((rr < c) == (cc < c)).astype(BF16)
    tt = lax.broadcasted_iota(jnp.int32, (c, c2), 0)
    lane = lax.broadcasted_iota(jnp.int32, (c, c2), 1)
    ss = lane & (c - 1)
    head0 = lane < c
    strict = (ss < tt, ss > tt)
    incl = (ss <= tt, ss >= tt)
    eye = (ss == tt).astype(F32)
    tr = lax.broadcasted_iota(jnp.int32, (c, c), 0)
    tc = lax.broadcasted_iota(jnp.int32, (c, c), 1)
    tri = ((tc <= tr).astype(BF16), (tc >= tr).astype(BF16))
    ww = ww_ref[0]
    wa = wa_ref[0]
    k_k = kk_ref[...]
    k_a = ka_ref[...]
    m_ref, n_ref, dec_ref = out_refs[0], out_refs[1], out_refs[2]

    def prep_chunks(js):
        nj = len(js)
        rows = [pl.ds(pl.multiple_of(j * c, c), c) for j in js]
        ch = [(d, i) for d in (0, 1) for i in range(nj)]
        half = lambda d: slice(d * LANES, (d + 1) * LANES)
        r = [r_ref[0, rw, :] for rw in rows]
        k = [k_ref[0, rw, :] for rw in rows]
        v_s = [stack(v_ref[0, rw, :]).astype(BF16) for rw in rows]
        lo_w = [_mm(jnp.tanh(lw_ref[0, rw, :]).astype(BF16), ww) for rw in rows]
        lo_a = [_mm(la_ref[0, rw, :].astype(BF16), wa) for rw in rows]
        kk = [x * k_k for x in k]
        kk = [x * lax.rsqrt(_mm_split(x * x, group_ones, False, pieces=2) + 1e-12) for x in kk]
        logw = [-jnp.exp(-_softplus(-(w0_ref[d:d + 1, :] + lo_w[i][:, half(d)])) - 0.5) for d, i in ch]
        a = [jax.nn.sigmoid(a0_ref[d:d + 1, :] + lo_a[i][:, half(d)]) for d, i in ch]
        pin = [_mm_split(x, tri[d], True, pieces=2) for (d, i), x in zip(ch, logw)]
        ptot = [x[0:1, :] if d else x[c - 1:c, :] for (d, i), x in zip(ch, pin)]
        kd = [k[i] * (1.0 + (a_ - 1.0) * k_a) for (d, i), a_ in zip(ch, a)]
        kka = [kk[i] * a_ for (d, i), a_ in zip(ch, a)]
        e_inv = [jnp.exp(-x) for x in pin]
        e_end = [jnp.exp(pt - x) for pt, x in zip(ptot, pin)]
        r_t = [r[i] * jnp.exp(x) for (d, i), x in zip(ch, pin)]
        a_t = [-kk[i] * jnp.exp(x - lw_) for (d, i), x, lw_ in zip(ch, pin, logw)]
        ar_t = [jnp.concatenate([x, y], axis=0).astype(BF16) for x, y in zip(a_t, r_t)]
        bk_t = [jnp.concatenate([stack(x * e), stack(y * e)], axis=0).astype(BF16)
                for x, y, e in zip(kka, kd, e_inv)]
        bk_end = [jnp.concatenate([x * e, y * e], axis=0).astype(BF16) for x, y, e in zip(kka, kd, e_end)]
        aa = [_mm_nt(x, y) for x, y in zip(ar_t, bk_t)]
        aab = [jnp.where(strict[d], x[0:c, 0:c2], 0.0) for (d, i), x in zip(ch, aa)]
        aak = [jnp.where(strict[d], x[0:c, c2:2 * c2], 0.0).astype(BF16) for (d, i), x in zip(ch, aa)]
        akv = [_mm(x, v_s[i]).astype(BF16) for (d, i), x in zip(ch, aak)]
        xs = [eye + x for x in aab]
        pb = [x.astype(BF16) for x in aab]
        ps = [_mm(x, stack(x)) for x in pb]
        for _ in range(c.bit_length() - 3):
            pb = [x.astype(BF16) for x in ps]
            both = [_mm(jnp.concatenate([x.astype(BF16), p], axis=0), stack(p)) for x, p in zip(xs, pb)]
            xs = [x + y[0:c] for x, y in zip(xs, both)]
            ps = [y[c:c2] for y in both]
        xs = [(x + _mm(x.astype(BF16), stack(p.astype(BF16)))).astype(BF16) for x, p in zip(xs, ps)]
        wu = [_mm(x, jnp.concatenate([stack(y[0:c]), stack(z)], axis=1)).astype(BF16)
              for x, y, z in zip(xs, ar_t, akv)]
        m_mat = [_mm_tn(x[:, 0:c2], y[0:c]) for x, y in zip(wu, bk_end)]
        n_mat = [_mm_tn(jnp.concatenate([x[:, c2:2 * c2], v_ref[0, rows[i], :].astype(BF16)], axis=0), y)
                 for (d, i), x, y in zip(ch, wu, bk_end)]
        for (d, i), mm_, nn_, pt in zip(ch, m_mat, n_mat, ptot):
            slot = (0, 0, d, js[i])
            m_ref[slot] = jnp.where(head0, mm_[0:c], mm_[c:c2]).astype(BF16)
            n_ref[slot] = jnp.where(head0, nn_[0:c], nn_[c:c2])
            dec_ref[slot] = jnp.broadcast_to(jnp.exp(pt), (8, LANES))
        if want_y:
            ark = [jnp.concatenate([jnp.where(incl[d], x[c:c2, 0:c2], 0.0),
                                    jnp.where(incl[d], x[c:c2, c2:2 * c2], 0.0)], axis=1).astype(BF16)
                   for (d, i), x in zip(ch, aa)]
            qy = [_mm(x, jnp.concatenate([
                      jnp.concatenate([stack(w_[:, 0:c2]), stack(w_[:, c2:2 * c2])], axis=1),
                      jnp.concatenate([jnp.zeros((c2, c2), BF16), v_s[i]], axis=1)], axis=0))
                  for (d, i), x, w_ in zip(ch, ark, wu)]
            for (d, i), rt, x in zip(ch, r_t, qy):
                slot = (0, 0, d, js[i])
                out_refs[3][slot] = (rt + x[:, 0:c2]).astype(BF16)
                out_refs[4][slot] = x[:, c2:2 * c2]
            lg0_ref, lg1_ref, wg_ref, rk_ref = extra_refs
            ksum = [k[i] * (2.0 + (a[i] + a[nj + i] - 2.0) * k_a) for i in range(nj)]
            bonus = [_mm_split(r[i] * ksum[i] * rk_ref[...], group_ones, False) * v_ref[0, rows[i], :]
                     for i in range(nj)]
            gate = [_mm(jax.nn.sigmoid(jnp.concatenate([lg0_ref[0, rw, :], lg1_ref[0, rw, :]], axis=1)
                                       ).astype(BF16), wg_ref[...]) for rw in rows]
            for rw, gt, bn in zip(rows, gate, bonus):
                out_refs[5][rw, :] = gt.astype(BF16)
                out_refs[6][rw, :] = (bn * gt).astype(BF16)

    n_chunks = r_ref.shape[1] // c
    width = min(RWKV_PREP_WIDTH, n_chunks)
    assert n_chunks % width == 0

    def body(t, carry):
        prep_chunks([t * width + u for u in range(width)])
        return carry

    lax.fori_loop(0, n_chunks // width, body, 0)


def _rwkv_scan_kernel(mc_ref, nc_ref, dc_ref, mf_ref, nf_ref, df_ref, qf_ref, ylf_ref,
                      mb_ref, nb_ref, db_ref, qb_ref, ylb_ref,
                      gate_ref, bg_ref, gnw_ref, gnb_ref,
                      o_ref, z_ref, yf_ref, yb_ref):
    c = RWKV_CHUNK
    g = pl.program_id(2)
    n_groups = pl.num_programs(2)
    group = mf_ref.shape[3]
    n_ctx = mc_ref.shape[3]
    pairs = range(RWKV_SCAN_PAIRS)

    def advance(z, m_c, n_c, dec):
        return z * dec[0:1, :] + _mm(z.astype(BF16), _rwkv_stack(m_c)) + _rwkv_stack(n_c)

    @pl.when(g == 0)
    def _():
        z_ref[...] = jnp.zeros(z_ref.shape, F32)

        def ctx_body(i, carry):
            for pp in pairs:
                for d, ii in ((0, i), (1, n_ctx - 1 - i)):
                    zi = 2 * pp + d
                    z_ref[zi] = advance(z_ref[zi], mc_ref[0, pp, d, ii], nc_ref[0, pp, d, ii],
                                        dc_ref[0, pp, d, ii])
            return carry

        lax.fori_loop(0, n_ctx, ctx_body, 0)

    def lat_body(j, carry):
        chains = [(pp,) + t for pp in pairs for t in (
            (0, j, g * group, mf_ref, nf_ref, df_ref, qf_ref, ylf_ref, yf_ref),
            (1, group - 1 - j, (n_groups - 1 - g) * group, mb_ref, nb_ref, db_ref, qb_ref, ylb_ref, yb_ref))]
        zs = [z_ref[2 * pp + d] for pp, d, *_ in chains]
        zb = [z.astype(BF16) for z in zs]
        ys = [_mm_nt(_rwkv_stack(q_ref[0, pp, 0, jj]), z)
              for (pp, d, jj, first, m_ref, n_ref, d_ref, q_ref, yl_ref, y_ref), z in zip(chains, zb)]
        zn = [z * d_ref[0, pp, 0, jj][0:1, :] + _mm(zh, _rwkv_stack(m_ref[0, pp, 0, jj]))
              + _rwkv_stack(n_ref[0, pp, 0, jj])
              for (pp, d, jj, first, m_ref, n_ref, d_ref, q_ref, yl_ref, y_ref), z, zh in zip(chains, zs, zb)]
        for (pp, d, jj, first, m_ref, n_ref, d_ref, q_ref, yl_ref, y_ref), y, z in zip(chains, ys, zn):
            z_ref[2 * pp + d] = z
            y_ref[pl.ds(pl.multiple_of((first + jj) * c, c), c), pp * LANES:(pp + 1) * LANES] = (
                _rwkv_fold(y) + yl_ref[0, pp, 0, jj])
        return carry

    lax.fori_loop(0, group, lat_body, 0)

    @pl.when(g == n_groups - 1)
    def _():
        fr = 4 * c
        r2 = lax.broadcasted_iota(jnp.int32, (LANES, LANES), 0)
        c2 = lax.broadcasted_iota(jnp.int32, (LANES, LANES), 1)
        group_ones = ((r2 < R_N) == (c2 < R_N)).astype(BF16)
        inv_n = 1.0 / R_N

        def fin(i, carry):
            rows = pl.ds(pl.multiple_of(i * fr, fr), fr)
            for pp in pairs:
                cols = slice(pp * LANES, (pp + 1) * LANES)
                y = yf_ref[rows, cols] + yb_ref[rows, cols]
                sums = _mm_split(jnp.concatenate([y, y * y], axis=0), group_ones, False, pieces=2)
                mu = sums[0:fr] * inv_n
                var = sums[fr:2 * fr] * inv_n - mu * mu
                yn = (y - mu) * lax.rsqrt(var + GN_EPS) * gnw_ref[:, cols] + gnb_ref[:, cols]
                o_ref[rows, cols] = (yn * gate_ref[rows, cols].astype(F32)
                                     + bg_ref[rows, cols].astype(F32)).astype(o_ref.dtype)
            return carry

        lax.fori_loop(0, o_ref.shape[0] // fr, fin, 0)


def _rwkv_prep(p_all, batch, group, want_y, ww, wa, w0, a0, k_k, k_a, wg=None, r_k=None):
    c = RWKV_CHUNK
    length = p_all.shape[1] // batch
    n_chunks = length // c
    n_groups = n_chunks // group
    n_pairs = ww.shape[0]
    rows = group * c

    def slab(first):
        return pl.BlockSpec((1, rows, LANES), lambda b, p, g: (first + p, b * n_groups + g, 0))

    def fixed(idx):
        return pl.BlockSpec((1, rows, LANES), lambda b, p, g: (idx, b * n_groups + g, 0))

    def vec(nrows):
        return pl.BlockSpec((nrows, LANES), lambda b, p, g: (0, p))

    lora = pl.BlockSpec((1, LANES, 2 * LANES), lambda b, p, g: (p, 0, 0))
    outs = [(c, BF16), (c, F32), (8, F32)] + ([(c, BF16), (c, F32)] if want_y else [])
    in_specs = [slab(SLAB_R), slab(SLAB_KR), slab(SLAB_VR), fixed(SLAB_LW), fixed(SLAB_LA),
                lora, lora, vec(2), vec(2), vec(1), vec(1)]
    operands = [p_all, p_all, p_all, p_all, p_all, ww, wa, w0, a0, k_k, k_a]
    out_specs = [pl.BlockSpec((1, 1, 2, group, nr, LANES), lambda b, p, g: (b, p, 0, g, 0, 0)) for nr, _ in outs]
    out_shape = [jax.ShapeDtypeStruct((batch, n_pairs, 2, n_chunks, nr, LANES), dt) for nr, dt in outs]
    if want_y:
        in_specs += [fixed(SLAB_LG0), fixed(SLAB_LG1),
                     pl.BlockSpec((2 * LANES, LANES), lambda b, p, g: (0, p)), vec(1)]
        operands += [p_all, p_all, wg, r_k]
        out_specs += [pl.BlockSpec((rows, LANES), lambda b, p, g: (b * n_groups + g, p))] * 2
        out_shape += [jax.ShapeDtypeStruct((batch * length, n_pairs * LANES), BF16)] * 2
    return pl.pallas_call(
        functools.partial(_rwkv_prep_kernel, want_y),
        grid=(batch, n_pairs, n_groups),
        in_specs=in_specs,
        out_specs=out_specs,
        out_shape=out_shape,
        compiler_params=_params(3),
    )(*operands)


def _rwkv_scan(ops_c, ops_x, gate, bonus_gated, batch, group, gn_w, gn_b):
    c = RWKV_CHUNK
    sp = RWKV_SCAN_PAIRS
    lx = gate.shape[0] // batch
    n_pairs, n_ctx = ops_c[0].shape[1], ops_c[0].shape[3]
    n_groups = ops_x[0].shape[3] // group
    assert n_pairs % sp == 0

    def ctx_block(a):
        return pl.BlockSpec((1, sp, 2, n_ctx, a.shape[4], LANES), lambda b, p, g: (b, p, 0, 0, 0, 0))

    def fwd_block(a):
        return pl.BlockSpec((1, sp, 1, group, a.shape[4], LANES), lambda b, p, g: (b, p, 0, g, 0, 0))

    def bwd_block(a):
        return pl.BlockSpec((1, sp, 1, group, a.shape[4], LANES),
                            lambda b, p, g: (b, p, 1, n_groups - 1 - g, 0, 0))

    tokens = pl.BlockSpec((lx, sp * LANES), lambda b, p, g: (b, p))
    tokens_out = pl.BlockSpec((lx, sp * LANES), lambda b, p, g: (b, p), pipeline_mode=pl.Buffered(1))
    vec = pl.BlockSpec((1, sp * LANES), lambda b, p, g: (0, p))
    return pl.pallas_call(
        _rwkv_scan_kernel,
        grid=(batch, n_pairs // sp, n_groups),
        in_specs=[ctx_block(a) for a in ops_c] + [fwd_block(a) for a in ops_x] + [bwd_block(a) for a in ops_x]
                 + [tokens, tokens, vec, vec],
        out_specs=tokens_out,
        out_shape=jax.ShapeDtypeStruct((batch * lx, n_pairs * LANES), BF16),
        scratch_shapes=[pltpu.VMEM((2 * sp, 2 * c, LANES), F32),
                        pltpu.VMEM((lx, sp * LANES), F32), pltpu.VMEM((lx, sp * LANES), F32)],
        compiler_params=_params(3),
    )(*ops_c, *ops_x, *ops_x, gate, bonus_gated, gn_w, gn_b)


def _rwkv(px, pc, batch, ww, wa, wg, w0, a0, k_k, k_a, r_k, gn_w, gn_b):
    c = RWKV_CHUNK
    n_ctx = pc.shape[1] // batch // c
    n_lat = px.shape[1] // batch // c
    group = min(RWKV_GROUP, n_lat)
    assert n_lat % group == 0
    ops_c = _rwkv_prep(pc, batch, n_ctx, False, ww, wa, w0, a0, k_k, k_a)
    *ops_x, gate, bonus_gated = _rwkv_prep(px, batch, group, True, ww, wa, w0, a0, k_k, k_a, wg, r_k)
    scan_group = min(RWKV_SCAN_GROUP, n_lat)
    assert n_lat % scan_group == 0
    return _rwkv_scan(ops_c, ops_x, gate, bonus_gated, batch, scan_group, gn_w, gn_b)


def _out_kernel(tiles_per_batch, mixm_ref, om_ref, mixr_ref, w_ref, x_ref, ng_ref,
                gt1_ref, g2_ref, sh2_ref, sc2_ref, x1_ref, hx2_ref):
    b = pl.program_id(0) // tiles_per_batch
    hm = mixm_ref[...].astype(F32)
    ng = ng_ref[...]
    parts = []
    for h in range(M_HEADS):
        cols = slice(h * M_DV, (h + 1) * M_DV)
        seg = hm[:, cols]
        seg = seg * lax.rsqrt(jnp.mean(seg * seg, axis=-1, keepdims=True) + NORM_EPS)
        og = jnp.concatenate([om_ref[2 * h], om_ref[2 * h + 1]], axis=1)
        parts.append((seg * ng[:, cols] * jax.nn.sigmoid(og)).astype(BF16))
    lhs = jnp.concatenate(parts + [mixr_ref[...]], axis=1)
    x1 = x_ref[...] + gt1_ref[pl.ds(b, 1), :] * _mm(lhs, w_ref[...])
    x1_ref[...] = x1
    y = x1 * lax.rsqrt(jnp.mean(x1 * x1, axis=-1, keepdims=True) + NORM_EPS) * g2_ref[...]
    hx2_ref[...] = (y * (1.0 + sc2_ref[pl.ds(b, 1), :]) + sh2_ref[pl.ds(b, 1), :]).astype(BF16)


def _out_proj(mixm, px, mixr, w_out, x2d, ng, gt1, g2, sh2, sc2, tm, tiles_per_batch):
    t, d = x2d.shape
    dm = mixm.shape[1]
    row = lambda i: (i, 0)
    const = lambda i: (0, 0)
    return pl.pallas_call(
        functools.partial(_out_kernel, tiles_per_batch),
        grid=(t // tm,),
        in_specs=[pl.BlockSpec((tm, dm), row),
                  pl.BlockSpec((8, tm, LANES), lambda i: (SLAB_O // 8, i, 0)),
                  pl.BlockSpec((tm, dm), row),
                  pl.BlockSpec((d, d), const),
                  pl.BlockSpec((tm, d), row),
                  pl.BlockSpec((1, dm), const),
                  pl.BlockSpec((8, d), const),
                  pl.BlockSpec((1, d), const),
                  pl.BlockSpec((8, d), const),
                  pl.BlockSpec((8, d), const)],
        out_specs=[pl.BlockSpec((tm, d), row), pl.BlockSpec((tm, d), row)],
        out_shape=[jax.ShapeDtypeStruct((t, d), F32), jax.ShapeDtypeStruct((t, d), BF16)],
        compiler_params=_params(1),
    )(mixm, px, mixr, w_out, x2d, ng, gt1, g2, sh2, sc2)


def _ffn_kernel(tiles_per_img, hx_ref, top_ref, bot_ref, wu_ref, wg_ref, wd_ref, cw_ref, cb_ref,
                x1_ref, gt2_ref, gf_ref, o_ref):
    i = pl.program_id(0)
    j = pl.program_id(1)
    tm = hx_ref.shape[0]
    ti = i % tiles_per_img

    @pl.when(j == 0)
    def _():
        o_ref[...] = jnp.zeros(o_ref.shape, F32)

    hx = hx_ref[...]
    top = jnp.where(ti > 0, top_ref[...], jnp.zeros_like(top_ref[...]))
    bot = jnp.where(ti < tiles_per_img - 1, bot_ref[...], jnp.zeros_like(bot_ref[...]))
    u = _mm(jnp.concatenate([top, hx, bot], axis=0), wu_ref[...])
    nr = u.shape[0]
    col = lax.broadcasted_iota(jnp.int32, u.shape, 0) & (GRID_W - 1)
    ul = jnp.where(col == 0, 0.0, pltpu.roll(u, 1, 0))
    ur = jnp.where(col == GRID_W - 1, 0.0, pltpu.roll(u, nr - 1, 0))
    cw = cw_ref[...]
    conv = cb_ref[...]
    for dy in range(3):
        rows = slice(dy * GRID_W, dy * GRID_W + tm)
        conv = (conv + ul[rows] * cw[3 * dy:3 * dy + 1, :] + u[rows] * cw[3 * dy + 1:3 * dy + 2, :]
                + ur[rows] * cw[3 * dy + 2:3 * dy + 3, :])
    gelu = 0.5 * conv * (1.0 + jnp.tanh(0.7978845608028654 * (conv + 0.044715 * conv * conv * conv)))
    act = (gelu * _mm(hx, wg_ref[...])).astype(BF16)
    o_ref[...] += _mm(act, wd_ref[...])

    @pl.when(j == pl.num_programs(1) - 1)
    def _():
        gate = gt2_ref[pl.ds(i // tiles_per_img, 1), :]
        nb = 256

        def finish_rows(k, carry):
            rows = pl.ds(pl.multiple_of(k * nb, nb), nb)
            x2 = x1_ref[rows, :] + gate * o_ref[rows, :]
            o_ref[rows, :] = x2 * lax.rsqrt(jnp.mean(x2 * x2, axis=-1, keepdims=True) + NORM_EPS) * gf_ref[...]
            return carry

        lax.fori_loop(0, tm // nb, finish_rows, 0)


def _conv_ffn(hx2, w_up, w_gate, w_down, cw, cb, x1, gt2, g_final, tm, tiles_per_img):
    t, d = hx2.shape
    f = w_up.shape[1]
    tf = FFN_TF
    rows_per_tile = tm // GRID_W
    n_rows = t // GRID_W
    return pl.pallas_call(
        functools.partial(_ffn_kernel, tiles_per_img),
        grid=(t // tm, f // tf),
        in_specs=[pl.BlockSpec((tm, d), lambda i, j: (i, 0)),
                  pl.BlockSpec((GRID_W, d), lambda i, j: (jnp.maximum(i * rows_per_tile - 1, 0), 0)),
                  pl.BlockSpec((GRID_W, d), lambda i, j: (jnp.minimum((i + 1) * rows_per_tile, n_rows - 1), 0)),
                  pl.BlockSpec((d, tf), lambda i, j: (0, j)),
                  pl.BlockSpec((d, tf), lambda i, j: (0, j)),
                  pl.BlockSpec((tf, d), lambda i, j: (j, 0)),
                  pl.BlockSpec((9, tf), lambda i, j: (0, j)),
                  pl.BlockSpec((1, tf), lambda i, j: (0, j)),
                  pl.BlockSpec((tm, d), lambda i, j: (i, 0)),
                  pl.BlockSpec((8, d), lambda i, j: (0, 0)),
                  pl.BlockSpec((1, d), lambda i, j: (0, 0))],
        out_specs=pl.BlockSpec((tm, d), lambda i, j: (i, 0)),
        out_shape=jax.ShapeDtypeStruct((t, d), F32),
        compiler_params=_params(2, 63 * 1024 * 1024),
    )(hx2, hx2, hx2, w_up, w_gate, w_down, cw, cb, x1, gt2, g_final)


def _lora_pairs(up):
    _, rank, width = up.shape
    u = up.reshape(2, rank, width // LANES, LANES).transpose(2, 0, 1, 3)
    z = jnp.zeros_like(u[:, 0])
    top = jnp.concatenate([u[:, 0], z], axis=-1)
    bot = jnp.concatenate([z, u[:, 1]], axis=-1)
    return jnp.concatenate([top, bot], axis=1).astype(BF16)


def kernel(x, c, ctx, c_ctx, w_mod, b_mod, g_norm1, g_norm2, w_in, m_conv_w, m_conv_b, m_gate_b, m_norm_g, r_w0, r_w_up, r_a0, r_a_up, r_g_up, r_k_k, r_k_a, r_r_k, r_gn_w, r_gn_b, w_out, f_w_up, f_w_gate, f_conv_w, f_conv_b, f_w_down, g_final):
    batch, seq, d = x.shape
    ctx_len = ctx.shape[1]
    assert w_mod.shape[0] == 1, "single-layer block"
    assert batch + 1 <= 8 and seq % 512 == 0 and ctx_len % MLSTM_CHUNK == 0

    cv8 = jnp.zeros((8, d), F32).at[:batch].set(c).at[batch].set(c_ctx)
    mod = _modulation(cv8, w_mod[0], b_mod[0])
    sh1, sc1, gt1, sh2, sc2, gt2 = (mod[:, k * d:(k + 1) * d] for k in range(6))

    n_gate = 4 * M_HEADS
    w_p = _permute_w(jnp.swapaxes(w_in, 1, 2), SLAB_R * LANES, n_gate, N_SLABS * LANES)
    g1 = g_norm1[0].reshape(1, d)
    tm_x = INPROJ_TM if seq % INPROJ_TM == 0 else OUT_TM
    px = _inproj(x.reshape(batch * seq, d), g1, sh1, sc1, w_p, tm_x, seq // tm_x, 0)
    tm_c = batch * ctx_len
    pc = _inproj(ctx.reshape(batch * ctx_len, d), g1, sh1, sc1, w_p, tm_c, 1 << 30, batch)

    gate_row = jnp.zeros((1, LANES), F32).at[0, GATE_LANE0:GATE_LANE0 + n_gate].set(m_gate_b[0].reshape(-1))
    mixm = _mlstm(px, pc, batch, m_conv_w[0], m_conv_b[0].reshape(1, -1), gate_row)

    rw = r_k_k.shape[1]
    wg = jnp.zeros((2 * LANES, rw), F32).at[:r_g_up.shape[1]].set(r_g_up[0]).astype(BF16)
    mixr = _rwkv(px, pc, batch, _lora_pairs(r_w_up[0]), _lora_pairs(r_a_up[0]), wg,
                 r_w0[0], r_a0[0], r_k_k, r_k_a, r_r_k[0].reshape(1, rw), r_gn_w, r_gn_b)

    tm_o = OUT_TM
    x1, hx2 = _out_proj(mixm, px, mixr, w_out[0].astype(BF16), x.reshape(batch * seq, d), m_norm_g,
                        gt1, g_norm2[0].reshape(1, d), sh2, sc2, tm_o, seq // tm_o)

    tm_f = FFN_TM if seq % FFN_TM == 0 else OUT_TM
    out = _conv_ffn(hx2, f_w_up[0].astype(BF16), f_w_gate[0].astype(BF16), f_w_down[0].astype(BF16),
                    f_conv_w[0].reshape(9, -1), f_conv_b, x1, gt2, g_final.reshape(1, d),
                    tm_f, seq // tm_f)
    return out.reshape(batch, seq, d)
```
